```python
import math
import jax, jax.numpy as jnp
from jax import lax
import numpy as np

D_MODEL = 1024
BATCH = 2
SEQ = 8192
DEPTH = 2

GRID_W = 64
CTX_LEN = 256

N_FOURIER_GROUPS = 4
FOURIER_GROUP = 128
FOURIER_WIDTH = N_FOURIER_GROUPS * FOURIER_GROUP

N_Q_HEADS = 8
N_KV_HEADS = 2
HEAD_DIM = 64
GROUP = N_Q_HEADS // N_KV_HEADS
ATTN_WIDTH = N_Q_HEADS * HEAD_DIM
KV_WIDTH = N_KV_HEADS * HEAD_DIM
WINDOW = 128
BLOCK = 128
ROPE_BASE = 10000.0

OFF_Q = FOURIER_WIDTH
OFF_K = OFF_Q + ATTN_WIDTH
OFF_V = OFF_K + KV_WIDTH
OFF_G = OFF_V + KV_WIDTH
IN_WIDTH = OFF_G + 2 * D_MODEL

D_FF_DENSE = 2816
N_EXPERTS = 8
TOP_K = 2
D_FF_EXPERT = 3584
N_DENSE = (DEPTH + 1) // 2
N_MOE = DEPTH // 2

N_MOD = 6
EPS = 1e-6
NEG = -1e30

kernel_name = "hybrid_fourier_swa_moe_dit"


def rmsnorm(x, g):
    xf = x.astype(jnp.float32)
    r = lax.rsqrt(jnp.mean(xf * xf, axis=-1, keepdims=True) + EPS)
    return (xf * r).astype(x.dtype) * g


def adaln(cvec, w_mod, b_mod, n_chunks):
    m = jax.nn.silu(cvec) @ w_mod[:, :n_chunks * D_MODEL] + b_mod[:n_chunks * D_MODEL]
    return jnp.split(m, n_chunks, axis=-1)


def modulate(h, shift, scale):
    return h * (1.0 + scale) + shift


def axial_rope_tables(n_rows):
    rows = jnp.broadcast_to(jnp.arange(n_rows)[:, None], (n_rows, GRID_W)).reshape(-1).astype(jnp.float32)
    cols = jnp.broadcast_to(jnp.arange(GRID_W)[None, :], (n_rows, GRID_W)).reshape(-1).astype(jnp.float32)
    n_freq = HEAD_DIM // 4
    inv = ROPE_BASE ** (-jnp.arange(n_freq, dtype=jnp.float32) / n_freq)
    ar = rows[:, None] * inv
    ac = cols[:, None] * inv
    return jnp.cos(ar), jnp.sin(ar), jnp.cos(ac), jnp.sin(ac)


def rope_rotate(x, cos, sin):
    h = x.shape[-1] // 2
    x1, x2 = x[..., :h], x[..., h:]
    c = cos[None, :, None, :].astype(x.dtype)
    s = sin[None, :, None, :].astype(x.dtype)
    return jnp.concatenate([x1 * c - x2 * s, x1 * s + x2 * c], axis=-1)


def axial_rope(x, tabs):
    cr, sr, cc, sc = tabs
    a = HEAD_DIM // 2
    return jnp.concatenate([rope_rotate(x[..., :a], cr, sr), rope_rotate(x[..., a:], cc, sc)], axis=-1)


def fourier_mix(u):
    b, t, _ = u.shape
    ug = u.astype(jnp.float32).reshape(b, t, N_FOURIER_GROUPS, FOURIER_GROUP)
    f = jnp.fft.fft2(ug, axes=(1, 3), norm="ortho").real
    return f.reshape(b, t, FOURIER_WIDTH).astype(u.dtype)


def sink_logits(sink, lead_shape, q_len):
    s = sink.astype(jnp.float32).reshape(N_KV_HEADS, GROUP)[:, :, None, None]
    return jnp.broadcast_to(s, lead_shape + (N_KV_HEADS, GROUP, q_len, 1))


def latent_attention(q, k, v, kc, vc, sink):
    b, t = q.shape[:2]
    nb = t // BLOCK
    scale = HEAD_DIM ** -0.5
    qb = q.reshape(b, nb, BLOCK, N_KV_HEADS, GROUP, HEAD_DIM)

    def band(a):
        ap = jnp.pad(a, ((0, 0), (BLOCK, BLOCK), (0, 0), (0, 0)))
        return jnp.concatenate(
            [ap[:, j * BLOCK:j * BLOCK + t].reshape(b, nb, BLOCK, N_KV_HEADS, HEAD_DIM) for j in range(3)],
            axis=2)

    kb, vb = band(k), band(v)
    s_loc = jnp.einsum('bnqhgd,bnkhd->bnhgqk', qb, kb).astype(jnp.float32) * scale
    qpos = jnp.arange(BLOCK)
    kpos = jnp.arange(3 * BLOCK) - BLOCK
    kabs = (jnp.arange(nb) * BLOCK)[:, None] + kpos[None, :]
    in_win = jnp.abs(kpos[None, :] - qpos[:, None]) <= WINDOW
    valid = in_win[None] & ((kabs >= 0) & (kabs < t))[:, None, :]
    s_loc = jnp.where(valid[None, :, None, None], s_loc, NEG)
    s_ctx = jnp.einsum('bnqhgd,bchd->bnhgqc', qb, kc).astype(jnp.float32) * scale
    s = jnp.concatenate([s_loc, s_ctx, sink_logits(sink, (b, nb), BLOCK)], axis=-1)
    p = jax.nn.softmax(s, axis=-1).astype(v.dtype)
    n_loc = 3 * BLOCK
    n_ctx = kc.shape[1]
    o = (jnp.einsum('bnhgqk,bnkhd->bnqhgd', p[..., :n_loc], vb)
         + jnp.einsum('bnhgqc,bchd->bnqhgd', p[..., n_loc:n_loc + n_ctx], vc))
    return o.reshape(b, t, ATTN_WIDTH)


def context_attention(qc, kc, vc, sink):
    b, c = qc.shape[:2]
    scale = HEAD_DIM ** -0.5
    q = qc.reshape(b, c, N_KV_HEADS, GROUP, HEAD_DIM)
    s = jnp.einsum('bqhgd,bkhd->bhgqk', q, kc).astype(jnp.float32) * scale
    s = jnp.concatenate([s, sink_logits(sink, (b,), c)], axis=-1)
    p = jax.nn.softmax(s, axis=-1).astype(vc.dtype)
    o = jnp.einsum('bhgqk,bkhd->bqhgd', p[..., :c], vc)
    return o.reshape(b, c, ATTN_WIDTH)


def split_heads(a, n_heads):
    return a.reshape(a.shape[:-1] + (n_heads, HEAD_DIM))


def merge_branches(u, o_attn, g_f, g_a, w_f, w_a, w_o):
    y = jax.nn.sigmoid(g_f) * (fourier_mix(u) @ w_f) + jax.nn.sigmoid(g_a) * (o_attn @ w_a)
    return y @ w_o


def swiglu(h, w_g, w_u, w_d):
    return (jax.nn.silu(h @ w_g) * (h @ w_u)) @ w_d


def moe_swiglu(h, w_router, w_g, w_u, w_d):
    shp = h.shape
    t = h.reshape(-1, shp[-1])
    logits = (t @ w_router).astype(jnp.float32)
    topv, topi = lax.top_k(logits, TOP_K)
    wts = jax.nn.softmax(topv, axis=-1)
    combine = jnp.sum(wts[..., None] * jax.nn.one_hot(topi, N_EXPERTS, dtype=jnp.float32), axis=1).astype(h.dtype)
    y = jnp.zeros_like(t)
    for e in range(N_EXPERTS):
        y = y + combine[:, e:e + 1] * swiglu(t, w_g[e], w_u[e], w_d[e])
    return y.reshape(shp)


def setup_inputs(seed: int = 0) -> dict:
    key = jax.random.key(seed)
    ks = jax.random.split(key, 21)
    f32 = jnp.float32
    nrm = lambda k, shp, s: jax.random.normal(k, shp, f32) * s
    D = D_MODEL
    return {
        "x": nrm(ks[0], (BATCH, SEQ, D), 1.0),
        "c": nrm(ks[1], (BATCH, D), 1.0),
        "ctx": nrm(ks[2], (BATCH, CTX_LEN, D), 1.0),
        "c_ctx": nrm(ks[3], (D,), 1.0),
        "w_mod": nrm(ks[4], (DEPTH, D, N_MOD * D), 0.5 * D ** -0.5),
        "b_mod": nrm(ks[5], (DEPTH, N_MOD * D), 0.02),
        "norm1_g": 1.0 + nrm(ks[6], (DEPTH, D), 0.02),
        "norm2_g": 1.0 + nrm(ks[7], (DEPTH, D), 0.02),
        "w_in": nrm(ks[8], (DEPTH, D, IN_WIDTH), D ** -0.5),
        "sink": nrm(ks[9], (DEPTH, N_Q_HEADS), 0.5),
        "w_fourier": nrm(ks[10], (DEPTH, FOURIER_WIDTH, D), FOURIER_WIDTH ** -0.5),
        "w_attn": nrm(ks[11], (DEPTH, ATTN_WIDTH, D), ATTN_WIDTH ** -0.5),
        "w_out": nrm(ks[12], (DEPTH, D, D), D ** -0.5),
        "w_gate_d": nrm(ks[13], (N_DENSE, D, D_FF_DENSE), D ** -0.5),
        "w_up_d": nrm(ks[14], (N_DENSE, D, D_FF_DENSE), D ** -0.5),
        "w_down_d": nrm(ks[15], (N_DENSE, D_FF_DENSE, D), D_FF_DENSE ** -0.5),
        "w_router": nrm(ks[16], (N_MOE, D, N_EXPERTS), D ** -0.5),
        "w_gate_e": nrm(ks[17], (N_MOE, N_EXPERTS, D, D_FF_EXPERT), D ** -0.5),
        "w_up_e": nrm(ks[18], (N_MOE, N_EXPERTS, D, D_FF_EXPERT), D ** -0.5),
        "w_down_e": nrm(ks[19], (N_MOE, N_EXPERTS, D_FF_EXPERT, D), D_FF_EXPERT ** -0.5),
        "final_g": 1.0 + nrm(ks[20], (D,), 0.02),
    }


def reference(x, c, ctx, c_ctx, w_mod, b_mod, norm1_g, norm2_g, w_in, sink, w_fourier, w_attn, w_out,
              w_gate_d, w_up_d, w_down_d, w_router, w_gate_e, w_up_e, w_down_e, final_g):
    xl, xc = x, ctx
    b, t = xl.shape[:2]
    n_rows = t // GRID_W
    tabs = axial_rope_tables(n_rows)
    c_lat = c[:, None, :]
    c_cx = c_ctx[None, None, :]
    for l in range(DEPTH):
        last = l == DEPTH - 1
        sh1, sc1, ga1, sh2, sc2, ga2 = adaln(c_lat, w_mod[l], b_mod[l], N_MOD)
        hl = modulate(rmsnorm(xl, norm1_g[l]), sh1, sc1)
        pl = hl @ w_in[l]
        q = axial_rope(split_heads(pl[..., OFF_Q:OFF_K], N_Q_HEADS), tabs)
        k = axial_rope(split_heads(pl[..., OFF_K:OFF_V], N_KV_HEADS), tabs)
        v = split_heads(pl[..., OFF_V:OFF_G], N_KV_HEADS)
        if last:
            csh1, csc1 = adaln(c_cx, w_mod[l], b_mod[l], 2)
            hc = modulate(rmsnorm(xc, norm1_g[l]), csh1, csc1)
            pkv = hc @ w_in[l][:, OFF_K:OFF_G]
            kc = split_heads(pkv[..., :KV_WIDTH], N_KV_HEADS)
            vc = split_heads(pkv[..., KV_WIDTH:], N_KV_HEADS)
        else:
            csh1, csc1, cga1, csh2, csc2, cga2 = adaln(c_cx, w_mod[l], b_mod[l], N_MOD)
            hc = modulate(rmsnorm(xc, norm1_g[l]), csh1, csc1)
            pc = hc @ w_in[l]
            qc = split_heads(pc[..., OFF_Q:OFF_K], N_Q_HEADS)
            kc = split_heads(pc[..., OFF_K:OFF_V], N_KV_HEADS)
            vc = split_heads(pc[..., OFF_V:OFF_G], N_KV_HEADS)
            oc = context_attention(qc, kc, vc, sink[l])
            yc = merge_branches(pc[..., :OFF_Q], oc, pc[..., OFF_G:OFF_G + D_MODEL], pc[..., OFF_G + D_MODEL:],
                                w_fourier[l], w_attn[l], w_out[l])
        ol = latent_attention(q, k, v, kc, vc, sink[l])
        yl = merge_branches(pl[..., :OFF_Q], ol, pl[..., OFF_G:OFF_G + D_MODEL], pl[..., OFF_G + D_MODEL:],
                            w_fourier[l], w_attn[l], w_out[l])
        xl = xl + ga1 * yl
        hl2 = modulate(rmsnorm(xl, norm2_g[l]), sh2, sc2)
        if l % 2 == 0:
            i = l // 2
            ffn = lambda h: swiglu(h, w_gate_d[i], w_up_d[i], w_down_d[i])
        else:
            i = l // 2
            ffn = lambda h: moe_swiglu(h, w_router[i], w_gate_e[i], w_up_e[i], w_down_e[i])
        xl = xl + ga2 * ffn(hl2)
        if not last:
            xc = xc + cga1 * yc
            hc2 = modulate(rmsnorm(xc, norm2_g[l]), csh2, csc2)
            xc = xc + cga2 * ffn(hc2)
    return rmsnorm(xl, final_g)
```

```python
import functools
import math

import numpy as np
import jax
import jax.numpy as jnp
from jax import lax
from jax.experimental import pallas as pl
from jax.experimental.pallas import tpu as pltpu

F32 = jnp.float32
BF16 = jnp.bfloat16

D_MODEL = 1024
SEQ = 8192
DEPTH = 2
GRID_W = 64
CTX_LEN = 256
N_GROUPS = 4
GROUP_W = 128
FOURIER_WIDTH = N_GROUPS * GROUP_W
N_Q_HEADS = 8
N_KV_HEADS = 2
HEAD_DIM = 64
ATTN_WIDTH = N_Q_HEADS * HEAD_DIM
KV_WIDTH = N_KV_HEADS * HEAD_DIM
WINDOW = 128
ROPE_BASE = 10000.0
OFF_Q = FOURIER_WIDTH
OFF_K = OFF_Q + ATTN_WIDTH
OFF_V = OFF_K + KV_WIDTH
OFF_G = OFF_V + KV_WIDTH
IN_WIDTH = OFF_G + 2 * D_MODEL
N_EXPERTS = 8
N_MOD = 6
EPS = 1e-6
NEG = -1e30

LANES = 128
SUBLANES = 8
ROW_SLABS = D_MODEL // LANES

TM = 256
QB = 128
FFT_R = SEQ // GRID_W
FFT_C = GRID_W
TME = 512
TMD = 256
FC_E = 1792
VMEM_LIMIT = 56 * 1024 * 1024


def _cparams(sem, vmem=None):
    return pltpu.CompilerParams(dimension_semantics=sem, vmem_limit_bytes=vmem)


def _dot(a, b):
    return jnp.dot(a, b, preferred_element_type=F32)


def _sigmoid(x):
    return 1.0 / (1.0 + jnp.exp(-x))


def _rms_mod(x, g, sh, sc):
    r = lax.rsqrt(jnp.mean(x * x, axis=-1, keepdims=True) + EPS)
    return (x * r) * g * (1.0 + sc) + sh


def _dft_cs(n):
    k = np.arange(n, dtype=np.float64)
    a = 2.0 * np.pi * np.outer(k, k) / n
    return np.cos(a), np.sin(a)


@functools.lru_cache(maxsize=None)
def _tables():
    c128, s128 = _dft_cs(FFT_R)
    c64, s64 = _dft_cs(FFT_C)
    c256, s256 = _dft_cs(CTX_LEN)
    wa = np.concatenate([c128, -s128], axis=0)
    k1 = np.arange(FFT_R, dtype=np.float64)[:, None]
    cc = np.arange(FFT_C, dtype=np.float64)[None, :]
    ang = 2.0 * np.pi * k1 * cc / SEQ
    twc = np.broadcast_to(np.cos(ang)[:, :, None], (FFT_R, FFT_C, LANES))
    tws = np.broadcast_to(np.sin(ang)[:, :, None], (FFT_R, FFT_C, LANES))
    mc = np.block([[c64, s64], [-s64, c64]])
    cg, sg = _dft_cs(GROUP_W)
    mch = np.concatenate([cg, sg], axis=0) / math.sqrt(SEQ * GROUP_W)
    wcx = np.concatenate([c256, s256], axis=0)
    mchc = np.concatenate([cg, -sg], axis=0)
    n_freq = HEAD_DIM // 4
    inv = ROPE_BASE ** (-np.arange(n_freq, dtype=np.float64) / n_freq)
    t = np.arange(SEQ)
    rows = (t // GRID_W).astype(np.float64)[:, None] * inv
    cols = (t % GRID_W).astype(np.float64)[:, None] * inv
    cos_h = np.concatenate([np.cos(rows), np.cos(rows), np.cos(cols), np.cos(cols)], axis=1)
    sin_h = np.concatenate([-np.sin(rows), np.sin(rows), -np.sin(cols), np.sin(cols)], axis=1)
    cos_t = np.concatenate([cos_h, cos_h], axis=1)
    sin_t = np.concatenate([sin_h, sin_h], axis=1)
    f = lambda a: np.ascontiguousarray(a, dtype=np.float32)
    return dict(wa=f(wa), twc=f(twc), tws=f(tws), mc=f(mc), mch=f(mch), wcx=f(wcx), mchc=f(mchc),
                cos=f(cos_t), sin=f(sin_t),
                cos1=np.ones((TM, LANES), np.float32), sin0=np.zeros((TM, LANES), np.float32))


def _mod_body(cv_ref, w_ref, b_ref, o_ref):
    c = cv_ref[...]
    s = c * _sigmoid(c)
    o_ref[0, 0] = jnp.dot(s, w_ref[0], precision=lax.Precision.HIGHEST,
                          preferred_element_type=F32) + b_ref[0]


def _modulation(cv, w_mod, b_mod):
    out = pl.pallas_call(
        _mod_body,
        grid=(DEPTH, N_MOD),
        in_specs=[pl.BlockSpec((SUBLANES, D_MODEL), lambda l, k: (0, 0)),
                  pl.BlockSpec((1, D_MODEL, D_MODEL), lambda l, k: (l, 0, k)),
                  pl.BlockSpec((1, 1, D_MODEL), lambda l, k: (l * N_MOD + k, 0, 0))],
        out_specs=pl.BlockSpec((1, 1, SUBLANES, D_MODEL), lambda l, k: (l, k, 0, 0)),
        out_shape=jax.ShapeDtypeStruct((DEPTH, N_MOD, SUBLANES, D_MODEL), F32),
        compiler_params=_cparams(("arbitrary", "arbitrary")),
        name="modulation",
    )(cv, w_mod, b_mod.reshape(DEPTH * N_MOD, 1, D_MODEL))
    out = jnp.transpose(out[:, :, :3, :], (0, 2, 1, 3))
    return out.reshape(DEPTH * 3 * N_MOD, 1, D_MODEL)


def _mod_spec(layer, k, cond_fn):
    return pl.BlockSpec((1, 1, D_MODEL),
                        lambda i, *_: ((layer * 3 + cond_fn(i)) * N_MOD + k, 0, 0))


def _in_proj_body(x_ref, sh_ref, sc_ref, g_ref, w_ref, cos_ref, sin_ref,
                  u_ref, q_ref, kv_ref, gt_ref):
    hb = _rms_mod(x_ref[...], g_ref[...], sh_ref[0], sc_ref[0]).astype(BF16)
    u_ref[...] = _dot(hb, w_ref[:, 0:OFF_Q])
    cos = cos_ref[...]
    sin = sin_ref[...]
    lane = lax.broadcasted_iota(jnp.int32, cos.shape, 1)
    first_half = (lane % (HEAD_DIM // 2)) < (HEAD_DIM // 4)
    quarter = HEAD_DIM // 4

    def rope(xs):
        below = pltpu.roll(xs, quarter, 1)
        above = pltpu.roll(xs, LANES - quarter, 1)
        return xs * cos + jnp.where(first_half, above, below) * sin

    qk = _dot(hb, w_ref[:, OFF_Q:OFF_V])
    scale = HEAD_DIM ** -0.5
    for j in range(ATTN_WIDTH // LANES):
        sl = slice(j * LANES, (j + 1) * LANES)
        q_ref[:, sl] = (rope(qk[:, sl]) * scale).astype(BF16)
    kv_ref[:, 0:KV_WIDTH] = rope(qk[:, ATTN_WIDTH:ATTN_WIDTH + KV_WIDTH]).astype(BF16)
    kv_ref[:, KV_WIDTH:2 * KV_WIDTH] = _dot(hb, w_ref[:, OFF_V:OFF_G]).astype(BF16)
    gc = 512
    for j in range(2 * D_MODEL // gc):
        z = _dot(hb, w_ref[:, OFF_G + j * gc:OFF_G + (j + 1) * gc])
        gt_ref[:, j * gc:(j + 1) * gc] = _sigmoid(z).astype(BF16)


def _in_proj(x, mods, layer, cond_fn, g, w_bf, cos, sin, rope_idx):
    rows = x.shape[0]
    return pl.pallas_call(
        _in_proj_body,
        grid=(rows // TM,),
        in_specs=[pl.BlockSpec((TM, D_MODEL), lambda i: (i, 0)),
                  _mod_spec(layer, 0, cond_fn),
                  _mod_spec(layer, 1, cond_fn),
                  pl.BlockSpec((1, D_MODEL), lambda i: (0, 0)),
                  pl.BlockSpec((D_MODEL, IN_WIDTH), lambda i: (0, 0)),
                  pl.BlockSpec((TM, LANES), lambda i: (rope_idx(i), 0)),
                  pl.BlockSpec((TM, LANES), lambda i: (rope_idx(i), 0))],
        out_specs=[pl.BlockSpec((TM, FOURIER_WIDTH), lambda i: (i, 0)),
                   pl.BlockSpec((TM, ATTN_WIDTH), lambda i: (i, 0)),
                   pl.BlockSpec((TM, 2 * KV_WIDTH), lambda i: (i, 0)),
                   pl.BlockSpec((TM, 2 * D_MODEL), lambda i: (i, 0))],
        out_shape=[jax.ShapeDtypeStruct((rows, FOURIER_WIDTH), F32),
                   jax.ShapeDtypeStruct((rows, ATTN_WIDTH), BF16),
                   jax.ShapeDtypeStruct((rows, 2 * KV_WIDTH), BF16),
                   jax.ShapeDtypeStruct((rows, 2 * D_MODEL), BF16)],
        compiler_params=_cparams(("arbitrary",), VMEM_LIMIT),
        name="in_proj",
    )(x, mods, mods, g, w_bf, cos, sin)


FFT_CB = 8


def _fft_a_body(x_ref, w_ref, o_ref):
    w = w_ref[...]
    for c in range(FFT_CB):
        o_ref[0, :, c, :] = _dot(w, x_ref[:, c, :].astype(BF16))


def _fft_c_body(re_ref, im_ref, tc_ref, ts_ref, mc_ref, mch_ref, o_ref):
    mc = mc_ref[...]
    mch = mch_ref[...]
    for j in range(FFT_CB):
        ar = re_ref[0, j]
        ai = im_ref[0, j]
        tc = jnp.concatenate([tc_ref[j]] * N_GROUPS, axis=1)
        ts = jnp.concatenate([ts_ref[j]] * N_GROUPS, axis=1)
        br = ar * tc + ai * ts
        bi = ai * tc - ar * ts
        x = _dot(mc, jnp.concatenate([br, bi], axis=0).astype(BF16))
        xr = x[:FFT_C]
        xi = x[FFT_C:]
        ys = []
        for g in range(N_GROUPS):
            sl = slice(g * GROUP_W, (g + 1) * GROUP_W)
            lhs = jnp.concatenate([xr[:, sl], xi[:, sl]], axis=1).astype(BF16)
            ys.append(_dot(lhs, mch))
        o_ref[0, :, j, :] = jnp.concatenate(ys, axis=1)


def _fourier_latent(u, tb, batch):
    u3 = u.reshape(batch * FFT_R, FFT_C, FOURIER_WIDTH)
    a = pl.pallas_call(
        _fft_a_body,
        grid=(batch, FFT_C // FFT_CB),
        in_specs=[pl.BlockSpec((FFT_R, FFT_CB, FOURIER_WIDTH), lambda b, j: (b, j, 0)),
                  pl.BlockSpec((2 * FFT_R, FFT_R), lambda b, j: (0, 0))],
        out_specs=pl.BlockSpec((1, 2 * FFT_R, FFT_CB, FOURIER_WIDTH), lambda b, j: (b, 0, j, 0)),
        out_shape=jax.ShapeDtypeStruct((batch, 2 * FFT_R, FFT_C, FOURIER_WIDTH), F32),
        compiler_params=_cparams(("arbitrary", "arbitrary"), VMEM_LIMIT),
        name="fft_rows",
    )(u3, tb["wa"].astype(BF16))
    nk = FFT_R // FFT_CB
    y = pl.pallas_call(
        _fft_c_body,
        grid=(batch, nk),
        in_specs=[pl.BlockSpec((1, FFT_CB, FFT_C, FOURIER_WIDTH), lambda b, k: (b, k, 0, 0)),
                  pl.BlockSpec((1, FFT_CB, FFT_C, FOURIER_WIDTH), lambda b, k: (b, nk + k, 0, 0)),
                  pl.BlockSpec((FFT_CB, FFT_C, LANES), lambda b, k: (k, 0, 0)),
                  pl.BlockSpec((FFT_CB, FFT_C, LANES), lambda b, k: (k, 0, 0)),
                  pl.BlockSpec((2 * FFT_C, 2 * FFT_C), lambda b, k: (0, 0)),
                  pl.BlockSpec((2 * GROUP_W, GROUP_W), lambda b, k: (0, 0))],
        out_specs=pl.BlockSpec((1, FFT_C, FFT_CB, FOURIER_WIDTH), lambda b, k: (b, 0, k, 0)),
        out_shape=jax.ShapeDtypeStruct((batch, FFT_C, FFT_R, FOURIER_WIDTH), F32),
        compiler_params=_cparams(("arbitrary", "arbitrary"), VMEM_LIMIT),
        name="fft_cols",
    )(a, a, tb["twc"], tb["tws"], tb["mc"].astype(BF16), tb["mch"].astype(BF16))
    return y.reshape(batch * SEQ, FOURIER_WIDTH)


def _fourier_ctx_body(u_ref, w_ref, m_ref, o_ref):
    pq = _dot(w_ref[...], u_ref[...].astype(BF16))
    p = pq[:CTX_LEN]
    q = pq[CTX_LEN:]
    m = m_ref[...]
    scale = 1.0 / math.sqrt(CTX_LEN * GROUP_W)
    for g in range(N_GROUPS):
        sl = slice(g * GROUP_W, (g + 1) * GROUP_W)
        lhs = jnp.concatenate([p[:, sl], q[:, sl]], axis=1).astype(BF16)
        o_ref[:, sl] = _dot(lhs, m) * scale


def _fourier_ctx(u, tb, batch):
    return pl.pallas_call(
        _fourier_ctx_body,
        grid=(batch,),
        in_specs=[pl.BlockSpec((CTX_LEN, FOURIER_WIDTH), lambda b: (b, 0)),
                  pl.BlockSpec((2 * CTX_LEN, CTX_LEN), lambda b: (0, 0)),
                  pl.BlockSpec((2 * GROUP_W, GROUP_W), lambda b: (0, 0))],
        out_specs=pl.BlockSpec((CTX_LEN, FOURIER_WIDTH), lambda b: (b, 0)),
        out_shape=jax.ShapeDtypeStruct((batch * CTX_LEN, FOURIER_WIDTH), F32),
        compiler_params=_cparams(("arbitrary",)),
        name="fft_ctx",
    )(u, tb["wcx"].astype(BF16), tb["mchc"].astype(BF16))


def _attn_body(sink_ref, q_ref, *refs, local):
    o_ref = refs[-1]
    if local:
        kp_ref, kc_ref, kn_ref, kx_ref = refs[:-1]
        kv = jnp.concatenate([kp_ref[...], kc_ref[...], kn_ref[...], kx_ref[...]], axis=0)
    else:
        kv = refs[0][...]
    nk = kv.shape[0]
    kv = kv.astype(F32)
    kcat = kv[:, :KV_WIDTH]
    vcat = kv[:, KV_WIDTH:]
    kroll = pltpu.roll(kcat, HEAD_DIM, 1)
    vroll = pltpu.roll(vcat, HEAD_DIM, 1)
    lo_k = lax.broadcasted_iota(jnp.int32, (nk, LANES), 1) < HEAD_DIM
    lo_q = lax.broadcasted_iota(jnp.int32, (QB, LANES), 1) < HEAD_DIM
    if local:
        n = pl.program_id(1)
        last = pl.num_programs(1) - 1
        qi = lax.broadcasted_iota(jnp.int32, (QB, nk), 0)
        kj = lax.broadcasted_iota(jnp.int32, (QB, nk), 1)
        rel = kj - QB - qi
        first_key = jnp.where(n == 0, QB, 0)
        end_key = jnp.where(n == last, 2 * QB, 3 * QB)
        in_window = (rel >= -WINDOW) & (rel <= WINDOW) & (kj >= first_key) & (kj < end_key)
        valid = in_window | (kj >= 3 * QB)
    zero = jnp.zeros_like(kcat)
    for hk in range(N_KV_HEADS):
        k_own, k_other = (kcat, kroll) if hk == 0 else (kroll, kcat)
        k_lo = jnp.where(lo_k, k_own, zero).astype(BF16)
        k_hi = jnp.where(lo_k, zero, k_other).astype(BF16)
        v2 = (jnp.where(lo_k, vcat, vroll) if hk == 0 else jnp.where(lo_k, vroll, vcat)).astype(BF16)
        for slab in range(2 * hk, 2 * hk + 2):
            qs = q_ref[:, slab * LANES:(slab + 1) * LANES]
            outs = []
            for half, kmat in ((0, k_lo), (1, k_hi)):
                head = 2 * slab + half
                s = lax.dot_general(qs, kmat, (((1,), (1,)), ((), ())), preferred_element_type=F32)
                if local:
                    s = jnp.where(valid, s, NEG)
                sk = sink_ref[head]
                m = jnp.maximum(jnp.max(s, axis=1, keepdims=True), sk)
                p = jnp.exp(s - m)
                den = jnp.sum(p, axis=1, keepdims=True) + jnp.exp(sk - m)
                outs.append(_dot(p.astype(BF16), v2) / den)
            o_ref[:, slab * LANES:(slab + 1) * LANES] = jnp.where(lo_q, outs[0], outs[1]).astype(BF16)


def _attention_latent(q, kv, kv_ctx, sink, batch):
    nb = SEQ // QB
    kvw = 2 * KV_WIDTH
    return pl.pallas_call(
        functools.partial(_attn_body, local=True),
        grid=(batch, nb),
        in_specs=[pl.BlockSpec(memory_space=pltpu.SMEM),
                  pl.BlockSpec((QB, ATTN_WIDTH), lambda b, n: (b * nb + n, 0)),
                  pl.BlockSpec((QB, kvw), lambda b, n: (b * nb + jnp.maximum(n - 1, 0), 0)),
                  pl.BlockSpec((QB, kvw), lambda b, n: (b * nb + n, 0)),
                  pl.BlockSpec((QB, kvw), lambda b, n: (b * nb + jnp.minimum(n + 1, nb - 1), 0)),
                  pl.BlockSpec((CTX_LEN, kvw), lambda b, n: (b, 0))],
        out_specs=pl.BlockSpec((QB, ATTN_WIDTH), lambda b, n: (b * nb + n, 0)),
        out_shape=jax.ShapeDtypeStruct((batch * SEQ, ATTN_WIDTH), BF16),
        compiler_params=_cparams(("arbitrary", "arbitrary")),
        name="attn_latent",
    )(sink, q, kv, kv, kv, kv_ctx)


def _attention_ctx(q, kv, sink, batch):
    nb = CTX_LEN // QB
    return pl.pallas_call(
        functools.partial(_attn_body, local=False),
        grid=(batch, nb),
        in_specs=[pl.BlockSpec(memory_space=pltpu.SMEM),
                  pl.BlockSpec((QB, ATTN_WIDTH), lambda b, n: (b * nb + n, 0)),
                  pl.BlockSpec((CTX_LEN, 2 * KV_WIDTH), lambda b, n: (b, 0))],
        out_specs=pl.BlockSpec((QB, ATTN_WIDTH), lambda b, n: (b * nb + n, 0)),
        out_shape=jax.ShapeDtypeStruct((batch * CTX_LEN, ATTN_WIDTH), BF16),
        compiler_params=_cparams(("arbitrary", "arbitrary")),
        name="attn_ctx",
    )(sink, q, kv)


def _merge_body(x_ref, fm_ref, ao_ref, gt_ref, wf_ref, wa_ref, wo_ref, ga_ref, *refs, with_h2):
    gt = gt_ref[...].astype(F32)
    y = (gt[:, :D_MODEL] * _dot(fm_ref[...].astype(BF16), wf_ref[...])
         + gt[:, D_MODEL:] * _dot(ao_ref[...], wa_ref[...]))
    xn = x_ref[...] + ga_ref[0] * _dot(y.astype(BF16), wo_ref[...])
    if with_h2:
        sh_ref, sc_ref, g_ref, xo_ref, h_ref = refs
        h_ref[...] = _rms_mod(xn, g_ref[...], sh_ref[0], sc_ref[0]).astype(BF16)
    else:
        (xo_ref,) = refs
    xo_ref[...] = xn


def _merge(x, fm, ao, gates, wf, wa, wo, mods, layer, cond_fn, g2, with_h2):
    rows = x.shape[0]
    row_spec = lambda w: pl.BlockSpec((TM, w), lambda i: (i, 0))
    full = lambda a: pl.BlockSpec(a.shape, lambda i: (0, 0))
    in_specs = [row_spec(D_MODEL), row_spec(FOURIER_WIDTH), row_spec(ATTN_WIDTH), row_spec(2 * D_MODEL),
                full(wf), full(wa), full(wo), _mod_spec(layer, 2, cond_fn)]
    args = [x, fm, ao, gates, wf, wa, wo, mods]
    out_specs = [row_spec(D_MODEL)]
    out_shape = [jax.ShapeDtypeStruct((rows, D_MODEL), F32)]
    if with_h2:
        in_specs += [_mod_spec(layer, 3, cond_fn), _mod_spec(layer, 4, cond_fn),
                     pl.BlockSpec((1, D_MODEL), lambda i: (0, 0))]
        args += [mods, mods, g2]
        out_specs.append(row_spec(D_MODEL))
        out_shape.append(jax.ShapeDtypeStruct((rows, D_MODEL), BF16))
    return pl.pallas_call(
        functools.partial(_merge_body, with_h2=with_h2),
        grid=(rows // TM,),
        in_specs=in_specs, out_specs=out_specs, out_shape=out_shape,
        compiler_params=_cparams(("arbitrary",), VMEM_LIMIT),
        name="merge",
    )(*args)


def _ffn_body(x_ref, h_ref, wg_ref, wu_ref, wd_ref, ga_ref, o_ref):
    h = h_ref[...]
    g = _dot(h, wg_ref[...])
    u = _dot(h, wu_ref[...])
    a = (g * _sigmoid(g) * u).astype(BF16)
    o_ref[...] = x_ref[...] + ga_ref[0] * _dot(a, wd_ref[...])


def _ffn_dense(x, h2, wg, wu, wd, mods, layer, cond_fn):
    rows = x.shape[0]
    resident = lambda a: pl.BlockSpec(a.shape, lambda i: (0, 0), pipeline_mode=pl.Buffered(1))
    return pl.pallas_call(
        _ffn_body,
        grid=(rows // TM,),
        in_specs=[pl.BlockSpec((TM, D_MODEL), lambda i: (i, 0)),
                  pl.BlockSpec((TM, D_MODEL), lambda i: (i, 0)),
                  resident(wg), resident(wu), resident(wd),
                  _mod_spec(layer, 5, cond_fn)],
        out_specs=pl.BlockSpec((TM, D_MODEL), lambda i: (i, 0)),
        out_shape=jax.ShapeDtypeStruct((rows, D_MODEL), F32),
        compiler_params=_cparams(("arbitrary",), VMEM_LIMIT),
        name="ffn_dense",
    )(x, h2, wg, wu, wd, mods)


INFO_E0, INFO_E1, INFO_R0, INFO_R1, INFO_W0, INFO_W1 = range(6)


def _router_body(x_ref, sh_ref, sc_ref, g_ref, wr_ref, h3_ref, info_ref, cnt_ref, run_ref):
    @pl.when(pl.program_id(0) == 0)
    def _():
        run_ref[...] = jnp.zeros_like(run_ref)

    h = _rms_mod(x_ref[...], g_ref[...], sh_ref[0], sc_ref[0])
    for s in range(ROW_SLABS):
        h3_ref[:, s, :] = h[:, s * LANES:(s + 1) * LANES]
    logits = jnp.dot(h, wr_ref[...], precision=lax.Precision.HIGHEST, preferred_element_type=F32)
    lane = lax.broadcasted_iota(jnp.int32, logits.shape, 1)
    neg_inf = jnp.float32(-jnp.inf)
    lg = jnp.where(lane < N_EXPERTS, logits, neg_inf)
    v0 = jnp.max(lg, axis=1, keepdims=True)
    i0 = jnp.min(jnp.where(lg == v0, lane, LANES), axis=1, keepdims=True)
    oh0 = lane == i0
    lg1 = jnp.where(oh0, neg_inf, lg)
    v1 = jnp.max(lg1, axis=1, keepdims=True)
    i1 = jnp.min(jnp.where(lg1 == v1, lane, LANES), axis=1, keepdims=True)
    oh1 = lane == i1
    e = jnp.exp(v1 - v0)
    w0 = 1.0 / (1.0 + e)
    w1 = e / (1.0 + e)
    oh = jnp.where(oh0 | oh1, 1.0, 0.0)
    row = lax.broadcasted_iota(jnp.int32, (TM, TM), 0)
    col = lax.broadcasted_iota(jnp.int32, (TM, TM), 1)
    tri = jnp.where(row > col, 1.0, 0.0).astype(BF16)
    before = _dot(tri, oh.astype(BF16)) + run_ref[0:1, :]
    r0 = jnp.sum(jnp.where(oh0, before, 0.0), axis=1, keepdims=True)
    r1 = jnp.sum(jnp.where(oh1, before, 0.0), axis=1, keepdims=True)
    run = run_ref[0:1, :] + jnp.sum(oh, axis=0, keepdims=True)
    run_ref[0:1, :] = run
    cnt_ref[...] = jnp.broadcast_to(run, cnt_ref.shape)
    info = jnp.zeros(logits.shape, F32)
    for idx, val in ((INFO_E0, i0.astype(F32)), (INFO_E1, i1.astype(F32)), (INFO_R0, r0),
                     (INFO_R1, r1), (INFO_W0, w0), (INFO_W1, w1)):
        info = jnp.where(lane == idx, val, info)
    info_ref[...] = info


def _router(x, mods, layer, cond_fn, g2, wr_pad):
    rows = x.shape[0]
    return pl.pallas_call(
        _router_body,
        grid=(rows // TM,),
        in_specs=[pl.BlockSpec((TM, D_MODEL), lambda i: (i, 0)),
                  _mod_spec(layer, 3, cond_fn), _mod_spec(layer, 4, cond_fn),
                  pl.BlockSpec((1, D_MODEL), lambda i: (0, 0)),
                  pl.BlockSpec((D_MODEL, LANES), lambda i: (0, 0))],
        out_specs=[pl.BlockSpec((TM, ROW_SLABS, LANES), lambda i: (i, 0, 0)),
                   pl.BlockSpec((TM, LANES), lambda i: (i, 0)),
                   pl.BlockSpec((SUBLANES, LANES), lambda i: (0, 0))],
        out_shape=[jax.ShapeDtypeStruct((rows, ROW_SLABS, LANES), F32),
                   jax.ShapeDtypeStruct((rows, LANES), F32),
                   jax.ShapeDtypeStruct((SUBLANES, LANES), F32)],
        scratch_shapes=[pltpu.VMEM((SUBLANES, LANES), F32)],
        compiler_params=_cparams(("arbitrary",), VMEM_LIMIT),
        name="router",
    )(x, mods, mods, g2, wr_pad)


def _row_copy(src_ref, src_row, dst_ref, dst_row, sem):
    return pltpu.make_async_copy(src_ref.at[src_row], dst_ref.at[dst_row], sem)


def _dispatch_body(d0_ref, d1_ref, h3_ref, xs_in_ref, xs_ref, sem):
    del xs_in_ref
    base = pl.program_id(0) * TMD

    def issue(r, c):
        _row_copy(h3_ref, r, xs_ref, d0_ref[base + r], sem).start()
        _row_copy(h3_ref, r, xs_ref, d1_ref[base + r], sem).start()
        return c

    def drain(r, c):
        _row_copy(h3_ref, r, xs_ref, d0_ref[base + r], sem).wait()
        _row_copy(h3_ref, r, xs_ref, d1_ref[base + r], sem).wait()
        return c

    lax.fori_loop(0, TMD, issue, 0)
    lax.fori_loop(0, TMD, drain, 0)


def _dispatch(dest0, dest1, h3, xs_init):
    rows = h3.shape[0]
    return pl.pallas_call(
        _dispatch_body,
        grid_spec=pltpu.PrefetchScalarGridSpec(
            num_scalar_prefetch=2,
            grid=(rows // TMD,),
            in_specs=[pl.BlockSpec((TMD, ROW_SLABS, LANES), lambda i, *_: (i, 0, 0)),
                      pl.BlockSpec(memory_space=pl.ANY)],
            out_specs=pl.BlockSpec(memory_space=pl.ANY),
            scratch_shapes=[pltpu.SemaphoreType.DMA(())]),
        out_shape=jax.ShapeDtypeStruct(xs_init.shape, F32),
        input_output_aliases={3: 0},
        compiler_params=_cparams(("arbitrary",)),
        name="dispatch",
    )(dest0, dest1, h3, xs_init)


def _expert_body(te_ref, nt_ref, xs_ref, wg_ref, wu_ref, wd_ref, ys_ref, xb_ref, acc_ref):
    j = pl.program_id(0)
    f = pl.program_id(1)
    nf = pl.num_programs(1)
    active = j < nt_ref[0]

    @pl.when(f == 0)
    def _():
        for s in range(ROW_SLABS):
            xb_ref[:, s * LANES:(s + 1) * LANES] = xs_ref[:, s, :].astype(BF16)
        acc_ref[...] = jnp.zeros_like(acc_ref)

    @pl.when(active)
    def _():
        xb = xb_ref[...]
        g = _dot(xb, wg_ref[0])
        u = _dot(xb, wu_ref[0])
        a = (g * _sigmoid(g) * u).astype(BF16)
        acc_ref[...] += _dot(a, wd_ref[0])

    @pl.when(f == nf - 1)
    def _():
        for s in range(ROW_SLABS):
            ys_ref[:, s, :] = acc_ref[:, s * LANES:(s + 1) * LANES]


def _experts(tile_expert, n_tiles, xs, wg, wu, wd):
    m_rows = xs.shape[0]
    nf = wg.shape[2] // FC_E

    def f_idx(j, f, nt):
        return jnp.where(j < nt[0], f, nf - 1)

    return pl.pallas_call(
        _expert_body,
        grid_spec=pltpu.PrefetchScalarGridSpec(
            num_scalar_prefetch=2,
            grid=(m_rows // TME, nf),
            in_specs=[pl.BlockSpec((TME, ROW_SLABS, LANES), lambda j, f, te, nt: (j, 0, 0)),
                      pl.BlockSpec((1, D_MODEL, FC_E), lambda j, f, te, nt: (te[j], 0, f_idx(j, f, nt))),
                      pl.BlockSpec((1, D_MODEL, FC_E), lambda j, f, te, nt: (te[j], 0, f_idx(j, f, nt))),
                      pl.BlockSpec((1, FC_E, D_MODEL), lambda j, f, te, nt: (te[j], f_idx(j, f, nt), 0))],
            out_specs=pl.BlockSpec((TME, ROW_SLABS, LANES), lambda j, f, te, nt: (j, 0, 0)),
            scratch_shapes=[pltpu.VMEM((TME, D_MODEL), BF16), pltpu.VMEM((TME, D_MODEL), F32)]),
        out_shape=jax.ShapeDtypeStruct(xs.shape, F32),
        compiler_params=_cparams(("arbitrary", "arbitrary"), VMEM_LIMIT),
        name="experts",
    )(tile_expert, n_tiles, xs, wg, wu, wd)


def _combine_body(d0_ref, d1_ref, ys_ref, x_ref, info_ref, ga_ref, g_ref, o_ref, b0_ref, b1_ref, sem):
    base = pl.program_id(0) * TMD

    def issue(r, c):
        _row_copy(ys_ref, d0_ref[base + r], b0_ref, r, sem).start()
        _row_copy(ys_ref, d1_ref[base + r], b1_ref, r, sem).start()
        return c

    def drain(r, c):
        _row_copy(ys_ref, d0_ref[base + r], b0_ref, r, sem).wait()
        _row_copy(ys_ref, d1_ref[base + r], b1_ref, r, sem).wait()
        return c

    lax.fori_loop(0, TMD, issue, 0)
    lax.fori_loop(0, TMD, drain, 0)
    info = info_ref[...]
    w0 = info[:, INFO_W0:INFO_W0 + 1]
    w1 = info[:, INFO_W1:INFO_W1 + 1]
    y = jnp.concatenate([w0 * b0_ref[:, s, :] + w1 * b1_ref[:, s, :] for s in range(ROW_SLABS)], axis=1)
    xn = x_ref[...] + ga_ref[0] * y
    r = lax.rsqrt(jnp.mean(xn * xn, axis=-1, keepdims=True) + EPS)
    o_ref[...] = (xn * r) * g_ref[...]


def _combine(dest0, dest1, ys, x, info, mods, layer, cond_fn, final_g):
    rows = x.shape[0]
    return pl.pallas_call(
        _combine_body,
        grid_spec=pltpu.PrefetchScalarGridSpec(
            num_scalar_prefetch=2,
            grid=(rows // TMD,),
            in_specs=[pl.BlockSpec(memory_space=pl.ANY),
                      pl.BlockSpec((TMD, D_MODEL), lambda i, *_: (i, 0)),
                      pl.BlockSpec((TMD, LANES), lambda i, *_: (i, 0)),
                      _mod_spec(layer, 5, cond_fn),
                      pl.BlockSpec((1, D_MODEL), lambda i, *_: (0, 0))],
            out_specs=pl.BlockSpec((TMD, D_MODEL), lambda i, *_: (i, 0)),
            scratch_shapes=[pltpu.VMEM((TMD, ROW_SLABS, LANES), F32),
                            pltpu.VMEM((TMD, ROW_SLABS, LANES), F32),
                            pltpu.SemaphoreType.DMA(())]),
        out_shape=jax.ShapeDtypeStruct((rows, D_MODEL), F32),
        compiler_params=_cparams(("arbitrary",), VMEM_LIMIT),
        name="combine",
    )(dest0, dest1, ys, x, info, mods, final_g)


def _moe(x, mods, layer, cond_fn, g2, w_router, wg, wu, wd, final_g):
    rows = x.shape[0]
    wr_pad = jnp.pad(w_router, ((0, 0), (0, LANES - N_EXPERTS)))
    h3, info, cnt = _router(x, mods, layer, cond_fn, g2, wr_pad)
    e0 = info[:, INFO_E0].astype(jnp.int32)
    e1 = info[:, INFO_E1].astype(jnp.int32)
    r0 = info[:, INFO_R0].astype(jnp.int32)
    r1 = info[:, INFO_R1].astype(jnp.int32)
    counts = cnt[0, :N_EXPERTS].astype(jnp.int32)
    tiles = (counts + TME - 1) // TME
    tile_start = jnp.cumsum(tiles) - tiles
    row_start = tile_start * TME
    dest0 = row_start[e0] + r0
    dest1 = row_start[e1] + r1
    max_tiles = 2 * rows // TME + N_EXPERTS
    n_tiles = jnp.sum(tiles).astype(jnp.int32).reshape(1)
    tid = jnp.minimum(jnp.arange(max_tiles, dtype=jnp.int32), n_tiles[0] - 1)
    tile_expert = (jnp.sum(tid[:, None] >= tile_start[None, :], axis=1) - 1).astype(jnp.int32)
    xs = _dispatch(dest0, dest1, h3, jnp.zeros((max_tiles * TME, ROW_SLABS, LANES), F32))
    ys = _experts(tile_expert, n_tiles, xs, wg, wu, wd)
    return _combine(dest0, dest1, ys, x, info, mods, layer, cond_fn, final_g)


def kernel(x, c, ctx, c_ctx, w_mod, b_mod, norm1_g, norm2_g, w_in, sink, w_fourier, w_attn, w_out,
           w_gate_d, w_up_d, w_down_d, w_router, w_gate_e, w_up_e, w_down_e, final_g):
    batch, seq, d = x.shape
    assert (seq, d) == (SEQ, D_MODEL) and ctx.shape == (batch, CTX_LEN, D_MODEL)
    tb = {k: jnp.asarray(v) for k, v in _tables().items()}
    tiles_per_batch = SEQ // TM
    lat_cond = lambda i: i // tiles_per_batch
    ctx_cond = lambda i: 2
    lat_rope = lambda i: i % tiles_per_batch
    ctx_rope = lambda i: 0

    cv = jnp.zeros((SUBLANES, D_MODEL), F32).at[:batch].set(c).at[batch].set(c_ctx)
    mods = _modulation(cv, w_mod, b_mod)

    xl = x.reshape(batch * SEQ, D_MODEL)
    xc = ctx.reshape(batch * CTX_LEN, D_MODEL)
    out = None
    for l in range(DEPTH):
        last = l == DEPTH - 1
        g1 = norm1_g[l].reshape(1, D_MODEL)
        g2 = norm2_g[l].reshape(1, D_MODEL)
        w_in_b = w_in[l].astype(BF16)
        wf = w_fourier[l].astype(BF16)
        wa = w_attn[l].astype(BF16)
        wo = w_out[l].astype(BF16)
        u, q, kv, gates = _in_proj(xl, mods, l, lat_cond, g1, w_in_b, tb["cos"], tb["sin"], lat_rope)
        uc, qc, kvc, gates_c = _in_proj(xc, mods, l, ctx_cond, g1, w_in_b, tb["cos1"], tb["sin0"], ctx_rope)
        fm = _fourier_latent(u, tb, batch)
        ao = _attention_latent(q, kv, kvc, sink[l], batch)
        if l % 2 == 0:
            i = l // 2
            xl, h2 = _merge(xl, fm, ao, gates, wf, wa, wo, mods, l, lat_cond, g2, True)
            wg, wu, wd = w_gate_d[i].astype(BF16), w_up_d[i].astype(BF16), w_down_d[i].astype(BF16)
            xl = _ffn_dense(xl, h2, wg, wu, wd, mods, l, lat_cond)
            if not last:
                fmc = _fourier_ctx(uc, tb, batch)
                aoc = _attention_ctx(qc, kvc, sink[l], batch)
                xc, h2c = _merge(xc, fmc, aoc, gates_c, wf, wa, wo, mods, l, ctx_cond, g2, True)
                xc = _ffn_dense(xc, h2c, wg, wu, wd, mods, l, ctx_cond)
            if last:
                raise NotImplementedError("final norm is fused into the routed-expert combine")
        else:
            i = l // 2
            (xl,) = _merge(xl, fm, ao, gates, wf, wa, wo, mods, l, lat_cond, g2, False)
            if not last:
                raise NotImplementedError("context update through a routed-expert layer")
            out = _moe(xl, mods, l, lat_cond, g2, w_router[i], w_gate_e[i].astype(BF16),
                       w_up_e[i].astype(BF16), w_down_e[i].astype(BF16), final_g.reshape(1, D_MODEL))
    return out.reshape(batch, SEQ, D_MODEL)
```

```python
import functools
import math

import numpy as np
import jax
import jax.numpy as jnp
from jax import lax
from jax.experimental import pallas as pl
from jax.experimental.pallas import tpu as pltpu

F32 = jnp.float32
BF16 = jnp.bfloat16

D_MODEL = 1024
SEQ = 8192
DEPTH = 2
GRID_W = 64
CTX_LEN = 256
N_GROUPS = 4
GROUP_W = 128
FOURIER_WIDTH = N_GROUPS * GROUP_W
N_Q_HEADS = 8
N_KV_HEADS = 2
HEAD_DIM = 64
ATTN_WIDTH = N_Q_HEADS * HEAD_DIM
KV_WIDTH = N_KV_HEADS * HEAD_DIM
WINDOW = 128
ROPE_BASE = 10000.0
OFF_Q = FOURIER_WIDTH
OFF_K = OFF_Q + ATTN_WIDTH
OFF_V = OFF_K + KV_WIDTH
OFF_G = OFF_V + KV_WIDTH
IN_WIDTH = OFF_G + 2 * D_MODEL
N_EXPERTS = 8
N_MOD = 6
EPS = 1e-6
NEG = -1e30
LOG2E = 1.0 / math.log(2.0)

LANES = 128
SUBLANES = 8

TM = 256
QB = 128
FFT_R = SEQ // GRID_W
FFT_C = GRID_W
TME = 512
TD = 512
SB = 2 * TD + N_EXPERTS * SUBLANES
XS_W = D_MODEL + LANES
SEG_BITS = TD.bit_length() - 3
TAIL_BITS = TME.bit_length() - 4
FC_E = 1792
VMEM_LIMIT = 56 * 1024 * 1024


def _cparams(sem, vmem=None):
    return pltpu.CompilerParams(dimension_semantics=sem, vmem_limit_bytes=vmem)


def _dot(a, b):
    return jnp.dot(a, b, preferred_element_type=F32)


def _sigmoid(x):
    return 1.0 / (1.0 + jnp.exp(-x))


def _rms_mod(x, g, sh, sc):
    r = lax.rsqrt(jnp.mean(x * x, axis=-1, keepdims=True) + EPS)
    return (x * r) * g * (1.0 + sc) + sh


def _dft_cs(n):
    k = np.arange(n, dtype=np.float64)
    a = 2.0 * np.pi * np.outer(k, k) / n
    return np.cos(a), np.sin(a)


@functools.lru_cache(maxsize=None)
def _tables():
    c128, s128 = _dft_cs(FFT_R)
    c64, s64 = _dft_cs(FFT_C)
    c256, s256 = _dft_cs(CTX_LEN)
    wa = np.concatenate([c128, -s128], axis=0)
    k1 = np.arange(FFT_R, dtype=np.float64)[:, None]
    cc = np.arange(FFT_C, dtype=np.float64)[None, :]
    ang = 2.0 * np.pi * k1 * cc / SEQ
    twc = np.broadcast_to(np.cos(ang)[:, :, None], (FFT_R, FFT_C, LANES))
    tws = np.broadcast_to(np.sin(ang)[:, :, None], (FFT_R, FFT_C, LANES))
    mc = np.block([[c64, s64], [-s64, c64]])
    cg, sg = _dft_cs(GROUP_W)
    mch = np.concatenate([cg, sg], axis=0) / math.sqrt(SEQ * GROUP_W)
    wcx = np.concatenate([c256, s256], axis=0)
    mchc = np.concatenate([cg, -sg], axis=0)
    n_freq = HEAD_DIM // 4
    inv = ROPE_BASE ** (-np.arange(n_freq, dtype=np.float64) / n_freq)
    t = np.arange(SEQ)
    rows = (t // GRID_W).astype(np.float64)[:, None] * inv
    cols = (t % GRID_W).astype(np.float64)[:, None] * inv
    cos_h = np.concatenate([np.cos(rows), np.cos(rows), np.cos(cols), np.cos(cols)], axis=1)
    sin_h = np.concatenate([-np.sin(rows), np.sin(rows), -np.sin(cols), np.sin(cols)], axis=1)
    cos_t = np.concatenate([cos_h, cos_h], axis=1)
    sin_t = np.concatenate([sin_h, sin_h], axis=1)
    f = lambda a: np.ascontiguousarray(a, dtype=np.float32)
    return dict(wa=f(wa), twc=f(twc), tws=f(tws), mc=f(mc), mch=f(mch), wcx=f(wcx), mchc=f(mchc),
                cos=f(cos_t), sin=f(sin_t),
                cos1=np.ones((TM, LANES), np.float32), sin0=np.zeros((TM, LANES), np.float32))


def _mod_body(cv_ref, w_ref, b_ref, o_ref):
    c = cv_ref[...]
    s = c * _sigmoid(c)
    o_ref[0, 0] = jnp.dot(s, w_ref[0], precision=lax.Precision.HIGHEST,
                          preferred_element_type=F32) + b_ref[0]


def _modulation(cv, w_mod, b_mod):
    out = pl.pallas_call(
        _mod_body,
        grid=(DEPTH, N_MOD),
        in_specs=[pl.BlockSpec((SUBLANES, D_MODEL), lambda l, k: (0, 0)),
                  pl.BlockSpec((1, D_MODEL, D_MODEL), lambda l, k: (l, 0, k)),
                  pl.BlockSpec((1, 1, D_MODEL), lambda l, k: (l * N_MOD + k, 0, 0))],
        out_specs=pl.BlockSpec((1, 1, SUBLANES, D_MODEL), lambda l, k: (l, k, 0, 0)),
        out_shape=jax.ShapeDtypeStruct((DEPTH, N_MOD, SUBLANES, D_MODEL), F32),
        compiler_params=_cparams(("arbitrary", "arbitrary")),
        name="modulation",
    )(cv, w_mod, b_mod.reshape(DEPTH * N_MOD, 1, D_MODEL))
    out = jnp.transpose(out[:, :, :3, :], (0, 2, 1, 3))
    return out.reshape(DEPTH * 3 * N_MOD, 1, D_MODEL)


def _mod_spec(layer, k, cond_fn):
    return pl.BlockSpec((1, 1, D_MODEL),
                        lambda i, *_: ((layer * 3 + cond_fn(i)) * N_MOD + k, 0, 0))


def _in_proj_body(x_ref, sh_ref, sc_ref, g_ref, w_ref, cos_ref, sin_ref,
                  u_ref, q_ref, kv_ref, gt_ref):
    hb = _rms_mod(x_ref[...], g_ref[...], sh_ref[0], sc_ref[0]).astype(BF16)
    u_ref[...] = _dot(hb, w_ref[:, 0:OFF_Q])
    cos = cos_ref[...]
    sin = sin_ref[...]
    lane = lax.broadcasted_iota(jnp.int32, cos.shape, 1)
    first_half = (lane % (HEAD_DIM // 2)) < (HEAD_DIM // 4)
    quarter = HEAD_DIM // 4

    def rope(xs):
        below = pltpu.roll(xs, quarter, 1)
        above = pltpu.roll(xs, LANES - quarter, 1)
        return xs * cos + jnp.where(first_half, above, below) * sin

    qk = _dot(hb, w_ref[:, OFF_Q:OFF_V])
    scale = HEAD_DIM ** -0.5 * LOG2E
    for j in range(ATTN_WIDTH // LANES):
        sl = slice(j * LANES, (j + 1) * LANES)
        q_ref[:, sl] = (rope(qk[:, sl]) * scale).astype(BF16)
    kv_ref[:, 0:KV_WIDTH] = rope(qk[:, ATTN_WIDTH:ATTN_WIDTH + KV_WIDTH]).astype(BF16)
    kv_ref[:, KV_WIDTH:2 * KV_WIDTH] = _dot(hb, w_ref[:, OFF_V:OFF_G]).astype(BF16)
    gc = 512
    for j in range(2 * D_MODEL // gc):
        z = _dot(hb, w_ref[:, OFF_G + j * gc:OFF_G + (j + 1) * gc])
        gt_ref[:, j * gc:(j + 1) * gc] = _sigmoid(z).astype(BF16)


def _in_proj(x, mods, layer, cond_fn, g, w_bf, cos, sin, rope_idx):
    rows = x.shape[0]
    return pl.pallas_call(
        _in_proj_body,
        grid=(rows // TM,),
        in_specs=[pl.BlockSpec((TM, D_MODEL), lambda i: (i, 0)),
                  _mod_spec(layer, 0, cond_fn),
                  _mod_spec(layer, 1, cond_fn),
                  pl.BlockSpec((1, D_MODEL), lambda i: (0, 0)),
                  pl.BlockSpec((D_MODEL, IN_WIDTH), lambda i: (0, 0)),
                  pl.BlockSpec((TM, LANES), lambda i: (rope_idx(i), 0)),
                  pl.BlockSpec((TM, LANES), lambda i: (rope_idx(i), 0))],
        out_specs=[pl.BlockSpec((TM, FOURIER_WIDTH), lambda i: (i, 0)),
                   pl.BlockSpec((TM, ATTN_WIDTH), lambda i: (i, 0)),
                   pl.BlockSpec((TM, 2 * KV_WIDTH), lambda i: (i, 0)),
                   pl.BlockSpec((TM, 2 * D_MODEL), lambda i: (i, 0))],
        out_shape=[jax.ShapeDtypeStruct((rows, FOURIER_WIDTH), F32),
                   jax.ShapeDtypeStruct((rows, ATTN_WIDTH), BF16),
                   jax.ShapeDtypeStruct((rows, 2 * KV_WIDTH), BF16),
                   jax.ShapeDtypeStruct((rows, 2 * D_MODEL), BF16)],
        compiler_params=_cparams(("arbitrary",), VMEM_LIMIT),
        name="in_proj",
    )(x, mods, mods, g, w_bf, cos, sin)


FFT_CB = 8


def _fft_a_body(x_ref, w_ref, o_ref):
    w = w_ref[...]
    for c in range(FFT_CB):
        o_ref[0, :, c, :] = _dot(w, x_ref[:, c, :].astype(BF16))


def _fft_c_body(re_ref, im_ref, tc_ref, ts_ref, mc_ref, mch_ref, o_ref):
    mc = mc_ref[...]
    mch = mch_ref[...]
    for j in range(FFT_CB):
        ar = re_ref[0, j]
        ai = im_ref[0, j]
        tc = jnp.concatenate([tc_ref[j]] * N_GROUPS, axis=1)
        ts = jnp.concatenate([ts_ref[j]] * N_GROUPS, axis=1)
        br = ar * tc + ai * ts
        bi = ai * tc - ar * ts
        x = _dot(mc, jnp.concatenate([br, bi], axis=0).astype(BF16))
        xr = x[:FFT_C]
        xi = x[FFT_C:]
        ys = []
        for g in range(N_GROUPS):
            sl = slice(g * GROUP_W, (g + 1) * GROUP_W)
            lhs = jnp.concatenate([xr[:, sl], xi[:, sl]], axis=1).astype(BF16)
            ys.append(_dot(lhs, mch))
        o_ref[0, :, j, :] = jnp.concatenate(ys, axis=1)


def _fourier_latent(u, tb, batch):
    u3 = u.reshape(batch * FFT_R, FFT_C, FOURIER_WIDTH)
    a = pl.pallas_call(
        _fft_a_body,
        grid=(batch, FFT_C // FFT_CB),
        in_specs=[pl.BlockSpec((FFT_R, FFT_CB, FOURIER_WIDTH), lambda b, j: (b, j, 0)),
                  pl.BlockSpec((2 * FFT_R, FFT_R), lambda b, j: (0, 0))],
        out_specs=pl.BlockSpec((1, 2 * FFT_R, FFT_CB, FOURIER_WIDTH), lambda b, j: (b, 0, j, 0)),
        out_shape=jax.ShapeDtypeStruct((batch, 2 * FFT_R, FFT_C, FOURIER_WIDTH), F32),
        compiler_params=_cparams(("arbitrary", "arbitrary"), VMEM_LIMIT),
        name="fft_rows",
    )(u3, tb["wa"].astype(BF16))
    nk = FFT_R // FFT_CB
    y = pl.pallas_call(
        _fft_c_body,
        grid=(batch, nk),
        in_specs=[pl.BlockSpec((1, FFT_CB, FFT_C, FOURIER_WIDTH), lambda b, k: (b, k, 0, 0)),
                  pl.BlockSpec((1, FFT_CB, FFT_C, FOURIER_WIDTH), lambda b, k: (b, nk + k, 0, 0)),
                  pl.BlockSpec((FFT_CB, FFT_C, LANES), lambda b, k: (k, 0, 0)),
                  pl.BlockSpec((FFT_CB, FFT_C, LANES), lambda b, k: (k, 0, 0)),
                  pl.BlockSpec((2 * FFT_C, 2 * FFT_C), lambda b, k: (0, 0)),
                  pl.BlockSpec((2 * GROUP_W, GROUP_W), lambda b, k: (0, 0))],
        out_specs=pl.BlockSpec((1, FFT_C, FFT_CB, FOURIER_WIDTH), lambda b, k: (b, 0, k, 0)),
        out_shape=jax.ShapeDtypeStruct((batch, FFT_C, FFT_R, FOURIER_WIDTH), F32),
        compiler_params=_cparams(("arbitrary", "arbitrary"), VMEM_LIMIT),
        name="fft_cols",
    )(a, a, tb["twc"], tb["tws"], tb["mc"].astype(BF16), tb["mch"].astype(BF16))
    return y.reshape(batch * SEQ, FOURIER_WIDTH)


def _fourier_ctx_body(u_ref, w_ref, m_ref, o_ref):
    pq = _dot(w_ref[...], u_ref[...].astype(BF16))
    p = pq[:CTX_LEN]
    q = pq[CTX_LEN:]
    m = m_ref[...]
    scale = 1.0 / math.sqrt(CTX_LEN * GROUP_W)
    for g in range(N_GROUPS):
        sl = slice(g * GROUP_W, (g + 1) * GROUP_W)
        lhs = jnp.concatenate([p[:, sl], q[:, sl]], axis=1).astype(BF16)
        o_ref[:, sl] = _dot(lhs, m) * scale


def _fourier_ctx(u, tb, batch):
    return pl.pallas_call(
        _fourier_ctx_body,
        grid=(batch,),
        in_specs=[pl.BlockSpec((CTX_LEN, FOURIER_WIDTH), lambda b: (b, 0)),
                  pl.BlockSpec((2 * CTX_LEN, CTX_LEN), lambda b: (0, 0)),
                  pl.BlockSpec((2 * GROUP_W, GROUP_W), lambda b: (0, 0))],
        out_specs=pl.BlockSpec((CTX_LEN, FOURIER_WIDTH), lambda b: (b, 0)),
        out_shape=jax.ShapeDtypeStruct((batch * CTX_LEN, FOURIER_WIDTH), F32),
        compiler_params=_cparams(("arbitrary",)),
        name="fft_ctx",
    )(u, tb["wcx"].astype(BF16), tb["mchc"].astype(BF16))


def _attn_body(sink_ref, q_ref, *refs, local):
    o_ref = refs[-1]
    if local:
        kp_ref, kc_ref, kn_ref, kx_ref = refs[:-1]
        kv = jnp.concatenate([kp_ref[...], kc_ref[...], kn_ref[...], kx_ref[...]], axis=0)
    else:
        kv = refs[0][...]
    nk = kv.shape[0]
    kv = kv.astype(F32)
    kcat = kv[:, :KV_WIDTH]
    vcat = kv[:, KV_WIDTH:]
    kroll = pltpu.roll(kcat, HEAD_DIM, 1)
    vroll = pltpu.roll(vcat, HEAD_DIM, 1)
    lo_k = lax.broadcasted_iota(jnp.int32, (nk, LANES), 1) < HEAD_DIM
    lo_q = lax.broadcasted_iota(jnp.int32, (QB, LANES), 1) < HEAD_DIM
    if local:
        n = pl.program_id(1)
        last = pl.num_programs(1) - 1
        qi = lax.broadcasted_iota(jnp.int32, (QB, QB), 0)
        kj = lax.broadcasted_iota(jnp.int32, (QB, QB), 1)
        prev_ok = kj >= qi + jnp.where(n == 0, 2 * QB, 0)
        next_ok = kj <= qi - jnp.where(n == last, 2 * QB, 0)
    zero = jnp.zeros_like(kcat)
    one = jnp.ones_like(vcat)
    kmats, vmats = [], []
    for hk in range(N_KV_HEADS):
        k_own, k_other = (kcat, kroll) if hk == 0 else (kroll, kcat)
        v_own, v_other = (vcat, vroll) if hk == 0 else (vroll, vcat)
        kmats.append((jnp.where(lo_k, k_own, zero).astype(BF16), jnp.where(lo_k, zero, k_other).astype(BF16)))
        vmats.append((jnp.where(lo_k, v_own, one).astype(BF16), jnp.where(lo_k, one, v_other).astype(BF16)))
    heads = [(slab, half) for slab in range(ATTN_WIDTH // LANES) for half in range(2)]
    scores = []
    for slab, half in heads:
        qs = q_ref[:, slab * LANES:(slab + 1) * LANES]
        s = lax.dot_general(qs, kmats[slab // 2][half], (((1,), (1,)), ((), ())), preferred_element_type=F32)
        if local:
            s = jnp.concatenate([jnp.where(prev_ok, s[:, :QB], NEG), s[:, QB:2 * QB],
                                 jnp.where(next_ok, s[:, 2 * QB:3 * QB], NEG), s[:, 3 * QB:]], axis=1)
        scores.append(s)
    sinks = [sink_ref[2 * slab + half] * LOG2E for slab, half in heads]
    maxes = [jnp.maximum(jnp.max(s, axis=1, keepdims=True), sk) for s, sk in zip(scores, sinks)]
    probs = [jnp.exp2(s - m).astype(BF16) for s, m in zip(scores, maxes)]
    sink_p = [jnp.exp2(sk - m) for sk, m in zip(sinks, maxes)]
    pvs = [_dot(p, vmats[slab // 2][half]) for p, (slab, half) in zip(probs, heads)]
    for slab in range(ATTN_WIDTH // LANES):
        lo, hi = pvs[2 * slab], pvs[2 * slab + 1]
        num = jnp.where(lo_q, lo, hi)
        den = (pltpu.roll(jnp.where(lo_q, hi, lo), HEAD_DIM, 1)
               + jnp.where(lo_q, sink_p[2 * slab], sink_p[2 * slab + 1]))
        o_ref[:, slab * LANES:(slab + 1) * LANES] = (num / den).astype(BF16)


def _attention_latent(q, kv, kv_ctx, sink, batch):
    nb = SEQ // QB
    kvw = 2 * KV_WIDTH
    return pl.pallas_call(
        functools.partial(_attn_body, local=True),
        grid=(batch, nb),
        in_specs=[pl.BlockSpec(memory_space=pltpu.SMEM),
                  pl.BlockSpec((QB, ATTN_WIDTH), lambda b, n: (b * nb + n, 0)),
                  pl.BlockSpec((QB, kvw), lambda b, n: (b * nb + jnp.maximum(n - 1, 0), 0)),
                  pl.BlockSpec((QB, kvw), lambda b, n: (b * nb + n, 0)),
                  pl.BlockSpec((QB, kvw), lambda b, n: (b * nb + jnp.minimum(n + 1, nb - 1), 0)),
                  pl.BlockSpec((CTX_LEN, kvw), lambda b, n: (b, 0))],
        out_specs=pl.BlockSpec((QB, ATTN_WIDTH), lambda b, n: (b * nb + n, 0)),
        out_shape=jax.ShapeDtypeStruct((batch * SEQ, ATTN_WIDTH), BF16),
        compiler_params=_cparams(("arbitrary", "arbitrary")),
        name="attn_latent",
    )(sink, q, kv, kv, kv, kv_ctx)


def _attention_ctx(q, kv, sink, batch):
    nb = CTX_LEN // QB
    return pl.pallas_call(
        functools.partial(_attn_body, local=False),
        grid=(batch, nb),
        in_specs=[pl.BlockSpec(memory_space=pltpu.SMEM),
                  pl.BlockSpec((QB, ATTN_WIDTH), lambda b, n: (b * nb + n, 0)),
                  pl.BlockSpec((CTX_LEN, 2 * KV_WIDTH), lambda b, n: (b, 0))],
        out_specs=pl.BlockSpec((QB, ATTN_WIDTH), lambda b, n: (b * nb + n, 0)),
        out_shape=jax.ShapeDtypeStruct((batch * CTX_LEN, ATTN_WIDTH), BF16),
        compiler_params=_cparams(("arbitrary", "arbitrary")),
        name="attn_ctx",
    )(sink, q, kv)


def _merge_body(x_ref, fm_ref, ao_ref, gt_ref, wf_ref, wa_ref, wo_ref, ga_ref, *refs, with_h2):
    gt = gt_ref[...].astype(F32)
    y = (gt[:, :D_MODEL] * _dot(fm_ref[...].astype(BF16), wf_ref[...])
         + gt[:, D_MODEL:] * _dot(ao_ref[...], wa_ref[...]))
    xn = x_ref[...] + ga_ref[0] * _dot(y.astype(BF16), wo_ref[...])
    if with_h2:
        sh_ref, sc_ref, g_ref, xo_ref, h_ref = refs
        h_ref[...] = _rms_mod(xn, g_ref[...], sh_ref[0], sc_ref[0]).astype(BF16)
    else:
        (xo_ref,) = refs
    xo_ref[...] = xn


def _merge(x, fm, ao, gates, wf, wa, wo, mods, layer, cond_fn, g2, with_h2):
    rows = x.shape[0]
    row_spec = lambda w: pl.BlockSpec((TM, w), lambda i: (i, 0))
    full = lambda a: pl.BlockSpec(a.shape, lambda i: (0, 0))
    in_specs = [row_spec(D_MODEL), row_spec(FOURIER_WIDTH), row_spec(ATTN_WIDTH), row_spec(2 * D_MODEL),
                full(wf), full(wa), full(wo), _mod_spec(layer, 2, cond_fn)]
    args = [x, fm, ao, gates, wf, wa, wo, mods]
    out_specs = [row_spec(D_MODEL)]
    out_shape = [jax.ShapeDtypeStruct((rows, D_MODEL), F32)]
    if with_h2:
        in_specs += [_mod_spec(layer, 3, cond_fn), _mod_spec(layer, 4, cond_fn),
                     pl.BlockSpec((1, D_MODEL), lambda i: (0, 0))]
        args += [mods, mods, g2]
        out_specs.append(row_spec(D_MODEL))
        out_shape.append(jax.ShapeDtypeStruct((rows, D_MODEL), BF16))
    return pl.pallas_call(
        functools.partial(_merge_body, with_h2=with_h2),
        grid=(rows // TM,),
        in_specs=in_specs, out_specs=out_specs, out_shape=out_shape,
        compiler_params=_cparams(("arbitrary",), VMEM_LIMIT),
        name="merge",
    )(*args)


def _ffn_body(x_ref, h_ref, wg_ref, wu_ref, wd_ref, ga_ref, o_ref):
    h = h_ref[...]
    g = _dot(h, wg_ref[...])
    u = _dot(h, wu_ref[...])
    a = (g * _sigmoid(g) * u).astype(BF16)
    o_ref[...] = x_ref[...] + ga_ref[0] * _dot(a, wd_ref[...])


def _ffn_dense(x, h2, wg, wu, wd, mods, layer, cond_fn):
    rows = x.shape[0]
    resident = lambda a: pl.BlockSpec(a.shape, lambda i: (0, 0), pipeline_mode=pl.Buffered(1))
    return pl.pallas_call(
        _ffn_body,
        grid=(rows // TM,),
        in_specs=[pl.BlockSpec((TM, D_MODEL), lambda i: (i, 0)),
                  pl.BlockSpec((TM, D_MODEL), lambda i: (i, 0)),
                  resident(wg), resident(wu), resident(wd),
                  _mod_spec(layer, 5, cond_fn)],
        out_specs=pl.BlockSpec((TM, D_MODEL), lambda i: (i, 0)),
        out_shape=jax.ShapeDtypeStruct((rows, D_MODEL), F32),
        compiler_params=_cparams(("arbitrary",), VMEM_LIMIT),
        name="ffn_dense",
    )(x, h2, wg, wu, wd, mods)


INFO_E0, INFO_E1, INFO_R0, INFO_R1, INFO_W0, INFO_W1 = range(6)


def _router_body(x_ref, sh_ref, sc_ref, g_ref, wr_ref, hb_ref, info_ref, cnt_ref):
    h = _rms_mod(x_ref[...], g_ref[...], sh_ref[0], sc_ref[0])
    hb_ref[...] = h.astype(BF16)
    logits = jnp.dot(h, wr_ref[...], precision=lax.Precision.HIGHEST, preferred_element_type=F32)
    lane = lax.broadcasted_iota(jnp.int32, logits.shape, 1)
    neg_inf = jnp.float32(-jnp.inf)
    lg = jnp.where(lane < N_EXPERTS, logits, neg_inf)
    v0 = jnp.max(lg, axis=1, keepdims=True)
    i0 = jnp.min(jnp.where(lg == v0, lane, LANES), axis=1, keepdims=True)
    oh0 = lane == i0
    lg1 = jnp.where(oh0, neg_inf, lg)
    v1 = jnp.max(lg1, axis=1, keepdims=True)
    i1 = jnp.min(jnp.where(lg1 == v1, lane, LANES), axis=1, keepdims=True)
    oh1 = lane == i1
    e = jnp.exp(v1 - v0)
    w0 = 1.0 / (1.0 + e)
    w1 = e / (1.0 + e)
    oh = jnp.where(oh0 | oh1, 1.0, 0.0)
    row = lax.broadcasted_iota(jnp.int32, (TD, TD), 0)
    col = lax.broadcasted_iota(jnp.int32, (TD, TD), 1)
    tri = jnp.where(row > col, 1.0, 0.0).astype(BF16)
    before = _dot(tri, oh.astype(BF16))
    r0 = jnp.sum(jnp.where(oh0, before, 0.0), axis=1, keepdims=True)
    r1 = jnp.sum(jnp.where(oh1, before, 0.0), axis=1, keepdims=True)
    cnt_ref[0] = jnp.broadcast_to(jnp.sum(oh, axis=0, keepdims=True), (SUBLANES, LANES))
    info = jnp.zeros(logits.shape, F32)
    for idx, val in ((INFO_E0, i0.astype(F32)), (INFO_E1, i1.astype(F32)), (INFO_R0, r0),
                     (INFO_R1, r1), (INFO_W0, w0), (INFO_W1, w1)):
        info = jnp.where(lane == idx, val, info)
    info_ref[...] = info


def _router(x, mods, layer, cond_fn, g2, wr_pad):
    rows = x.shape[0]
    return pl.pallas_call(
        _router_body,
        grid=(rows // TD,),
        in_specs=[pl.BlockSpec((TD, D_MODEL), lambda i: (i, 0)),
                  _mod_spec(layer, 3, cond_fn), _mod_spec(layer, 4, cond_fn),
                  pl.BlockSpec((1, D_MODEL), lambda i: (0, 0)),
                  pl.BlockSpec((D_MODEL, LANES), lambda i: (0, 0))],
        out_specs=[pl.BlockSpec((TD, D_MODEL), lambda i: (i, 0)),
                   pl.BlockSpec((TD, LANES), lambda i: (i, 0)),
                   pl.BlockSpec((1, SUBLANES, LANES), lambda i: (i, 0, 0))],
        out_shape=[jax.ShapeDtypeStruct((rows, D_MODEL), BF16),
                   jax.ShapeDtypeStruct((rows, LANES), F32),
                   jax.ShapeDtypeStruct((rows // TD, SUBLANES, LANES), F32)],
        compiler_params=_cparams(("arbitrary",), VMEM_LIMIT),
        name="router",
    )(x, mods, mods, g2, wr_pad)


def _segment_copies(src_ref, src_off, dst_ref, dst_off, len8, sem, bits, wait):
    for k in reversed(range(bits)):
        size = SUBLANES << k
        done = ((len8 >> (k + 1)) << (k + 1)) * SUBLANES

        @pl.when(((len8 >> k) & 1) == 1)
        def _(size=size, done=done):
            cp = pltpu.make_async_copy(
                src_ref.at[pl.ds(pl.multiple_of(src_off + done, SUBLANES), size)],
                dst_ref.at[pl.ds(pl.multiple_of(dst_off + done, SUBLANES), size)], sem)
            if wait:
                cp.wait()
            else:
                cp.start()


def _slot_positions(e0, e1, r0, r1, seg_ref, base):
    pos0, pos1 = r0, r1
    for e in range(N_EXPERTS):
        start = seg_ref[base + e].astype(F32)
        pos0 = pos0 + jnp.where(e0 == e, start, 0.0)
        pos1 = pos1 + jnp.where(e1 == e, start, 0.0)
    return pos0.astype(jnp.int32), pos1.astype(jnp.int32)


def _dispatch_body(seg_ref, dst_ref, len_ref, tdst_ref, tlen_ref, nt_ref, hb_ref, info_ref, xs_ref,
                   buf_ref, zero_ref, sem, *, min_tiles):
    i = pl.program_id(0)
    base = i * N_EXPERTS
    info_t = info_ref[...].T
    row = lambda k: info_t[k:k + 1, :]
    pos0, pos1 = _slot_positions(row(INFO_E0), row(INFO_E1), row(INFO_R0), row(INFO_R1), seg_ref, base)
    slot = lax.broadcasted_iota(jnp.int32, (SB, TD), 0)
    p0 = slot == pos0
    p1 = slot == pos1
    perm = jnp.where(p0 | p1, 1.0, 0.0).astype(BF16)
    buf_ref[:, :D_MODEL] = _dot(perm, hb_ref[...])
    wsel = jnp.where(p0, row(INFO_W0), 0.0) + jnp.where(p1, row(INFO_W1), 0.0)
    buf_ref[:, D_MODEL:] = jnp.broadcast_to(jnp.sum(wsel, axis=1, keepdims=True), (SB, LANES))
    for e in range(N_EXPERTS):
        _segment_copies(buf_ref, seg_ref[base + e], xs_ref, dst_ref[base + e], len_ref[base + e],
                        sem, SEG_BITS, wait=False)

    @pl.when(i == pl.num_programs(0) - 1)
    def _():
        zero_ref[...] = jnp.zeros_like(zero_ref)
        spare_tiles = range(min_tiles, xs_ref.shape[0] // TME)
        spare = [pltpu.make_async_copy(zero_ref, xs_ref.at[pl.ds(j * TME, TME)], sem) for j in spare_tiles]
        for wait in (False, True):
            for e in range(N_EXPERTS):
                _segment_copies(zero_ref, 0, xs_ref, tdst_ref[e], tlen_ref[e], sem, TAIL_BITS, wait=wait)
            for j, cp in zip(spare_tiles, spare):
                @pl.when(j >= nt_ref[0])
                def _(cp=cp, wait=wait):
                    if wait:
                        cp.wait()
                    else:
                        cp.start()

    for e in range(N_EXPERTS):
        _segment_copies(buf_ref, seg_ref[base + e], xs_ref, dst_ref[base + e], len_ref[base + e],
                        sem, SEG_BITS, wait=True)


def _dispatch(seg, dst, len8, tail_dst, tail_len8, n_tiles, hb, info, m_rows):
    rows = hb.shape[0]
    return pl.pallas_call(
        functools.partial(_dispatch_body, min_tiles=2 * rows // TME),
        grid_spec=pltpu.PrefetchScalarGridSpec(
            num_scalar_prefetch=6,
            grid=(rows // TD,),
            in_specs=[pl.BlockSpec((TD, D_MODEL), lambda i, *_: (i, 0)),
                      pl.BlockSpec((TD, LANES), lambda i, *_: (i, 0))],
            out_specs=pl.BlockSpec(memory_space=pl.ANY),
            scratch_shapes=[pltpu.VMEM((SB, XS_W), F32), pltpu.VMEM((TME, XS_W), F32),
                            pltpu.SemaphoreType.DMA(())]),
        out_shape=jax.ShapeDtypeStruct((m_rows, XS_W), F32),
        compiler_params=_cparams(("arbitrary",), VMEM_LIMIT),
        name="dispatch",
    )(seg, dst, len8, tail_dst, tail_len8, n_tiles, hb, info)


def _expert_body(te_ref, nt_ref, xs_ref, wg_ref, wu_ref, wd_ref, ys_ref, xb_ref, acc_ref):
    j = pl.program_id(0)
    f = pl.program_id(1)
    nf = pl.num_programs(1)
    active = j < nt_ref[0]

    @pl.when(active & (f == 0))
    def _():
        xb_ref[...] = xs_ref[:, :D_MODEL].astype(BF16)
        acc_ref[...] = jnp.zeros_like(acc_ref)

    @pl.when(active)
    def _():
        xb = xb_ref[...]
        g = _dot(xb, wg_ref[0])
        u = _dot(xb, wu_ref[0])
        a = (g * _sigmoid(g) * u).astype(BF16)
        acc_ref[...] += _dot(a, wd_ref[0])

    @pl.when(active & (f == nf - 1))
    def _():
        ys_ref[...] = acc_ref[...] * xs_ref[:, D_MODEL:D_MODEL + 1]

    @pl.when(jnp.logical_not(active) & (f == nf - 1))
    def _():
        ys_ref[...] = jnp.zeros_like(ys_ref)


def _experts(tile_expert, n_tiles, xs, wg, wu, wd):
    m_rows = xs.shape[0]
    nf = wg.shape[2] // FC_E

    def f_idx(j, f, nt):
        return jnp.where(j < nt[0], f, nf - 1)

    def j_idx(j, nt):
        return jnp.minimum(j, nt[0] - 1)

    return pl.pallas_call(
        _expert_body,
        grid_spec=pltpu.PrefetchScalarGridSpec(
            num_scalar_prefetch=2,
            grid=(m_rows // TME, nf),
            in_specs=[pl.BlockSpec((TME, XS_W), lambda j, f, te, nt: (j_idx(j, nt), 0)),
                      pl.BlockSpec((1, D_MODEL, FC_E), lambda j, f, te, nt: (te[j], 0, f_idx(j, f, nt))),
                      pl.BlockSpec((1, D_MODEL, FC_E), lambda j, f, te, nt: (te[j], 0, f_idx(j, f, nt))),
                      pl.BlockSpec((1, FC_E, D_MODEL), lambda j, f, te, nt: (te[j], f_idx(j, f, nt), 0))],
            out_specs=pl.BlockSpec((TME, D_MODEL), lambda j, f, te, nt: (j, 0)),
            scratch_shapes=[pltpu.VMEM((TME, D_MODEL), BF16), pltpu.VMEM((TME, D_MODEL), F32)]),
        out_shape=jax.ShapeDtypeStruct((m_rows, D_MODEL), F32),
        compiler_params=_cparams(("arbitrary", "arbitrary"), VMEM_LIMIT),
        name="experts",
    )(tile_expert, n_tiles, xs, wg, wu, wd)


def _combine_body(seg_ref, dst_ref, len_ref, ys_ref, x_ref, info_ref, ga_ref, g_ref, o_ref, buf_ref, sem):
    i = pl.program_id(0)
    base = i * N_EXPERTS

    @pl.when(i == 0)
    def _():
        buf_ref[...] = jnp.zeros_like(buf_ref)

    for e in range(N_EXPERTS):
        _segment_copies(ys_ref, dst_ref[base + e], buf_ref, seg_ref[base + e], len_ref[base + e],
                        sem, SEG_BITS, wait=False)
    info = info_ref[...]
    col = lambda k: info[:, k:k + 1]
    pos0, pos1 = _slot_positions(col(INFO_E0), col(INFO_E1), col(INFO_R0), col(INFO_R1), seg_ref, base)
    slot = lax.broadcasted_iota(jnp.int32, (TD, SB), 1)
    unperm = jnp.where((slot == pos0) | (slot == pos1), 1.0, 0.0).astype(BF16)
    for e in range(N_EXPERTS):
        _segment_copies(ys_ref, dst_ref[base + e], buf_ref, seg_ref[base + e], len_ref[base + e],
                        sem, SEG_BITS, wait=True)
    y = _dot(unperm, buf_ref[...].astype(BF16))
    xn = x_ref[...] + ga_ref[0] * y
    r = lax.rsqrt(jnp.mean(xn * xn, axis=-1, keepdims=True) + EPS)
    o_ref[...] = (xn * r) * g_ref[...]


def _combine(seg, dst, len8, ys, x, info, mods, layer, cond_fn, final_g):
    rows = x.shape[0]
    return pl.pallas_call(
        _combine_body,
        grid_spec=pltpu.PrefetchScalarGridSpec(
            num_scalar_prefetch=3,
            grid=(rows // TD,),
            in_specs=[pl.BlockSpec(memory_space=pl.ANY),
                      pl.BlockSpec((TD, D_MODEL), lambda i, *_: (i, 0)),
                      pl.BlockSpec((TD, LANES), lambda i, *_: (i, 0)),
                      _mod_spec(layer, 5, cond_fn),
                      pl.BlockSpec((1, D_MODEL), lambda i, *_: (0, 0))],
            out_specs=pl.BlockSpec((TD, D_MODEL), lambda i, *_: (i, 0)),
            scratch_shapes=[pltpu.VMEM((SB, D_MODEL), F32), pltpu.SemaphoreType.DMA(())]),
        out_shape=jax.ShapeDtypeStruct((rows, D_MODEL), F32),
        compiler_params=_cparams(("arbitrary",), VMEM_LIMIT),
        name="combine",
    )(seg, dst, len8, ys, x, info, mods, final_g)


def _moe(x, mods, layer, cond_fn, g2, w_router, wg, wu, wd, final_g):
    rows = x.shape[0]
    n_tok_tiles = rows // TD
    wr_pad = jnp.pad(w_router, ((0, 0), (0, LANES - N_EXPERTS)))
    hb, info, cnt = _router(x, mods, layer, cond_fn, g2, wr_pad)
    counts = cnt[:, 0, :N_EXPERTS].astype(jnp.int32)
    seg_len = (counts + SUBLANES - 1) // SUBLANES * SUBLANES
    seg_start = jnp.cumsum(seg_len, axis=1) - seg_len
    group = jnp.sum(seg_len, axis=0)
    group_tiles = (group + TME - 1) // TME
    tile_start = jnp.cumsum(group_tiles) - group_tiles
    group_base = tile_start * TME
    dst = group_base[None, :] + jnp.cumsum(seg_len, axis=0) - seg_len
    tail_dst = group_base + group
    tail_len8 = (group_tiles * TME - group) // SUBLANES
    max_tiles = (2 * rows + n_tok_tiles * N_EXPERTS * (SUBLANES - 1) + TME - 1) // TME + N_EXPERTS
    n_tiles = jnp.sum(group_tiles).astype(jnp.int32).reshape(1)
    tid = jnp.minimum(jnp.arange(max_tiles, dtype=jnp.int32), n_tiles[0] - 1)
    tile_expert = (jnp.sum(tid[:, None] >= tile_start[None, :], axis=1) - 1).astype(jnp.int32)
    flat = lambda a: a.reshape(-1).astype(jnp.int32)
    seg, dst, len8 = flat(seg_start), flat(dst), flat(seg_len // SUBLANES)
    xs = _dispatch(seg, dst, len8, flat(tail_dst), flat(tail_len8), n_tiles, hb, info, max_tiles * TME)
    ys = _experts(tile_expert, n_tiles, xs, wg, wu, wd)
    return _combine(seg, dst, len8, ys, x, info, mods, layer, cond_fn, final_g)


def kernel(x, c, ctx, c_ctx, w_mod, b_mod, norm1_g, norm2_g, w_in, sink, w_fourier, w_attn, w_out,
           w_gate_d, w_up_d, w_down_d, w_router, w_gate_e, w_up_e, w_down_e, final_g):
    batch, seq, d = x.shape
    assert (seq, d) == (SEQ, D_MODEL) and ctx.shape == (batch, CTX_LEN, D_MODEL)
    tb = {k: jnp.asarray(v) for k, v in _tables().items()}
    tiles_per_batch = SEQ // TM
    lat_cond = lambda i: i // tiles_per_batch
    ctx_cond = lambda i: 2
    lat_rope = lambda i: i % tiles_per_batch
    ctx_rope = lambda i: 0

    cv = jnp.zeros((SUBLANES, D_MODEL), F32).at[:batch].set(c).at[batch].set(c_ctx)
    mods = _modulation(cv, w_mod, b_mod)

    xl = x.reshape(batch * SEQ, D_MODEL)
    xc = ctx.reshape(batch * CTX_LEN, D_MODEL)
    out = None
    for l in range(DEPTH):
        last = l == DEPTH - 1
        g1 = norm1_g[l].reshape(1, D_MODEL)
        g2 = norm2_g[l].reshape(1, D_MODEL)
        w_in_b = w_in[l].astype(BF16)
        wf = w_fourier[l].astype(BF16)
        wa = w_attn[l].astype(BF16)
        wo = w_out[l].astype(BF16)
        u, q, kv, gates = _in_proj(xl, mods, l, lat_cond, g1, w_in_b, tb["cos"], tb["sin"], lat_rope)
        uc, qc, kvc, gates_c = _in_proj(xc, mods, l, ctx_cond, g1, w_in_b, tb["cos1"], tb["sin0"], ctx_rope)
        fm = _fourier_latent(u, tb, batch)
        ao = _attention_latent(q, kv, kvc, sink[l], batch)
        if l % 2 == 0:
            i = l // 2
            xl, h2 = _merge(xl, fm, ao, gates, wf, wa, wo, mods, l, lat_cond, g2, True)
            wg, wu, wd = w_gate_d[i].astype(BF16), w_up_d[i].astype(BF16), w_down_d[i].astype(BF16)
            xl = _ffn_dense(xl, h2, wg, wu, wd, mods, l, lat_cond)
            if not last:
                fmc = _fourier_ctx(uc, tb, batch)
                aoc = _attention_ctx(qc, kvc, sink[l], batch)
                xc, h2c = _merge(xc, fmc, aoc, gates_c, wf, wa, wo, mods, l, ctx_cond, g2, True)
                xc = _ffn_dense(xc, h2c, wg, wu, wd, mods, l, ctx_cond)
            if last:
                raise NotImplementedError("final norm is fused into the routed-expert combine")
        else:
            i = l // 2
            (xl,) = _merge(xl, fm, ao, gates, wf, wa, wo, mods, l, lat_cond, g2, False)
            if not last:
                raise NotImplementedError("context update through a routed-expert layer")
            out = _moe(xl, mods, l, lambda t: t // (SEQ // TD), g2, w_router[i], w_gate_e[i].astype(BF16),
                       w_up_e[i].astype(BF16), w_down_e[i].astype(BF16), final_g.reshape(1, D_MODEL))
    return out.reshape(batch, SEQ, D_MODEL)
```

```python
import functools
import math

import numpy as np
import jax
import jax.numpy as jnp
from jax import lax
from jax.experimental import pallas as pl
from jax.experimental.pallas import tpu as pltpu

F32 = jnp.float32
BF16 = jnp.bfloat16

D_MODEL = 1024
SEQ = 8192
DEPTH = 2
GRID_W = 64
CTX_LEN = 256
N_GROUPS = 4
GROUP_W = 128
FOURIER_WIDTH = N_GROUPS * GROUP_W
N_Q_HEADS = 8
N_KV_HEADS = 2
HEAD_DIM = 64
ATTN_WIDTH = N_Q_HEADS * HEAD_DIM
KV_WIDTH = N_KV_HEADS * HEAD_DIM
WINDOW = 128
ROPE_BASE = 10000.0
OFF_Q = FOURIER_WIDTH
OFF_K = OFF_Q + ATTN_WIDTH
OFF_V = OFF_K + KV_WIDTH
OFF_G = OFF_V + KV_WIDTH
IN_WIDTH = OFF_G + 2 * D_MODEL
N_EXPERTS = 8
N_MOD = 6
EPS = 1e-6
NEG = -1e30
LOG2E = 1.0 / math.log(2.0)

LANES = 128
SUBLANES = 8

TM = 256
QB = 128
FFT_R = SEQ // GRID_W
FFT_C = GRID_W
TME = 512
HALVES = 2
TMP = HALVES * TME
TD = 512
SB = 2 * TD + N_EXPERTS * SUBLANES
XS_W = D_MODEL + LANES
SEG_BITS = TD.bit_length() - 3
TAIL_BITS = TMP.bit_length() - 4
FC_E = 512
VMEM_LIMIT = 56 * 1024 * 1024


def _cparams(sem, vmem=None):
    return pltpu.CompilerParams(dimension_semantics=sem, vmem_limit_bytes=vmem)


def _dot(a, b):
    return jnp.dot(a, b, preferred_element_type=F32)


def _sigmoid(x):
    return 1.0 / (1.0 + jnp.exp(-x))


def _rms_mod(x, g, sh, sc):
    r = lax.rsqrt(jnp.mean(x * x, axis=-1, keepdims=True) + EPS)
    return (x * r) * g * (1.0 + sc) + sh


def _dft_cs(n):
    k = np.arange(n, dtype=np.float64)
    a = 2.0 * np.pi * np.outer(k, k) / n
    return np.cos(a), np.sin(a)


@functools.lru_cache(maxsize=None)
def _tables():
    c128, s128 = _dft_cs(FFT_R)
    c64, s64 = _dft_cs(FFT_C)
    c256, s256 = _dft_cs(CTX_LEN)
    wa = np.concatenate([c128, -s128], axis=0)
    k1 = np.arange(FFT_R, dtype=np.float64)[:, None]
    cc = np.arange(FFT_C, dtype=np.float64)[None, :]
    ang = 2.0 * np.pi * k1 * cc / SEQ
    twc = np.broadcast_to(np.cos(ang)[:, :, None], (FFT_R, FFT_C, LANES))
    tws = np.broadcast_to(np.sin(ang)[:, :, None], (FFT_R, FFT_C, LANES))
    mc = np.block([[c64, s64], [-s64, c64]])
    cg, sg = _dft_cs(GROUP_W)
    mch = np.concatenate([cg, sg], axis=0) / math.sqrt(SEQ * GROUP_W)
    wcx = np.concatenate([c256, s256], axis=0)
    mchc = np.concatenate([cg, -sg], axis=0)
    n_freq = HEAD_DIM // 4
    inv = ROPE_BASE ** (-np.arange(n_freq, dtype=np.float64) / n_freq)
    t = np.arange(SEQ)
    rows = (t // GRID_W).astype(np.float64)[:, None] * inv
    cols = (t % GRID_W).astype(np.float64)[:, None] * inv
    cos_h = np.concatenate([np.cos(rows), np.cos(rows), np.cos(cols), np.cos(cols)], axis=1)
    sin_h = np.concatenate([-np.sin(rows), np.sin(rows), -np.sin(cols), np.sin(cols)], axis=1)
    cos_t = np.concatenate([cos_h, cos_h], axis=1)
    sin_t = np.concatenate([sin_h, sin_h], axis=1)
    f = lambda a: np.ascontiguousarray(a, dtype=np.float32)
    return dict(wa=f(wa), twc=f(twc), tws=f(tws), mc=f(mc), mch=f(mch), wcx=f(wcx), mchc=f(mchc),
                cos=f(cos_t), sin=f(sin_t),
                cos1=np.ones((TM, LANES), np.float32), sin0=np.zeros((TM, LANES), np.float32))


def _mod_body(cv_ref, w_ref, b_ref, o_ref):
    c = cv_ref[...]
    s = c * _sigmoid(c)
    o_ref[0, 0] = jnp.dot(s, w_ref[0], precision=lax.Precision.HIGHEST,
                          preferred_element_type=F32) + b_ref[0]


def _modulation(cv, w_mod, b_mod):
    out = pl.pallas_call(
        _mod_body,
        grid=(DEPTH, N_MOD),
        in_specs=[pl.BlockSpec((SUBLANES, D_MODEL), lambda l, k: (0, 0)),
                  pl.BlockSpec((1, D_MODEL, D_MODEL), lambda l, k: (l, 0, k)),
                  pl.BlockSpec((1, 1, D_MODEL), lambda l, k: (l * N_MOD + k, 0, 0))],
        out_specs=pl.BlockSpec((1, 1, SUBLANES, D_MODEL), lambda l, k: (l, k, 0, 0)),
        out_shape=jax.ShapeDtypeStruct((DEPTH, N_MOD, SUBLANES, D_MODEL), F32),
        compiler_params=_cparams(("arbitrary", "arbitrary")),
        name="modulation",
    )(cv, w_mod, b_mod.reshape(DEPTH * N_MOD, 1, D_MODEL))
    out = jnp.transpose(out[:, :, :3, :], (0, 2, 1, 3))
    return out.reshape(DEPTH * 3 * N_MOD, 1, D_MODEL)


def _mod_spec(layer, k, cond_fn):
    return pl.BlockSpec((1, 1, D_MODEL),
                        lambda i, *_: ((layer * 3 + cond_fn(i)) * N_MOD + k, 0, 0))


def _in_proj_body(x_ref, sh_ref, sc_ref, g_ref, w_ref, cos_ref, sin_ref,
                  u_ref, q_ref, kv_ref, gt_ref):
    hb = _rms_mod(x_ref[...], g_ref[...], sh_ref[0], sc_ref[0]).astype(BF16)
    u_ref[...] = _dot(hb, w_ref[:, 0:OFF_Q])
    cos = cos_ref[...]
    sin = sin_ref[...]
    lane = lax.broadcasted_iota(jnp.int32, cos.shape, 1)
    first_half = (lane % (HEAD_DIM // 2)) < (HEAD_DIM // 4)
    quarter = HEAD_DIM // 4

    def rope(xs):
        below = pltpu.roll(xs, quarter, 1)
        above = pltpu.roll(xs, LANES - quarter, 1)
        return xs * cos + jnp.where(first_half, above, below) * sin

    qk = _dot(hb, w_ref[:, OFF_Q:OFF_V])
    scale = HEAD_DIM ** -0.5 * LOG2E
    for j in range(ATTN_WIDTH // LANES):
        sl = slice(j * LANES, (j + 1) * LANES)
        q_ref[:, sl] = (rope(qk[:, sl]) * scale).astype(BF16)
    kv_ref[:, 0:KV_WIDTH] = rope(qk[:, ATTN_WIDTH:ATTN_WIDTH + KV_WIDTH]).astype(BF16)
    kv_ref[:, KV_WIDTH:2 * KV_WIDTH] = _dot(hb, w_ref[:, OFF_V:OFF_G]).astype(BF16)
    gc = 512
    for j in range(2 * D_MODEL // gc):
        z = _dot(hb, w_ref[:, OFF_G + j * gc:OFF_G + (j + 1) * gc])
        gt_ref[:, j * gc:(j + 1) * gc] = _sigmoid(z).astype(BF16)


def _in_proj(x, mods, layer, cond_fn, g, w_bf, cos, sin, rope_idx):
    rows = x.shape[0]
    return pl.pallas_call(
        _in_proj_body,
        grid=(rows // TM,),
        in_specs=[pl.BlockSpec((TM, D_MODEL), lambda i: (i, 0)),
                  _mod_spec(layer, 0, cond_fn),
                  _mod_spec(layer, 1, cond_fn),
                  pl.BlockSpec((1, D_MODEL), lambda i: (0, 0)),
                  pl.BlockSpec((D_MODEL, IN_WIDTH), lambda i: (0, 0)),
                  pl.BlockSpec((TM, LANES), lambda i: (rope_idx(i), 0)),
                  pl.BlockSpec((TM, LANES), lambda i: (rope_idx(i), 0))],
        out_specs=[pl.BlockSpec((TM, FOURIER_WIDTH), lambda i: (i, 0)),
                   pl.BlockSpec((TM, ATTN_WIDTH), lambda i: (i, 0)),
                   pl.BlockSpec((TM, 2 * KV_WIDTH), lambda i: (i, 0)),
                   pl.BlockSpec((TM, 2 * D_MODEL), lambda i: (i, 0))],
        out_shape=[jax.ShapeDtypeStruct((rows, FOURIER_WIDTH), F32),
                   jax.ShapeDtypeStruct((rows, ATTN_WIDTH), BF16),
                   jax.ShapeDtypeStruct((rows, 2 * KV_WIDTH), BF16),
                   jax.ShapeDtypeStruct((rows, 2 * D_MODEL), BF16)],
        compiler_params=_cparams(("arbitrary",), VMEM_LIMIT),
        name="in_proj",
    )(x, mods, mods, g, w_bf, cos, sin)


FFT_CB = 8


def _fft_a_body(x_ref, w_ref, o_ref):
    w = w_ref[...]
    xt = pltpu.einshape("rcf->crf", x_ref[...])
    res = jnp.stack([_dot(w, xt[c].astype(BF16)) for c in range(FFT_CB)])
    o_ref[0] = pltpu.einshape("ckf->kcf", res)


def _fft_c_body(re_ref, im_ref, tc_ref, ts_ref, mc_ref, mch_ref, o_ref):
    mc = mc_ref[...]
    mch = mch_ref[...]
    xrs, xis = [], []
    for j in range(FFT_CB):
        ar = re_ref[0, j]
        ai = im_ref[0, j]
        tc = jnp.concatenate([tc_ref[j]] * N_GROUPS, axis=1)
        ts = jnp.concatenate([ts_ref[j]] * N_GROUPS, axis=1)
        br = ar * tc + ai * ts
        bi = ai * tc - ar * ts
        x = _dot(mc, jnp.concatenate([br, bi], axis=0).astype(BF16))
        xrs.append(x[:FFT_C])
        xis.append(x[FFT_C:])
    xr = jnp.concatenate(xrs, axis=0).astype(BF16)
    xi = jnp.concatenate(xis, axis=0).astype(BF16)
    ys = []
    for g in range(N_GROUPS):
        sl = slice(g * GROUP_W, (g + 1) * GROUP_W)
        ys.append(_dot(jnp.concatenate([xr[:, sl], xi[:, sl]], axis=1), mch))
    y = jnp.concatenate(ys, axis=1).reshape(FFT_CB, FFT_C, FOURIER_WIDTH)
    o_ref[0] = pltpu.einshape("jkf->kjf", y)


def _fourier_latent(u, tb, batch):
    u3 = u.reshape(batch * FFT_R, FFT_C, FOURIER_WIDTH)
    a = pl.pallas_call(
        _fft_a_body,
        grid=(batch, FFT_C // FFT_CB),
        in_specs=[pl.BlockSpec((FFT_R, FFT_CB, FOURIER_WIDTH), lambda b, j: (b, j, 0)),
                  pl.BlockSpec((2 * FFT_R, FFT_R), lambda b, j: (0, 0))],
        out_specs=pl.BlockSpec((1, 2 * FFT_R, FFT_CB, FOURIER_WIDTH), lambda b, j: (b, 0, j, 0)),
        out_shape=jax.ShapeDtypeStruct((batch, 2 * FFT_R, FFT_C, FOURIER_WIDTH), F32),
        compiler_params=_cparams(("arbitrary", "arbitrary"), VMEM_LIMIT),
        name="fft_rows",
    )(u3, tb["wa"].astype(BF16))
    nk = FFT_R // FFT_CB
    y = pl.pallas_call(
        _fft_c_body,
        grid=(batch, nk),
        in_specs=[pl.BlockSpec((1, FFT_CB, FFT_C, FOURIER_WIDTH), lambda b, k: (b, k, 0, 0)),
                  pl.BlockSpec((1, FFT_CB, FFT_C, FOURIER_WIDTH), lambda b, k: (b, nk + k, 0, 0)),
                  pl.BlockSpec((FFT_CB, FFT_C, LANES), lambda b, k: (k, 0, 0)),
                  pl.BlockSpec((FFT_CB, FFT_C, LANES), lambda b, k: (k, 0, 0)),
                  pl.BlockSpec((2 * FFT_C, 2 * FFT_C), lambda b, k: (0, 0)),
                  pl.BlockSpec((2 * GROUP_W, GROUP_W), lambda b, k: (0, 0))],
        out_specs=pl.BlockSpec((1, FFT_C, FFT_CB, FOURIER_WIDTH), lambda b, k: (b, 0, k, 0)),
        out_shape=jax.ShapeDtypeStruct((batch, FFT_C, FFT_R, FOURIER_WIDTH), F32),
        compiler_params=_cparams(("arbitrary", "arbitrary"), VMEM_LIMIT),
        name="fft_cols",
    )(a, a, tb["twc"], tb["tws"], tb["mc"].astype(BF16), tb["mch"].astype(BF16))
    return y.reshape(batch * SEQ, FOURIER_WIDTH)


def _fourier_ctx_body(u_ref, w_ref, m_ref, o_ref):
    pq = _dot(w_ref[...], u_ref[...].astype(BF16))
    p = pq[:CTX_LEN]
    q = pq[CTX_LEN:]
    m = m_ref[...]
    scale = 1.0 / math.sqrt(CTX_LEN * GROUP_W)
    for g in range(N_GROUPS):
        sl = slice(g * GROUP_W, (g + 1) * GROUP_W)
        lhs = jnp.concatenate([p[:, sl], q[:, sl]], axis=1).astype(BF16)
        o_ref[:, sl] = _dot(lhs, m) * scale


def _fourier_ctx(u, tb, batch):
    return pl.pallas_call(
        _fourier_ctx_body,
        grid=(batch,),
        in_specs=[pl.BlockSpec((CTX_LEN, FOURIER_WIDTH), lambda b: (b, 0)),
                  pl.BlockSpec((2 * CTX_LEN, CTX_LEN), lambda b: (0, 0)),
                  pl.BlockSpec((2 * GROUP_W, GROUP_W), lambda b: (0, 0))],
        out_specs=pl.BlockSpec((CTX_LEN, FOURIER_WIDTH), lambda b: (b, 0)),
        out_shape=jax.ShapeDtypeStruct((batch * CTX_LEN, FOURIER_WIDTH), F32),
        compiler_params=_cparams(("arbitrary",)),
        name="fft_ctx",
    )(u, tb["wcx"].astype(BF16), tb["mchc"].astype(BF16))


def _attn_body(sink_ref, q_ref, *refs, local):
    o_ref = refs[-1]
    if local:
        kp_ref, kc_ref, kn_ref, kx_ref = refs[:-1]
        groups = [jnp.concatenate([kp_ref[...], kn_ref[...]], axis=0),
                  jnp.concatenate([kc_ref[...], kx_ref[...]], axis=0)]
        n = pl.program_id(1)
        last = pl.num_programs(1) - 1
        qi = lax.broadcasted_iota(jnp.int32, (2 * QB, 2 * QB), 0) % QB
        kj = lax.broadcasted_iota(jnp.int32, (2 * QB, 2 * QB), 1)
        prev_ok = (kj < QB) & (kj >= qi + jnp.where(n == 0, 2 * QB, 0))
        next_ok = (kj >= QB) & (kj - QB <= qi - jnp.where(n == last, 2 * QB, 0))
        masks = [prev_ok | next_ok, None]
    else:
        groups = [refs[0][...]]
        masks = [None]
    lo_q = lax.broadcasted_iota(jnp.int32, (2 * QB, LANES), 1) < HEAD_DIM
    kmats, vmats = [], []
    for kv in groups:
        kv = kv.astype(F32)
        kcat = kv[:, :KV_WIDTH]
        vcat = kv[:, KV_WIDTH:]
        kroll = pltpu.roll(kcat, HEAD_DIM, 1)
        vroll = pltpu.roll(vcat, HEAD_DIM, 1)
        lo_k = lax.broadcasted_iota(jnp.int32, kcat.shape, 1) < HEAD_DIM
        zero = jnp.zeros_like(kcat)
        one = jnp.ones_like(vcat)
        km, vm = [], []
        for hk in range(N_KV_HEADS):
            k_own, k_other = (kcat, kroll) if hk == 0 else (kroll, kcat)
            v_own, v_other = (vcat, vroll) if hk == 0 else (vroll, vcat)
            km.append((jnp.where(lo_k, k_own, zero).astype(BF16), jnp.where(lo_k, zero, k_other).astype(BF16)))
            vm.append((jnp.where(lo_k, v_own, one).astype(BF16), jnp.where(lo_k, one, v_other).astype(BF16)))
        kmats.append(km)
        vmats.append(vm)
    pairs = [(hk, half) for hk in range(N_KV_HEADS) for half in range(2)]
    first_slab = lax.broadcasted_iota(jnp.int32, (2 * QB, 1), 0) < QB
    scores = []
    for hk, half in pairs:
        qs = jnp.concatenate([q_ref[:, (2 * hk) * LANES:(2 * hk + 1) * LANES],
                              q_ref[:, (2 * hk + 1) * LANES:(2 * hk + 2) * LANES]], axis=0)
        parts = []
        for km, mask in zip(kmats, masks):
            s = lax.dot_general(qs, km[hk][half], (((1,), (1,)), ((), ())), preferred_element_type=F32)
            parts.append(s if mask is None else jnp.where(mask, s, NEG))
        scores.append(parts)
    sinks = [jnp.where(first_slab, sink_ref[4 * hk + half], sink_ref[4 * hk + 2 + half]) * LOG2E
             for hk, half in pairs]
    maxes = []
    for parts, sk in zip(scores, sinks):
        m = sk
        for s in parts:
            m = jnp.maximum(m, jnp.max(s, axis=1, keepdims=True))
        maxes.append(m)
    sink_p = [jnp.exp2(sk - m) for sk, m in zip(sinks, maxes)]
    pvs = []
    for parts, m, (hk, half) in zip(scores, maxes, pairs):
        pv = None
        for s, vm in zip(parts, vmats):
            term = _dot(jnp.exp2(s - m).astype(BF16), vm[hk][half])
            pv = term if pv is None else pv + term
        pvs.append(pv)
    for hk in range(N_KV_HEADS):
        lo, hi = pvs[2 * hk], pvs[2 * hk + 1]
        num = jnp.where(lo_q, lo, hi)
        den = (pltpu.roll(jnp.where(lo_q, hi, lo), HEAD_DIM, 1)
               + jnp.where(lo_q, sink_p[2 * hk], sink_p[2 * hk + 1]))
        out = (num / den).astype(BF16)
        o_ref[:, (2 * hk) * LANES:(2 * hk + 1) * LANES] = out[:QB]
        o_ref[:, (2 * hk + 1) * LANES:(2 * hk + 2) * LANES] = out[QB:]


def _attention_latent(q, kv, kv_ctx, sink, batch):
    nb = SEQ // QB
    kvw = 2 * KV_WIDTH
    return pl.pallas_call(
        functools.partial(_attn_body, local=True),
        grid=(batch, nb),
        in_specs=[pl.BlockSpec(memory_space=pltpu.SMEM),
                  pl.BlockSpec((QB, ATTN_WIDTH), lambda b, n: (b * nb + n, 0)),
                  pl.BlockSpec((QB, kvw), lambda b, n: (b * nb + jnp.maximum(n - 1, 0), 0)),
                  pl.BlockSpec((QB, kvw), lambda b, n: (b * nb + n, 0)),
                  pl.BlockSpec((QB, kvw), lambda b, n: (b * nb + jnp.minimum(n + 1, nb - 1), 0)),
                  pl.BlockSpec((CTX_LEN, kvw), lambda b, n: (b, 0))],
        out_specs=pl.BlockSpec((QB, ATTN_WIDTH), lambda b, n: (b * nb + n, 0)),
        out_shape=jax.ShapeDtypeStruct((batch * SEQ, ATTN_WIDTH), BF16),
        compiler_params=_cparams(("arbitrary", "arbitrary")),
        name="attn_latent",
    )(sink, q, kv, kv, kv, kv_ctx)


def _attention_ctx(q, kv, sink, batch):
    nb = CTX_LEN // QB
    return pl.pallas_call(
        functools.partial(_attn_body, local=False),
        grid=(batch, nb),
        in_specs=[pl.BlockSpec(memory_space=pltpu.SMEM),
                  pl.BlockSpec((QB, ATTN_WIDTH), lambda b, n: (b * nb + n, 0)),
                  pl.BlockSpec((CTX_LEN, 2 * KV_WIDTH), lambda b, n: (b, 0))],
        out_specs=pl.BlockSpec((QB, ATTN_WIDTH), lambda b, n: (b * nb + n, 0)),
        out_shape=jax.ShapeDtypeStruct((batch * CTX_LEN, ATTN_WIDTH), BF16),
        compiler_params=_cparams(("arbitrary", "arbitrary")),
        name="attn_ctx",
    )(sink, q, kv)


def _merge_body(x_ref, fm_ref, ao_ref, gt_ref, wf_ref, wa_ref, wo_ref, ga_ref, *refs, with_h2):
    gt = gt_ref[...].astype(F32)
    y = (gt[:, :D_MODEL] * _dot(fm_ref[...].astype(BF16), wf_ref[...])
         + gt[:, D_MODEL:] * _dot(ao_ref[...], wa_ref[...]))
    xn = x_ref[...] + ga_ref[0] * _dot(y.astype(BF16), wo_ref[...])
    if with_h2:
        sh_ref, sc_ref, g_ref, xo_ref, h_ref = refs
        h_ref[...] = _rms_mod(xn, g_ref[...], sh_ref[0], sc_ref[0]).astype(BF16)
    else:
        (xo_ref,) = refs
    xo_ref[...] = xn


def _merge(x, fm, ao, gates, wf, wa, wo, mods, layer, cond_fn, g2, with_h2):
    rows = x.shape[0]
    row_spec = lambda w: pl.BlockSpec((TM, w), lambda i: (i, 0))
    full = lambda a: pl.BlockSpec(a.shape, lambda i: (0, 0))
    in_specs = [row_spec(D_MODEL), row_spec(FOURIER_WIDTH), row_spec(ATTN_WIDTH), row_spec(2 * D_MODEL),
                full(wf), full(wa), full(wo), _mod_spec(layer, 2, cond_fn)]
    args = [x, fm, ao, gates, wf, wa, wo, mods]
    out_specs = [row_spec(D_MODEL)]
    out_shape = [jax.ShapeDtypeStruct((rows, D_MODEL), F32)]
    if with_h2:
        in_specs += [_mod_spec(layer, 3, cond_fn), _mod_spec(layer, 4, cond_fn),
                     pl.BlockSpec((1, D_MODEL), lambda i: (0, 0))]
        args += [mods, mods, g2]
        out_specs.append(row_spec(D_MODEL))
        out_shape.append(jax.ShapeDtypeStruct((rows, D_MODEL), BF16))
    return pl.pallas_call(
        functools.partial(_merge_body, with_h2=with_h2),
        grid=(rows // TM,),
        in_specs=in_specs, out_specs=out_specs, out_shape=out_shape,
        compiler_params=_cparams(("arbitrary",), VMEM_LIMIT),
        name="merge",
    )(*args)


def _ffn_body(x_ref, h_ref, wg_ref, wu_ref, wd_ref, ga_ref, o_ref):
    h = h_ref[...]
    g = _dot(h, wg_ref[...])
    u = _dot(h, wu_ref[...])
    a = (g * _sigmoid(g) * u).astype(BF16)
    o_ref[...] = x_ref[...] + ga_ref[0] * _dot(a, wd_ref[...])


def _ffn_dense(x, h2, wg, wu, wd, mods, layer, cond_fn):
    rows = x.shape[0]
    resident = lambda a: pl.BlockSpec(a.shape, lambda i: (0, 0), pipeline_mode=pl.Buffered(1))
    return pl.pallas_call(
        _ffn_body,
        grid=(rows // TM,),
        in_specs=[pl.BlockSpec((TM, D_MODEL), lambda i: (i, 0)),
                  pl.BlockSpec((TM, D_MODEL), lambda i: (i, 0)),
                  resident(wg), resident(wu), resident(wd),
                  _mod_spec(layer, 5, cond_fn)],
        out_specs=pl.BlockSpec((TM, D_MODEL), lambda i: (i, 0)),
        out_shape=jax.ShapeDtypeStruct((rows, D_MODEL), F32),
        compiler_params=_cparams(("arbitrary",), VMEM_LIMIT),
        name="ffn_dense",
    )(x, h2, wg, wu, wd, mods)


INFO_E0, INFO_E1, INFO_R0, INFO_R1, INFO_W0, INFO_W1 = range(6)


def _router_body(x_ref, sh_ref, sc_ref, g_ref, wr_ref, hb_ref, info_ref, cnt_ref):
    h = _rms_mod(x_ref[...], g_ref[...], sh_ref[0], sc_ref[0])
    hb_ref[...] = h.astype(BF16)
    logits = jnp.dot(h, wr_ref[...], precision=lax.Precision.HIGHEST, preferred_element_type=F32)
    lane = lax.broadcasted_iota(jnp.int32, logits.shape, 1)
    neg_inf = jnp.float32(-jnp.inf)
    lg = jnp.where(lane < N_EXPERTS, logits, neg_inf)
    v0 = jnp.max(lg, axis=1, keepdims=True)
    i0 = jnp.min(jnp.where(lg == v0, lane, LANES), axis=1, keepdims=True)
    oh0 = lane == i0
    lg1 = jnp.where(oh0, neg_inf, lg)
    v1 = jnp.max(lg1, axis=1, keepdims=True)
    i1 = jnp.min(jnp.where(lg1 == v1, lane, LANES), axis=1, keepdims=True)
    oh1 = lane == i1
    e = jnp.exp(v1 - v0)
    w0 = 1.0 / (1.0 + e)
    w1 = e / (1.0 + e)
    oh = jnp.where(oh0 | oh1, 1.0, 0.0)
    row = lax.broadcasted_iota(jnp.int32, (TD, TD), 0)
    col = lax.broadcasted_iota(jnp.int32, (TD, TD), 1)
    tri = jnp.where(row > col, 1.0, 0.0).astype(BF16)
    before = _dot(tri, oh.astype(BF16))
    r0 = jnp.sum(jnp.where(oh0, before, 0.0), axis=1, keepdims=True)
    r1 = jnp.sum(jnp.where(oh1, before, 0.0), axis=1, keepdims=True)
    cnt_ref[0] = jnp.broadcast_to(jnp.sum(oh, axis=0, keepdims=True), (SUBLANES, LANES))
    info = jnp.zeros(logits.shape, F32)
    for idx, val in ((INFO_E0, i0.astype(F32)), (INFO_E1, i1.astype(F32)), (INFO_R0, r0),
                     (INFO_R1, r1), (INFO_W0, w0), (INFO_W1, w1)):
        info = jnp.where(lane == idx, val, info)
    info_ref[...] = info


def _router(x, mods, layer, cond_fn, g2, wr_pad):
    rows = x.shape[0]
    return pl.pallas_call(
        _router_body,
        grid=(rows // TD,),
        in_specs=[pl.BlockSpec((TD, D_MODEL), lambda i: (i, 0)),
                  _mod_spec(layer, 3, cond_fn), _mod_spec(layer, 4, cond_fn),
                  pl.BlockSpec((1, D_MODEL), lambda i: (0, 0)),
                  pl.BlockSpec((D_MODEL, LANES), lambda i: (0, 0))],
        out_specs=[pl.BlockSpec((TD, D_MODEL), lambda i: (i, 0)),
                   pl.BlockSpec((TD, LANES), lambda i: (i, 0)),
                   pl.BlockSpec((1, SUBLANES, LANES), lambda i: (i, 0, 0))],
        out_shape=[jax.ShapeDtypeStruct((rows, D_MODEL), BF16),
                   jax.ShapeDtypeStruct((rows, LANES), F32),
                   jax.ShapeDtypeStruct((rows // TD, SUBLANES, LANES), F32)],
        compiler_params=_cparams(("arbitrary",), VMEM_LIMIT),
        name="router",
    )(x, mods, mods, g2, wr_pad)


def _segment_copies(src_ref, src_off, dst_ref, dst_off, len8, sem, bits, wait):
    for k in reversed(range(bits)):
        size = SUBLANES << k
        done = ((len8 >> (k + 1)) << (k + 1)) * SUBLANES

        @pl.when(((len8 >> k) & 1) == 1)
        def _(size=size, done=done):
            cp = pltpu.make_async_copy(
                src_ref.at[pl.ds(pl.multiple_of(src_off + done, SUBLANES), size)],
                dst_ref.at[pl.ds(pl.multiple_of(dst_off + done, SUBLANES), size)], sem)
            if wait:
                cp.wait()
            else:
                cp.start()


def _slot_positions(e0, e1, r0, r1, seg_ref, base):
    pos0, pos1 = r0, r1
    for e in range(N_EXPERTS):
        start = seg_ref[base + e].astype(F32)
        pos0 = pos0 + jnp.where(e0 == e, start, 0.0)
        pos1 = pos1 + jnp.where(e1 == e, start, 0.0)
    return pos0.astype(jnp.int32), pos1.astype(jnp.int32)


def _dispatch_body(seg_ref, dst_ref, len_ref, tdst_ref, tlen_ref, nt_ref, hb_ref, info_ref, xs_ref,
                   buf_ref, zero_ref, sem, *, min_tiles):
    i = pl.program_id(0)
    base = i * N_EXPERTS
    info_t = info_ref[...].T
    row = lambda k: info_t[k:k + 1, :]
    pos0, pos1 = _slot_positions(row(INFO_E0), row(INFO_E1), row(INFO_R0), row(INFO_R1), seg_ref, base)
    slot = lax.broadcasted_iota(jnp.int32, (SB, TD), 0)
    p0 = slot == pos0
    p1 = slot == pos1
    perm = jnp.where(p0 | p1, 1.0, 0.0).astype(BF16)
    buf_ref[:, :D_MODEL] = _dot(perm, hb_ref[...])
    wsel = jnp.where(p0, row(INFO_W0), 0.0) + jnp.where(p1, row(INFO_W1), 0.0)
    buf_ref[:, D_MODEL:] = jnp.broadcast_to(jnp.sum(wsel, axis=1, keepdims=True), (SB, LANES))
    for e in range(N_EXPERTS):
        _segment_copies(buf_ref, seg_ref[base + e], xs_ref, dst_ref[base + e], len_ref[base + e],
                        sem, SEG_BITS, wait=False)

    @pl.when(i == pl.num_programs(0) - 1)
    def _():
        zero_ref[...] = jnp.zeros_like(zero_ref)
        spare_tiles = range(min_tiles, xs_ref.shape[0] // TMP)
        spare = [pltpu.make_async_copy(zero_ref, xs_ref.at[pl.ds(j * TMP, TMP)], sem) for j in spare_tiles]
        for wait in (False, True):
            for e in range(N_EXPERTS):
                _segment_copies(zero_ref, 0, xs_ref, tdst_ref[e], tlen_ref[e], sem, TAIL_BITS, wait=wait)
            for j, cp in zip(spare_tiles, spare):
                @pl.when(j >= nt_ref[0])
                def _(cp=cp, wait=wait):
                    if wait:
                        cp.wait()
                    else:
                        cp.start()

    for e in range(N_EXPERTS):
        _segment_copies(buf_ref, seg_ref[base + e], xs_ref, dst_ref[base + e], len_ref[base + e],
                        sem, SEG_BITS, wait=True)


def _dispatch(seg, dst, len8, tail_dst, tail_len8, n_tiles, hb, info, m_rows):
    rows = hb.shape[0]
    return pl.pallas_call(
        functools.partial(_dispatch_body, min_tiles=2 * rows // TMP),
        grid_spec=pltpu.PrefetchScalarGridSpec(
            num_scalar_prefetch=6,
            grid=(rows // TD,),
            in_specs=[pl.BlockSpec((TD, D_MODEL), lambda i, *_: (i, 0)),
                      pl.BlockSpec((TD, LANES), lambda i, *_: (i, 0))],
            out_specs=pl.BlockSpec(memory_space=pl.ANY),
            scratch_shapes=[pltpu.VMEM((SB, XS_W), F32), pltpu.VMEM((TMP, XS_W), F32),
                            pltpu.SemaphoreType.DMA(())]),
        out_shape=jax.ShapeDtypeStruct((m_rows, XS_W), F32),
        compiler_params=_cparams(("arbitrary",), VMEM_LIMIT),
        name="dispatch",
    )(seg, dst, len8, tail_dst, tail_len8, n_tiles, hb, info)


def _expert_body(te_ref, nh_ref, nt_ref, xs_ref, wg_ref, wu_ref, wd_ref, ys_ref,
                 xb_ref, acc_ref, wgb_ref, wub_ref, wdb_ref):
    j = pl.program_id(0)
    f = pl.program_id(1)
    nf = pl.num_programs(1)
    halves = nh_ref[j]

    @pl.when(halves > 0)
    def _():
        wgb_ref[...] = wg_ref[0].astype(BF16)
        wub_ref[...] = wu_ref[0].astype(BF16)
        wdb_ref[...] = wd_ref[0].astype(BF16)

    for h in range(HALVES):
        rows = slice(h * TME, (h + 1) * TME)

        @pl.when(halves > h)
        def _(rows=rows):
            @pl.when(f == 0)
            def _():
                xb_ref[rows, :] = xs_ref[rows, :D_MODEL].astype(BF16)
                acc_ref[rows, :] = jnp.zeros((TME, D_MODEL), F32)

            xb = xb_ref[rows, :]
            g = _dot(xb, wgb_ref[...])
            u = _dot(xb, wub_ref[...])
            a = (g * _sigmoid(g) * u).astype(BF16)
            acc_ref[rows, :] += _dot(a, wdb_ref[...])

            @pl.when(f == nf - 1)
            def _():
                ys_ref[rows, :] = acc_ref[rows, :] * xs_ref[rows, D_MODEL:D_MODEL + 1]

        @pl.when((halves <= h) & (f == nf - 1))
        def _(rows=rows):
            ys_ref[rows, :] = jnp.zeros((TME, D_MODEL), F32)


def _experts(tile_expert, tile_halves, n_tiles, xs, wg, wu, wd):
    m_rows = xs.shape[0]
    nf = wg.shape[2] // FC_E

    def f_idx(j, f, nt):
        return jnp.where(j < nt[0], f, nf - 1)

    def j_idx(j, nt):
        return jnp.minimum(j, nt[0] - 1)

    return pl.pallas_call(
        _expert_body,
        grid_spec=pltpu.PrefetchScalarGridSpec(
            num_scalar_prefetch=3,
            grid=(m_rows // TMP, nf),
            in_specs=[pl.BlockSpec((TMP, XS_W), lambda j, f, te, nh, nt: (j_idx(j, nt), 0)),
                      pl.BlockSpec((1, D_MODEL, FC_E), lambda j, f, te, nh, nt: (te[j], 0, f_idx(j, f, nt))),
                      pl.BlockSpec((1, D_MODEL, FC_E), lambda j, f, te, nh, nt: (te[j], 0, f_idx(j, f, nt))),
                      pl.BlockSpec((1, FC_E, D_MODEL), lambda j, f, te, nh, nt: (te[j], f_idx(j, f, nt), 0))],
            out_specs=pl.BlockSpec((TMP, D_MODEL), lambda j, f, te, nh, nt: (j, 0)),
            scratch_shapes=[pltpu.VMEM((TMP, D_MODEL), BF16), pltpu.VMEM((TMP, D_MODEL), F32),
                            pltpu.VMEM((D_MODEL, FC_E), BF16), pltpu.VMEM((D_MODEL, FC_E), BF16),
                            pltpu.VMEM((FC_E, D_MODEL), BF16)]),
        out_shape=jax.ShapeDtypeStruct((m_rows, D_MODEL), F32),
        compiler_params=_cparams(("arbitrary", "arbitrary"), VMEM_LIMIT),
        name="experts",
    )(tile_expert, tile_halves, n_tiles, xs, wg, wu, wd)


def _combine_body(seg_ref, dst_ref, len_ref, ys_ref, x_ref, info_ref, ga_ref, g_ref, o_ref, buf_ref, sem):
    i = pl.program_id(0)
    base = i * N_EXPERTS

    @pl.when(i == 0)
    def _():
        buf_ref[...] = jnp.zeros_like(buf_ref)

    for e in range(N_EXPERTS):
        _segment_copies(ys_ref, dst_ref[base + e], buf_ref, seg_ref[base + e], len_ref[base + e],
                        sem, SEG_BITS, wait=False)
    info = info_ref[...]
    col = lambda k: info[:, k:k + 1]
    pos0, pos1 = _slot_positions(col(INFO_E0), col(INFO_E1), col(INFO_R0), col(INFO_R1), seg_ref, base)
    slot = lax.broadcasted_iota(jnp.int32, (TD, SB), 1)
    unperm = jnp.where((slot == pos0) | (slot == pos1), 1.0, 0.0).astype(BF16)
    for e in range(N_EXPERTS):
        _segment_copies(ys_ref, dst_ref[base + e], buf_ref, seg_ref[base + e], len_ref[base + e],
                        sem, SEG_BITS, wait=True)
    y = _dot(unperm, buf_ref[...].astype(BF16))
    xn = x_ref[...] + ga_ref[0] * y
    r = lax.rsqrt(jnp.mean(xn * xn, axis=-1, keepdims=True) + EPS)
    o_ref[...] = (xn * r) * g_ref[...]


def _combine(seg, dst, len8, ys, x, info, mods, layer, cond_fn, final_g):
    rows = x.shape[0]
    return pl.pallas_call(
        _combine_body,
        grid_spec=pltpu.PrefetchScalarGridSpec(
            num_scalar_prefetch=3,
            grid=(rows // TD,),
            in_specs=[pl.BlockSpec(memory_space=pl.ANY),
                      pl.BlockSpec((TD, D_MODEL), lambda i, *_: (i, 0)),
                      pl.BlockSpec((TD, LANES), lambda i, *_: (i, 0)),
                      _mod_spec(layer, 5, cond_fn),
                      pl.BlockSpec((1, D_MODEL), lambda i, *_: (0, 0))],
            out_specs=pl.BlockSpec((TD, D_MODEL), lambda i, *_: (i, 0)),
            scratch_shapes=[pltpu.VMEM((SB, D_MODEL), F32), pltpu.SemaphoreType.DMA(())]),
        out_shape=jax.ShapeDtypeStruct((rows, D_MODEL), F32),
        compiler_params=_cparams(("arbitrary",), VMEM_LIMIT),
        name="combine",
    )(seg, dst, len8, ys, x, info, mods, final_g)


def _moe(x, mods, layer, cond_fn, g2, w_router, wg, wu, wd, final_g):
    rows = x.shape[0]
    n_tok_tiles = rows // TD
    wr_pad = jnp.pad(w_router, ((0, 0), (0, LANES - N_EXPERTS)))
    hb, info, cnt = _router(x, mods, layer, cond_fn, g2, wr_pad)
    counts = cnt[:, 0, :N_EXPERTS].astype(jnp.int32)
    seg_len = (counts + SUBLANES - 1) // SUBLANES * SUBLANES
    seg_start = jnp.cumsum(seg_len, axis=1) - seg_len
    group = jnp.sum(seg_len, axis=0)
    group_halves = (group + TME - 1) // TME
    group_tiles = (group + TMP - 1) // TMP
    tile_start = jnp.cumsum(group_tiles) - group_tiles
    group_base = tile_start * TMP
    dst = group_base[None, :] + jnp.cumsum(seg_len, axis=0) - seg_len
    tail_dst = group_base + group
    tail_len8 = (group_tiles * TMP - group) // SUBLANES
    max_tiles = (2 * rows + n_tok_tiles * N_EXPERTS * (SUBLANES - 1) + TMP - 1) // TMP + N_EXPERTS
    n_tiles = jnp.sum(group_tiles).astype(jnp.int32).reshape(1)
    tiles = jnp.arange(max_tiles, dtype=jnp.int32)
    tid = jnp.minimum(tiles, n_tiles[0] - 1)
    tile_expert = (jnp.sum(tid[:, None] >= tile_start[None, :], axis=1) - 1).astype(jnp.int32)
    in_group = tid - tile_start[tile_expert]
    tile_halves = jnp.clip(group_halves[tile_expert] - HALVES * in_group, 0, HALVES)
    tile_halves = jnp.where(tiles < n_tiles[0], tile_halves, 0).astype(jnp.int32)
    flat = lambda a: a.reshape(-1).astype(jnp.int32)
    seg, dst, len8 = flat(seg_start), flat(dst), flat(seg_len // SUBLANES)
    xs = _dispatch(seg, dst, len8, flat(tail_dst), flat(tail_len8), n_tiles, hb, info, max_tiles * TMP)
    ys = _experts(tile_expert, tile_halves, n_tiles, xs, wg, wu, wd)
    return _combine(seg, dst, len8, ys, x, info, mods, layer, cond_fn, final_g)


def kernel(x, c, ctx, c_ctx, w_mod, b_mod, norm1_g, norm2_g, w_in, sink, w_fourier, w_attn, w_out,
           w_gate_d, w_up_d, w_down_d, w_router, w_gate_e, w_up_e, w_down_e, final_g):
    batch, seq, d = x.shape
    assert (seq, d) == (SEQ, D_MODEL) and ctx.shape == (batch, CTX_LEN, D_MODEL)
    tb = {k: jnp.asarray(v) for k, v in _tables().items()}
    tiles_per_batch = SEQ // TM
    lat_cond = lambda i: i // tiles_per_batch
    ctx_cond = lambda i: 2
    lat_rope = lambda i: i % tiles_per_batch
    ctx_rope = lambda i: 0

    cv = jnp.zeros((SUBLANES, D_MODEL), F32).at[:batch].set(c).at[batch].set(c_ctx)
    mods = _modulation(cv, w_mod, b_mod)

    xl = x.reshape(batch * SEQ, D_MODEL)
    xc = ctx.reshape(batch * CTX_LEN, D_MODEL)
    out = None
    for l in range(DEPTH):
        last = l == DEPTH - 1
        g1 = norm1_g[l].reshape(1, D_MODEL)
        g2 = norm2_g[l].reshape(1, D_MODEL)
        w_in_b = w_in[l].astype(BF16)
        wf = w_fourier[l].astype(BF16)
        wa = w_attn[l].astype(BF16)
        wo = w_out[l].astype(BF16)
        u, q, kv, gates = _in_proj(xl, mods, l, lat_cond, g1, w_in_b, tb["cos"], tb["sin"], lat_rope)
        uc, qc, kvc, gates_c = _in_proj(xc, mods, l, ctx_cond, g1, w_in_b, tb["cos1"], tb["sin0"], ctx_rope)
        fm = _fourier_latent(u, tb, batch)
        ao = _attention_latent(q, kv, kvc, sink[l], batch)
        if l % 2 == 0:
            i = l // 2
            xl, h2 = _merge(xl, fm, ao, gates, wf, wa, wo, mods, l, lat_cond, g2, True)
            wg, wu, wd = w_gate_d[i].astype(BF16), w_up_d[i].astype(BF16), w_down_d[i].astype(BF16)
            xl = _ffn_dense(xl, h2, wg, wu, wd, mods, l, lat_cond)
            if not last:
                fmc = _fourier_ctx(uc, tb, batch)
                aoc = _attention_ctx(qc, kvc, sink[l], batch)
                xc, h2c = _merge(xc, fmc, aoc, gates_c, wf, wa, wo, mods, l, ctx_cond, g2, True)
                xc = _ffn_dense(xc, h2c, wg, wu, wd, mods, l, ctx_cond)
            if last:
                raise NotImplementedError("final norm is fused into the routed-expert combine")
        else:
            i = l // 2
            (xl,) = _merge(xl, fm, ao, gates, wf, wa, wo, mods, l, lat_cond, g2, False)
            if not last:
                raise NotImplementedError("context update through a routed-expert layer")
            out = _moe(xl, mods, l, lambda t: t // (SEQ // TD), g2, w_router[i], w_gate_e[i], w_up_e[i], w_down_e[i],
                       final_g.reshape(1, D_MODEL))
    return out.reshape(batch, SEQ, D_MODEL)
```

```python
import functools
import math

import numpy as np
import jax
import jax.numpy as jnp
from jax import lax
from jax.experimental import pallas as pl
from jax.experimental.pallas import tpu as pltpu

F32 = jnp.float32
BF16 = jnp.bfloat16

D_MODEL = 1024
SEQ = 8192
DEPTH = 2
GRID_W = 64
CTX_LEN = 256
N_GROUPS = 4
GROUP_W = 128
FOURIER_WIDTH = N_GROUPS * GROUP_W
N_Q_HEADS = 8
N_KV_HEADS = 2
HEAD_DIM = 64
ATTN_WIDTH = N_Q_HEADS * HEAD_DIM
KV_WIDTH = N_KV_HEADS * HEAD_DIM
WINDOW = 128
ROPE_BASE = 10000.0
OFF_Q = FOURIER_WIDTH
OFF_K = OFF_Q + ATTN_WIDTH
OFF_V = OFF_K + KV_WIDTH
OFF_G = OFF_V + KV_WIDTH
IN_WIDTH = OFF_G + 2 * D_MODEL
N_EXPERTS = 8
N_MOD = 6
EPS = 1e-6
NEG = -1e30
LOG2E = 1.0 / math.log(2.0)

LANES = 128
SUBLANES = 8

TM = 256
QB = 128
FFT_R = SEQ // GRID_W
FFT_C = GRID_W
TME = 512
HALVES = 2
TMP = HALVES * TME
TD = 512
SB = 2 * TD + N_EXPERTS * SUBLANES
XS_W = D_MODEL + LANES
SEG_BITS = TD.bit_length() - 3
TAIL_BITS = TMP.bit_length() - 4
FC_E = 512
DC_E = 256
VMEM_LIMIT = 56 * 1024 * 1024


def _cparams(sem, vmem=None):
    return pltpu.CompilerParams(dimension_semantics=sem, vmem_limit_bytes=vmem)


def _dot(a, b):
    return jnp.dot(a, b, preferred_element_type=F32)


def _sigmoid(x):
    return 1.0 / (1.0 + jnp.exp(-x))


def _rms_mod(x, g, sh, sc):
    r = lax.rsqrt(jnp.mean(x * x, axis=-1, keepdims=True) + EPS)
    return (x * r) * g * (1.0 + sc) + sh


def _dft_cs(n):
    k = np.arange(n, dtype=np.float64)
    a = 2.0 * np.pi * np.outer(k, k) / n
    return np.cos(a), np.sin(a)


@functools.lru_cache(maxsize=None)
def _tables():
    c128, s128 = _dft_cs(FFT_R)
    c64, s64 = _dft_cs(FFT_C)
    c256, s256 = _dft_cs(CTX_LEN)
    wa = np.concatenate([c128, -s128], axis=0)
    k1 = np.arange(FFT_R, dtype=np.float64)[:, None]
    cc = np.arange(FFT_C, dtype=np.float64)[None, :]
    ang = 2.0 * np.pi * k1 * cc / SEQ
    twc = np.broadcast_to(np.cos(ang)[:, :, None], (FFT_R, FFT_C, LANES))
    tws = np.broadcast_to(np.sin(ang)[:, :, None], (FFT_R, FFT_C, LANES))
    mc = np.block([[c64, s64], [-s64, c64]])
    cg, sg = _dft_cs(GROUP_W)
    mch = np.concatenate([cg, sg], axis=0) / math.sqrt(SEQ * GROUP_W)
    wcx = np.concatenate([c256, s256], axis=0)
    mchc = np.concatenate([cg, -sg], axis=0)
    n_freq = HEAD_DIM // 4
    inv = ROPE_BASE ** (-np.arange(n_freq, dtype=np.float64) / n_freq)
    t = np.arange(SEQ)
    rows = (t // GRID_W).astype(np.float64)[:, None] * inv
    cols = (t % GRID_W).astype(np.float64)[:, None] * inv
    cos_h = np.concatenate([np.cos(rows), np.cos(rows), np.cos(cols), np.cos(cols)], axis=1)
    sin_h = np.concatenate([-np.sin(rows), np.sin(rows), -np.sin(cols), np.sin(cols)], axis=1)
    cos_t = np.concatenate([cos_h, cos_h], axis=1)
    sin_t = np.concatenate([sin_h, sin_h], axis=1)
    f = lambda a: np.ascontiguousarray(a, dtype=np.float32)
    return dict(wa=f(wa), twc=f(twc), tws=f(tws), mc=f(mc), mch=f(mch), wcx=f(wcx), mchc=f(mchc),
                cos=f(cos_t), sin=f(sin_t),
                cos1=np.ones((TM, LANES), np.float32), sin0=np.zeros((TM, LANES), np.float32))


def _mod_body(cv_ref, w_ref, b_ref, o_ref):
    c = cv_ref[...]
    s = c * _sigmoid(c)
    o_ref[0, 0] = jnp.dot(s, w_ref[0], precision=lax.Precision.HIGHEST,
                          preferred_element_type=F32) + b_ref[0]


def _modulation(cv, w_mod, b_mod):
    out = pl.pallas_call(
        _mod_body,
        grid=(DEPTH, N_MOD),
        in_specs=[pl.BlockSpec((SUBLANES, D_MODEL), lambda l, k: (0, 0)),
                  pl.BlockSpec((1, D_MODEL, D_MODEL), lambda l, k: (l, 0, k)),
                  pl.BlockSpec((1, 1, D_MODEL), lambda l, k: (l * N_MOD + k, 0, 0))],
        out_specs=pl.BlockSpec((1, 1, SUBLANES, D_MODEL), lambda l, k: (l, k, 0, 0)),
        out_shape=jax.ShapeDtypeStruct((DEPTH, N_MOD, SUBLANES, D_MODEL), F32),
        compiler_params=_cparams(("arbitrary", "arbitrary")),
        name="modulation",
    )(cv, w_mod, b_mod.reshape(DEPTH * N_MOD, 1, D_MODEL))
    out = jnp.transpose(out[:, :, :3, :], (0, 2, 1, 3))
    return out.reshape(DEPTH * 3 * N_MOD, 1, D_MODEL)


def _mod_spec(layer, k, cond_fn):
    return pl.BlockSpec((1, 1, D_MODEL),
                        lambda i, *_: ((layer * 3 + cond_fn(i)) * N_MOD + k, 0, 0))


def _in_proj_body(x_ref, sh_ref, sc_ref, g_ref, w_ref, cos_ref, sin_ref,
                  u_ref, q_ref, kv_ref, gt_ref):
    hb = _rms_mod(x_ref[...], g_ref[...], sh_ref[0], sc_ref[0]).astype(BF16)
    u_ref[...] = _dot(hb, w_ref[:, 0:OFF_Q])
    cos = cos_ref[...]
    sin = sin_ref[...]
    lane = lax.broadcasted_iota(jnp.int32, cos.shape, 1)
    first_half = (lane % (HEAD_DIM // 2)) < (HEAD_DIM // 4)
    quarter = HEAD_DIM // 4

    def rope(xs):
        below = pltpu.roll(xs, quarter, 1)
        above = pltpu.roll(xs, LANES - quarter, 1)
        return xs * cos + jnp.where(first_half, above, below) * sin

    qk = _dot(hb, w_ref[:, OFF_Q:OFF_V])
    scale = HEAD_DIM ** -0.5 * LOG2E
    for j in range(ATTN_WIDTH // LANES):
        sl = slice(j * LANES, (j + 1) * LANES)
        q_ref[:, sl] = (rope(qk[:, sl]) * scale).astype(BF16)
    kv_ref[:, 0:KV_WIDTH] = rope(qk[:, ATTN_WIDTH:ATTN_WIDTH + KV_WIDTH]).astype(BF16)
    kv_ref[:, KV_WIDTH:2 * KV_WIDTH] = _dot(hb, w_ref[:, OFF_V:OFF_G]).astype(BF16)
    gc = 512
    for j in range(2 * D_MODEL // gc):
        z = _dot(hb, w_ref[:, OFF_G + j * gc:OFF_G + (j + 1) * gc])
        gt_ref[:, j * gc:(j + 1) * gc] = _sigmoid(z).astype(BF16)


def _in_proj(x, mods, layer, cond_fn, g, w_bf, cos, sin, rope_idx):
    rows = x.shape[0]
    return pl.pallas_call(
        _in_proj_body,
        grid=(rows // TM,),
        in_specs=[pl.BlockSpec((TM, D_MODEL), lambda i: (i, 0)),
                  _mod_spec(layer, 0, cond_fn),
                  _mod_spec(layer, 1, cond_fn),
                  pl.BlockSpec((1, D_MODEL), lambda i: (0, 0)),
                  pl.BlockSpec((D_MODEL, IN_WIDTH), lambda i: (0, 0)),
                  pl.BlockSpec((TM, LANES), lambda i: (rope_idx(i), 0)),
                  pl.BlockSpec((TM, LANES), lambda i: (rope_idx(i), 0))],
        out_specs=[pl.BlockSpec((TM, FOURIER_WIDTH), lambda i: (i, 0)),
                   pl.BlockSpec((TM, ATTN_WIDTH), lambda i: (i, 0)),
                   pl.BlockSpec((TM, 2 * KV_WIDTH), lambda i: (i, 0)),
                   pl.BlockSpec((TM, 2 * D_MODEL), lambda i: (i, 0))],
        out_shape=[jax.ShapeDtypeStruct((rows, FOURIER_WIDTH), F32),
                   jax.ShapeDtypeStruct((rows, ATTN_WIDTH), BF16),
                   jax.ShapeDtypeStruct((rows, 2 * KV_WIDTH), BF16),
                   jax.ShapeDtypeStruct((rows, 2 * D_MODEL), BF16)],
        compiler_params=_cparams(("arbitrary",), VMEM_LIMIT),
        name="in_proj",
    )(x, mods, mods, g, w_bf, cos, sin)


FFT_CB = 8


def _fft_a_body(x_ref, w_ref, o_ref):
    w = w_ref[...]
    xt = jnp.swapaxes(x_ref[...], 0, 1)
    res = jnp.stack([_dot(w, xt[c].astype(BF16)) for c in range(FFT_CB)])
    o_ref[0] = jnp.swapaxes(res, 0, 1)


def _fft_c_body(re_ref, im_ref, tc_ref, ts_ref, mc_ref, mch_ref, o_ref):
    mc = mc_ref[...]
    mch = mch_ref[...]
    xrs, xis = [], []
    for j in range(FFT_CB):
        ar = re_ref[0, j]
        ai = im_ref[0, j]
        tc = jnp.concatenate([tc_ref[j]] * N_GROUPS, axis=1)
        ts = jnp.concatenate([ts_ref[j]] * N_GROUPS, axis=1)
        br = ar * tc + ai * ts
        bi = ai * tc - ar * ts
        x = _dot(mc, jnp.concatenate([br, bi], axis=0).astype(BF16))
        xrs.append(x[:FFT_C])
        xis.append(x[FFT_C:])
    xr = jnp.concatenate(xrs, axis=0).astype(BF16)
    xi = jnp.concatenate(xis, axis=0).astype(BF16)
    ys = []
    for g in range(N_GROUPS):
        sl = slice(g * GROUP_W, (g + 1) * GROUP_W)
        ys.append(_dot(jnp.concatenate([xr[:, sl], xi[:, sl]], axis=1), mch))
    y = jnp.concatenate(ys, axis=1).reshape(FFT_CB, FFT_C, FOURIER_WIDTH)
    o_ref[0] = jnp.swapaxes(y, 0, 1)


def _fourier_latent(u, tb, batch):
    u3 = u.reshape(batch * FFT_R, FFT_C, FOURIER_WIDTH)
    a = pl.pallas_call(
        _fft_a_body,
        grid=(batch, FFT_C // FFT_CB),
        in_specs=[pl.BlockSpec((FFT_R, FFT_CB, FOURIER_WIDTH), lambda b, j: (b, j, 0)),
                  pl.BlockSpec((2 * FFT_R, FFT_R), lambda b, j: (0, 0))],
        out_specs=pl.BlockSpec((1, 2 * FFT_R, FFT_CB, FOURIER_WIDTH), lambda b, j: (b, 0, j, 0)),
        out_shape=jax.ShapeDtypeStruct((batch, 2 * FFT_R, FFT_C, FOURIER_WIDTH), F32),
        compiler_params=_cparams(("arbitrary", "arbitrary"), VMEM_LIMIT),
        name="fft_rows",
    )(u3, tb["wa"].astype(BF16))
    nk = FFT_R // FFT_CB
    y = pl.pallas_call(
        _fft_c_body,
        grid=(batch, nk),
        in_specs=[pl.BlockSpec((1, FFT_CB, FFT_C, FOURIER_WIDTH), lambda b, k: (b, k, 0, 0)),
                  pl.BlockSpec((1, FFT_CB, FFT_C, FOURIER_WIDTH), lambda b, k: (b, nk + k, 0, 0)),
                  pl.BlockSpec((FFT_CB, FFT_C, LANES), lambda b, k: (k, 0, 0)),
                  pl.BlockSpec((FFT_CB, FFT_C, LANES), lambda b, k: (k, 0, 0)),
                  pl.BlockSpec((2 * FFT_C, 2 * FFT_C), lambda b, k: (0, 0)),
                  pl.BlockSpec((2 * GROUP_W, GROUP_W), lambda b, k: (0, 0))],
        out_specs=pl.BlockSpec((1, FFT_C, FFT_CB, FOURIER_WIDTH), lambda b, k: (b, 0, k, 0)),
        out_shape=jax.ShapeDtypeStruct((batch, FFT_C, FFT_R, FOURIER_WIDTH), F32),
        compiler_params=_cparams(("arbitrary", "arbitrary"), VMEM_LIMIT),
        name="fft_cols",
    )(a, a, tb["twc"], tb["tws"], tb["mc"].astype(BF16), tb["mch"].astype(BF16))
    return y.reshape(batch * SEQ, FOURIER_WIDTH)


def _fourier_ctx_body(u_ref, w_ref, m_ref, o_ref):
    pq = _dot(w_ref[...], u_ref[...].astype(BF16))
    p = pq[:CTX_LEN]
    q = pq[CTX_LEN:]
    m = m_ref[...]
    scale = 1.0 / math.sqrt(CTX_LEN * GROUP_W)
    for g in range(N_GROUPS):
        sl = slice(g * GROUP_W, (g + 1) * GROUP_W)
        lhs = jnp.concatenate([p[:, sl], q[:, sl]], axis=1).astype(BF16)
        o_ref[:, sl] = _dot(lhs, m) * scale


def _fourier_ctx(u, tb, batch):
    return pl.pallas_call(
        _fourier_ctx_body,
        grid=(batch,),
        in_specs=[pl.BlockSpec((CTX_LEN, FOURIER_WIDTH), lambda b: (b, 0)),
                  pl.BlockSpec((2 * CTX_LEN, CTX_LEN), lambda b: (0, 0)),
                  pl.BlockSpec((2 * GROUP_W, GROUP_W), lambda b: (0, 0))],
        out_specs=pl.BlockSpec((CTX_LEN, FOURIER_WIDTH), lambda b: (b, 0)),
        out_shape=jax.ShapeDtypeStruct((batch * CTX_LEN, FOURIER_WIDTH), F32),
        compiler_params=_cparams(("arbitrary",)),
        name="fft_ctx",
    )(u, tb["wcx"].astype(BF16), tb["mchc"].astype(BF16))


def _attn_body(sink_ref, q_ref, *refs, local):
    o_ref = refs[-1]
    if local:
        kp_ref, kc_ref, kn_ref, kx_ref = refs[:-1]
        groups = [jnp.concatenate([kp_ref[...], kn_ref[...]], axis=0),
                  jnp.concatenate([kc_ref[...], kx_ref[...]], axis=0)]
        n = pl.program_id(1)
        last = pl.num_programs(1) - 1
        qi = lax.broadcasted_iota(jnp.int32, (2 * QB, 2 * QB), 0) % QB
        kj = lax.broadcasted_iota(jnp.int32, (2 * QB, 2 * QB), 1)
        prev_ok = (kj < QB) & (kj >= qi + jnp.where(n == 0, 2 * QB, 0))
        next_ok = (kj >= QB) & (kj - QB <= qi - jnp.where(n == last, 2 * QB, 0))
        masks = [prev_ok | next_ok, None]
    else:
        groups = [refs[0][...]]
        masks = [None]
    lo_q = lax.broadcasted_iota(jnp.int32, (2 * QB, LANES), 1) < HEAD_DIM
    kmats, vmats = [], []
    for kv in groups:
        kv = kv.astype(F32)
        kcat = kv[:, :KV_WIDTH]
        vcat = kv[:, KV_WIDTH:]
        kroll = pltpu.roll(kcat, HEAD_DIM, 1)
        vroll = pltpu.roll(vcat, HEAD_DIM, 1)
        lo_k = lax.broadcasted_iota(jnp.int32, kcat.shape, 1) < HEAD_DIM
        zero = jnp.zeros_like(kcat)
        one = jnp.ones_like(vcat)
        km, vm = [], []
        for hk in range(N_KV_HEADS):
            k_own, k_other = (kcat, kroll) if hk == 0 else (kroll, kcat)
            v_own, v_other = (vcat, vroll) if hk == 0 else (vroll, vcat)
            km.append((jnp.where(lo_k, k_own, zero).astype(BF16), jnp.where(lo_k, zero, k_other).astype(BF16)))
            vm.append((jnp.where(lo_k, v_own, one).astype(BF16), jnp.where(lo_k, one, v_other).astype(BF16)))
        kmats.append(km)
        vmats.append(vm)
    pairs = [(hk, half) for hk in range(N_KV_HEADS) for half in range(2)]
    first_slab = lax.broadcasted_iota(jnp.int32, (2 * QB, 1), 0) < QB
    scores = []
    for hk, half in pairs:
        qs = jnp.concatenate([q_ref[:, (2 * hk) * LANES:(2 * hk + 1) * LANES],
                              q_ref[:, (2 * hk + 1) * LANES:(2 * hk + 2) * LANES]], axis=0)
        parts = []
        for km, mask in zip(kmats, masks):
            s = lax.dot_general(qs, km[hk][half], (((1,), (1,)), ((), ())), preferred_element_type=F32)
            parts.append(s if mask is None else jnp.where(mask, s, NEG))
        scores.append(parts)
    sinks = [jnp.where(first_slab, sink_ref[4 * hk + half], sink_ref[4 * hk + 2 + half]) * LOG2E
             for hk, half in pairs]
    maxes = []
    for parts, sk in zip(scores, sinks):
        m = sk
        for s in parts:
            m = jnp.maximum(m, jnp.max(s, axis=1, keepdims=True))
        maxes.append(m)
    sink_p = [jnp.exp2(sk - m) for sk, m in zip(sinks, maxes)]
    pvs = []
    for parts, m, (hk, half) in zip(scores, maxes, pairs):
        pv = None
        for s, vm in zip(parts, vmats):
            term = _dot(jnp.exp2(s - m).astype(BF16), vm[hk][half])
            pv = term if pv is None else pv + term
        pvs.append(pv)
    for hk in range(N_KV_HEADS):
        lo, hi = pvs[2 * hk], pvs[2 * hk + 1]
        num = jnp.where(lo_q, lo, hi)
        den = (pltpu.roll(jnp.where(lo_q, hi, lo), HEAD_DIM, 1)
               + jnp.where(lo_q, sink_p[2 * hk], sink_p[2 * hk + 1]))
        out = (num / den).astype(BF16)
        o_ref[:, (2 * hk) * LANES:(2 * hk + 1) * LANES] = out[:QB]
        o_ref[:, (2 * hk + 1) * LANES:(2 * hk + 2) * LANES] = out[QB:]


def _attention_latent(q, kv, kv_ctx, sink, batch):
    nb = SEQ // QB
    kvw = 2 * KV_WIDTH
    return pl.pallas_call(
        functools.partial(_attn_body, local=True),
        grid=(batch, nb),
        in_specs=[pl.BlockSpec(memory_space=pltpu.SMEM),
                  pl.BlockSpec((QB, ATTN_WIDTH), lambda b, n: (b * nb + n, 0)),
                  pl.BlockSpec((QB, kvw), lambda b, n: (b * nb + jnp.maximum(n - 1, 0), 0)),
                  pl.BlockSpec((QB, kvw), lambda b, n: (b * nb + n, 0)),
                  pl.BlockSpec((QB, kvw), lambda b, n: (b * nb + jnp.minimum(n + 1, nb - 1), 0)),
                  pl.BlockSpec((CTX_LEN, kvw), lambda b, n: (b, 0))],
        out_specs=pl.BlockSpec((QB, ATTN_WIDTH), lambda b, n: (b * nb + n, 0)),
        out_shape=jax.ShapeDtypeStruct((batch * SEQ, ATTN_WIDTH), BF16),
        compiler_params=_cparams(("arbitrary", "arbitrary")),
        name="attn_latent",
    )(sink, q, kv, kv, kv, kv_ctx)


def _attention_ctx(q, kv, sink, batch):
    nb = CTX_LEN // QB
    return pl.pallas_call(
        functools.partial(_attn_body, local=False),
        grid=(batch, nb),
        in_specs=[pl.BlockSpec(memory_space=pltpu.SMEM),
                  pl.BlockSpec((QB, ATTN_WIDTH), lambda b, n: (b * nb + n, 0)),
                  pl.BlockSpec((CTX_LEN, 2 * KV_WIDTH), lambda b, n: (b, 0))],
        out_specs=pl.BlockSpec((QB, ATTN_WIDTH), lambda b, n: (b * nb + n, 0)),
        out_shape=jax.ShapeDtypeStruct((batch * CTX_LEN, ATTN_WIDTH), BF16),
        compiler_params=_cparams(("arbitrary", "arbitrary")),
        name="attn_ctx",
    )(sink, q, kv)


def _merge_body(x_ref, fm_ref, ao_ref, gt_ref, wf_ref, wa_ref, wo_ref, ga_ref, *refs, with_h2):
    gt = gt_ref[...].astype(F32)
    y = (gt[:, :D_MODEL] * _dot(fm_ref[...].astype(BF16), wf_ref[...])
         + gt[:, D_MODEL:] * _dot(ao_ref[...], wa_ref[...]))
    xn = x_ref[...] + ga_ref[0] * _dot(y.astype(BF16), wo_ref[...])
    if with_h2:
        sh_ref, sc_ref, g_ref, xo_ref, h_ref = refs
        h_ref[...] = _rms_mod(xn, g_ref[...], sh_ref[0], sc_ref[0]).astype(BF16)
    else:
        (xo_ref,) = refs
    xo_ref[...] = xn


def _merge(x, fm, ao, gates, wf, wa, wo, mods, layer, cond_fn, g2, with_h2):
    rows = x.shape[0]
    row_spec = lambda w: pl.BlockSpec((TM, w), lambda i: (i, 0))
    full = lambda a: pl.BlockSpec(a.shape, lambda i: (0, 0))
    in_specs = [row_spec(D_MODEL), row_spec(FOURIER_WIDTH), row_spec(ATTN_WIDTH), row_spec(2 * D_MODEL),
                full(wf), full(wa), full(wo), _mod_spec(layer, 2, cond_fn)]
    args = [x, fm, ao, gates, wf, wa, wo, mods]
    out_specs = [row_spec(D_MODEL)]
    out_shape = [jax.ShapeDtypeStruct((rows, D_MODEL), F32)]
    if with_h2:
        in_specs += [_mod_spec(layer, 3, cond_fn), _mod_spec(layer, 4, cond_fn),
                     pl.BlockSpec((1, D_MODEL), lambda i: (0, 0))]
        args += [mods, mods, g2]
        out_specs.append(row_spec(D_MODEL))
        out_shape.append(jax.ShapeDtypeStruct((rows, D_MODEL), BF16))
    return pl.pallas_call(
        functools.partial(_merge_body, with_h2=with_h2),
        grid=(rows // TM,),
        in_specs=in_specs, out_specs=out_specs, out_shape=out_shape,
        compiler_params=_cparams(("arbitrary",), VMEM_LIMIT),
        name="merge",
    )(*args)


def _ffn_body(x_ref, h_ref, wg_ref, wu_ref, wd_ref, ga_ref, o_ref):
    h = h_ref[...]
    g = _dot(h, wg_ref[...])
    u = _dot(h, wu_ref[...])
    a = (g * _sigmoid(g) * u).astype(BF16)
    o_ref[...] = x_ref[...] + ga_ref[0] * _dot(a, wd_ref[...])


def _ffn_dense(x, h2, wg, wu, wd, mods, layer, cond_fn):
    rows = x.shape[0]
    resident = lambda a: pl.BlockSpec(a.shape, lambda i: (0, 0), pipeline_mode=pl.Buffered(1))
    return pl.pallas_call(
        _ffn_body,
        grid=(rows // TM,),
        in_specs=[pl.BlockSpec((TM, D_MODEL), lambda i: (i, 0)),
                  pl.BlockSpec((TM, D_MODEL), lambda i: (i, 0)),
                  resident(wg), resident(wu), resident(wd),
                  _mod_spec(layer, 5, cond_fn)],
        out_specs=pl.BlockSpec((TM, D_MODEL), lambda i: (i, 0)),
        out_shape=jax.ShapeDtypeStruct((rows, D_MODEL), F32),
        compiler_params=_cparams(("arbitrary",), VMEM_LIMIT),
        name="ffn_dense",
    )(x, h2, wg, wu, wd, mods)


INFO_E0, INFO_E1, INFO_R0, INFO_R1, INFO_W0, INFO_W1 = range(6)


def _router_body(x_ref, sh_ref, sc_ref, g_ref, wr_ref, hb_ref, info_ref, cnt_ref):
    h = _rms_mod(x_ref[...], g_ref[...], sh_ref[0], sc_ref[0])
    hb_ref[...] = h.astype(BF16)
    logits = jnp.dot(h, wr_ref[...], precision=lax.Precision.HIGHEST, preferred_element_type=F32)
    lane = lax.broadcasted_iota(jnp.int32, logits.shape, 1)
    neg_inf = jnp.float32(-jnp.inf)
    lg = jnp.where(lane < N_EXPERTS, logits, neg_inf)
    v0 = jnp.max(lg, axis=1, keepdims=True)
    i0 = jnp.min(jnp.where(lg == v0, lane, LANES), axis=1, keepdims=True)
    oh0 = lane == i0
    lg1 = jnp.where(oh0, neg_inf, lg)
    v1 = jnp.max(lg1, axis=1, keepdims=True)
    i1 = jnp.min(jnp.where(lg1 == v1, lane, LANES), axis=1, keepdims=True)
    oh1 = lane == i1
    e = jnp.exp(v1 - v0)
    w0 = 1.0 / (1.0 + e)
    w1 = e / (1.0 + e)
    oh = jnp.where(oh0 | oh1, 1.0, 0.0)
    row = lax.broadcasted_iota(jnp.int32, (TD, TD), 0)
    col = lax.broadcasted_iota(jnp.int32, (TD, TD), 1)
    tri = jnp.where(row > col, 1.0, 0.0).astype(BF16)
    before = _dot(tri, oh.astype(BF16))
    r0 = jnp.sum(jnp.where(oh0, before, 0.0), axis=1, keepdims=True)
    r1 = jnp.sum(jnp.where(oh1, before, 0.0), axis=1, keepdims=True)
    cnt_ref[0] = jnp.broadcast_to(jnp.sum(oh, axis=0, keepdims=True), (SUBLANES, LANES))
    info = jnp.zeros(logits.shape, F32)
    for idx, val in ((INFO_E0, i0.astype(F32)), (INFO_E1, i1.astype(F32)), (INFO_R0, r0),
                     (INFO_R1, r1), (INFO_W0, w0), (INFO_W1, w1)):
        info = jnp.where(lane == idx, val, info)
    info_ref[...] = info


def _router(x, mods, layer, cond_fn, g2, wr_pad):
    rows = x.shape[0]
    return pl.pallas_call(
        _router_body,
        grid=(rows // TD,),
        in_specs=[pl.BlockSpec((TD, D_MODEL), lambda i: (i, 0)),
                  _mod_spec(layer, 3, cond_fn), _mod_spec(layer, 4, cond_fn),
                  pl.BlockSpec((1, D_MODEL), lambda i: (0, 0)),
                  pl.BlockSpec((D_MODEL, LANES), lambda i: (0, 0))],
        out_specs=[pl.BlockSpec((TD, D_MODEL), lambda i: (i, 0)),
                   pl.BlockSpec((TD, LANES), lambda i: (i, 0)),
                   pl.BlockSpec((1, SUBLANES, LANES), lambda i: (i, 0, 0))],
        out_shape=[jax.ShapeDtypeStruct((rows, D_MODEL), BF16),
                   jax.ShapeDtypeStruct((rows, LANES), F32),
                   jax.ShapeDtypeStruct((rows // TD, SUBLANES, LANES), F32)],
        compiler_params=_cparams(("arbitrary",), VMEM_LIMIT),
        name="router",
    )(x, mods, mods, g2, wr_pad)


def _segment_copies(src_ref, src_off, dst_ref, dst_off, len8, sem, bits, wait):
    for k in reversed(range(bits)):
        size = SUBLANES << k
        done = ((len8 >> (k + 1)) << (k + 1)) * SUBLANES

        @pl.when(((len8 >> k) & 1) == 1)
        def _(size=size, done=done):
            cp = pltpu.make_async_copy(
                src_ref.at[pl.ds(pl.multiple_of(src_off + done, SUBLANES), size)],
                dst_ref.at[pl.ds(pl.multiple_of(dst_off + done, SUBLANES), size)], sem)
            if wait:
                cp.wait()
            else:
                cp.start()


def _slot_positions(e0, e1, r0, r1, seg_ref, base):
    pos0, pos1 = r0, r1
    for e in range(N_EXPERTS):
        start = seg_ref[base + e].astype(F32)
        pos0 = pos0 + jnp.where(e0 == e, start, 0.0)
        pos1 = pos1 + jnp.where(e1 == e, start, 0.0)
    return pos0.astype(jnp.int32), pos1.astype(jnp.int32)


def _dispatch_body(seg_ref, dst_ref, len_ref, tdst_ref, tlen_ref, nt_ref, hb_ref, info_ref, xs_ref,
                   buf_ref, zero_ref, sems, zero_sem, *, min_tiles):
    i = pl.program_id(0)
    last = pl.num_programs(0) - 1
    base = i * N_EXPERTS
    cur = i % 2

    def scatter(tile, slot, wait):
        for e in range(N_EXPERTS):
            k = tile * N_EXPERTS + e
            _segment_copies(buf_ref.at[slot], seg_ref[k], xs_ref, dst_ref[k], len_ref[k],
                            sems.at[slot], SEG_BITS, wait=wait)

    info_t = info_ref[...].T
    row = lambda k: info_t[k:k + 1, :]
    pos0, pos1 = _slot_positions(row(INFO_E0), row(INFO_E1), row(INFO_R0), row(INFO_R1), seg_ref, base)
    slot = lax.broadcasted_iota(jnp.int32, (SB, TD), 0)
    p0 = slot == pos0
    p1 = slot == pos1
    perm = jnp.where(p0 | p1, 1.0, 0.0).astype(BF16)
    buf_ref[cur, :, :D_MODEL] = _dot(perm, hb_ref[...])
    wsel = jnp.where(p0, row(INFO_W0), 0.0) + jnp.where(p1, row(INFO_W1), 0.0)
    buf_ref[cur, :, D_MODEL:] = jnp.broadcast_to(jnp.sum(wsel, axis=1, keepdims=True), (SB, LANES))
    scatter(i, cur, wait=False)

    @pl.when(i > 0)
    def _():
        scatter(i - 1, 1 - cur, wait=True)

    @pl.when(i == last)
    def _():
        zero_ref[...] = jnp.zeros_like(zero_ref)
        spare_tiles = range(min_tiles, xs_ref.shape[0] // TMP)
        spare = [pltpu.make_async_copy(zero_ref, xs_ref.at[pl.ds(j * TMP, TMP)], zero_sem) for j in spare_tiles]
        for wait in (False, True):
            for e in range(N_EXPERTS):
                _segment_copies(zero_ref, 0, xs_ref, tdst_ref[e], tlen_ref[e], zero_sem, TAIL_BITS, wait=wait)
            for j, cp in zip(spare_tiles, spare):
                @pl.when(j >= nt_ref[0])
                def _(cp=cp, wait=wait):
                    if wait:
                        cp.wait()
                    else:
                        cp.start()
        scatter(i, cur, wait=True)


def _dispatch(seg, dst, len8, tail_dst, tail_len8, n_tiles, hb, info, m_rows):
    rows = hb.shape[0]
    return pl.pallas_call(
        functools.partial(_dispatch_body, min_tiles=2 * rows // TMP),
        grid_spec=pltpu.PrefetchScalarGridSpec(
            num_scalar_prefetch=6,
            grid=(rows // TD,),
            in_specs=[pl.BlockSpec((TD, D_MODEL), lambda i, *_: (i, 0)),
                      pl.BlockSpec((TD, LANES), lambda i, *_: (i, 0))],
            out_specs=pl.BlockSpec(memory_space=pl.ANY),
            scratch_shapes=[pltpu.VMEM((2, SB, XS_W), F32), pltpu.VMEM((TMP, XS_W), F32),
                            pltpu.SemaphoreType.DMA((2,)), pltpu.SemaphoreType.DMA(())]),
        out_shape=jax.ShapeDtypeStruct((m_rows, XS_W), F32),
        compiler_params=_cparams(("arbitrary",), VMEM_LIMIT),
        name="dispatch",
    )(seg, dst, len8, tail_dst, tail_len8, n_tiles, hb, info)


def _expert_body(te_ref, nh_ref, nt_ref, xs_ref, wg_ref, wu_ref, wd_ref, ys_ref,
                 xb_ref, a_ref, wgb_ref, wub_ref, wdb_ref, *, nf):
    j = pl.program_id(0)
    s = pl.program_id(1)
    halves = nh_ref[j]
    up_phase = s < nf

    @pl.when((halves > 0) & up_phase)
    def _():
        wgb_ref[...] = wg_ref[0].astype(BF16)
        wub_ref[...] = wu_ref[0].astype(BF16)

    @pl.when((halves > 0) & jnp.logical_not(up_phase))
    def _():
        wdb_ref[...] = wd_ref[0].astype(BF16)

    for h in range(HALVES):
        rows = slice(h * TME, (h + 1) * TME)

        @pl.when((halves > h) & (s == 0))
        def _(rows=rows):
            xb_ref[rows, :] = xs_ref[rows, :D_MODEL].astype(BF16)

        @pl.when((halves > h) & up_phase)
        def _(rows=rows):
            xb = xb_ref[rows, :]
            g = _dot(xb, wgb_ref[...])
            u = _dot(xb, wub_ref[...])
            a_ref[s, rows, :] = (g * _sigmoid(g) * u).astype(BF16)

        @pl.when((halves > h) & jnp.logical_not(up_phase))
        def _(rows=rows):
            y = _dot(a_ref[0, rows, :], wdb_ref[0:FC_E, :])
            for c in range(1, nf):
                y += _dot(a_ref[c, rows, :], wdb_ref[c * FC_E:(c + 1) * FC_E, :])
            ys_ref[rows, :] = y * xs_ref[rows, D_MODEL:D_MODEL + 1]

        @pl.when((halves <= h) & jnp.logical_not(up_phase))
        def _(rows=rows):
            ys_ref[rows, :] = jnp.zeros((TME, DC_E), F32)


def _experts(tile_expert, tile_halves, n_tiles, xs, wg, wu, wd):
    m_rows = xs.shape[0]
    d_ff = wg.shape[2]
    nf = d_ff // FC_E
    nd = D_MODEL // DC_E

    def live(j, nt, idx, parked):
        return jnp.where(j < nt[0], idx, parked)

    def j_idx(j, nt):
        return jnp.minimum(j, nt[0] - 1)

    up_idx = lambda j, s, te, nh, nt: (te[j], 0, live(j, nt, jnp.minimum(s, nf - 1), nf - 1))
    down_idx = lambda j, s, te, nh, nt: (te[j], 0, live(j, nt, jnp.maximum(s - nf, 0), nd - 1))
    return pl.pallas_call(
        functools.partial(_expert_body, nf=nf),
        grid_spec=pltpu.PrefetchScalarGridSpec(
            num_scalar_prefetch=3,
            grid=(m_rows // TMP, nf + nd),
            in_specs=[pl.BlockSpec((TMP, XS_W), lambda j, s, te, nh, nt: (j_idx(j, nt), 0)),
                      pl.BlockSpec((1, D_MODEL, FC_E), up_idx),
                      pl.BlockSpec((1, D_MODEL, FC_E), up_idx),
                      pl.BlockSpec((1, d_ff, DC_E), down_idx)],
            out_specs=pl.BlockSpec((TMP, DC_E), lambda j, s, te, nh, nt: (j, jnp.maximum(s - nf, 0))),
            scratch_shapes=[pltpu.VMEM((TMP, D_MODEL), BF16), pltpu.VMEM((nf, TMP, FC_E), BF16),
                            pltpu.VMEM((D_MODEL, FC_E), BF16), pltpu.VMEM((D_MODEL, FC_E), BF16),
                            pltpu.VMEM((d_ff, DC_E), BF16)]),
        out_shape=jax.ShapeDtypeStruct((m_rows, D_MODEL), F32),
        compiler_params=_cparams(("arbitrary", "arbitrary"), VMEM_LIMIT),
        name="experts",
    )(tile_expert, tile_halves, n_tiles, xs, wg, wu, wd)


def _combine_body(seg_ref, dst_ref, len_ref, ys_ref, x_ref, info_ref, ga_ref, g_ref, o_ref, buf_ref, sems):
    i = pl.program_id(0)
    base = i * N_EXPERTS
    cur = i % 2

    def gather(tile, slot, wait):
        for e in range(N_EXPERTS):
            k = tile * N_EXPERTS + e
            _segment_copies(ys_ref, dst_ref[k], buf_ref.at[slot], seg_ref[k], len_ref[k],
                            sems.at[slot], SEG_BITS, wait=wait)

    @pl.when(i == 0)
    def _():
        buf_ref[...] = jnp.zeros_like(buf_ref)
        gather(0, 0, wait=False)

    @pl.when(i + 1 < pl.num_programs(0))
    def _():
        gather(i + 1, 1 - cur, wait=False)

    info = info_ref[...]
    col = lambda k: info[:, k:k + 1]
    pos0, pos1 = _slot_positions(col(INFO_E0), col(INFO_E1), col(INFO_R0), col(INFO_R1), seg_ref, base)
    slot = lax.broadcasted_iota(jnp.int32, (TD, SB), 1)
    unperm = jnp.where((slot == pos0) | (slot == pos1), 1.0, 0.0).astype(BF16)
    gather(i, cur, wait=True)
    y = _dot(unperm, buf_ref[cur].astype(BF16))
    xn = x_ref[...] + ga_ref[0] * y
    r = lax.rsqrt(jnp.mean(xn * xn, axis=-1, keepdims=True) + EPS)
    o_ref[...] = (xn * r) * g_ref[...]


def _combine(seg, dst, len8, ys, x, info, mods, layer, cond_fn, final_g):
    rows = x.shape[0]
    return pl.pallas_call(
        _combine_body,
        grid_spec=pltpu.PrefetchScalarGridSpec(
            num_scalar_prefetch=3,
            grid=(rows // TD,),
            in_specs=[pl.BlockSpec(memory_space=pl.ANY),
                      pl.BlockSpec((TD, D_MODEL), lambda i, *_: (i, 0)),
                      pl.BlockSpec((TD, LANES), lambda i, *_: (i, 0)),
                      _mod_spec(layer, 5, cond_fn),
                      pl.BlockSpec((1, D_MODEL), lambda i, *_: (0, 0))],
            out_specs=pl.BlockSpec((TD, D_MODEL), lambda i, *_: (i, 0)),
            scratch_shapes=[pltpu.VMEM((2, SB, D_MODEL), F32), pltpu.SemaphoreType.DMA((2,))]),
        out_shape=jax.ShapeDtypeStruct((rows, D_MODEL), F32),
        compiler_params=_cparams(("arbitrary",), VMEM_LIMIT),
        name="combine",
    )(seg, dst, len8, ys, x, info, mods, final_g)


def _moe(x, mods, layer, cond_fn, g2, w_router, wg, wu, wd, final_g):
    rows = x.shape[0]
    n_tok_tiles = rows // TD
    wr_pad = jnp.pad(w_router, ((0, 0), (0, LANES - N_EXPERTS)))
    hb, info, cnt = _router(x, mods, layer, cond_fn, g2, wr_pad)
    counts = cnt[:, 0, :N_EXPERTS].astype(jnp.int32)
    seg_len = (counts + SUBLANES - 1) // SUBLANES * SUBLANES
    seg_start = jnp.cumsum(seg_len, axis=1) - seg_len
    group = jnp.sum(seg_len, axis=0)
    group_halves = (group + TME - 1) // TME
    group_tiles = (group + TMP - 1) // TMP
    tile_start = jnp.cumsum(group_tiles) - group_tiles
    group_base = tile_start * TMP
    dst = group_base[None, :] + jnp.cumsum(seg_len, axis=0) - seg_len
    tail_dst = group_base + group
    tail_len8 = (group_tiles * TMP - group) // SUBLANES
    max_tiles = (2 * rows + n_tok_tiles * N_EXPERTS * (SUBLANES - 1) + TMP - 1) // TMP + N_EXPERTS
    n_tiles = jnp.sum(group_tiles).astype(jnp.int32).reshape(1)
    tiles = jnp.arange(max_tiles, dtype=jnp.int32)
    tid = jnp.minimum(tiles, n_tiles[0] - 1)
    tile_expert = (jnp.sum(tid[:, None] >= tile_start[None, :], axis=1) - 1).astype(jnp.int32)
    in_group = tid - tile_start[tile_expert]
    tile_halves = jnp.clip(group_halves[tile_expert] - HALVES * in_group, 0, HALVES)
    tile_halves = jnp.where(tiles < n_tiles[0], tile_halves, 0).astype(jnp.int32)
    flat = lambda a: a.reshape(-1).astype(jnp.int32)
    seg, dst, len8 = flat(seg_start), flat(dst), flat(seg_len // SUBLANES)
    xs = _dispatch(seg, dst, len8, flat(tail_dst), flat(tail_len8), n_tiles, hb, info, max_tiles * TMP)
    ys = _experts(tile_expert, tile_halves, n_tiles, xs, wg, wu, wd)
    return _combine(seg, dst, len8, ys, x, info, mods, layer, cond_fn, final_g)


def kernel(x, c, ctx, c_ctx, w_mod, b_mod, norm1_g, norm2_g, w_in, sink, w_fourier, w_attn, w_out,
           w_gate_d, w_up_d, w_down_d, w_router, w_gate_e, w_up_e, w_down_e, final_g):
    batch, seq, d = x.shape
    assert (seq, d) == (SEQ, D_MODEL) and ctx.shape == (batch, CTX_LEN, D_MODEL)
    tb = {k: jnp.asarray(v) for k, v in _tables().items()}
    tiles_per_batch = SEQ // TM
    lat_cond = lambda i: i // tiles_per_batch
    ctx_cond = lambda i: 2
    lat_rope = lambda i: i % tiles_per_batch
    ctx_rope = lambda i: 0

    cv = jnp.zeros((SUBLANES, D_MODEL), F32).at[:batch].set(c).at[batch].set(c_ctx)
    mods = _modulation(cv, w_mod, b_mod)

    xl = x.reshape(batch * SEQ, D_MODEL)
    xc = ctx.reshape(batch * CTX_LEN, D_MODEL)
    out = None
    for l in range(DEPTH):
        last = l == DEPTH - 1
        g1 = norm1_g[l].reshape(1, D_MODEL)
        g2 = norm2_g[l].reshape(1, D_MODEL)
        w_in_b = w_in[l].astype(BF16)
        wf = w_fourier[l].astype(BF16)
        wa = w_attn[l].astype(BF16)
        wo = w_out[l].astype(BF16)
        u, q, kv, gates = _in_proj(xl, mods, l, lat_cond, g1, w_in_b, tb["cos"], tb["sin"], lat_rope)
        uc, qc, kvc, gates_c = _in_proj(xc, mods, l, ctx_cond, g1, w_in_b, tb["cos1"], tb["sin0"], ctx_rope)
        fm = _fourier_latent(u, tb, batch)
        ao = _attention_latent(q, kv, kvc, sink[l], batch)
        if l % 2 == 0:
            i = l // 2
            xl, h2 = _merge(xl, fm, ao, gates, wf, wa, wo, mods, l, lat_cond, g2, True)
            wg, wu, wd = w_gate_d[i].astype(BF16), w_up_d[i].astype(BF16), w_down_d[i].astype(BF16)
            xl = _ffn_dense(xl, h2, wg, wu, wd, mods, l, lat_cond)
            if not last:
                fmc = _fourier_ctx(uc, tb, batch)
                aoc = _attention_ctx(qc, kvc, sink[l], batch)
                xc, h2c = _merge(xc, fmc, aoc, gates_c, wf, wa, wo, mods, l, ctx_cond, g2, True)
                xc = _ffn_dense(xc, h2c, wg, wu, wd, mods, l, ctx_cond)
            if last:
                raise NotImplementedError("final norm is fused into the routed-expert combine")
        else:
            i = l // 2
            (xl,) = _merge(xl, fm, ao, gates, wf, wa, wo, mods, l, lat_cond, g2, False)
            if not last:
                raise NotImplementedError("context update through a routed-expert layer")
            out = _moe(xl, mods, l, lambda t: t // (SEQ // TD), g2, w_router[i], w_gate_e[i], w_up_e[i], w_down_e[i],
                       final_g.reshape(1, D_MODEL))
    return out.reshape(batch, SEQ, D_MODEL)
```

```python
import functools
import math

import numpy as np
import jax
import jax.numpy as jnp
from jax import lax
from jax.experimental import pallas as pl
from jax.experimental.pallas import tpu as pltpu

F32 = jnp.float32
BF16 = jnp.bfloat16

D_MODEL = 1024
SEQ = 8192
DEPTH = 2
GRID_W = 64
CTX_LEN = 256
N_GROUPS = 4
GROUP_W = 128
FOURIER_WIDTH = N_GROUPS * GROUP_W
N_Q_HEADS = 8
N_KV_HEADS = 2
HEAD_DIM = 64
ATTN_WIDTH = N_Q_HEADS * HEAD_DIM
KV_WIDTH = N_KV_HEADS * HEAD_DIM
WINDOW = 128
ROPE_BASE = 10000.0
OFF_Q = FOURIER_WIDTH
OFF_K = OFF_Q + ATTN_WIDTH
OFF_V = OFF_K + KV_WIDTH
OFF_G = OFF_V + KV_WIDTH
IN_WIDTH = OFF_G + 2 * D_MODEL
N_EXPERTS = 8
N_MOD = 6
EPS = 1e-6
NEG = -1e30
LOG2E = 1.0 / math.log(2.0)

LANES = 128
SUBLANES = 8

TM = 256
QB = 128
FFT_R = SEQ // GRID_W
FFT_C = GRID_W
TME = 512
HALVES = 2
TMP = HALVES * TME
TD = 512
SB = 2 * TD + N_EXPERTS * SUBLANES
XS_W = D_MODEL + LANES
SEG_BITS = TD.bit_length() - 3
TAIL_BITS = TMP.bit_length() - 4
FC_E = 512
VMEM_LIMIT = 56 * 1024 * 1024


def _cparams(sem, vmem=None):
    return pltpu.CompilerParams(dimension_semantics=sem, vmem_limit_bytes=vmem)


def _dot(a, b):
    return jnp.dot(a, b, preferred_element_type=F32)


def _sigmoid(x):
    return 1.0 / (1.0 + jnp.exp(-x))


def _rms_mod(x, g, sh, sc):
    r = lax.rsqrt(jnp.mean(x * x, axis=-1, keepdims=True) + EPS)
    return (x * r) * g * (1.0 + sc) + sh


def _dft_cs(n):
    k = np.arange(n, dtype=np.float64)
    a = 2.0 * np.pi * np.outer(k, k) / n
    return np.cos(a), np.sin(a)


@functools.lru_cache(maxsize=None)
def _tables():
    c128, s128 = _dft_cs(FFT_R)
    c64, s64 = _dft_cs(FFT_C)
    c256, s256 = _dft_cs(CTX_LEN)
    wa = np.concatenate([c128, -s128], axis=0)
    k1 = np.arange(FFT_R, dtype=np.float64)[:, None]
    cc = np.arange(FFT_C, dtype=np.float64)[None, :]
    ang = 2.0 * np.pi * k1 * cc / SEQ
    twc = np.broadcast_to(np.cos(ang)[:, :, None], (FFT_R, FFT_C, LANES))
    tws = np.broadcast_to(np.sin(ang)[:, :, None], (FFT_R, FFT_C, LANES))
    mc = np.block([[c64, s64], [-s64, c64]])
    cg, sg = _dft_cs(GROUP_W)
    mch = np.concatenate([cg, sg], axis=0) / math.sqrt(SEQ * GROUP_W)
    wcx = np.concatenate([c256, s256], axis=0)
    mchc = np.concatenate([cg, -sg], axis=0)
    n_freq = HEAD_DIM // 4
    inv = ROPE_BASE ** (-np.arange(n_freq, dtype=np.float64) / n_freq)
    t = np.arange(SEQ)
    rows = (t // GRID_W).astype(np.float64)[:, None] * inv
    cols = (t % GRID_W).astype(np.float64)[:, None] * inv
    cos_h = np.concatenate([np.cos(rows), np.cos(rows), np.cos(cols), np.cos(cols)], axis=1)
    sin_h = np.concatenate([-np.sin(rows), np.sin(rows), -np.sin(cols), np.sin(cols)], axis=1)
    cos_t = np.concatenate([cos_h, cos_h], axis=1)
    sin_t = np.concatenate([sin_h, sin_h], axis=1)
    f = lambda a: np.ascontiguousarray(a, dtype=np.float32)
    return dict(wa=f(wa), twc=f(twc), tws=f(tws), mc=f(mc), mch=f(mch), wcx=f(wcx), mchc=f(mchc),
                cos=f(cos_t), sin=f(sin_t),
                cos1=np.ones((TM, LANES), np.float32), sin0=np.zeros((TM, LANES), np.float32))


def _mod_body(cv_ref, w_ref, b_ref, o_ref):
    c = cv_ref[...]
    s = c * _sigmoid(c)
    o_ref[0, 0] = jnp.dot(s, w_ref[0], precision=lax.Precision.HIGHEST,
                          preferred_element_type=F32) + b_ref[0]


def _modulation(cv, w_mod, b_mod):
    out = pl.pallas_call(
        _mod_body,
        grid=(DEPTH, N_MOD),
        in_specs=[pl.BlockSpec((SUBLANES, D_MODEL), lambda l, k: (0, 0)),
                  pl.BlockSpec((1, D_MODEL, D_MODEL), lambda l, k: (l, 0, k)),
                  pl.BlockSpec((1, 1, D_MODEL), lambda l, k: (l * N_MOD + k, 0, 0))],
        out_specs=pl.BlockSpec((1, 1, SUBLANES, D_MODEL), lambda l, k: (l, k, 0, 0)),
        out_shape=jax.ShapeDtypeStruct((DEPTH, N_MOD, SUBLANES, D_MODEL), F32),
        compiler_params=_cparams(("arbitrary", "arbitrary")),
        name="modulation",
    )(cv, w_mod, b_mod.reshape(DEPTH * N_MOD, 1, D_MODEL))
    out = jnp.transpose(out[:, :, :3, :], (0, 2, 1, 3))
    return out.reshape(DEPTH * 3 * N_MOD, 1, D_MODEL)


def _mod_spec(layer, k, cond_fn):
    return pl.BlockSpec((1, 1, D_MODEL),
                        lambda i, *_: ((layer * 3 + cond_fn(i)) * N_MOD + k, 0, 0))


def _in_proj_body(x_ref, sh_ref, sc_ref, g_ref, w_ref, cos_ref, sin_ref,
                  u_ref, q_ref, kv_ref, gt_ref):
    hb = _rms_mod(x_ref[...], g_ref[...], sh_ref[0], sc_ref[0]).astype(BF16)
    u_ref[...] = _dot(hb, w_ref[:, 0:OFF_Q])
    cos = cos_ref[...]
    sin = sin_ref[...]
    lane = lax.broadcasted_iota(jnp.int32, cos.shape, 1)
    first_half = (lane % (HEAD_DIM // 2)) < (HEAD_DIM // 4)
    quarter = HEAD_DIM // 4

    def rope(xs):
        below = pltpu.roll(xs, quarter, 1)
        above = pltpu.roll(xs, LANES - quarter, 1)
        return xs * cos + jnp.where(first_half, above, below) * sin

    qk = _dot(hb, w_ref[:, OFF_Q:OFF_V])
    scale = HEAD_DIM ** -0.5 * LOG2E
    for j in range(ATTN_WIDTH // LANES):
        sl = slice(j * LANES, (j + 1) * LANES)
        q_ref[:, sl] = (rope(qk[:, sl]) * scale).astype(BF16)
    kv_ref[:, 0:KV_WIDTH] = rope(qk[:, ATTN_WIDTH:ATTN_WIDTH + KV_WIDTH]).astype(BF16)
    kv_ref[:, KV_WIDTH:2 * KV_WIDTH] = _dot(hb, w_ref[:, OFF_V:OFF_G]).astype(BF16)
    gc = 512
    for j in range(2 * D_MODEL // gc):
        z = _dot(hb, w_ref[:, OFF_G + j * gc:OFF_G + (j + 1) * gc])
        gt_ref[:, j * gc:(j + 1) * gc] = _sigmoid(z).astype(BF16)


def _in_proj(x, mods, layer, cond_fn, g, w_bf, cos, sin, rope_idx):
    rows = x.shape[0]
    return pl.pallas_call(
        _in_proj_body,
        grid=(rows // TM,),
        in_specs=[pl.BlockSpec((TM, D_MODEL), lambda i: (i, 0)),
                  _mod_spec(layer, 0, cond_fn),
                  _mod_spec(layer, 1, cond_fn),
                  pl.BlockSpec((1, D_MODEL), lambda i: (0, 0)),
                  pl.BlockSpec((D_MODEL, IN_WIDTH), lambda i: (0, 0)),
                  pl.BlockSpec((TM, LANES), lambda i: (rope_idx(i), 0)),
                  pl.BlockSpec((TM, LANES), lambda i: (rope_idx(i), 0))],
        out_specs=[pl.BlockSpec((TM, FOURIER_WIDTH), lambda i: (i, 0)),
                   pl.BlockSpec((TM, ATTN_WIDTH), lambda i: (i, 0)),
                   pl.BlockSpec((TM, 2 * KV_WIDTH), lambda i: (i, 0)),
                   pl.BlockSpec((TM, 2 * D_MODEL), lambda i: (i, 0))],
        out_shape=[jax.ShapeDtypeStruct((rows, FOURIER_WIDTH), F32),
                   jax.ShapeDtypeStruct((rows, ATTN_WIDTH), BF16),
                   jax.ShapeDtypeStruct((rows, 2 * KV_WIDTH), BF16),
                   jax.ShapeDtypeStruct((rows, 2 * D_MODEL), BF16)],
        compiler_params=_cparams(("arbitrary",), VMEM_LIMIT),
        name="in_proj",
    )(x, mods, mods, g, w_bf, cos, sin)


FFT_CB = 8


def _fft_a_body(x_ref, w_ref, o_ref):
    w = w_ref[...]
    xt = jnp.swapaxes(x_ref[...], 0, 1)
    res = jnp.stack([_dot(w, xt[c].astype(BF16)) for c in range(FFT_CB)])
    o_ref[0] = jnp.swapaxes(res, 0, 1)


def _fft_c_body(re_ref, im_ref, tc_ref, ts_ref, mc_ref, mch_ref, o_ref):
    mc = mc_ref[...]
    mch = mch_ref[...]
    xrs, xis = [], []
    for j in range(FFT_CB):
        ar = re_ref[0, j]
        ai = im_ref[0, j]
        tc = jnp.concatenate([tc_ref[j]] * N_GROUPS, axis=1)
        ts = jnp.concatenate([ts_ref[j]] * N_GROUPS, axis=1)
        br = ar * tc + ai * ts
        bi = ai * tc - ar * ts
        x = _dot(mc, jnp.concatenate([br, bi], axis=0).astype(BF16))
        xrs.append(x[:FFT_C])
        xis.append(x[FFT_C:])
    xr = jnp.concatenate(xrs, axis=0).astype(BF16)
    xi = jnp.concatenate(xis, axis=0).astype(BF16)
    ys = []
    for g in range(N_GROUPS):
        sl = slice(g * GROUP_W, (g + 1) * GROUP_W)
        ys.append(_dot(jnp.concatenate([xr[:, sl], xi[:, sl]], axis=1), mch))
    y = jnp.concatenate(ys, axis=1).reshape(FFT_CB, FFT_C, FOURIER_WIDTH)
    o_ref[0] = jnp.swapaxes(y, 0, 1)


def _fourier_latent(u, tb, batch):
    u3 = u.reshape(batch * FFT_R, FFT_C, FOURIER_WIDTH)
    a = pl.pallas_call(
        _fft_a_body,
        grid=(batch, FFT_C // FFT_CB),
        in_specs=[pl.BlockSpec((FFT_R, FFT_CB, FOURIER_WIDTH), lambda b, j: (b, j, 0)),
                  pl.BlockSpec((2 * FFT_R, FFT_R), lambda b, j: (0, 0))],
        out_specs=pl.BlockSpec((1, 2 * FFT_R, FFT_CB, FOURIER_WIDTH), lambda b, j: (b, 0, j, 0)),
        out_shape=jax.ShapeDtypeStruct((batch, 2 * FFT_R, FFT_C, FOURIER_WIDTH), F32),
        compiler_params=_cparams(("arbitrary", "arbitrary"), VMEM_LIMIT),
        name="fft_rows",
    )(u3, tb["wa"].astype(BF16))
    nk = FFT_R // FFT_CB
    y = pl.pallas_call(
        _fft_c_body,
        grid=(batch, nk),
        in_specs=[pl.BlockSpec((1, FFT_CB, FFT_C, FOURIER_WIDTH), lambda b, k: (b, k, 0, 0)),
                  pl.BlockSpec((1, FFT_CB, FFT_C, FOURIER_WIDTH), lambda b, k: (b, nk + k, 0, 0)),
                  pl.BlockSpec((FFT_CB, FFT_C, LANES), lambda b, k: (k, 0, 0)),
                  pl.BlockSpec((FFT_CB, FFT_C, LANES), lambda b, k: (k, 0, 0)),
                  pl.BlockSpec((2 * FFT_C, 2 * FFT_C), lambda b, k: (0, 0)),
                  pl.BlockSpec((2 * GROUP_W, GROUP_W), lambda b, k: (0, 0))],
        out_specs=pl.BlockSpec((1, FFT_C, FFT_CB, FOURIER_WIDTH), lambda b, k: (b, 0, k, 0)),
        out_shape=jax.ShapeDtypeStruct((batch, FFT_C, FFT_R, FOURIER_WIDTH), F32),
        compiler_params=_cparams(("arbitrary", "arbitrary"), VMEM_LIMIT),
        name="fft_cols",
    )(a, a, tb["twc"], tb["tws"], tb["mc"].astype(BF16), tb["mch"].astype(BF16))
    return y.reshape(batch * SEQ, FOURIER_WIDTH)


def _fourier_ctx_body(u_ref, w_ref, m_ref, o_ref):
    pq = _dot(w_ref[...], u_ref[...].astype(BF16))
    p = pq[:CTX_LEN]
    q = pq[CTX_LEN:]
    m = m_ref[...]
    scale = 1.0 / math.sqrt(CTX_LEN * GROUP_W)
    for g in range(N_GROUPS):
        sl = slice(g * GROUP_W, (g + 1) * GROUP_W)
        lhs = jnp.concatenate([p[:, sl], q[:, sl]], axis=1).astype(BF16)
        o_ref[:, sl] = _dot(lhs, m) * scale


def _fourier_ctx(u, tb, batch):
    return pl.pallas_call(
        _fourier_ctx_body,
        grid=(batch,),
        in_specs=[pl.BlockSpec((CTX_LEN, FOURIER_WIDTH), lambda b: (b, 0)),
                  pl.BlockSpec((2 * CTX_LEN, CTX_LEN), lambda b: (0, 0)),
                  pl.BlockSpec((2 * GROUP_W, GROUP_W), lambda b: (0, 0))],
        out_specs=pl.BlockSpec((CTX_LEN, FOURIER_WIDTH), lambda b: (b, 0)),
        out_shape=jax.ShapeDtypeStruct((batch * CTX_LEN, FOURIER_WIDTH), F32),
        compiler_params=_cparams(("arbitrary",)),
        name="fft_ctx",
    )(u, tb["wcx"].astype(BF16), tb["mchc"].astype(BF16))


def _attn_body(sink_ref, q_ref, *refs, local):
    o_ref = refs[-1]
    if local:
        kp_ref, kc_ref, kn_ref, kx_ref = refs[:-1]
        groups = [jnp.concatenate([kp_ref[...], kn_ref[...]], axis=0),
                  jnp.concatenate([kc_ref[...], kx_ref[...]], axis=0)]
        n = pl.program_id(1)
        last = pl.num_programs(1) - 1
        qi = lax.broadcasted_iota(jnp.int32, (2 * QB, 2 * QB), 0) % QB
        kj = lax.broadcasted_iota(jnp.int32, (2 * QB, 2 * QB), 1)
        prev_ok = (kj < QB) & (kj >= qi + jnp.where(n == 0, 2 * QB, 0))
        next_ok = (kj >= QB) & (kj - QB <= qi - jnp.where(n == last, 2 * QB, 0))
        masks = [prev_ok | next_ok, None]
    else:
        groups = [refs[0][...]]
        masks = [None]
    lo_q = lax.broadcasted_iota(jnp.int32, (2 * QB, LANES), 1) < HEAD_DIM
    kmats, vmats = [], []
    for kv in groups:
        kv = kv.astype(F32)
        kcat = kv[:, :KV_WIDTH]
        vcat = kv[:, KV_WIDTH:]
        kroll = pltpu.roll(kcat, HEAD_DIM, 1)
        vroll = pltpu.roll(vcat, HEAD_DIM, 1)
        lo_k = lax.broadcasted_iota(jnp.int32, kcat.shape, 1) < HEAD_DIM
        zero = jnp.zeros_like(kcat)
        one = jnp.ones_like(vcat)
        km, vm = [], []
        for hk in range(N_KV_HEADS):
            k_own, k_other = (kcat, kroll) if hk == 0 else (kroll, kcat)
            v_own, v_other = (vcat, vroll) if hk == 0 else (vroll, vcat)
            km.append((jnp.where(lo_k, k_own, zero).astype(BF16), jnp.where(lo_k, zero, k_other).astype(BF16)))
            vm.append((jnp.where(lo_k, v_own, one).astype(BF16), jnp.where(lo_k, one, v_other).astype(BF16)))
        kmats.append(km)
        vmats.append(vm)
    pairs = [(hk, half) for hk in range(N_KV_HEADS) for half in range(2)]
    first_slab = lax.broadcasted_iota(jnp.int32, (2 * QB, 1), 0) < QB
    scores = []
    for hk, half in pairs:
        qs = jnp.concatenate([q_ref[:, (2 * hk) * LANES:(2 * hk + 1) * LANES],
                              q_ref[:, (2 * hk + 1) * LANES:(2 * hk + 2) * LANES]], axis=0)
        parts = []
        for km, mask in zip(kmats, masks):
            s = lax.dot_general(qs, km[hk][half], (((1,), (1,)), ((), ())), preferred_element_type=F32)
            parts.append(s if mask is None else jnp.where(mask, s, NEG))
        scores.append(parts)
    sinks = [jnp.where(first_slab, sink_ref[4 * hk + half], sink_ref[4 * hk + 2 + half]) * LOG2E
             for hk, half in pairs]
    maxes = []
    for parts, sk in zip(scores, sinks):
        m = sk
        for s in parts:
            m = jnp.maximum(m, jnp.max(s, axis=1, keepdims=True))
        maxes.append(m)
    sink_p = [jnp.exp2(sk - m) for sk, m in zip(sinks, maxes)]
    pvs = []
    for parts, m, (hk, half) in zip(scores, maxes, pairs):
        pv = None
        for s, vm in zip(parts, vmats):
            term = _dot(jnp.exp2(s - m).astype(BF16), vm[hk][half])
            pv = term if pv is None else pv + term
        pvs.append(pv)
    for hk in range(N_KV_HEADS):
        lo, hi = pvs[2 * hk], pvs[2 * hk + 1]
        num = jnp.where(lo_q, lo, hi)
        den = (pltpu.roll(jnp.where(lo_q, hi, lo), HEAD_DIM, 1)
               + jnp.where(lo_q, sink_p[2 * hk], sink_p[2 * hk + 1]))
        out = (num / den).astype(BF16)
        o_ref[:, (2 * hk) * LANES:(2 * hk + 1) * LANES] = out[:QB]
        o_ref[:, (2 * hk + 1) * LANES:(2 * hk + 2) * LANES] = out[QB:]


def _attention_latent(q, kv, kv_ctx, sink, batch):
    nb = SEQ // QB
    kvw = 2 * KV_WIDTH
    return pl.pallas_call(
        functools.partial(_attn_body, local=True),
        grid=(batch, nb),
        in_specs=[pl.BlockSpec(memory_space=pltpu.SMEM),
                  pl.BlockSpec((QB, ATTN_WIDTH), lambda b, n: (b * nb + n, 0)),
                  pl.BlockSpec((QB, kvw), lambda b, n: (b * nb + jnp.maximum(n - 1, 0), 0)),
                  pl.BlockSpec((QB, kvw), lambda b, n: (b * nb + n, 0)),
                  pl.BlockSpec((QB, kvw), lambda b, n: (b * nb + jnp.minimum(n + 1, nb - 1), 0)),
                  pl.BlockSpec((CTX_LEN, kvw), lambda b, n: (b, 0))],
        out_specs=pl.BlockSpec((QB, ATTN_WIDTH), lambda b, n: (b * nb + n, 0)),
        out_shape=jax.ShapeDtypeStruct((batch * SEQ, ATTN_WIDTH), BF16),
        compiler_params=_cparams(("arbitrary", "arbitrary")),
        name="attn_latent",
    )(sink, q, kv, kv, kv, kv_ctx)


def _attention_ctx(q, kv, sink, batch):
    nb = CTX_LEN // QB
    return pl.pallas_call(
        functools.partial(_attn_body, local=False),
        grid=(batch, nb),
        in_specs=[pl.BlockSpec(memory_space=pltpu.SMEM),
                  pl.BlockSpec((QB, ATTN_WIDTH), lambda b, n: (b * nb + n, 0)),
                  pl.BlockSpec((CTX_LEN, 2 * KV_WIDTH), lambda b, n: (b, 0))],
        out_specs=pl.BlockSpec((QB, ATTN_WIDTH), lambda b, n: (b * nb + n, 0)),
        out_shape=jax.ShapeDtypeStruct((batch * CTX_LEN, ATTN_WIDTH), BF16),
        compiler_params=_cparams(("arbitrary", "arbitrary")),
        name="attn_ctx",
    )(sink, q, kv)


def _merge_body(x_ref, fm_ref, ao_ref, gt_ref, wf_ref, wa_ref, wo_ref, ga_ref, *refs, with_h2):
    gt = gt_ref[...].astype(F32)
    y = (gt[:, :D_MODEL] * _dot(fm_ref[...].astype(BF16), wf_ref[...])
         + gt[:, D_MODEL:] * _dot(ao_ref[...], wa_ref[...]))
    xn = x_ref[...] + ga_ref[0] * _dot(y.astype(BF16), wo_ref[...])
    if with_h2:
        sh_ref, sc_ref, g_ref, xo_ref, h_ref = refs
        h_ref[...] = _rms_mod(xn, g_ref[...], sh_ref[0], sc_ref[0]).astype(BF16)
    else:
        (xo_ref,) = refs
    xo_ref[...] = xn


def _merge(x, fm, ao, gates, wf, wa, wo, mods, layer, cond_fn, g2, with_h2):
    rows = x.shape[0]
    row_spec = lambda w: pl.BlockSpec((TM, w), lambda i: (i, 0))
    full = lambda a: pl.BlockSpec(a.shape, lambda i: (0, 0))
    in_specs = [row_spec(D_MODEL), row_spec(FOURIER_WIDTH), row_spec(ATTN_WIDTH), row_spec(2 * D_MODEL),
                full(wf), full(wa), full(wo), _mod_spec(layer, 2, cond_fn)]
    args = [x, fm, ao, gates, wf, wa, wo, mods]
    out_specs = [row_spec(D_MODEL)]
    out_shape = [jax.ShapeDtypeStruct((rows, D_MODEL), F32)]
    if with_h2:
        in_specs += [_mod_spec(layer, 3, cond_fn), _mod_spec(layer, 4, cond_fn),
                     pl.BlockSpec((1, D_MODEL), lambda i: (0, 0))]
        args += [mods, mods, g2]
        out_specs.append(row_spec(D_MODEL))
        out_shape.append(jax.ShapeDtypeStruct((rows, D_MODEL), BF16))
    return pl.pallas_call(
        functools.partial(_merge_body, with_h2=with_h2),
        grid=(rows // TM,),
        in_specs=in_specs, out_specs=out_specs, out_shape=out_shape,
        compiler_params=_cparams(("arbitrary",), VMEM_LIMIT),
        name="merge",
    )(*args)


def _ffn_body(x_ref, h_ref, wg_ref, wu_ref, wd_ref, ga_ref, o_ref):
    h = h_ref[...]
    g = _dot(h, wg_ref[...])
    u = _dot(h, wu_ref[...])
    a = (g * _sigmoid(g) * u).astype(BF16)
    o_ref[...] = x_ref[...] + ga_ref[0] * _dot(a, wd_ref[...])


def _ffn_dense(x, h2, wg, wu, wd, mods, layer, cond_fn):
    rows = x.shape[0]
    resident = lambda a: pl.BlockSpec(a.shape, lambda i: (0, 0), pipeline_mode=pl.Buffered(1))
    return pl.pallas_call(
        _ffn_body,
        grid=(rows // TM,),
        in_specs=[pl.BlockSpec((TM, D_MODEL), lambda i: (i, 0)),
                  pl.BlockSpec((TM, D_MODEL), lambda i: (i, 0)),
                  resident(wg), resident(wu), resident(wd),
                  _mod_spec(layer, 5, cond_fn)],
        out_specs=pl.BlockSpec((TM, D_MODEL), lambda i: (i, 0)),
        out_shape=jax.ShapeDtypeStruct((rows, D_MODEL), F32),
        compiler_params=_cparams(("arbitrary",), VMEM_LIMIT),
        name="ffn_dense",
    )(x, h2, wg, wu, wd, mods)


INFO_E0, INFO_E1, INFO_R0, INFO_R1, INFO_W0, INFO_W1 = range(6)


def _router_body(x_ref, sh_ref, sc_ref, g_ref, wr_ref, hb_ref, info_ref, cnt_ref):
    h = _rms_mod(x_ref[...], g_ref[...], sh_ref[0], sc_ref[0])
    hb_ref[...] = h.astype(BF16)
    logits = jnp.dot(h, wr_ref[...], precision=lax.Precision.HIGHEST, preferred_element_type=F32)
    lane = lax.broadcasted_iota(jnp.int32, logits.shape, 1)
    neg_inf = jnp.float32(-jnp.inf)
    lg = jnp.where(lane < N_EXPERTS, logits, neg_inf)
    v0 = jnp.max(lg, axis=1, keepdims=True)
    i0 = jnp.min(jnp.where(lg == v0, lane, LANES), axis=1, keepdims=True)
    oh0 = lane == i0
    lg1 = jnp.where(oh0, neg_inf, lg)
    v1 = jnp.max(lg1, axis=1, keepdims=True)
    i1 = jnp.min(jnp.where(lg1 == v1, lane, LANES), axis=1, keepdims=True)
    oh1 = lane == i1
    e = jnp.exp(v1 - v0)
    w0 = 1.0 / (1.0 + e)
    w1 = e / (1.0 + e)
    oh = jnp.where(oh0 | oh1, 1.0, 0.0)
    row = lax.broadcasted_iota(jnp.int32, (TD, TD), 0)
    col = lax.broadcasted_iota(jnp.int32, (TD, TD), 1)
    tri = jnp.where(row > col, 1.0, 0.0).astype(BF16)
    before = _dot(tri, oh.astype(BF16))
    r0 = jnp.sum(jnp.where(oh0, before, 0.0), axis=1, keepdims=True)
    r1 = jnp.sum(jnp.where(oh1, before, 0.0), axis=1, keepdims=True)
    cnt_ref[0] = jnp.broadcast_to(jnp.sum(oh, axis=0, keepdims=True), (SUBLANES, LANES))
    info = jnp.zeros(logits.shape, F32)
    for idx, val in ((INFO_E0, i0.astype(F32)), (INFO_E1, i1.astype(F32)), (INFO_R0, r0),
                     (INFO_R1, r1), (INFO_W0, w0), (INFO_W1, w1)):
        info = jnp.where(lane == idx, val, info)
    info_ref[...] = info


def _router(x, mods, layer, cond_fn, g2, wr_pad):
    rows = x.shape[0]
    return pl.pallas_call(
        _router_body,
        grid=(rows // TD,),
        in_specs=[pl.BlockSpec((TD, D_MODEL), lambda i: (i, 0)),
                  _mod_spec(layer, 3, cond_fn), _mod_spec(layer, 4, cond_fn),
                  pl.BlockSpec((1, D_MODEL), lambda i: (0, 0)),
                  pl.BlockSpec((D_MODEL, LANES), lambda i: (0, 0))],
        out_specs=[pl.BlockSpec((TD, D_MODEL), lambda i: (i, 0)),
                   pl.BlockSpec((TD, LANES), lambda i: (i, 0)),
                   pl.BlockSpec((1, SUBLANES, LANES), lambda i: (i, 0, 0))],
        out_shape=[jax.ShapeDtypeStruct((rows, D_MODEL), BF16),
                   jax.ShapeDtypeStruct((rows, LANES), F32),
                   jax.ShapeDtypeStruct((rows // TD, SUBLANES, LANES), F32)],
        compiler_params=_cparams(("arbitrary",), VMEM_LIMIT),
        name="router",
    )(x, mods, mods, g2, wr_pad)


def _segment_copies(src_ref, src_off, dst_ref, dst_off, len8, sem, bits, wait):
    for k in reversed(range(bits)):
        size = SUBLANES << k
        done = ((len8 >> (k + 1)) << (k + 1)) * SUBLANES

        @pl.when(((len8 >> k) & 1) == 1)
        def _(size=size, done=done):
            cp = pltpu.make_async_copy(
                src_ref.at[pl.ds(pl.multiple_of(src_off + done, SUBLANES), size)],
                dst_ref.at[pl.ds(pl.multiple_of(dst_off + done, SUBLANES), size)], sem)
            if wait:
                cp.wait()
            else:
                cp.start()


def _slot_positions(e0, e1, r0, r1, seg_ref, base):
    pos0, pos1 = r0, r1
    for e in range(N_EXPERTS):
        start = seg_ref[base + e].astype(F32)
        pos0 = pos0 + jnp.where(e0 == e, start, 0.0)
        pos1 = pos1 + jnp.where(e1 == e, start, 0.0)
    return pos0.astype(jnp.int32), pos1.astype(jnp.int32)


def _dispatch_body(seg_ref, dst_ref, len_ref, tdst_ref, tlen_ref, nt_ref, hb_ref, info_ref, xs_ref,
                   buf_ref, zero_ref, sems, zero_sem, *, min_tiles):
    i = pl.program_id(0)
    last = pl.num_programs(0) - 1
    base = i * N_EXPERTS
    cur = i % 2

    def scatter(tile, slot, wait):
        for e in range(N_EXPERTS):
            k = tile * N_EXPERTS + e
            _segment_copies(buf_ref.at[slot], seg_ref[k], xs_ref, dst_ref[k], len_ref[k],
                            sems.at[slot], SEG_BITS, wait=wait)

    info_t = info_ref[...].T
    row = lambda k: info_t[k:k + 1, :]
    pos0, pos1 = _slot_positions(row(INFO_E0), row(INFO_E1), row(INFO_R0), row(INFO_R1), seg_ref, base)
    slot = lax.broadcasted_iota(jnp.int32, (SB, TD), 0)
    p0 = slot == pos0
    p1 = slot == pos1
    perm = jnp.where(p0 | p1, 1.0, 0.0).astype(BF16)
    buf_ref[cur, :, :D_MODEL] = _dot(perm, hb_ref[...])
    wsel = jnp.where(p0, row(INFO_W0), 0.0) + jnp.where(p1, row(INFO_W1), 0.0)
    buf_ref[cur, :, D_MODEL:] = jnp.broadcast_to(jnp.sum(wsel, axis=1, keepdims=True), (SB, LANES))
    scatter(i, cur, wait=False)

    @pl.when(i > 0)
    def _():
        scatter(i - 1, 1 - cur, wait=True)

    @pl.when(i == last)
    def _():
        zero_ref[...] = jnp.zeros_like(zero_ref)
        spare_tiles = range(min_tiles, xs_ref.shape[0] // TMP)
        spare = [pltpu.make_async_copy(zero_ref, xs_ref.at[pl.ds(j * TMP, TMP)], zero_sem) for j in spare_tiles]
        for wait in (False, True):
            for e in range(N_EXPERTS):
                _segment_copies(zero_ref, 0, xs_ref, tdst_ref[e], tlen_ref[e], zero_sem, TAIL_BITS, wait=wait)
            for j, cp in zip(spare_tiles, spare):
                @pl.when(j >= nt_ref[0])
                def _(cp=cp, wait=wait):
                    if wait:
                        cp.wait()
                    else:
                        cp.start()
        scatter(i, cur, wait=True)


def _dispatch(seg, dst, len8, tail_dst, tail_len8, n_tiles, hb, info, m_rows):
    rows = hb.shape[0]
    return pl.pallas_call(
        functools.partial(_dispatch_body, min_tiles=2 * rows // TMP),
        grid_spec=pltpu.PrefetchScalarGridSpec(
            num_scalar_prefetch=6,
            grid=(rows // TD,),
            in_specs=[pl.BlockSpec((TD, D_MODEL), lambda i, *_: (i, 0)),
                      pl.BlockSpec((TD, LANES), lambda i, *_: (i, 0))],
            out_specs=pl.BlockSpec(memory_space=pl.ANY),
            scratch_shapes=[pltpu.VMEM((2, SB, XS_W), F32), pltpu.VMEM((TMP, XS_W), F32),
                            pltpu.SemaphoreType.DMA((2,)), pltpu.SemaphoreType.DMA(())]),
        out_shape=jax.ShapeDtypeStruct((m_rows, XS_W), F32),
        compiler_params=_cparams(("arbitrary",), VMEM_LIMIT),
        name="dispatch",
    )(seg, dst, len8, tail_dst, tail_len8, n_tiles, hb, info)


def _expert_body(te_ref, nh_ref, nt_ref, xs_ref, wg_ref, wu_ref, wd_ref, ys_ref, acc_ref):
    j = pl.program_id(0)
    f = pl.program_id(1)
    nf = pl.num_programs(1)
    halves = nh_ref[j]

    blocks = [slice(h * TME, (h + 1) * TME) for h in range(HALVES)]

    @pl.when((halves > 0) & (f == 0))
    def _():
        acc_ref[...] = jnp.zeros_like(acc_ref)

    for live in range(1, HALVES + 1):
        @pl.when(halves == live)
        def _(live=live):
            ups = [(_dot(xs_ref[rows, :D_MODEL], wg_ref[0]), _dot(xs_ref[rows, :D_MODEL], wu_ref[0]))
                   for rows in blocks[:live]]
            downs = [_dot(g * _sigmoid(g) * u, wd_ref[0]) for g, u in ups]
            for rows, d in zip(blocks[:live], downs):
                acc_ref[rows, :] += d

    @pl.when(f == nf - 1)
    def _():
        live_rows = lax.broadcasted_iota(jnp.int32, (TMP, 1), 0) < halves * TME
        ys_ref[...] = jnp.where(live_rows, acc_ref[...] * xs_ref[:, D_MODEL:D_MODEL + 1], 0.0)


def _experts(tile_expert, tile_halves, n_tiles, xs, wg, wu, wd):
    m_rows = xs.shape[0]
    nf = wg.shape[2] // FC_E

    def f_idx(j, f, nt):
        return jnp.where(j < nt[0], f, nf - 1)

    def j_idx(j, nt):
        return jnp.minimum(j, nt[0] - 1)

    return pl.pallas_call(
        _expert_body,
        grid_spec=pltpu.PrefetchScalarGridSpec(
            num_scalar_prefetch=3,
            grid=(m_rows // TMP, nf),
            in_specs=[pl.BlockSpec((TMP, XS_W), lambda j, f, te, nh, nt: (j_idx(j, nt), 0)),
                      pl.BlockSpec((1, D_MODEL, FC_E), lambda j, f, te, nh, nt: (te[j], 0, f_idx(j, f, nt))),
                      pl.BlockSpec((1, D_MODEL, FC_E), lambda j, f, te, nh, nt: (te[j], 0, f_idx(j, f, nt))),
                      pl.BlockSpec((1, FC_E, D_MODEL), lambda j, f, te, nh, nt: (te[j], f_idx(j, f, nt), 0))],
            out_specs=pl.BlockSpec((TMP, D_MODEL), lambda j, f, te, nh, nt: (j, 0)),
            scratch_shapes=[pltpu.VMEM((TMP, D_MODEL), F32)]),
        out_shape=jax.ShapeDtypeStruct((m_rows, D_MODEL), F32),
        compiler_params=_cparams(("arbitrary", "arbitrary"), VMEM_LIMIT),
        name="experts",
    )(tile_expert, tile_halves, n_tiles, xs, wg, wu, wd)


def _combine_body(seg_ref, dst_ref, len_ref, ys_ref, x_ref, info_ref, ga_ref, g_ref, o_ref, buf_ref, sems):
    i = pl.program_id(0)
    base = i * N_EXPERTS
    cur = i % 2

    def gather(tile, slot, wait):
        for e in range(N_EXPERTS):
            k = tile * N_EXPERTS + e
            _segment_copies(ys_ref, dst_ref[k], buf_ref.at[slot], seg_ref[k], len_ref[k],
                            sems.at[slot], SEG_BITS, wait=wait)

    @pl.when(i == 0)
    def _():
        buf_ref[...] = jnp.zeros_like(buf_ref)
        gather(0, 0, wait=False)

    @pl.when(i + 1 < pl.num_programs(0))
    def _():
        gather(i + 1, 1 - cur, wait=False)

    info = info_ref[...]
    col = lambda k: info[:, k:k + 1]
    pos0, pos1 = _slot_positions(col(INFO_E0), col(INFO_E1), col(INFO_R0), col(INFO_R1), seg_ref, base)
    slot = lax.broadcasted_iota(jnp.int32, (TD, SB), 1)
    unperm = jnp.where((slot == pos0) | (slot == pos1), 1.0, 0.0).astype(BF16)
    gather(i, cur, wait=True)
    y = _dot(unperm, buf_ref[cur].astype(BF16))
    xn = x_ref[...] + ga_ref[0] * y
    r = lax.rsqrt(jnp.mean(xn * xn, axis=-1, keepdims=True) + EPS)
    o_ref[...] = (xn * r) * g_ref[...]


def _combine(seg, dst, len8, ys, x, info, mods, layer, cond_fn, final_g):
    rows = x.shape[0]
    return pl.pallas_call(
        _combine_body,
        grid_spec=pltpu.PrefetchScalarGridSpec(
            num_scalar_prefetch=3,
            grid=(rows // TD,),
            in_specs=[pl.BlockSpec(memory_space=pl.ANY),
                      pl.BlockSpec((TD, D_MODEL), lambda i, *_: (i, 0)),
                      pl.BlockSpec((TD, LANES), lambda i, *_: (i, 0)),
                      _mod_spec(layer, 5, cond_fn),
                      pl.BlockSpec((1, D_MODEL), lambda i, *_: (0, 0))],
            out_specs=pl.BlockSpec((TD, D_MODEL), lambda i, *_: (i, 0)),
            scratch_shapes=[pltpu.VMEM((2, SB, D_MODEL), F32), pltpu.SemaphoreType.DMA((2,))]),
        out_shape=jax.ShapeDtypeStruct((rows, D_MODEL), F32),
        compiler_params=_cparams(("arbitrary",), VMEM_LIMIT),
        name="combine",
    )(seg, dst, len8, ys, x, info, mods, final_g)


def _moe(x, mods, layer, cond_fn, g2, w_router, wg, wu, wd, final_g):
    rows = x.shape[0]
    n_tok_tiles = rows // TD
    wr_pad = jnp.pad(w_router, ((0, 0), (0, LANES - N_EXPERTS)))
    hb, info, cnt = _router(x, mods, layer, cond_fn, g2, wr_pad)
    counts = cnt[:, 0, :N_EXPERTS].astype(jnp.int32)
    seg_len = (counts + SUBLANES - 1) // SUBLANES * SUBLANES
    seg_start = jnp.cumsum(seg_len, axis=1) - seg_len
    group = jnp.sum(seg_len, axis=0)
    group_halves = (group + TME - 1) // TME
    group_tiles = (group + TMP - 1) // TMP
    tile_start = jnp.cumsum(group_tiles) - group_tiles
    group_base = tile_start * TMP
    dst = group_base[None, :] + jnp.cumsum(seg_len, axis=0) - seg_len
    tail_dst = group_base + group
    tail_len8 = (group_tiles * TMP - group) // SUBLANES
    max_tiles = (2 * rows + n_tok_tiles * N_EXPERTS * (SUBLANES - 1) + TMP - 1) // TMP + N_EXPERTS
    n_tiles = jnp.sum(group_tiles).astype(jnp.int32).reshape(1)
    tiles = jnp.arange(max_tiles, dtype=jnp.int32)
    tid = jnp.minimum(tiles, n_tiles[0] - 1)
    tile_expert = (jnp.sum(tid[:, None] >= tile_start[None, :], axis=1) - 1).astype(jnp.int32)
    in_group = tid - tile_start[tile_expert]
    tile_halves = jnp.clip(group_halves[tile_expert] - HALVES * in_group, 0, HALVES)
    tile_halves = jnp.where(tiles < n_tiles[0], tile_halves, 0).astype(jnp.int32)
    flat = lambda a: a.reshape(-1).astype(jnp.int32)
    seg, dst, len8 = flat(seg_start), flat(dst), flat(seg_len // SUBLANES)
    xs = _dispatch(seg, dst, len8, flat(tail_dst), flat(tail_len8), n_tiles, hb, info, max_tiles * TMP)
    ys = _experts(tile_expert, tile_halves, n_tiles, xs, wg, wu, wd)
    return _combine(seg, dst, len8, ys, x, info, mods, layer, cond_fn, final_g)


def kernel(x, c, ctx, c_ctx, w_mod, b_mod, norm1_g, norm2_g, w_in, sink, w_fourier, w_attn, w_out,
           w_gate_d, w_up_d, w_down_d, w_router, w_gate_e, w_up_e, w_down_e, final_g):
    batch, seq, d = x.shape
    assert (seq, d) == (SEQ, D_MODEL) and ctx.shape == (batch, CTX_LEN, D_MODEL)
    tb = {k: jnp.asarray(v) for k, v in _tables().items()}
    tiles_per_batch = SEQ // TM
    lat_cond = lambda i: i // tiles_per_batch
    ctx_cond = lambda i: 2
    lat_rope = lambda i: i % tiles_per_batch
    ctx_rope = lambda i: 0

    cv = jnp.zeros((SUBLANES, D_MODEL), F32).at[:batch].set(c).at[batch].set(c_ctx)
    mods = _modulation(cv, w_mod, b_mod)

    xl = x.reshape(batch * SEQ, D_MODEL)
    xc = ctx.reshape(batch * CTX_LEN, D_MODEL)
    out = None
    for l in range(DEPTH):
        last = l == DEPTH - 1
        g1 = norm1_g[l].reshape(1, D_MODEL)
        g2 = norm2_g[l].reshape(1, D_MODEL)
        w_in_b = w_in[l].astype(BF16)
        wf = w_fourier[l].astype(BF16)
        wa = w_attn[l].astype(BF16)
        wo = w_out[l].astype(BF16)
        u, q, kv, gates = _in_proj(xl, mods, l, lat_cond, g1, w_in_b, tb["cos"], tb["sin"], lat_rope)
        uc, qc, kvc, gates_c = _in_proj(xc, mods, l, ctx_cond, g1, w_in_b, tb["cos1"], tb["sin0"], ctx_rope)
        fm = _fourier_latent(u, tb, batch)
        ao = _attention_latent(q, kv, kvc, sink[l], batch)
        if l % 2 == 0:
            i = l // 2
            xl, h2 = _merge(xl, fm, ao, gates, wf, wa, wo, mods, l, lat_cond, g2, True)
            wg, wu, wd = w_gate_d[i].astype(BF16), w_up_d[i].astype(BF16), w_down_d[i].astype(BF16)
            xl = _ffn_dense(xl, h2, wg, wu, wd, mods, l, lat_cond)
            if not last:
                fmc = _fourier_ctx(uc, tb, batch)
                aoc = _attention_ctx(qc, kvc, sink[l], batch)
                xc, h2c = _merge(xc, fmc, aoc, gates_c, wf, wa, wo, mods, l, ctx_cond, g2, True)
                xc = _ffn_dense(xc, h2c, wg, wu, wd, mods, l, ctx_cond)
            if last:
                raise NotImplementedError("final norm is fused into the routed-expert combine")
        else:
            i = l // 2
            (xl,) = _merge(xl, fm, ao, gates, wf, wa, wo, mods, l, lat_cond, g2, False)
            if not last:
                raise NotImplementedError("context update through a routed-expert layer")
            out = _moe(xl, mods, l, lambda t: t // (SEQ // TD), g2, w_router[i], w_gate_e[i], w_up_e[i], w_down_e[i],
                       final_g.reshape(1, D_MODEL))
    return out.reshape(batch, SEQ, D_MODEL)
```

```python
import functools
import math

import numpy as np
import jax
import jax.numpy as jnp
from jax import lax
from jax.experimental import pallas as pl
from jax.experimental.pallas import tpu as pltpu

F32 = jnp.float32
BF16 = jnp.bfloat16

D_MODEL = 1024
SEQ = 8192
DEPTH = 2
GRID_W = 64
CTX_LEN = 256
N_GROUPS = 4
GROUP_W = 128
FOURIER_WIDTH = N_GROUPS * GROUP_W
N_Q_HEADS = 8
N_KV_HEADS = 2
HEAD_DIM = 64
ATTN_WIDTH = N_Q_HEADS * HEAD_DIM
KV_WIDTH = N_KV_HEADS * HEAD_DIM
WINDOW = 128
ROPE_BASE = 10000.0
OFF_Q = FOURIER_WIDTH
OFF_K = OFF_Q + ATTN_WIDTH
OFF_V = OFF_K + KV_WIDTH
OFF_G = OFF_V + KV_WIDTH
IN_WIDTH = OFF_G + 2 * D_MODEL
N_EXPERTS = 8
N_MOD = 6
EPS = 1e-6
NEG = -1e30
LOG2E = 1.0 / math.log(2.0)

LANES = 128
SUBLANES = 8

TM = 256
QB = 128
FFT_R = SEQ // GRID_W
FFT_C = GRID_W
TME = 512
HALVES = 2
TMP = HALVES * TME
TD = 512
SB = 2 * TD + N_EXPERTS * SUBLANES
XS_W = D_MODEL + LANES
SEG_BITS = TD.bit_length() - 3
TAIL_BITS = TMP.bit_length() - 4
FC_E = 512
VMEM_LIMIT = 56 * 1024 * 1024


def _cparams(sem, vmem=None):
    return pltpu.CompilerParams(dimension_semantics=sem, vmem_limit_bytes=vmem)


def _dot(a, b):
    return lax.dot_general(a, b, (((1,), (0,)), ((), ())), preferred_element_type=F32)


def _resident(a):
    return pl.BlockSpec(a.shape, lambda i, *_: (0,) * a.ndim, pipeline_mode=pl.Buffered(1))


def _sigmoid(x):
    return 1.0 / (1.0 + jnp.exp(-x))


def _rms_mod(x, g, sh, sc):
    r = lax.rsqrt(jnp.mean(x * x, axis=-1, keepdims=True) + EPS)
    return (x * r) * g * (1.0 + sc) + sh


def _dft_cs(n):
    k = np.arange(n, dtype=np.float64)
    a = 2.0 * np.pi * np.outer(k, k) / n
    return np.cos(a), np.sin(a)


@functools.lru_cache(maxsize=None)
def _tables():
    c128, s128 = _dft_cs(FFT_R)
    c64, s64 = _dft_cs(FFT_C)
    c256, s256 = _dft_cs(CTX_LEN)
    wa = np.concatenate([c128, -s128], axis=0)
    k1 = np.arange(FFT_R, dtype=np.float64)[:, None]
    cc = np.arange(FFT_C, dtype=np.float64)[None, :]
    ang = 2.0 * np.pi * k1 * cc / SEQ
    twc = np.broadcast_to(np.cos(ang)[:, :, None], (FFT_R, FFT_C, LANES))
    tws = np.broadcast_to(np.sin(ang)[:, :, None], (FFT_R, FFT_C, LANES))
    mc = np.block([[c64, s64], [-s64, c64]])
    cg, sg = _dft_cs(GROUP_W)
    mch = np.concatenate([cg, sg], axis=0) / math.sqrt(SEQ * GROUP_W)
    wcx = np.concatenate([c256, s256], axis=0)
    mchc = np.concatenate([cg, -sg], axis=0)
    n_freq = HEAD_DIM // 4
    inv = ROPE_BASE ** (-np.arange(n_freq, dtype=np.float64) / n_freq)
    t = np.arange(SEQ)
    rows = (t // GRID_W).astype(np.float64)[:, None] * inv
    cols = (t % GRID_W).astype(np.float64)[:, None] * inv
    cos_h = np.concatenate([np.cos(rows), np.cos(rows), np.cos(cols), np.cos(cols)], axis=1)
    sin_h = np.concatenate([-np.sin(rows), np.sin(rows), -np.sin(cols), np.sin(cols)], axis=1)
    cos_t = np.concatenate([cos_h, cos_h], axis=1)
    sin_t = np.concatenate([sin_h, sin_h], axis=1)
    f = lambda a: np.ascontiguousarray(a, dtype=np.float32)
    return dict(wa=f(wa), twc=f(twc), tws=f(tws), mc=f(mc), mch=f(mch), wcx=f(wcx), mchc=f(mchc),
                cos=f(cos_t), sin=f(sin_t),
                cos1=np.ones((TM, LANES), np.float32), sin0=np.zeros((TM, LANES), np.float32))


def _mod_body(cv_ref, w_ref, b_ref, o_ref):
    c = cv_ref[...]
    s = c * _sigmoid(c)
    o_ref[0, 0] = jnp.dot(s, w_ref[0], precision=lax.Precision.HIGHEST,
                          preferred_element_type=F32) + b_ref[0]


def _modulation(cv, w_mod, b_mod):
    out = pl.pallas_call(
        _mod_body,
        grid=(DEPTH, N_MOD),
        in_specs=[pl.BlockSpec((SUBLANES, D_MODEL), lambda l, k: (0, 0)),
                  pl.BlockSpec((1, D_MODEL, D_MODEL), lambda l, k: (l, 0, k)),
                  pl.BlockSpec((1, 1, D_MODEL), lambda l, k: (l * N_MOD + k, 0, 0))],
        out_specs=pl.BlockSpec((1, 1, SUBLANES, D_MODEL), lambda l, k: (l, k, 0, 0)),
        out_shape=jax.ShapeDtypeStruct((DEPTH, N_MOD, SUBLANES, D_MODEL), F32),
        compiler_params=_cparams(("arbitrary", "arbitrary")),
        name="modulation",
    )(cv, w_mod, b_mod.reshape(DEPTH * N_MOD, 1, D_MODEL))
    out = jnp.transpose(out[:, :, :3, :], (0, 2, 1, 3))
    return out.reshape(DEPTH * 3 * N_MOD, 1, D_MODEL)


def _mod_spec(layer, k, cond_fn):
    return pl.BlockSpec((1, 1, D_MODEL),
                        lambda i, *_: ((layer * 3 + cond_fn(i)) * N_MOD + k, 0, 0))


def _in_proj_body(x_ref, sh_ref, sc_ref, g_ref, w_ref, cos_ref, sin_ref,
                  u_ref, q_ref, kv_ref, gt_ref):
    hb = _rms_mod(x_ref[...], g_ref[...], sh_ref[0], sc_ref[0]).astype(BF16)
    u_ref[...] = _dot(hb, w_ref[:, 0:OFF_Q])
    cos = cos_ref[...]
    sin = sin_ref[...]
    lane = lax.broadcasted_iota(jnp.int32, cos.shape, 1)
    first_half = (lane % (HEAD_DIM // 2)) < (HEAD_DIM // 4)
    quarter = HEAD_DIM // 4

    def rope(xs):
        below = pltpu.roll(xs, quarter, 1)
        above = pltpu.roll(xs, LANES - quarter, 1)
        return xs * cos + jnp.where(first_half, above, below) * sin

    qk = _dot(hb, w_ref[:, OFF_Q:OFF_V])
    scale = HEAD_DIM ** -0.5 * LOG2E
    for j in range(ATTN_WIDTH // LANES):
        sl = slice(j * LANES, (j + 1) * LANES)
        q_ref[:, sl] = (rope(qk[:, sl]) * scale).astype(BF16)
    kv_ref[:, 0:KV_WIDTH] = rope(qk[:, ATTN_WIDTH:ATTN_WIDTH + KV_WIDTH]).astype(BF16)
    kv_ref[:, KV_WIDTH:2 * KV_WIDTH] = _dot(hb, w_ref[:, OFF_V:OFF_G]).astype(BF16)
    gc = 512
    for j in range(2 * D_MODEL // gc):
        z = _dot(hb, w_ref[:, OFF_G + j * gc:OFF_G + (j + 1) * gc])
        gt_ref[:, j * gc:(j + 1) * gc] = _sigmoid(z).astype(BF16)


def _in_proj(x, mods, layer, cond_fn, g, w_in, cos, sin, rope_idx):
    rows = x.shape[0]
    return pl.pallas_call(
        _in_proj_body,
        grid=(rows // TM,),
        in_specs=[pl.BlockSpec((TM, D_MODEL), lambda i: (i, 0)),
                  _mod_spec(layer, 0, cond_fn),
                  _mod_spec(layer, 1, cond_fn),
                  pl.BlockSpec((1, D_MODEL), lambda i: (0, 0)),
                  _resident(w_in),
                  pl.BlockSpec((TM, LANES), lambda i: (rope_idx(i), 0)),
                  pl.BlockSpec((TM, LANES), lambda i: (rope_idx(i), 0))],
        out_specs=[pl.BlockSpec((TM, FOURIER_WIDTH), lambda i: (i, 0)),
                   pl.BlockSpec((TM, ATTN_WIDTH), lambda i: (i, 0)),
                   pl.BlockSpec((TM, 2 * KV_WIDTH), lambda i: (i, 0)),
                   pl.BlockSpec((TM, 2 * D_MODEL), lambda i: (i, 0))],
        out_shape=[jax.ShapeDtypeStruct((rows, FOURIER_WIDTH), F32),
                   jax.ShapeDtypeStruct((rows, ATTN_WIDTH), BF16),
                   jax.ShapeDtypeStruct((rows, 2 * KV_WIDTH), BF16),
                   jax.ShapeDtypeStruct((rows, 2 * D_MODEL), BF16)],
        compiler_params=_cparams(("arbitrary",), VMEM_LIMIT),
        name="in_proj",
    )(x, mods, mods, g, w_in, cos, sin)


FFT_CB = 8


def _fft_a_body(x_ref, w_ref, o_ref):
    w = w_ref[...]
    xt = jnp.swapaxes(x_ref[...], 0, 1)
    res = jnp.stack([_dot(w, xt[c].astype(BF16)) for c in range(FFT_CB)])
    o_ref[0] = jnp.swapaxes(res, 0, 1)


def _fft_c_body(re_ref, im_ref, tc_ref, ts_ref, mc_ref, mch_ref, o_ref):
    mc = mc_ref[...]
    mch = mch_ref[...]
    xrs, xis = [], []
    for j in range(FFT_CB):
        ar = re_ref[0, j]
        ai = im_ref[0, j]
        tc = jnp.concatenate([tc_ref[j]] * N_GROUPS, axis=1)
        ts = jnp.concatenate([ts_ref[j]] * N_GROUPS, axis=1)
        br = ar * tc + ai * ts
        bi = ai * tc - ar * ts
        x = _dot(mc, jnp.concatenate([br, bi], axis=0).astype(BF16))
        xrs.append(x[:FFT_C])
        xis.append(x[FFT_C:])
    xr = jnp.concatenate(xrs, axis=0).astype(BF16)
    xi = jnp.concatenate(xis, axis=0).astype(BF16)
    ys = []
    for g in range(N_GROUPS):
        sl = slice(g * GROUP_W, (g + 1) * GROUP_W)
        ys.append(_dot(jnp.concatenate([xr[:, sl], xi[:, sl]], axis=1), mch))
    y = jnp.concatenate(ys, axis=1).reshape(FFT_CB, FFT_C, FOURIER_WIDTH)
    o_ref[0] = jnp.swapaxes(y, 0, 1)


def _fourier_latent(u, tb, batch):
    u3 = u.reshape(batch * FFT_R, FFT_C, FOURIER_WIDTH)
    a = pl.pallas_call(
        _fft_a_body,
        grid=(batch, FFT_C // FFT_CB),
        in_specs=[pl.BlockSpec((FFT_R, FFT_CB, FOURIER_WIDTH), lambda b, j: (b, j, 0)),
                  pl.BlockSpec((2 * FFT_R, FFT_R), lambda b, j: (0, 0))],
        out_specs=pl.BlockSpec((1, 2 * FFT_R, FFT_CB, FOURIER_WIDTH), lambda b, j: (b, 0, j, 0)),
        out_shape=jax.ShapeDtypeStruct((batch, 2 * FFT_R, FFT_C, FOURIER_WIDTH), F32),
        compiler_params=_cparams(("arbitrary", "arbitrary"), VMEM_LIMIT),
        name="fft_rows",
    )(u3, tb["wa"].astype(BF16))
    nk = FFT_R // FFT_CB
    y = pl.pallas_call(
        _fft_c_body,
        grid=(batch, nk),
        in_specs=[pl.BlockSpec((1, FFT_CB, FFT_C, FOURIER_WIDTH), lambda b, k: (b, k, 0, 0)),
                  pl.BlockSpec((1, FFT_CB, FFT_C, FOURIER_WIDTH), lambda b, k: (b, nk + k, 0, 0)),
                  pl.BlockSpec((FFT_CB, FFT_C, LANES), lambda b, k: (k, 0, 0)),
                  pl.BlockSpec((FFT_CB, FFT_C, LANES), lambda b, k: (k, 0, 0)),
                  pl.BlockSpec((2 * FFT_C, 2 * FFT_C), lambda b, k: (0, 0)),
                  pl.BlockSpec((2 * GROUP_W, GROUP_W), lambda b, k: (0, 0))],
        out_specs=pl.BlockSpec((1, FFT_C, FFT_CB, FOURIER_WIDTH), lambda b, k: (b, 0, k, 0)),
        out_shape=jax.ShapeDtypeStruct((batch, FFT_C, FFT_R, FOURIER_WIDTH), F32),
        compiler_params=_cparams(("arbitrary", "arbitrary"), VMEM_LIMIT),
        name="fft_cols",
    )(a, a, tb["twc"], tb["tws"], tb["mc"].astype(BF16), tb["mch"].astype(BF16))
    return y.reshape(batch * SEQ, FOURIER_WIDTH)


def _fourier_ctx_body(u_ref, w_ref, m_ref, o_ref):
    pq = _dot(w_ref[...], u_ref[...].astype(BF16))
    p = pq[:CTX_LEN]
    q = pq[CTX_LEN:]
    m = m_ref[...]
    scale = 1.0 / math.sqrt(CTX_LEN * GROUP_W)
    for g in range(N_GROUPS):
        sl = slice(g * GROUP_W, (g + 1) * GROUP_W)
        lhs = jnp.concatenate([p[:, sl], q[:, sl]], axis=1).astype(BF16)
        o_ref[:, sl] = _dot(lhs, m) * scale


def _fourier_ctx(u, tb, batch):
    return pl.pallas_call(
        _fourier_ctx_body,
        grid=(batch,),
        in_specs=[pl.BlockSpec((CTX_LEN, FOURIER_WIDTH), lambda b: (b, 0)),
                  pl.BlockSpec((2 * CTX_LEN, CTX_LEN), lambda b: (0, 0)),
                  pl.BlockSpec((2 * GROUP_W, GROUP_W), lambda b: (0, 0))],
        out_specs=pl.BlockSpec((CTX_LEN, FOURIER_WIDTH), lambda b: (b, 0)),
        out_shape=jax.ShapeDtypeStruct((batch * CTX_LEN, FOURIER_WIDTH), F32),
        compiler_params=_cparams(("arbitrary",)),
        name="fft_ctx",
    )(u, tb["wcx"].astype(BF16), tb["mchc"].astype(BF16))


def _attn_body(sink_ref, q_ref, *refs, local):
    o_ref = refs[-1]
    if local:
        kp_ref, kc_ref, kn_ref, kx_ref = refs[:-1]
        groups = [jnp.concatenate([kp_ref[...], kn_ref[...]], axis=0),
                  jnp.concatenate([kc_ref[...], kx_ref[...]], axis=0)]
        n = pl.program_id(1)
        last = pl.num_programs(1) - 1
        qi = lax.broadcasted_iota(jnp.int32, (2 * QB, 2 * QB), 0) % QB
        kj = lax.broadcasted_iota(jnp.int32, (2 * QB, 2 * QB), 1)
        prev_ok = (kj < QB) & (kj >= qi + jnp.where(n == 0, 2 * QB, 0))
        next_ok = (kj >= QB) & (kj - QB <= qi - jnp.where(n == last, 2 * QB, 0))
        masks = [prev_ok | next_ok, None]
    else:
        groups = [refs[0][...]]
        masks = [None]
    lo_q = lax.broadcasted_iota(jnp.int32, (2 * QB, LANES), 1) < HEAD_DIM
    kmats, vmats = [], []
    for kv in groups:
        kv = kv.astype(F32)
        kcat = kv[:, :KV_WIDTH]
        vcat = kv[:, KV_WIDTH:]
        kroll = pltpu.roll(kcat, HEAD_DIM, 1)
        vroll = pltpu.roll(vcat, HEAD_DIM, 1)
        lo_k = lax.broadcasted_iota(jnp.int32, kcat.shape, 1) < HEAD_DIM
        zero = jnp.zeros_like(kcat)
        one = jnp.ones_like(vcat)
        km, vm = [], []
        for hk in range(N_KV_HEADS):
            k_own, k_other = (kcat, kroll) if hk == 0 else (kroll, kcat)
            v_own, v_other = (vcat, vroll) if hk == 0 else (vroll, vcat)
            km.append((jnp.where(lo_k, k_own, zero).astype(BF16), jnp.where(lo_k, zero, k_other).astype(BF16)))
            vm.append((jnp.where(lo_k, v_own, one).astype(BF16), jnp.where(lo_k, one, v_other).astype(BF16)))
        kmats.append(km)
        vmats.append(vm)
    pairs = [(hk, half) for hk in range(N_KV_HEADS) for half in range(2)]
    first_slab = lax.broadcasted_iota(jnp.int32, (2 * QB, 1), 0) < QB
    scores = []
    for hk, half in pairs:
        qs = jnp.concatenate([q_ref[:, (2 * hk) * LANES:(2 * hk + 1) * LANES],
                              q_ref[:, (2 * hk + 1) * LANES:(2 * hk + 2) * LANES]], axis=0)
        parts = []
        for km, mask in zip(kmats, masks):
            s = lax.dot_general(qs, km[hk][half], (((1,), (1,)), ((), ())), preferred_element_type=F32)
            parts.append(s if mask is None else jnp.where(mask, s, NEG))
        scores.append(parts)
    sinks = [jnp.where(first_slab, sink_ref[4 * hk + half], sink_ref[4 * hk + 2 + half]) * LOG2E
             for hk, half in pairs]
    maxes = []
    for parts, sk in zip(scores, sinks):
        m = sk
        for s in parts:
            m = jnp.maximum(m, jnp.max(s, axis=1, keepdims=True))
        maxes.append(m)
    sink_p = [jnp.exp2(sk - m) for sk, m in zip(sinks, maxes)]
    pvs = []
    for parts, m, (hk, half) in zip(scores, maxes, pairs):
        pv = None
        for s, vm in zip(parts, vmats):
            term = _dot(jnp.exp2(s - m).astype(BF16), vm[hk][half])
            pv = term if pv is None else pv + term
        pvs.append(pv)
    for hk in range(N_KV_HEADS):
        lo, hi = pvs[2 * hk], pvs[2 * hk + 1]
        num = jnp.where(lo_q, lo, hi)
        den = (pltpu.roll(jnp.where(lo_q, hi, lo), HEAD_DIM, 1)
               + jnp.where(lo_q, sink_p[2 * hk], sink_p[2 * hk + 1]))
        out = (num / den).astype(BF16)
        o_ref[:, (2 * hk) * LANES:(2 * hk + 1) * LANES] = out[:QB]
        o_ref[:, (2 * hk + 1) * LANES:(2 * hk + 2) * LANES] = out[QB:]


def _attention_latent(q, kv, kv_ctx, sink, batch):
    nb = SEQ // QB
    kvw = 2 * KV_WIDTH
    return pl.pallas_call(
        functools.partial(_attn_body, local=True),
        grid=(batch, nb),
        in_specs=[pl.BlockSpec(memory_space=pltpu.SMEM),
                  pl.BlockSpec((QB, ATTN_WIDTH), lambda b, n: (b * nb + n, 0)),
                  pl.BlockSpec((QB, kvw), lambda b, n: (b * nb + jnp.maximum(n - 1, 0), 0)),
                  pl.BlockSpec((QB, kvw), lambda b, n: (b * nb + n, 0)),
                  pl.BlockSpec((QB, kvw), lambda b, n: (b * nb + jnp.minimum(n + 1, nb - 1), 0)),
                  pl.BlockSpec((CTX_LEN, kvw), lambda b, n: (b, 0))],
        out_specs=pl.BlockSpec((QB, ATTN_WIDTH), lambda b, n: (b * nb + n, 0)),
        out_shape=jax.ShapeDtypeStruct((batch * SEQ, ATTN_WIDTH), BF16),
        compiler_params=_cparams(("arbitrary", "arbitrary")),
        name="attn_latent",
    )(sink, q, kv, kv, kv, kv_ctx)


def _attention_ctx(q, kv, sink, batch):
    nb = CTX_LEN // QB
    return pl.pallas_call(
        functools.partial(_attn_body, local=False),
        grid=(batch, nb),
        in_specs=[pl.BlockSpec(memory_space=pltpu.SMEM),
                  pl.BlockSpec((QB, ATTN_WIDTH), lambda b, n: (b * nb + n, 0)),
                  pl.BlockSpec((CTX_LEN, 2 * KV_WIDTH), lambda b, n: (b, 0))],
        out_specs=pl.BlockSpec((QB, ATTN_WIDTH), lambda b, n: (b * nb + n, 0)),
        out_shape=jax.ShapeDtypeStruct((batch * CTX_LEN, ATTN_WIDTH), BF16),
        compiler_params=_cparams(("arbitrary", "arbitrary")),
        name="attn_ctx",
    )(sink, q, kv)


def _merge_body(x_ref, fm_ref, ao_ref, gt_ref, wf_ref, wa_ref, wo_ref, ga_ref, *refs, with_h2):
    gt = gt_ref[...].astype(F32)
    y = (gt[:, :D_MODEL] * _dot(fm_ref[...], wf_ref[...])
         + gt[:, D_MODEL:] * _dot(ao_ref[...], wa_ref[...]))
    xn = x_ref[...] + ga_ref[0] * _dot(y, wo_ref[...])
    if with_h2:
        sh_ref, sc_ref, g_ref, xo_ref, h_ref = refs
        h_ref[...] = _rms_mod(xn, g_ref[...], sh_ref[0], sc_ref[0]).astype(BF16)
    else:
        (xo_ref,) = refs
    xo_ref[...] = xn


def _merge(x, fm, ao, gates, wf, wa, wo, mods, layer, cond_fn, g2, with_h2):
    rows = x.shape[0]
    row_spec = lambda w: pl.BlockSpec((TM, w), lambda i: (i, 0))
    in_specs = [row_spec(D_MODEL), row_spec(FOURIER_WIDTH), row_spec(ATTN_WIDTH), row_spec(2 * D_MODEL),
                _resident(wf), _resident(wa), _resident(wo), _mod_spec(layer, 2, cond_fn)]
    args = [x, fm, ao, gates, wf, wa, wo, mods]
    out_specs = [row_spec(D_MODEL)]
    out_shape = [jax.ShapeDtypeStruct((rows, D_MODEL), F32)]
    if with_h2:
        in_specs += [_mod_spec(layer, 3, cond_fn), _mod_spec(layer, 4, cond_fn),
                     pl.BlockSpec((1, D_MODEL), lambda i: (0, 0))]
        args += [mods, mods, g2]
        out_specs.append(row_spec(D_MODEL))
        out_shape.append(jax.ShapeDtypeStruct((rows, D_MODEL), BF16))
    return pl.pallas_call(
        functools.partial(_merge_body, with_h2=with_h2),
        grid=(rows // TM,),
        in_specs=in_specs, out_specs=out_specs, out_shape=out_shape,
        compiler_params=_cparams(("arbitrary",), VMEM_LIMIT),
        name="merge",
    )(*args)


def _ffn_body(x_ref, h_ref, wg_ref, wu_ref, wd_ref, ga_ref, o_ref):
    h = h_ref[...]
    g = _dot(h, wg_ref[...])
    u = _dot(h, wu_ref[...])
    a = g * _sigmoid(g) * u
    o_ref[...] = x_ref[...] + ga_ref[0] * _dot(a, wd_ref[...])


def _ffn_dense(x, h2, wg, wu, wd, mods, layer, cond_fn):
    rows = x.shape[0]
    return pl.pallas_call(
        _ffn_body,
        grid=(rows // TM,),
        in_specs=[pl.BlockSpec((TM, D_MODEL), lambda i: (i, 0)),
                  pl.BlockSpec((TM, D_MODEL), lambda i: (i, 0)),
                  _resident(wg), _resident(wu), _resident(wd),
                  _mod_spec(layer, 5, cond_fn)],
        out_specs=pl.BlockSpec((TM, D_MODEL), lambda i: (i, 0)),
        out_shape=jax.ShapeDtypeStruct((rows, D_MODEL), F32),
        compiler_params=_cparams(("arbitrary",), VMEM_LIMIT),
        name="ffn_dense",
    )(x, h2, wg, wu, wd, mods)


INFO_E0, INFO_E1, INFO_R0, INFO_R1, INFO_W0, INFO_W1 = range(6)


def _router_body(x_ref, sh_ref, sc_ref, g_ref, wr_ref, hb_ref, info_ref, cnt_ref):
    h = _rms_mod(x_ref[...], g_ref[...], sh_ref[0], sc_ref[0])
    hb_ref[...] = h.astype(BF16)
    logits = jnp.dot(h, wr_ref[...], precision=lax.Precision.HIGHEST, preferred_element_type=F32)
    lane = lax.broadcasted_iota(jnp.int32, logits.shape, 1)
    neg_inf = jnp.float32(-jnp.inf)
    lg = jnp.where(lane < N_EXPERTS, logits, neg_inf)
    v0 = jnp.max(lg, axis=1, keepdims=True)
    i0 = jnp.min(jnp.where(lg == v0, lane, LANES), axis=1, keepdims=True)
    oh0 = lane == i0
    lg1 = jnp.where(oh0, neg_inf, lg)
    v1 = jnp.max(lg1, axis=1, keepdims=True)
    i1 = jnp.min(jnp.where(lg1 == v1, lane, LANES), axis=1, keepdims=True)
    oh1 = lane == i1
    e = jnp.exp(v1 - v0)
    w0 = 1.0 / (1.0 + e)
    w1 = e / (1.0 + e)
    oh = jnp.where(oh0 | oh1, 1.0, 0.0)
    row = lax.broadcasted_iota(jnp.int32, (TD, TD), 0)
    col = lax.broadcasted_iota(jnp.int32, (TD, TD), 1)
    tri = jnp.where(row > col, 1.0, 0.0).astype(BF16)
    before = _dot(tri, oh.astype(BF16))
    r0 = jnp.sum(jnp.where(oh0, before, 0.0), axis=1, keepdims=True)
    r1 = jnp.sum(jnp.where(oh1, before, 0.0), axis=1, keepdims=True)
    cnt_ref[0] = jnp.broadcast_to(jnp.sum(oh, axis=0, keepdims=True), (SUBLANES, LANES))
    info = jnp.zeros(logits.shape, F32)
    for idx, val in ((INFO_E0, i0.astype(F32)), (INFO_E1, i1.astype(F32)), (INFO_R0, r0),
                     (INFO_R1, r1), (INFO_W0, w0), (INFO_W1, w1)):
        info = jnp.where(lane == idx, val, info)
    info_ref[...] = info


def _router(x, mods, layer, cond_fn, g2, wr_pad):
    rows = x.shape[0]
    return pl.pallas_call(
        _router_body,
        grid=(rows // TD,),
        in_specs=[pl.BlockSpec((TD, D_MODEL), lambda i: (i, 0)),
                  _mod_spec(layer, 3, cond_fn), _mod_spec(layer, 4, cond_fn),
                  pl.BlockSpec((1, D_MODEL), lambda i: (0, 0)),
                  pl.BlockSpec((D_MODEL, LANES), lambda i: (0, 0))],
        out_specs=[pl.BlockSpec((TD, D_MODEL), lambda i: (i, 0)),
                   pl.BlockSpec((TD, LANES), lambda i: (i, 0)),
                   pl.BlockSpec((1, SUBLANES, LANES), lambda i: (i, 0, 0))],
        out_shape=[jax.ShapeDtypeStruct((rows, D_MODEL), BF16),
                   jax.ShapeDtypeStruct((rows, LANES), F32),
                   jax.ShapeDtypeStruct((rows // TD, SUBLANES, LANES), F32)],
        compiler_params=_cparams(("arbitrary",), VMEM_LIMIT),
        name="router",
    )(x, mods, mods, g2, wr_pad)


def _segment_copies(src_ref, src_off, dst_ref, dst_off, len8, sem, bits, wait):
    for k in reversed(range(bits)):
        size = SUBLANES << k
        done = ((len8 >> (k + 1)) << (k + 1)) * SUBLANES

        @pl.when(((len8 >> k) & 1) == 1)
        def _(size=size, done=done):
            cp = pltpu.make_async_copy(
                src_ref.at[pl.ds(pl.multiple_of(src_off + done, SUBLANES), size)],
                dst_ref.at[pl.ds(pl.multiple_of(dst_off + done, SUBLANES), size)], sem)
            if wait:
                cp.wait()
            else:
                cp.start()


def _slot_positions(e0, e1, r0, r1, seg_ref, base):
    pos0, pos1 = r0, r1
    for e in range(N_EXPERTS):
        start = seg_ref[base + e].astype(F32)
        pos0 = pos0 + jnp.where(e0 == e, start, 0.0)
        pos1 = pos1 + jnp.where(e1 == e, start, 0.0)
    return pos0.astype(jnp.int32), pos1.astype(jnp.int32)


def _dispatch_body(seg_ref, dst_ref, len_ref, tdst_ref, tlen_ref, nt_ref, hb_ref, info_ref, xs_ref,
                   buf_ref, zero_ref, sems, zero_sem, *, min_tiles):
    i = pl.program_id(0)
    last = pl.num_programs(0) - 1
    base = i * N_EXPERTS
    cur = i % 2

    def scatter(tile, slot, wait):
        for e in range(N_EXPERTS):
            k = tile * N_EXPERTS + e
            _segment_copies(buf_ref.at[slot], seg_ref[k], xs_ref, dst_ref[k], len_ref[k],
                            sems.at[slot], SEG_BITS, wait=wait)

    info_t = info_ref[...].T
    row = lambda k: info_t[k:k + 1, :]
    pos0, pos1 = _slot_positions(row(INFO_E0), row(INFO_E1), row(INFO_R0), row(INFO_R1), seg_ref, base)
    slot = lax.broadcasted_iota(jnp.int32, (SB, TD), 0)
    p0 = slot == pos0
    p1 = slot == pos1
    perm = jnp.where(p0 | p1, 1.0, 0.0).astype(BF16)
    buf_ref[cur, :, :D_MODEL] = _dot(perm, hb_ref[...])
    wsel = jnp.where(p0, row(INFO_W0), 0.0) + jnp.where(p1, row(INFO_W1), 0.0)
    buf_ref[cur, :, D_MODEL:] = jnp.broadcast_to(jnp.sum(wsel, axis=1, keepdims=True), (SB, LANES))
    scatter(i, cur, wait=False)

    @pl.when(i > 0)
    def _():
        scatter(i - 1, 1 - cur, wait=True)

    @pl.when(i == last)
    def _():
        zero_ref[...] = jnp.zeros_like(zero_ref)
        spare_tiles = range(min_tiles, xs_ref.shape[0] // TMP)
        spare = [pltpu.make_async_copy(zero_ref, xs_ref.at[pl.ds(j * TMP, TMP)], zero_sem) for j in spare_tiles]
        for wait in (False, True):
            for e in range(N_EXPERTS):
                _segment_copies(zero_ref, 0, xs_ref, tdst_ref[e], tlen_ref[e], zero_sem, TAIL_BITS, wait=wait)
            for j, cp in zip(spare_tiles, spare):
                @pl.when(j >= nt_ref[0])
                def _(cp=cp, wait=wait):
                    if wait:
                        cp.wait()
                    else:
                        cp.start()
        scatter(i, cur, wait=True)


def _dispatch(seg, dst, len8, tail_dst, tail_len8, n_tiles, hb, info, m_rows):
    rows = hb.shape[0]
    return pl.pallas_call(
        functools.partial(_dispatch_body, min_tiles=2 * rows // TMP),
        grid_spec=pltpu.PrefetchScalarGridSpec(
            num_scalar_prefetch=6,
            grid=(rows // TD,),
            in_specs=[pl.BlockSpec((TD, D_MODEL), lambda i, *_: (i, 0)),
                      pl.BlockSpec((TD, LANES), lambda i, *_: (i, 0))],
            out_specs=pl.BlockSpec(memory_space=pl.ANY),
            scratch_shapes=[pltpu.VMEM((2, SB, XS_W), F32), pltpu.VMEM((TMP, XS_W), F32),
                            pltpu.SemaphoreType.DMA((2,)), pltpu.SemaphoreType.DMA(())]),
        out_shape=jax.ShapeDtypeStruct((m_rows, XS_W), F32),
        compiler_params=_cparams(("arbitrary",), VMEM_LIMIT),
        name="dispatch",
    )(seg, dst, len8, tail_dst, tail_len8, n_tiles, hb, info)


def _expert_body(te_ref, nh_ref, nt_ref, xs_ref, wg_ref, wu_ref, wd_ref, ys_ref, acc_ref):
    j = pl.program_id(0)
    f = pl.program_id(1)
    nf = pl.num_programs(1)
    halves = nh_ref[j]

    blocks = [slice(h * TME, (h + 1) * TME) for h in range(HALVES)]

    @pl.when((halves > 0) & (f == 0))
    def _():
        acc_ref[...] = jnp.zeros_like(acc_ref)

    for live in range(1, HALVES + 1):
        @pl.when(halves == live)
        def _(live=live):
            ups = [(_dot(xs_ref[rows, :D_MODEL], wg_ref[0]), _dot(xs_ref[rows, :D_MODEL], wu_ref[0]))
                   for rows in blocks[:live]]
            downs = [_dot(g * _sigmoid(g) * u, wd_ref[0]) for g, u in ups]
            for rows, d in zip(blocks[:live], downs):
                acc_ref[rows, :] += d

    @pl.when(f == nf - 1)
    def _():
        live_rows = lax.broadcasted_iota(jnp.int32, (TMP, 1), 0) < halves * TME
        ys_ref[...] = jnp.where(live_rows, acc_ref[...] * xs_ref[:, D_MODEL:D_MODEL + 1], 0.0)


def _experts(tile_expert, tile_halves, n_tiles, xs, wg, wu, wd):
    m_rows = xs.shape[0]
    nf = wg.shape[2] // FC_E

    def f_idx(j, f, nt):
        return jnp.where(j < nt[0], f, nf - 1)

    def j_idx(j, nt):
        return jnp.minimum(j, nt[0] - 1)

    return pl.pallas_call(
        _expert_body,
        grid_spec=pltpu.PrefetchScalarGridSpec(
            num_scalar_prefetch=3,
            grid=(m_rows // TMP, nf),
            in_specs=[pl.BlockSpec((TMP, XS_W), lambda j, f, te, nh, nt: (j_idx(j, nt), 0)),
                      pl.BlockSpec((1, D_MODEL, FC_E), lambda j, f, te, nh, nt: (te[j], 0, f_idx(j, f, nt))),
                      pl.BlockSpec((1, D_MODEL, FC_E), lambda j, f, te, nh, nt: (te[j], 0, f_idx(j, f, nt))),
                      pl.BlockSpec((1, FC_E, D_MODEL), lambda j, f, te, nh, nt: (te[j], f_idx(j, f, nt), 0))],
            out_specs=pl.BlockSpec((TMP, D_MODEL), lambda j, f, te, nh, nt: (j, 0)),
            scratch_shapes=[pltpu.VMEM((TMP, D_MODEL), F32)]),
        out_shape=jax.ShapeDtypeStruct((m_rows, D_MODEL), F32),
        compiler_params=_cparams(("arbitrary", "arbitrary"), VMEM_LIMIT),
        name="experts",
    )(tile_expert, tile_halves, n_tiles, xs, wg, wu, wd)


def _combine_body(seg_ref, dst_ref, len_ref, ys_ref, x_ref, info_ref, ga_ref, g_ref, o_ref, buf_ref, sems):
    i = pl.program_id(0)
    base = i * N_EXPERTS
    cur = i % 2

    def gather(tile, slot, wait):
        for e in range(N_EXPERTS):
            k = tile * N_EXPERTS + e
            _segment_copies(ys_ref, dst_ref[k], buf_ref.at[slot], seg_ref[k], len_ref[k],
                            sems.at[slot], SEG_BITS, wait=wait)

    @pl.when(i == 0)
    def _():
        buf_ref[...] = jnp.zeros_like(buf_ref)
        gather(0, 0, wait=False)

    @pl.when(i + 1 < pl.num_programs(0))
    def _():
        gather(i + 1, 1 - cur, wait=False)

    info = info_ref[...]
    col = lambda k: info[:, k:k + 1]
    pos0, pos1 = _slot_positions(col(INFO_E0), col(INFO_E1), col(INFO_R0), col(INFO_R1), seg_ref, base)
    slot = lax.broadcasted_iota(jnp.int32, (TD, SB), 1)
    unperm = jnp.where((slot == pos0) | (slot == pos1), 1.0, 0.0).astype(BF16)
    gather(i, cur, wait=True)
    y = _dot(unperm, buf_ref[cur])
    xn = x_ref[...] + ga_ref[0] * y
    r = lax.rsqrt(jnp.mean(xn * xn, axis=-1, keepdims=True) + EPS)
    o_ref[...] = (xn * r) * g_ref[...]


def _combine(seg, dst, len8, ys, x, info, mods, layer, cond_fn, final_g):
    rows = x.shape[0]
    return pl.pallas_call(
        _combine_body,
        grid_spec=pltpu.PrefetchScalarGridSpec(
            num_scalar_prefetch=3,
            grid=(rows // TD,),
            in_specs=[pl.BlockSpec(memory_space=pl.ANY),
                      pl.BlockSpec((TD, D_MODEL), lambda i, *_: (i, 0)),
                      pl.BlockSpec((TD, LANES), lambda i, *_: (i, 0)),
                      _mod_spec(layer, 5, cond_fn),
                      pl.BlockSpec((1, D_MODEL), lambda i, *_: (0, 0))],
            out_specs=pl.BlockSpec((TD, D_MODEL), lambda i, *_: (i, 0)),
            scratch_shapes=[pltpu.VMEM((2, SB, D_MODEL), F32), pltpu.SemaphoreType.DMA((2,))]),
        out_shape=jax.ShapeDtypeStruct((rows, D_MODEL), F32),
        compiler_params=_cparams(("arbitrary",), VMEM_LIMIT),
        name="combine",
    )(seg, dst, len8, ys, x, info, mods, final_g)


def _moe(x, mods, layer, cond_fn, g2, w_router, wg, wu, wd, final_g):
    rows = x.shape[0]
    n_tok_tiles = rows // TD
    wr_pad = jnp.pad(w_router, ((0, 0), (0, LANES - N_EXPERTS)))
    hb, info, cnt = _router(x, mods, layer, cond_fn, g2, wr_pad)
    counts = cnt[:, 0, :N_EXPERTS].astype(jnp.int32)
    seg_len = (counts + SUBLANES - 1) // SUBLANES * SUBLANES
    seg_start = jnp.cumsum(seg_len, axis=1) - seg_len
    group = jnp.sum(seg_len, axis=0)
    group_halves = (group + TME - 1) // TME
    group_tiles = (group + TMP - 1) // TMP
    tile_start = jnp.cumsum(group_tiles) - group_tiles
    group_base = tile_start * TMP
    dst = group_base[None, :] + jnp.cumsum(seg_len, axis=0) - seg_len
    tail_dst = group_base + group
    tail_len8 = (group_tiles * TMP - group) // SUBLANES
    max_tiles = (2 * rows + n_tok_tiles * N_EXPERTS * (SUBLANES - 1) + TMP - 1) // TMP + N_EXPERTS
    n_tiles = jnp.sum(group_tiles).astype(jnp.int32).reshape(1)
    tiles = jnp.arange(max_tiles, dtype=jnp.int32)
    tid = jnp.minimum(tiles, n_tiles[0] - 1)
    tile_expert = (jnp.sum(tid[:, None] >= tile_start[None, :], axis=1) - 1).astype(jnp.int32)
    in_group = tid - tile_start[tile_expert]
    tile_halves = jnp.clip(group_halves[tile_expert] - HALVES * in_group, 0, HALVES)
    tile_halves = jnp.where(tiles < n_tiles[0], tile_halves, 0).astype(jnp.int32)
    flat = lambda a: a.reshape(-1).astype(jnp.int32)
    seg, dst, len8 = flat(seg_start), flat(dst), flat(seg_len // SUBLANES)
    xs = _dispatch(seg, dst, len8, flat(tail_dst), flat(tail_len8), n_tiles, hb, info, max_tiles * TMP)
    ys = _experts(tile_expert, tile_halves, n_tiles, xs, wg, wu, wd)
    return _combine(seg, dst, len8, ys, x, info, mods, layer, cond_fn, final_g)


def kernel(x, c, ctx, c_ctx, w_mod, b_mod, norm1_g, norm2_g, w_in, sink, w_fourier, w_attn, w_out,
           w_gate_d, w_up_d, w_down_d, w_router, w_gate_e, w_up_e, w_down_e, final_g):
    batch, seq, d = x.shape
    assert (seq, d) == (SEQ, D_MODEL) and ctx.shape == (batch, CTX_LEN, D_MODEL)
    tb = {k: jnp.asarray(v) for k, v in _tables().items()}
    tiles_per_batch = SEQ // TM
    lat_cond = lambda i: i // tiles_per_batch
    ctx_cond = lambda i: 2
    lat_rope = lambda i: i % tiles_per_batch
    ctx_rope = lambda i: 0

    cv = jnp.zeros((SUBLANES, D_MODEL), F32).at[:batch].set(c).at[batch].set(c_ctx)
    mods = _modulation(cv, w_mod, b_mod)

    xl = x.reshape(batch * SEQ, D_MODEL)
    xc = ctx.reshape(batch * CTX_LEN, D_MODEL)
    out = None
    for l in range(DEPTH):
        last = l == DEPTH - 1
        g1 = norm1_g[l].reshape(1, D_MODEL)
        g2 = norm2_g[l].reshape(1, D_MODEL)
        wf, wa, wo = w_fourier[l], w_attn[l], w_out[l]
        u, q, kv, gates = _in_proj(xl, mods, l, lat_cond, g1, w_in[l], tb["cos"], tb["sin"], lat_rope)
        uc, qc, kvc, gates_c = _in_proj(xc, mods, l, ctx_cond, g1, w_in[l], tb["cos1"], tb["sin0"], ctx_rope)
        fm = _fourier_latent(u, tb, batch)
        ao = _attention_latent(q, kv, kvc, sink[l], batch)
        if l % 2 == 0:
            i = l // 2
            xl, h2 = _merge(xl, fm, ao, gates, wf, wa, wo, mods, l, lat_cond, g2, True)
            wg, wu, wd = w_gate_d[i], w_up_d[i], w_down_d[i]
            xl = _ffn_dense(xl, h2, wg, wu, wd, mods, l, lat_cond)
            if not last:
                fmc = _fourier_ctx(uc, tb, batch)
                aoc = _attention_ctx(qc, kvc, sink[l], batch)
                xc, h2c = _merge(xc, fmc, aoc, gates_c, wf, wa, wo, mods, l, ctx_cond, g2, True)
                xc = _ffn_dense(xc, h2c, wg, wu, wd, mods, l, ctx_cond)
            if last:
                raise NotImplementedError("final norm is fused into the routed-expert combine")
        else:
            i = l // 2
            (xl,) = _merge(xl, fm, ao, gates, wf, wa, wo, mods, l, lat_cond, g2, False)
            if not last:
                raise NotImplementedError("context update through a routed-expert layer")
            out = _moe(xl, mods, l, lambda t: t // (SEQ // TD), g2, w_router[i], w_gate_e[i], w_up_e[i], w_down_e[i],
                       final_g.reshape(1, D_MODEL))
    return out.reshape(batch, SEQ, D_MODEL)
```

```python
import functools
import math

import numpy as np
import jax
import jax.numpy as jnp
from jax import lax
from jax.experimental import pallas as pl
from jax.experimental.pallas import tpu as pltpu

F32 = jnp.float32
BF16 = jnp.bfloat16

D_MODEL = 1024
SEQ = 8192
DEPTH = 2
GRID_W = 64
CTX_LEN = 256
N_GROUPS = 4
GROUP_W = 128
FOURIER_WIDTH = N_GROUPS * GROUP_W
N_Q_HEADS = 8
N_KV_HEADS = 2
HEAD_DIM = 64
ATTN_WIDTH = N_Q_HEADS * HEAD_DIM
KV_WIDTH = N_KV_HEADS * HEAD_DIM
WINDOW = 128
ROPE_BASE = 10000.0
OFF_Q = FOURIER_WIDTH
OFF_K = OFF_Q + ATTN_WIDTH
OFF_V = OFF_K + KV_WIDTH
OFF_G = OFF_V + KV_WIDTH
IN_WIDTH = OFF_G + 2 * D_MODEL
N_EXPERTS = 8
N_MOD = 6
EPS = 1e-6
NEG = -1e30
LOG2E = 1.0 / math.log(2.0)

LANES = 128
SUBLANES = 8

TM = 256
QB = 128
FFT_R = SEQ // GRID_W
FFT_C = GRID_W
TME = 512
HALVES = 2
TMP = HALVES * TME
TD = 512
SB = 2 * TD + N_EXPERTS * SUBLANES
XS_W = D_MODEL + LANES
SEG_BITS = TD.bit_length() - 3
TAIL_BITS = TMP.bit_length() - 4
FC_E = 512
VMEM_LIMIT = 56 * 1024 * 1024


def _cparams(sem, vmem=None):
    return pltpu.CompilerParams(dimension_semantics=sem, vmem_limit_bytes=vmem)


def _dot(a, b):
    return lax.dot_general(a, b, (((1,), (0,)), ((), ())), preferred_element_type=F32)


def _resident(a, layer):
    return pl.BlockSpec((None,) + a.shape[1:], lambda i, *_: (layer, 0, 0), pipeline_mode=pl.Buffered(1))


def _sigmoid(x):
    return 1.0 / (1.0 + jnp.exp(-x))


def _rms_mod(x, g, sh, sc):
    r = lax.rsqrt(jnp.mean(x * x, axis=-1, keepdims=True) + EPS)
    return (x * r) * g * (1.0 + sc) + sh


def _dft_cs(n):
    k = np.arange(n, dtype=np.float64)
    a = 2.0 * np.pi * np.outer(k, k) / n
    return np.cos(a), np.sin(a)


@functools.lru_cache(maxsize=None)
def _tables():
    c128, s128 = _dft_cs(FFT_R)
    c64, s64 = _dft_cs(FFT_C)
    c256, s256 = _dft_cs(CTX_LEN)
    wa = np.concatenate([c128, -s128], axis=0)
    k1 = np.arange(FFT_R, dtype=np.float64)[:, None]
    cc = np.arange(FFT_C, dtype=np.float64)[None, :]
    ang = 2.0 * np.pi * k1 * cc / SEQ
    twc = np.broadcast_to(np.cos(ang)[:, :, None], (FFT_R, FFT_C, LANES))
    tws = np.broadcast_to(np.sin(ang)[:, :, None], (FFT_R, FFT_C, LANES))
    mc = np.block([[c64, s64], [-s64, c64]])
    cg, sg = _dft_cs(GROUP_W)
    mch = np.concatenate([cg, sg], axis=0) / math.sqrt(SEQ * GROUP_W)
    wcx = np.concatenate([c256, s256], axis=0)
    mchc = np.concatenate([cg, -sg], axis=0)
    n_freq = HEAD_DIM // 4
    inv = ROPE_BASE ** (-np.arange(n_freq, dtype=np.float64) / n_freq)
    t = np.arange(SEQ)
    rows = (t // GRID_W).astype(np.float64)[:, None] * inv
    cols = (t % GRID_W).astype(np.float64)[:, None] * inv
    cos_h = np.concatenate([np.cos(rows), np.cos(rows), np.cos(cols), np.cos(cols)], axis=1)
    sin_h = np.concatenate([-np.sin(rows), np.sin(rows), -np.sin(cols), np.sin(cols)], axis=1)
    cos_t = np.concatenate([cos_h, cos_h], axis=1)
    sin_t = np.concatenate([sin_h, sin_h], axis=1)
    f = lambda a: np.ascontiguousarray(a, dtype=np.float32)
    return dict(wa=f(wa), twc=f(twc), tws=f(tws), mc=f(mc), mch=f(mch), wcx=f(wcx), mchc=f(mchc),
                cos=f(cos_t), sin=f(sin_t),
                cos1=np.ones((TM, LANES), np.float32), sin0=np.zeros((TM, LANES), np.float32))


def _mod_body(cv_ref, w_ref, b_ref, o_ref):
    c = cv_ref[...]
    s = c * _sigmoid(c)
    o_ref[0, 0] = jnp.dot(s, w_ref[0], precision=lax.Precision.HIGHEST,
                          preferred_element_type=F32) + b_ref[0]


def _modulation(cv, w_mod, b_mod):
    out = pl.pallas_call(
        _mod_body,
        grid=(DEPTH, N_MOD),
        in_specs=[pl.BlockSpec((SUBLANES, D_MODEL), lambda l, k: (0, 0)),
                  pl.BlockSpec((1, D_MODEL, D_MODEL), lambda l, k: (l, 0, k)),
                  pl.BlockSpec((1, 1, D_MODEL), lambda l, k: (l * N_MOD + k, 0, 0))],
        out_specs=pl.BlockSpec((1, 1, SUBLANES, D_MODEL), lambda l, k: (l, k, 0, 0)),
        out_shape=jax.ShapeDtypeStruct((DEPTH, N_MOD, SUBLANES, D_MODEL), F32),
        compiler_params=_cparams(("arbitrary", "arbitrary")),
        name="modulation",
    )(cv, w_mod, b_mod.reshape(DEPTH * N_MOD, 1, D_MODEL))
    out = jnp.transpose(out[:, :, :3, :], (0, 2, 1, 3))
    return out.reshape(DEPTH * 3 * N_MOD, 1, D_MODEL)


def _mod_spec(layer, k, cond_fn):
    return pl.BlockSpec((1, 1, D_MODEL),
                        lambda i, *_: ((layer * 3 + cond_fn(i)) * N_MOD + k, 0, 0))


def _in_proj_body(x_ref, sh_ref, sc_ref, g_ref, w_ref, cos_ref, sin_ref,
                  u_ref, q_ref, kv_ref, gt_ref):
    hb = _rms_mod(x_ref[...], g_ref[...], sh_ref[0], sc_ref[0]).astype(BF16)
    u_ref[...] = _dot(hb, w_ref[:, 0:OFF_Q]).astype(BF16)
    cos = cos_ref[...]
    sin = sin_ref[...]
    lane = lax.broadcasted_iota(jnp.int32, cos.shape, 1)
    first_half = (lane % (HEAD_DIM // 2)) < (HEAD_DIM // 4)
    quarter = HEAD_DIM // 4

    def rope(xs):
        below = pltpu.roll(xs, quarter, 1)
        above = pltpu.roll(xs, LANES - quarter, 1)
        return xs * cos + jnp.where(first_half, above, below) * sin

    qk = _dot(hb, w_ref[:, OFF_Q:OFF_V])
    scale = HEAD_DIM ** -0.5 * LOG2E
    for j in range(ATTN_WIDTH // LANES):
        sl = slice(j * LANES, (j + 1) * LANES)
        q_ref[:, sl] = (rope(qk[:, sl]) * scale).astype(BF16)
    kv_ref[:, 0:KV_WIDTH] = rope(qk[:, ATTN_WIDTH:ATTN_WIDTH + KV_WIDTH]).astype(BF16)
    kv_ref[:, KV_WIDTH:2 * KV_WIDTH] = _dot(hb, w_ref[:, OFF_V:OFF_G]).astype(BF16)
    gc = 512
    for j in range(2 * D_MODEL // gc):
        z = _dot(hb, w_ref[:, OFF_G + j * gc:OFF_G + (j + 1) * gc])
        gt_ref[:, j * gc:(j + 1) * gc] = _sigmoid(z).astype(BF16)


def _in_proj(x, mods, layer, cond_fn, g, w_in, cos, sin, rope_idx):
    rows = x.shape[0]
    return pl.pallas_call(
        _in_proj_body,
        grid=(rows // TM,),
        in_specs=[pl.BlockSpec((TM, D_MODEL), lambda i: (i, 0)),
                  _mod_spec(layer, 0, cond_fn),
                  _mod_spec(layer, 1, cond_fn),
                  pl.BlockSpec((1, D_MODEL), lambda i: (0, 0)),
                  _resident(w_in, layer),
                  pl.BlockSpec((TM, LANES), lambda i: (rope_idx(i), 0)),
                  pl.BlockSpec((TM, LANES), lambda i: (rope_idx(i), 0))],
        out_specs=[pl.BlockSpec((TM, FOURIER_WIDTH), lambda i: (i, 0)),
                   pl.BlockSpec((TM, ATTN_WIDTH), lambda i: (i, 0)),
                   pl.BlockSpec((TM, 2 * KV_WIDTH), lambda i: (i, 0)),
                   pl.BlockSpec((TM, 2 * D_MODEL), lambda i: (i, 0))],
        out_shape=[jax.ShapeDtypeStruct((rows, FOURIER_WIDTH), BF16),
                   jax.ShapeDtypeStruct((rows, ATTN_WIDTH), BF16),
                   jax.ShapeDtypeStruct((rows, 2 * KV_WIDTH), BF16),
                   jax.ShapeDtypeStruct((rows, 2 * D_MODEL), BF16)],
        compiler_params=_cparams(("arbitrary",), VMEM_LIMIT),
        name="in_proj",
    )(x, mods, mods, g, w_in, cos, sin)


FFT_CB = 16


def _fft_a_body(x_ref, w_ref, o_ref):
    w = w_ref[...]
    xt = jnp.swapaxes(x_ref[...].astype(F32), 0, 1)
    res = jnp.stack([_dot(w, xt[c]) for c in range(FFT_CB)])
    o_ref[0] = jnp.swapaxes(res, 0, 1).astype(BF16)


def _fft_c_body(re_ref, im_ref, tc_ref, ts_ref, mc_ref, mch_ref, o_ref):
    mc = mc_ref[...]
    mch = mch_ref[...]
    xrs, xis = [], []
    for j in range(FFT_CB):
        ar = re_ref[0, j].astype(F32)
        ai = im_ref[0, j].astype(F32)
        tc = jnp.concatenate([tc_ref[j]] * N_GROUPS, axis=1)
        ts = jnp.concatenate([ts_ref[j]] * N_GROUPS, axis=1)
        br = ar * tc + ai * ts
        bi = ai * tc - ar * ts
        x = _dot(mc, jnp.concatenate([br, bi], axis=0))
        xrs.append(x[:FFT_C])
        xis.append(x[FFT_C:])
    xr = jnp.concatenate(xrs, axis=0).astype(BF16)
    xi = jnp.concatenate(xis, axis=0).astype(BF16)
    ys = []
    for g in range(N_GROUPS):
        sl = slice(g * GROUP_W, (g + 1) * GROUP_W)
        ys.append(_dot(jnp.concatenate([xr[:, sl], xi[:, sl]], axis=1), mch))
    y = jnp.concatenate(ys, axis=1).reshape(FFT_CB, FFT_C, FOURIER_WIDTH)
    o_ref[0] = jnp.swapaxes(y, 0, 1).astype(BF16)


def _fourier_latent(u, tb, batch):
    u3 = u.reshape(batch * FFT_R, FFT_C, FOURIER_WIDTH)
    a = pl.pallas_call(
        _fft_a_body,
        grid=(batch, FFT_C // FFT_CB),
        in_specs=[pl.BlockSpec((FFT_R, FFT_CB, FOURIER_WIDTH), lambda b, j: (b, j, 0)),
                  pl.BlockSpec((2 * FFT_R, FFT_R), lambda b, j: (0, 0))],
        out_specs=pl.BlockSpec((1, 2 * FFT_R, FFT_CB, FOURIER_WIDTH), lambda b, j: (b, 0, j, 0)),
        out_shape=jax.ShapeDtypeStruct((batch, 2 * FFT_R, FFT_C, FOURIER_WIDTH), BF16),
        compiler_params=_cparams(("arbitrary", "arbitrary"), VMEM_LIMIT),
        name="fft_rows",
    )(u3, tb["wa"].astype(BF16))
    nk = FFT_R // FFT_CB
    y = pl.pallas_call(
        _fft_c_body,
        grid=(batch, nk),
        in_specs=[pl.BlockSpec((1, FFT_CB, FFT_C, FOURIER_WIDTH), lambda b, k: (b, k, 0, 0)),
                  pl.BlockSpec((1, FFT_CB, FFT_C, FOURIER_WIDTH), lambda b, k: (b, nk + k, 0, 0)),
                  pl.BlockSpec((FFT_CB, FFT_C, LANES), lambda b, k: (k, 0, 0)),
                  pl.BlockSpec((FFT_CB, FFT_C, LANES), lambda b, k: (k, 0, 0)),
                  pl.BlockSpec((2 * FFT_C, 2 * FFT_C), lambda b, k: (0, 0)),
                  pl.BlockSpec((2 * GROUP_W, GROUP_W), lambda b, k: (0, 0))],
        out_specs=pl.BlockSpec((1, FFT_C, FFT_CB, FOURIER_WIDTH), lambda b, k: (b, 0, k, 0)),
        out_shape=jax.ShapeDtypeStruct((batch, FFT_C, FFT_R, FOURIER_WIDTH), BF16),
        compiler_params=_cparams(("arbitrary", "arbitrary"), VMEM_LIMIT),
        name="fft_cols",
    )(a, a, tb["twc"], tb["tws"], tb["mc"].astype(BF16), tb["mch"].astype(BF16))
    return y.reshape(batch * SEQ, FOURIER_WIDTH)


def _fourier_ctx_body(u_ref, w_ref, m_ref, o_ref):
    pq = _dot(w_ref[...], u_ref[...])
    p = pq[:CTX_LEN]
    q = pq[CTX_LEN:]
    m = m_ref[...]
    scale = 1.0 / math.sqrt(CTX_LEN * GROUP_W)
    for g in range(N_GROUPS):
        sl = slice(g * GROUP_W, (g + 1) * GROUP_W)
        lhs = jnp.concatenate([p[:, sl], q[:, sl]], axis=1).astype(BF16)
        o_ref[:, sl] = _dot(lhs, m) * scale


def _fourier_ctx(u, tb, batch):
    return pl.pallas_call(
        _fourier_ctx_body,
        grid=(batch,),
        in_specs=[pl.BlockSpec((CTX_LEN, FOURIER_WIDTH), lambda b: (b, 0)),
                  pl.BlockSpec((2 * CTX_LEN, CTX_LEN), lambda b: (0, 0)),
                  pl.BlockSpec((2 * GROUP_W, GROUP_W), lambda b: (0, 0))],
        out_specs=pl.BlockSpec((CTX_LEN, FOURIER_WIDTH), lambda b: (b, 0)),
        out_shape=jax.ShapeDtypeStruct((batch * CTX_LEN, FOURIER_WIDTH), F32),
        compiler_params=_cparams(("arbitrary",)),
        name="fft_ctx",
    )(u, tb["wcx"].astype(BF16), tb["mchc"].astype(BF16))


def _attn_body(sink_ref, q_ref, *refs, local):
    o_ref = refs[-1]
    if local:
        kp_ref, kc_ref, kn_ref, kx_ref = refs[:-1]
        groups = [jnp.concatenate([kp_ref[...], kn_ref[...]], axis=0),
                  jnp.concatenate([kc_ref[...], kx_ref[...]], axis=0)]
        n = pl.program_id(1)
        last = pl.num_programs(1) - 1
        qi = lax.broadcasted_iota(jnp.int32, (2 * QB, 2 * QB), 0) % QB
        kj = lax.broadcasted_iota(jnp.int32, (2 * QB, 2 * QB), 1)
        prev_ok = (kj < QB) & (kj >= qi + jnp.where(n == 0, 2 * QB, 0))
        next_ok = (kj >= QB) & (kj - QB <= qi - jnp.where(n == last, 2 * QB, 0))
        masks = [prev_ok | next_ok, None]
    else:
        groups = [refs[0][...]]
        masks = [None]
    lo_q = lax.broadcasted_iota(jnp.int32, (2 * QB, LANES), 1) < HEAD_DIM
    kmats, vmats = [], []
    for kv in groups:
        kv = kv.astype(F32)
        kcat = kv[:, :KV_WIDTH]
        vcat = kv[:, KV_WIDTH:]
        kroll = pltpu.roll(kcat, HEAD_DIM, 1)
        vroll = pltpu.roll(vcat, HEAD_DIM, 1)
        lo_k = lax.broadcasted_iota(jnp.int32, kcat.shape, 1) < HEAD_DIM
        zero = jnp.zeros_like(kcat)
        one = jnp.ones_like(vcat)
        km, vm = [], []
        for hk in range(N_KV_HEADS):
            k_own, k_other = (kcat, kroll) if hk == 0 else (kroll, kcat)
            v_own, v_other = (vcat, vroll) if hk == 0 else (vroll, vcat)
            km.append((jnp.where(lo_k, k_own, zero).astype(BF16), jnp.where(lo_k, zero, k_other).astype(BF16)))
            vm.append((jnp.where(lo_k, v_own, one).astype(BF16), jnp.where(lo_k, one, v_other).astype(BF16)))
        kmats.append(km)
        vmats.append(vm)
    pairs = [(hk, half) for hk in range(N_KV_HEADS) for half in range(2)]
    first_slab = lax.broadcasted_iota(jnp.int32, (2 * QB, 1), 0) < QB
    scores = []
    for hk, half in pairs:
        qs = jnp.concatenate([q_ref[:, (2 * hk) * LANES:(2 * hk + 1) * LANES],
                              q_ref[:, (2 * hk + 1) * LANES:(2 * hk + 2) * LANES]], axis=0)
        parts = []
        for km, mask in zip(kmats, masks):
            s = lax.dot_general(qs, km[hk][half], (((1,), (1,)), ((), ())), preferred_element_type=F32)
            parts.append(s if mask is None else jnp.where(mask, s, NEG))
        scores.append(parts)
    sinks = [jnp.where(first_slab, sink_ref[4 * hk + half], sink_ref[4 * hk + 2 + half]) * LOG2E
             for hk, half in pairs]
    maxes = []
    for parts, sk in zip(scores, sinks):
        m = sk
        for s in parts:
            m = jnp.maximum(m, jnp.max(s, axis=1, keepdims=True))
        maxes.append(m)
    sink_p = [jnp.exp2(sk - m) for sk, m in zip(sinks, maxes)]
    pvs = []
    for parts, m, (hk, half) in zip(scores, maxes, pairs):
        pv = None
        for s, vm in zip(parts, vmats):
            term = _dot(jnp.exp2(s - m).astype(BF16), vm[hk][half])
            pv = term if pv is None else pv + term
        pvs.append(pv)
    for hk in range(N_KV_HEADS):
        lo, hi = pvs[2 * hk], pvs[2 * hk + 1]
        num = jnp.where(lo_q, lo, hi)
        den = (pltpu.roll(jnp.where(lo_q, hi, lo), HEAD_DIM, 1)
               + jnp.where(lo_q, sink_p[2 * hk], sink_p[2 * hk + 1]))
        out = (num / den).astype(BF16)
        o_ref[:, (2 * hk) * LANES:(2 * hk + 1) * LANES] = out[:QB]
        o_ref[:, (2 * hk + 1) * LANES:(2 * hk + 2) * LANES] = out[QB:]


def _attention_latent(q, kv, kv_ctx, sink, batch):
    nb = SEQ // QB
    kvw = 2 * KV_WIDTH
    return pl.pallas_call(
        functools.partial(_attn_body, local=True),
        grid=(batch, nb),
        in_specs=[pl.BlockSpec(memory_space=pltpu.SMEM),
                  pl.BlockSpec((QB, ATTN_WIDTH), lambda b, n: (b * nb + n, 0)),
                  pl.BlockSpec((QB, kvw), lambda b, n: (b * nb + jnp.maximum(n - 1, 0), 0)),
                  pl.BlockSpec((QB, kvw), lambda b, n: (b * nb + n, 0)),
                  pl.BlockSpec((QB, kvw), lambda b, n: (b * nb + jnp.minimum(n + 1, nb - 1), 0)),
                  pl.BlockSpec((CTX_LEN, kvw), lambda b, n: (b, 0))],
        out_specs=pl.BlockSpec((QB, ATTN_WIDTH), lambda b, n: (b * nb + n, 0)),
        out_shape=jax.ShapeDtypeStruct((batch * SEQ, ATTN_WIDTH), BF16),
        compiler_params=_cparams(("arbitrary", "arbitrary")),
        name="attn_latent",
    )(sink, q, kv, kv, kv, kv_ctx)


def _attention_ctx(q, kv, sink, batch):
    nb = CTX_LEN // QB
    return pl.pallas_call(
        functools.partial(_attn_body, local=False),
        grid=(batch, nb),
        in_specs=[pl.BlockSpec(memory_space=pltpu.SMEM),
                  pl.BlockSpec((QB, ATTN_WIDTH), lambda b, n: (b * nb + n, 0)),
                  pl.BlockSpec((CTX_LEN, 2 * KV_WIDTH), lambda b, n: (b, 0))],
        out_specs=pl.BlockSpec((QB, ATTN_WIDTH), lambda b, n: (b * nb + n, 0)),
        out_shape=jax.ShapeDtypeStruct((batch * CTX_LEN, ATTN_WIDTH), BF16),
        compiler_params=_cparams(("arbitrary", "arbitrary")),
        name="attn_ctx",
    )(sink, q, kv)


def _merge_body(x_ref, fm_ref, ao_ref, gt_ref, wf_ref, wa_ref, wo_ref, ga_ref, *refs, with_h2):
    gt = gt_ref[...].astype(F32)
    y = (gt[:, :D_MODEL] * _dot(fm_ref[...], wf_ref[...])
         + gt[:, D_MODEL:] * _dot(ao_ref[...], wa_ref[...]))
    xn = x_ref[...] + ga_ref[0] * _dot(y, wo_ref[...])
    if with_h2:
        sh_ref, sc_ref, g_ref, xo_ref, h_ref = refs
        h_ref[...] = _rms_mod(xn, g_ref[...], sh_ref[0], sc_ref[0]).astype(BF16)
    else:
        (xo_ref,) = refs
    xo_ref[...] = xn


def _merge(x, fm, ao, gates, wf, wa, wo, mods, layer, cond_fn, g2, with_h2):
    rows = x.shape[0]
    row_spec = lambda w: pl.BlockSpec((TM, w), lambda i: (i, 0))
    in_specs = [row_spec(D_MODEL), row_spec(FOURIER_WIDTH), row_spec(ATTN_WIDTH), row_spec(2 * D_MODEL),
                _resident(wf, layer), _resident(wa, layer), _resident(wo, layer), _mod_spec(layer, 2, cond_fn)]
    args = [x, fm, ao, gates, wf, wa, wo, mods]
    out_specs = [row_spec(D_MODEL)]
    out_shape = [jax.ShapeDtypeStruct((rows, D_MODEL), F32)]
    if with_h2:
        in_specs += [_mod_spec(layer, 3, cond_fn), _mod_spec(layer, 4, cond_fn),
                     pl.BlockSpec((1, D_MODEL), lambda i: (0, 0))]
        args += [mods, mods, g2]
        out_specs.append(row_spec(D_MODEL))
        out_shape.append(jax.ShapeDtypeStruct((rows, D_MODEL), BF16))
    return pl.pallas_call(
        functools.partial(_merge_body, with_h2=with_h2),
        grid=(rows // TM,),
        in_specs=in_specs, out_specs=out_specs, out_shape=out_shape,
        compiler_params=_cparams(("arbitrary",), VMEM_LIMIT),
        name="merge",
    )(*args)


def _ffn_body(x_ref, h_ref, wg_ref, wu_ref, wd_ref, ga_ref, o_ref):
    h = h_ref[...]
    g = _dot(h, wg_ref[...])
    u = _dot(h, wu_ref[...])
    a = g * _sigmoid(g) * u
    o_ref[...] = x_ref[...] + ga_ref[0] * _dot(a, wd_ref[...])


def _ffn_dense(x, h2, wg, wu, wd, mods, layer, cond_fn):
    rows = x.shape[0]
    return pl.pallas_call(
        _ffn_body,
        grid=(rows // TM,),
        in_specs=[pl.BlockSpec((TM, D_MODEL), lambda i: (i, 0)),
                  pl.BlockSpec((TM, D_MODEL), lambda i: (i, 0)),
                  _resident(wg, layer // 2), _resident(wu, layer // 2), _resident(wd, layer // 2),
                  _mod_spec(layer, 5, cond_fn)],
        out_specs=pl.BlockSpec((TM, D_MODEL), lambda i: (i, 0)),
        out_shape=jax.ShapeDtypeStruct((rows, D_MODEL), F32),
        compiler_params=_cparams(("arbitrary",), VMEM_LIMIT),
        name="ffn_dense",
    )(x, h2, wg, wu, wd, mods)


INFO_E0, INFO_E1, INFO_R0, INFO_R1, INFO_W0, INFO_W1 = range(6)


def _router_body(x_ref, sh_ref, sc_ref, g_ref, wr_ref, hb_ref, info_ref, cnt_ref):
    h = _rms_mod(x_ref[...], g_ref[...], sh_ref[0], sc_ref[0])
    hb_ref[...] = h.astype(BF16)
    logits = jnp.dot(h, wr_ref[...], precision=lax.Precision.HIGHEST, preferred_element_type=F32)
    lane = lax.broadcasted_iota(jnp.int32, logits.shape, 1)
    neg_inf = jnp.float32(-jnp.inf)
    lg = jnp.where(lane < N_EXPERTS, logits, neg_inf)
    v0 = jnp.max(lg, axis=1, keepdims=True)
    i0 = jnp.min(jnp.where(lg == v0, lane, LANES), axis=1, keepdims=True)
    oh0 = lane == i0
    lg1 = jnp.where(oh0, neg_inf, lg)
    v1 = jnp.max(lg1, axis=1, keepdims=True)
    i1 = jnp.min(jnp.where(lg1 == v1, lane, LANES), axis=1, keepdims=True)
    oh1 = lane == i1
    e = jnp.exp(v1 - v0)
    w0 = 1.0 / (1.0 + e)
    w1 = e / (1.0 + e)
    oh = jnp.where(oh0 | oh1, 1.0, 0.0)
    row = lax.broadcasted_iota(jnp.int32, (TD, TD), 0)
    col = lax.broadcasted_iota(jnp.int32, (TD, TD), 1)
    tri = jnp.where(row > col, 1.0, 0.0).astype(BF16)
    before = _dot(tri, oh.astype(BF16))
    r0 = jnp.sum(jnp.where(oh0, before, 0.0), axis=1, keepdims=True)
    r1 = jnp.sum(jnp.where(oh1, before, 0.0), axis=1, keepdims=True)
    cnt_ref[0] = jnp.broadcast_to(jnp.sum(oh, axis=0, keepdims=True), (SUBLANES, LANES))
    info = jnp.zeros(logits.shape, F32)
    for idx, val in ((INFO_E0, i0.astype(F32)), (INFO_E1, i1.astype(F32)), (INFO_R0, r0),
                     (INFO_R1, r1), (INFO_W0, w0), (INFO_W1, w1)):
        info = jnp.where(lane == idx, val, info)
    info_ref[...] = info


def _router(x, mods, layer, cond_fn, g2, wr_pad):
    rows = x.shape[0]
    return pl.pallas_call(
        _router_body,
        grid=(rows // TD,),
        in_specs=[pl.BlockSpec((TD, D_MODEL), lambda i: (i, 0)),
                  _mod_spec(layer, 3, cond_fn), _mod_spec(layer, 4, cond_fn),
                  pl.BlockSpec((1, D_MODEL), lambda i: (0, 0)),
                  pl.BlockSpec((D_MODEL, LANES), lambda i: (0, 0))],
        out_specs=[pl.BlockSpec((TD, D_MODEL), lambda i: (i, 0)),
                   pl.BlockSpec((TD, LANES), lambda i: (i, 0)),
                   pl.BlockSpec((1, SUBLANES, LANES), lambda i: (i, 0, 0))],
        out_shape=[jax.ShapeDtypeStruct((rows, D_MODEL), BF16),
                   jax.ShapeDtypeStruct((rows, LANES), F32),
                   jax.ShapeDtypeStruct((rows // TD, SUBLANES, LANES), F32)],
        compiler_params=_cparams(("arbitrary",), VMEM_LIMIT),
        name="router",
    )(x, mods, mods, g2, wr_pad)


def _segment_copies(src_ref, src_off, dst_ref, dst_off, len8, sem, bits, wait):
    for k in reversed(range(bits)):
        size = SUBLANES << k
        done = ((len8 >> (k + 1)) << (k + 1)) * SUBLANES

        @pl.when(((len8 >> k) & 1) == 1)
        def _(size=size, done=done):
            cp = pltpu.make_async_copy(
                src_ref.at[pl.ds(pl.multiple_of(src_off + done, SUBLANES), size)],
                dst_ref.at[pl.ds(pl.multiple_of(dst_off + done, SUBLANES), size)], sem)
            if wait:
                cp.wait()
            else:
                cp.start()


def _slot_positions(e0, e1, r0, r1, seg_ref, base):
    pos0, pos1 = r0, r1
    for e in range(N_EXPERTS):
        start = seg_ref[base + e].astype(F32)
        pos0 = pos0 + jnp.where(e0 == e, start, 0.0)
        pos1 = pos1 + jnp.where(e1 == e, start, 0.0)
    return pos0.astype(jnp.int32), pos1.astype(jnp.int32)


def _dispatch_body(seg_ref, dst_ref, len_ref, tdst_ref, tlen_ref, nt_ref, hb_ref, info_ref, xs_ref,
                   buf_ref, zero_ref, sems, zero_sem, *, min_tiles):
    i = pl.program_id(0)
    last = pl.num_programs(0) - 1
    base = i * N_EXPERTS
    cur = i % 2

    def scatter(tile, slot, wait):
        for e in range(N_EXPERTS):
            k = tile * N_EXPERTS + e
            _segment_copies(buf_ref.at[slot], seg_ref[k], xs_ref, dst_ref[k], len_ref[k],
                            sems.at[slot], SEG_BITS, wait=wait)

    info_t = info_ref[...].T
    row = lambda k: info_t[k:k + 1, :]
    pos0, pos1 = _slot_positions(row(INFO_E0), row(INFO_E1), row(INFO_R0), row(INFO_R1), seg_ref, base)
    slot = lax.broadcasted_iota(jnp.int32, (SB, TD), 0)
    p0 = slot == pos0
    p1 = slot == pos1
    perm = jnp.where(p0 | p1, 1.0, 0.0).astype(BF16)
    buf_ref[cur, :, :D_MODEL] = _dot(perm, hb_ref[...])
    wsel = jnp.where(p0, row(INFO_W0), 0.0) + jnp.where(p1, row(INFO_W1), 0.0)
    buf_ref[cur, :, D_MODEL:] = jnp.broadcast_to(jnp.sum(wsel, axis=1, keepdims=True), (SB, LANES))
    scatter(i, cur, wait=False)

    @pl.when(i > 0)
    def _():
        scatter(i - 1, 1 - cur, wait=True)

    @pl.when(i == last)
    def _():
        zero_ref[...] = jnp.zeros_like(zero_ref)
        spare_tiles = range(min_tiles, xs_ref.shape[0] // TMP)
        spare = [pltpu.make_async_copy(zero_ref, xs_ref.at[pl.ds(j * TMP, TMP)], zero_sem) for j in spare_tiles]
        for wait in (False, True):
            for e in range(N_EXPERTS):
                _segment_copies(zero_ref, 0, xs_ref, tdst_ref[e], tlen_ref[e], zero_sem, TAIL_BITS, wait=wait)
            for j, cp in zip(spare_tiles, spare):
                @pl.when(j >= nt_ref[0])
                def _(cp=cp, wait=wait):
                    if wait:
                        cp.wait()
                    else:
                        cp.start()
        scatter(i, cur, wait=True)


def _dispatch(seg, dst, len8, tail_dst, tail_len8, n_tiles, hb, info, m_rows):
    rows = hb.shape[0]
    return pl.pallas_call(
        functools.partial(_dispatch_body, min_tiles=2 * rows // TMP),
        grid_spec=pltpu.PrefetchScalarGridSpec(
            num_scalar_prefetch=6,
            grid=(rows // TD,),
            in_specs=[pl.BlockSpec((TD, D_MODEL), lambda i, *_: (i, 0)),
                      pl.BlockSpec((TD, LANES), lambda i, *_: (i, 0))],
            out_specs=pl.BlockSpec(memory_space=pl.ANY),
            scratch_shapes=[pltpu.VMEM((2, SB, XS_W), F32), pltpu.VMEM((TMP, XS_W), F32),
                            pltpu.SemaphoreType.DMA((2,)), pltpu.SemaphoreType.DMA(())]),
        out_shape=jax.ShapeDtypeStruct((m_rows, XS_W), F32),
        compiler_params=_cparams(("arbitrary",), VMEM_LIMIT),
        name="dispatch",
    )(seg, dst, len8, tail_dst, tail_len8, n_tiles, hb, info)


def _expert_body(te_ref, nh_ref, nt_ref, xs_ref, wg_ref, wu_ref, wd_ref, ys_ref, acc_ref):
    j = pl.program_id(0)
    f = pl.program_id(1)
    nf = pl.num_programs(1)
    halves = nh_ref[j]

    blocks = [slice(h * TME, (h + 1) * TME) for h in range(HALVES)]

    @pl.when((halves > 0) & (f == 0))
    def _():
        acc_ref[...] = jnp.zeros_like(acc_ref)

    for live in range(1, HALVES + 1):
        @pl.when(halves == live)
        def _(live=live):
            ups = [(_dot(xs_ref[rows, :D_MODEL], wg_ref[0]), _dot(xs_ref[rows, :D_MODEL], wu_ref[0]))
                   for rows in blocks[:live]]
            downs = [_dot(g * _sigmoid(g) * u, wd_ref[0]) for g, u in ups]
            for rows, d in zip(blocks[:live], downs):
                acc_ref[rows, :] += d

    @pl.when(f == nf - 1)
    def _():
        live_rows = lax.broadcasted_iota(jnp.int32, (TMP, 1), 0) < halves * TME
        ys_ref[...] = jnp.where(live_rows, acc_ref[...] * xs_ref[:, D_MODEL:D_MODEL + 1], 0.0)


def _experts(tile_expert, tile_halves, n_tiles, xs, wg, wu, wd):
    m_rows = xs.shape[0]
    nf = wg.shape[2] // FC_E

    def f_idx(j, f, nt):
        return jnp.where(j < nt[0], f, nf - 1)

    def j_idx(j, nt):
        return jnp.minimum(j, nt[0] - 1)

    return pl.pallas_call(
        _expert_body,
        grid_spec=pltpu.PrefetchScalarGridSpec(
            num_scalar_prefetch=3,
            grid=(m_rows // TMP, nf),
            in_specs=[pl.BlockSpec((TMP, XS_W), lambda j, f, te, nh, nt: (j_idx(j, nt), 0)),
                      pl.BlockSpec((1, D_MODEL, FC_E), lambda j, f, te, nh, nt: (te[j], 0, f_idx(j, f, nt))),
                      pl.BlockSpec((1, D_MODEL, FC_E), lambda j, f, te, nh, nt: (te[j], 0, f_idx(j, f, nt))),
                      pl.BlockSpec((1, FC_E, D_MODEL), lambda j, f, te, nh, nt: (te[j], f_idx(j, f, nt), 0))],
            out_specs=pl.BlockSpec((TMP, D_MODEL), lambda j, f, te, nh, nt: (j, 0)),
            scratch_shapes=[pltpu.VMEM((TMP, D_MODEL), F32)]),
        out_shape=jax.ShapeDtypeStruct((m_rows, D_MODEL), F32),
        compiler_params=_cparams(("arbitrary", "arbitrary"), VMEM_LIMIT),
        name="experts",
    )(tile_expert, tile_halves, n_tiles, xs, wg, wu, wd)


def _combine_body(seg_ref, dst_ref, len_ref, ys_ref, x_ref, info_ref, ga_ref, g_ref, o_ref, buf_ref, sems):
    i = pl.program_id(0)
    base = i * N_EXPERTS
    cur = i % 2

    def gather(tile, slot, wait):
        for e in range(N_EXPERTS):
            k = tile * N_EXPERTS + e
            _segment_copies(ys_ref, dst_ref[k], buf_ref.at[slot], seg_ref[k], len_ref[k],
                            sems.at[slot], SEG_BITS, wait=wait)

    @pl.when(i == 0)
    def _():
        buf_ref[...] = jnp.zeros_like(buf_ref)
        gather(0, 0, wait=False)

    @pl.when(i + 1 < pl.num_programs(0))
    def _():
        gather(i + 1, 1 - cur, wait=False)

    info = info_ref[...]
    col = lambda k: info[:, k:k + 1]
    pos0, pos1 = _slot_positions(col(INFO_E0), col(INFO_E1), col(INFO_R0), col(INFO_R1), seg_ref, base)
    slot = lax.broadcasted_iota(jnp.int32, (TD, SB), 1)
    unperm = jnp.where((slot == pos0) | (slot == pos1), 1.0, 0.0).astype(BF16)
    gather(i, cur, wait=True)
    y = _dot(unperm, buf_ref[cur])
    xn = x_ref[...] + ga_ref[0] * y
    r = lax.rsqrt(jnp.mean(xn * xn, axis=-1, keepdims=True) + EPS)
    o_ref[...] = (xn * r) * g_ref[...]


def _combine(seg, dst, len8, ys, x, info, mods, layer, cond_fn, final_g):
    rows = x.shape[0]
    return pl.pallas_call(
        _combine_body,
        grid_spec=pltpu.PrefetchScalarGridSpec(
            num_scalar_prefetch=3,
            grid=(rows // TD,),
            in_specs=[pl.BlockSpec(memory_space=pl.ANY),
                      pl.BlockSpec((TD, D_MODEL), lambda i, *_: (i, 0)),
                      pl.BlockSpec((TD, LANES), lambda i, *_: (i, 0)),
                      _mod_spec(layer, 5, cond_fn),
                      pl.BlockSpec((1, D_MODEL), lambda i, *_: (0, 0))],
            out_specs=pl.BlockSpec((TD, D_MODEL), lambda i, *_: (i, 0)),
            scratch_shapes=[pltpu.VMEM((2, SB, D_MODEL), F32), pltpu.SemaphoreType.DMA((2,))]),
        out_shape=jax.ShapeDtypeStruct((rows, D_MODEL), F32),
        compiler_params=_cparams(("arbitrary",), VMEM_LIMIT),
        name="combine",
    )(seg, dst, len8, ys, x, info, mods, final_g)


def _moe(x, mods, layer, cond_fn, g2, w_router, wg, wu, wd, final_g):
    rows = x.shape[0]
    n_tok_tiles = rows // TD
    wr_pad = jnp.pad(w_router, ((0, 0), (0, LANES - N_EXPERTS)))
    hb, info, cnt = _router(x, mods, layer, cond_fn, g2, wr_pad)
    counts = cnt[:, 0, :N_EXPERTS].astype(jnp.int32)
    seg_len = (counts + SUBLANES - 1) // SUBLANES * SUBLANES
    seg_start = jnp.cumsum(seg_len, axis=1) - seg_len
    group = jnp.sum(seg_len, axis=0)
    group_halves = (group + TME - 1) // TME
    group_tiles = (group + TMP - 1) // TMP
    tile_start = jnp.cumsum(group_tiles) - group_tiles
    group_base = tile_start * TMP
    dst = group_base[None, :] + jnp.cumsum(seg_len, axis=0) - seg_len
    tail_dst = group_base + group
    tail_len8 = (group_tiles * TMP - group) // SUBLANES
    max_tiles = (2 * rows + n_tok_tiles * N_EXPERTS * (SUBLANES - 1) + TMP - 1) // TMP + N_EXPERTS
    n_tiles = jnp.sum(group_tiles).astype(jnp.int32).reshape(1)
    tiles = jnp.arange(max_tiles, dtype=jnp.int32)
    tid = jnp.minimum(tiles, n_tiles[0] - 1)
    tile_expert = (jnp.sum(tid[:, None] >= tile_start[None, :], axis=1) - 1).astype(jnp.int32)
    own = (tile_expert[:, None] == jnp.arange(N_EXPERTS, dtype=jnp.int32)[None, :]).astype(jnp.int32)
    in_group = tid - jnp.sum(own * tile_start[None, :], axis=1)
    tile_halves = jnp.clip(jnp.sum(own * group_halves[None, :], axis=1) - HALVES * in_group, 0, HALVES)
    tile_halves = jnp.where(tiles < n_tiles[0], tile_halves, 0).astype(jnp.int32)
    flat = lambda a: a.reshape(-1).astype(jnp.int32)
    seg, dst, len8 = flat(seg_start), flat(dst), flat(seg_len // SUBLANES)
    xs = _dispatch(seg, dst, len8, flat(tail_dst), flat(tail_len8), n_tiles, hb, info, max_tiles * TMP)
    ys = _experts(tile_expert, tile_halves, n_tiles, xs, wg, wu, wd)
    return _combine(seg, dst, len8, ys, x, info, mods, layer, cond_fn, final_g)


def kernel(x, c, ctx, c_ctx, w_mod, b_mod, norm1_g, norm2_g, w_in, sink, w_fourier, w_attn, w_out,
           w_gate_d, w_up_d, w_down_d, w_router, w_gate_e, w_up_e, w_down_e, final_g):
    batch, seq, d = x.shape
    assert (seq, d) == (SEQ, D_MODEL) and ctx.shape == (batch, CTX_LEN, D_MODEL)
    tb = {k: jnp.asarray(v) for k, v in _tables().items()}
    tiles_per_batch = SEQ // TM
    lat_cond = lambda i: i // tiles_per_batch
    ctx_cond = lambda i: 2
    lat_rope = lambda i: i % tiles_per_batch
    ctx_rope = lambda i: 0

    cv = jnp.zeros((SUBLANES, D_MODEL), F32).at[:batch].set(c).at[batch].set(c_ctx)
    mods = _modulation(cv, w_mod, b_mod)

    xl = x.reshape(batch * SEQ, D_MODEL)
    xc = ctx.reshape(batch * CTX_LEN, D_MODEL)
    out = None
    for l in range(DEPTH):
        last = l == DEPTH - 1
        g1 = norm1_g[l].reshape(1, D_MODEL)
        g2 = norm2_g[l].reshape(1, D_MODEL)
        wf, wa, wo = w_fourier, w_attn, w_out
        u, q, kv, gates = _in_proj(xl, mods, l, lat_cond, g1, w_in, tb["cos"], tb["sin"], lat_rope)
        uc, qc, kvc, gates_c = _in_proj(xc, mods, l, ctx_cond, g1, w_in, tb["cos1"], tb["sin0"], ctx_rope)
        fm = _fourier_latent(u, tb, batch)
        ao = _attention_latent(q, kv, kvc, sink[l], batch)
        if l % 2 == 0:
            i = l // 2
            xl, h2 = _merge(xl, fm, ao, gates, wf, wa, wo, mods, l, lat_cond, g2, True)
            wg, wu, wd = w_gate_d, w_up_d, w_down_d
            xl = _ffn_dense(xl, h2, wg, wu, wd, mods, l, lat_cond)
            if not last:
                fmc = _fourier_ctx(uc, tb, batch)
                aoc = _attention_ctx(qc, kvc, sink[l], batch)
                xc, h2c = _merge(xc, fmc, aoc, gates_c, wf, wa, wo, mods, l, ctx_cond, g2, True)
                xc = _ffn_dense(xc, h2c, wg, wu, wd, mods, l, ctx_cond)
            if last:
                raise NotImplementedError("final norm is fused into the routed-expert combine")
        else:
            i = l // 2
            (xl,) = _merge(xl, fm, ao, gates, wf, wa, wo, mods, l, lat_cond, g2, False)
            if not last:
                raise NotImplementedError("context update through a routed-expert layer")
            out = _moe(xl, mods, l, lambda t: t // (SEQ // TD), g2, w_router[i], w_gate_e[i], w_up_e[i], w_down_e[i],
                       final_g.reshape(1, D_MODEL))
    return out.reshape(batch, SEQ, D_MODEL)
```

```python
import functools
import math

import numpy as np
import jax
import jax.numpy as jnp
from jax import lax
from jax.experimental import pallas as pl
from jax.experimental.pallas import tpu as pltpu

F32 = jnp.float32
BF16 = jnp.bfloat16

D_MODEL = 1024
SEQ = 8192
DEPTH = 2
GRID_W = 64
CTX_LEN = 256
N_GROUPS = 4
GROUP_W = 128
FOURIER_WIDTH = N_GROUPS * GROUP_W
N_Q_HEADS = 8
N_KV_HEADS = 2
HEAD_DIM = 64
ATTN_WIDTH = N_Q_HEADS * HEAD_DIM
KV_WIDTH = N_KV_HEADS * HEAD_DIM
WINDOW = 128
ROPE_BASE = 10000.0
OFF_Q = FOURIER_WIDTH
OFF_K = OFF_Q + ATTN_WIDTH
OFF_V = OFF_K + KV_WIDTH
OFF_G = OFF_V + KV_WIDTH
IN_WIDTH = OFF_G + 2 * D_MODEL
N_EXPERTS = 8
N_MOD = 6
EPS = 1e-6
NEG = -1e30
LOG2E = 1.0 / math.log(2.0)

LANES = 128
SUBLANES = 8

TM = 256
TMW = 512
SUB = 256
QB = 128
FFT_R = SEQ // GRID_W
FFT_C = GRID_W
TME = 256
HALVES = 4
TMP = HALVES * TME
TD = 512
SB = 2 * TD + N_EXPERTS * SUBLANES
XS_W = D_MODEL + LANES
SEG_BITS = TD.bit_length() - 3
TAIL_BITS = TMP.bit_length() - 4
FC_E = 512
VMEM_LIMIT = 56 * 1024 * 1024


def _cparams(sem, vmem=None):
    return pltpu.CompilerParams(dimension_semantics=sem, vmem_limit_bytes=vmem)


def _dot(a, b):
    return lax.dot_general(a, b, (((1,), (0,)), ((), ())), preferred_element_type=F32)


def _resident(a, layer):
    return pl.BlockSpec((None,) + a.shape[1:], lambda i, *_: (layer, 0, 0), pipeline_mode=pl.Buffered(1))


def _sigmoid(x):
    return 1.0 / (1.0 + jnp.exp(-x))


def _rms_mod(x, g, sh, sc):
    r = lax.rsqrt(jnp.mean(x * x, axis=-1, keepdims=True) + EPS)
    return (x * r) * g * (1.0 + sc) + sh


def _dft_cs(n):
    k = np.arange(n, dtype=np.float64)
    a = 2.0 * np.pi * np.outer(k, k) / n
    return np.cos(a), np.sin(a)


@functools.lru_cache(maxsize=None)
def _tables():
    c128, s128 = _dft_cs(FFT_R)
    c64, s64 = _dft_cs(FFT_C)
    c256, s256 = _dft_cs(CTX_LEN)
    wa = np.concatenate([c128, -s128], axis=0)
    k1 = np.arange(FFT_R, dtype=np.float64)[:, None]
    cc = np.arange(FFT_C, dtype=np.float64)[None, :]
    ang = 2.0 * np.pi * k1 * cc / SEQ
    twc = np.broadcast_to(np.cos(ang)[:, :, None], (FFT_R, FFT_C, LANES))
    tws = np.broadcast_to(np.sin(ang)[:, :, None], (FFT_R, FFT_C, LANES))
    mc = np.block([[c64, s64], [-s64, c64]])
    cg, sg = _dft_cs(GROUP_W)
    mch = np.concatenate([cg, sg], axis=0) / math.sqrt(SEQ * GROUP_W)
    wcx = np.concatenate([c256, s256], axis=0)
    mchc = np.concatenate([cg, -sg], axis=0)
    n_freq = HEAD_DIM // 4
    inv = ROPE_BASE ** (-np.arange(n_freq, dtype=np.float64) / n_freq)
    t = np.arange(SEQ)
    rows = (t // GRID_W).astype(np.float64)[:, None] * inv
    cols = (t % GRID_W).astype(np.float64)[:, None] * inv
    cos_h = np.concatenate([np.cos(rows), np.cos(rows), np.cos(cols), np.cos(cols)], axis=1)
    sin_h = np.concatenate([-np.sin(rows), np.sin(rows), -np.sin(cols), np.sin(cols)], axis=1)
    cos_t = np.concatenate([cos_h, cos_h], axis=1)
    sin_t = np.concatenate([sin_h, sin_h], axis=1)
    f = lambda a: np.ascontiguousarray(a, dtype=np.float32)
    return dict(wa=f(wa), twc=f(twc), tws=f(tws), mc=f(mc), mch=f(mch), wcx=f(wcx), mchc=f(mchc),
                cos=f(cos_t), sin=f(sin_t),
                cos1=np.ones((TMW, LANES), np.float32), sin0=np.zeros((TMW, LANES), np.float32))


def _mod_body(cv_ref, w_ref, b_ref, o_ref):
    c = cv_ref[...]
    s = c * _sigmoid(c)
    o_ref[0, 0] = jnp.dot(s, w_ref[0], precision=lax.Precision.HIGHEST,
                          preferred_element_type=F32) + b_ref[0]


def _modulation(cv, w_mod, b_mod):
    out = pl.pallas_call(
        _mod_body,
        grid=(DEPTH, N_MOD),
        in_specs=[pl.BlockSpec((SUBLANES, D_MODEL), lambda l, k: (0, 0)),
                  pl.BlockSpec((1, D_MODEL, D_MODEL), lambda l, k: (l, 0, k)),
                  pl.BlockSpec((1, 1, D_MODEL), lambda l, k: (l * N_MOD + k, 0, 0))],
        out_specs=pl.BlockSpec((1, 1, SUBLANES, D_MODEL), lambda l, k: (l, k, 0, 0)),
        out_shape=jax.ShapeDtypeStruct((DEPTH, N_MOD, SUBLANES, D_MODEL), F32),
        compiler_params=_cparams(("arbitrary", "arbitrary")),
        name="modulation",
    )(cv, w_mod, b_mod.reshape(DEPTH * N_MOD, 1, D_MODEL))
    out = jnp.transpose(out[:, :, :3, :], (0, 2, 1, 3))
    return out.reshape(DEPTH * 3 * N_MOD, 1, D_MODEL)


def _mod_spec(layer, k, cond_fn):
    return pl.BlockSpec((1, 1, D_MODEL),
                        lambda i, *_: ((layer * 3 + cond_fn(i)) * N_MOD + k, 0, 0))


def _in_proj_body(x_ref, sh_ref, sc_ref, g_ref, w_ref, cos_ref, sin_ref,
                  u_ref, q_ref, kv_ref, gt_ref):
    lane = lax.broadcasted_iota(jnp.int32, (SUB, LANES), 1)
    first_half = (lane % (HEAD_DIM // 2)) < (HEAD_DIM // 4)
    quarter = HEAD_DIM // 4
    scale = HEAD_DIM ** -0.5 * LOG2E
    gc = 512
    for sb in range(TMW // SUB):
        rows = slice(sb * SUB, (sb + 1) * SUB)
        hb = _rms_mod(x_ref[rows, :], g_ref[...], sh_ref[0], sc_ref[0]).astype(BF16)
        u_ref[rows, :] = _dot(hb, w_ref[:, 0:OFF_Q]).astype(BF16)
        cos = cos_ref[rows, :]
        sin = sin_ref[rows, :]

        def rope(xs):
            below = pltpu.roll(xs, quarter, 1)
            above = pltpu.roll(xs, LANES - quarter, 1)
            return xs * cos + jnp.where(first_half, above, below) * sin

        qk = _dot(hb, w_ref[:, OFF_Q:OFF_V])
        for j in range(ATTN_WIDTH // LANES):
            sl = slice(j * LANES, (j + 1) * LANES)
            q_ref[rows, sl] = (rope(qk[:, sl]) * scale).astype(BF16)
        kv_ref[rows, 0:KV_WIDTH] = rope(qk[:, ATTN_WIDTH:ATTN_WIDTH + KV_WIDTH]).astype(BF16)
        kv_ref[rows, KV_WIDTH:2 * KV_WIDTH] = _dot(hb, w_ref[:, OFF_V:OFF_G]).astype(BF16)
        for j in range(2 * D_MODEL // gc):
            z = _dot(hb, w_ref[:, OFF_G + j * gc:OFF_G + (j + 1) * gc])
            gt_ref[rows, j * gc:(j + 1) * gc] = _sigmoid(z).astype(BF16)


def _in_proj(x, mods, layer, cond_fn, g, w_in, cos, sin, rope_idx):
    rows = x.shape[0]
    return pl.pallas_call(
        _in_proj_body,
        grid=(rows // TMW,),
        in_specs=[pl.BlockSpec((TMW, D_MODEL), lambda i: (i, 0)),
                  _mod_spec(layer, 0, cond_fn),
                  _mod_spec(layer, 1, cond_fn),
                  pl.BlockSpec((1, D_MODEL), lambda i: (0, 0)),
                  _resident(w_in, layer),
                  pl.BlockSpec((TMW, LANES), lambda i: (rope_idx(i), 0)),
                  pl.BlockSpec((TMW, LANES), lambda i: (rope_idx(i), 0))],
        out_specs=[pl.BlockSpec((TMW, FOURIER_WIDTH), lambda i: (i, 0)),
                   pl.BlockSpec((TMW, ATTN_WIDTH), lambda i: (i, 0)),
                   pl.BlockSpec((TMW, 2 * KV_WIDTH), lambda i: (i, 0)),
                   pl.BlockSpec((TMW, 2 * D_MODEL), lambda i: (i, 0))],
        out_shape=[jax.ShapeDtypeStruct((rows, FOURIER_WIDTH), BF16),
                   jax.ShapeDtypeStruct((rows, ATTN_WIDTH), BF16),
                   jax.ShapeDtypeStruct((rows, 2 * KV_WIDTH), BF16),
                   jax.ShapeDtypeStruct((rows, 2 * D_MODEL), BF16)],
        compiler_params=_cparams(("arbitrary",), VMEM_LIMIT),
        name="in_proj",
    )(x, mods, mods, g, w_in, cos, sin)


FFT_CB = 16


def _fft_a_body(x_ref, w_ref, o_ref):
    w = w_ref[...]
    xt = jnp.swapaxes(x_ref[...].astype(F32), 0, 1)
    res = jnp.stack([_dot(w, xt[c]) for c in range(FFT_CB)])
    o_ref[0] = jnp.swapaxes(res, 0, 1).astype(BF16)


def _fft_c_body(re_ref, im_ref, tc_ref, ts_ref, mc_ref, mch_ref, o_ref):
    mc = mc_ref[...]
    mch = mch_ref[...]
    xrs, xis = [], []
    for j in range(FFT_CB):
        ar = re_ref[0, j].astype(F32)
        ai = im_ref[0, j].astype(F32)
        tc = jnp.concatenate([tc_ref[j]] * N_GROUPS, axis=1)
        ts = jnp.concatenate([ts_ref[j]] * N_GROUPS, axis=1)
        br = ar * tc + ai * ts
        bi = ai * tc - ar * ts
        x = _dot(mc, jnp.concatenate([br, bi], axis=0))
        xrs.append(x[:FFT_C])
        xis.append(x[FFT_C:])
    xr = jnp.concatenate(xrs, axis=0).astype(BF16)
    xi = jnp.concatenate(xis, axis=0).astype(BF16)
    ys = []
    for g in range(N_GROUPS):
        sl = slice(g * GROUP_W, (g + 1) * GROUP_W)
        ys.append(_dot(jnp.concatenate([xr[:, sl], xi[:, sl]], axis=1), mch))
    y = jnp.concatenate(ys, axis=1).reshape(FFT_CB, FFT_C, FOURIER_WIDTH)
    o_ref[0] = jnp.swapaxes(y, 0, 1).astype(BF16)


def _fourier_latent(u, tb, batch):
    u3 = u.reshape(batch * FFT_R, FFT_C, FOURIER_WIDTH)
    a = pl.pallas_call(
        _fft_a_body,
        grid=(batch, FFT_C // FFT_CB),
        in_specs=[pl.BlockSpec((FFT_R, FFT_CB, FOURIER_WIDTH), lambda b, j: (b, j, 0)),
                  pl.BlockSpec((2 * FFT_R, FFT_R), lambda b, j: (0, 0))],
        out_specs=pl.BlockSpec((1, 2 * FFT_R, FFT_CB, FOURIER_WIDTH), lambda b, j: (b, 0, j, 0)),
        out_shape=jax.ShapeDtypeStruct((batch, 2 * FFT_R, FFT_C, FOURIER_WIDTH), BF16),
        compiler_params=_cparams(("arbitrary", "arbitrary"), VMEM_LIMIT),
        name="fft_rows",
    )(u3, tb["wa"].astype(BF16))
    nk = FFT_R // FFT_CB
    y = pl.pallas_call(
        _fft_c_body,
        grid=(batch, nk),
        in_specs=[pl.BlockSpec((1, FFT_CB, FFT_C, FOURIER_WIDTH), lambda b, k: (b, k, 0, 0)),
                  pl.BlockSpec((1, FFT_CB, FFT_C, FOURIER_WIDTH), lambda b, k: (b, nk + k, 0, 0)),
                  pl.BlockSpec((FFT_CB, FFT_C, LANES), lambda b, k: (k, 0, 0)),
                  pl.BlockSpec((FFT_CB, FFT_C, LANES), lambda b, k: (k, 0, 0)),
                  pl.BlockSpec((2 * FFT_C, 2 * FFT_C), lambda b, k: (0, 0)),
                  pl.BlockSpec((2 * GROUP_W, GROUP_W), lambda b, k: (0, 0))],
        out_specs=pl.BlockSpec((1, FFT_C, FFT_CB, FOURIER_WIDTH), lambda b, k: (b, 0, k, 0)),
        out_shape=jax.ShapeDtypeStruct((batch, FFT_C, FFT_R, FOURIER_WIDTH), BF16),
        compiler_params=_cparams(("arbitrary", "arbitrary"), VMEM_LIMIT),
        name="fft_cols",
    )(a, a, tb["twc"], tb["tws"], tb["mc"].astype(BF16), tb["mch"].astype(BF16))
    return y.reshape(batch * SEQ, FOURIER_WIDTH)


def _fourier_ctx_body(u_ref, w_ref, m_ref, o_ref):
    pq = _dot(w_ref[...], u_ref[...])
    p = pq[:CTX_LEN]
    q = pq[CTX_LEN:]
    m = m_ref[...]
    scale = 1.0 / math.sqrt(CTX_LEN * GROUP_W)
    for g in range(N_GROUPS):
        sl = slice(g * GROUP_W, (g + 1) * GROUP_W)
        lhs = jnp.concatenate([p[:, sl], q[:, sl]], axis=1).astype(BF16)
        o_ref[:, sl] = _dot(lhs, m) * scale


def _fourier_ctx(u, tb, batch):
    return pl.pallas_call(
        _fourier_ctx_body,
        grid=(batch,),
        in_specs=[pl.BlockSpec((CTX_LEN, FOURIER_WIDTH), lambda b: (b, 0)),
                  pl.BlockSpec((2 * CTX_LEN, CTX_LEN), lambda b: (0, 0)),
                  pl.BlockSpec((2 * GROUP_W, GROUP_W), lambda b: (0, 0))],
        out_specs=pl.BlockSpec((CTX_LEN, FOURIER_WIDTH), lambda b: (b, 0)),
        out_shape=jax.ShapeDtypeStruct((batch * CTX_LEN, FOURIER_WIDTH), F32),
        compiler_params=_cparams(("arbitrary",)),
        name="fft_ctx",
    )(u, tb["wcx"].astype(BF16), tb["mchc"].astype(BF16))


def _attn_body(sink_ref, q_ref, *refs, local):
    o_ref = refs[-1]
    if local:
        kp_ref, kc_ref, kn_ref, kx_ref = refs[:-1]
        groups = [jnp.concatenate([kp_ref[...], kn_ref[...]], axis=0),
                  jnp.concatenate([kc_ref[...], kx_ref[...]], axis=0)]
        n = pl.program_id(1)
        last = pl.num_programs(1) - 1
        qi = lax.broadcasted_iota(jnp.int32, (2 * QB, 2 * QB), 0) % QB
        kj = lax.broadcasted_iota(jnp.int32, (2 * QB, 2 * QB), 1)
        prev_ok = (kj < QB) & (kj >= qi + jnp.where(n == 0, 2 * QB, 0))
        next_ok = (kj >= QB) & (kj - QB <= qi - jnp.where(n == last, 2 * QB, 0))
        masks = [prev_ok | next_ok, None]
    else:
        groups = [refs[0][...]]
        masks = [None]
    lo_q = lax.broadcasted_iota(jnp.int32, (2 * QB, LANES), 1) < HEAD_DIM
    kmats, vmats = [], []
    for kv in groups:
        kv = kv.astype(F32)
        kcat = kv[:, :KV_WIDTH]
        vcat = kv[:, KV_WIDTH:]
        kroll = pltpu.roll(kcat, HEAD_DIM, 1)
        vroll = pltpu.roll(vcat, HEAD_DIM, 1)
        lo_k = lax.broadcasted_iota(jnp.int32, kcat.shape, 1) < HEAD_DIM
        zero = jnp.zeros_like(kcat)
        one = jnp.ones_like(vcat)
        km, vm = [], []
        for hk in range(N_KV_HEADS):
            k_own, k_other = (kcat, kroll) if hk == 0 else (kroll, kcat)
            v_own, v_other = (vcat, vroll) if hk == 0 else (vroll, vcat)
            km.append((jnp.where(lo_k, k_own, zero).astype(BF16), jnp.where(lo_k, zero, k_other).astype(BF16)))
            vm.append((jnp.where(lo_k, v_own, one).astype(BF16), jnp.where(lo_k, one, v_other).astype(BF16)))
        kmats.append(km)
        vmats.append(vm)
    pairs = [(hk, half) for hk in range(N_KV_HEADS) for half in range(2)]
    first_slab = lax.broadcasted_iota(jnp.int32, (2 * QB, 1), 0) < QB
    scores = []
    for hk, half in pairs:
        qs = jnp.concatenate([q_ref[:, (2 * hk) * LANES:(2 * hk + 1) * LANES],
                              q_ref[:, (2 * hk + 1) * LANES:(2 * hk + 2) * LANES]], axis=0)
        parts = []
        for km, mask in zip(kmats, masks):
            s = lax.dot_general(qs, km[hk][half], (((1,), (1,)), ((), ())), preferred_element_type=F32)
            parts.append(s if mask is None else jnp.where(mask, s, NEG))
        scores.append(parts)
    sinks = [jnp.where(first_slab, sink_ref[4 * hk + half], sink_ref[4 * hk + 2 + half]) * LOG2E
             for hk, half in pairs]
    maxes = []
    for parts, sk in zip(scores, sinks):
        m = sk
        for s in parts:
            m = jnp.maximum(m, jnp.max(s, axis=1, keepdims=True))
        maxes.append(m)
    sink_p = [jnp.exp2(sk - m) for sk, m in zip(sinks, maxes)]
    pvs = []
    for parts, m, (hk, half) in zip(scores, maxes, pairs):
        pv = None
        for s, vm in zip(parts, vmats):
            term = _dot(jnp.exp2(s - m).astype(BF16), vm[hk][half])
            pv = term if pv is None else pv + term
        pvs.append(pv)
    for hk in range(N_KV_HEADS):
        lo, hi = pvs[2 * hk], pvs[2 * hk + 1]
        num = jnp.where(lo_q, lo, hi)
        den = (pltpu.roll(jnp.where(lo_q, hi, lo), HEAD_DIM, 1)
               + jnp.where(lo_q, sink_p[2 * hk], sink_p[2 * hk + 1]))
        out = (num / den).astype(BF16)
        o_ref[:, (2 * hk) * LANES:(2 * hk + 1) * LANES] = out[:QB]
        o_ref[:, (2 * hk + 1) * LANES:(2 * hk + 2) * LANES] = out[QB:]


def _attention_latent(q, kv, kv_ctx, sink, batch):
    nb = SEQ // QB
    kvw = 2 * KV_WIDTH
    return pl.pallas_call(
        functools.partial(_attn_body, local=True),
        grid=(batch, nb),
        in_specs=[pl.BlockSpec(memory_space=pltpu.SMEM),
                  pl.BlockSpec((QB, ATTN_WIDTH), lambda b, n: (b * nb + n, 0)),
                  pl.BlockSpec((QB, kvw), lambda b, n: (b * nb + jnp.maximum(n - 1, 0), 0)),
                  pl.BlockSpec((QB, kvw), lambda b, n: (b * nb + n, 0)),
                  pl.BlockSpec((QB, kvw), lambda b, n: (b * nb + jnp.minimum(n + 1, nb - 1), 0)),
                  pl.BlockSpec((CTX_LEN, kvw), lambda b, n: (b, 0))],
        out_specs=pl.BlockSpec((QB, ATTN_WIDTH), lambda b, n: (b * nb + n, 0)),
        out_shape=jax.ShapeDtypeStruct((batch * SEQ, ATTN_WIDTH), BF16),
        compiler_params=_cparams(("arbitrary", "arbitrary")),
        name="attn_latent",
    )(sink, q, kv, kv, kv, kv_ctx)


def _attention_ctx(q, kv, sink, batch):
    nb = CTX_LEN // QB
    return pl.pallas_call(
        functools.partial(_attn_body, local=False),
        grid=(batch, nb),
        in_specs=[pl.BlockSpec(memory_space=pltpu.SMEM),
                  pl.BlockSpec((QB, ATTN_WIDTH), lambda b, n: (b * nb + n, 0)),
                  pl.BlockSpec((CTX_LEN, 2 * KV_WIDTH), lambda b, n: (b, 0))],
        out_specs=pl.BlockSpec((QB, ATTN_WIDTH), lambda b, n: (b * nb + n, 0)),
        out_shape=jax.ShapeDtypeStruct((batch * CTX_LEN, ATTN_WIDTH), BF16),
        compiler_params=_cparams(("arbitrary", "arbitrary")),
        name="attn_ctx",
    )(sink, q, kv)


def _merge_body(x_ref, fm_ref, ao_ref, gt_ref, wf_ref, wa_ref, wo_ref, ga_ref, *refs, with_h2):
    if with_h2:
        sh_ref, sc_ref, g_ref, xo_ref, h_ref = refs
    else:
        (xo_ref,) = refs
    for sb in range(TMW // SUB):
        rows = slice(sb * SUB, (sb + 1) * SUB)
        gt = gt_ref[rows, :].astype(F32)
        y = (gt[:, :D_MODEL] * _dot(fm_ref[rows, :], wf_ref[...])
             + gt[:, D_MODEL:] * _dot(ao_ref[rows, :], wa_ref[...]))
        xn = x_ref[rows, :] + ga_ref[0] * _dot(y, wo_ref[...])
        if with_h2:
            h_ref[rows, :] = _rms_mod(xn, g_ref[...], sh_ref[0], sc_ref[0]).astype(BF16)
        xo_ref[rows, :] = xn


def _merge(x, fm, ao, gates, wf, wa, wo, mods, layer, cond_fn, g2, with_h2):
    rows = x.shape[0]
    row_spec = lambda w: pl.BlockSpec((TMW, w), lambda i: (i, 0))
    in_specs = [row_spec(D_MODEL), row_spec(FOURIER_WIDTH), row_spec(ATTN_WIDTH), row_spec(2 * D_MODEL),
                _resident(wf, layer), _resident(wa, layer), _resident(wo, layer), _mod_spec(layer, 2, cond_fn)]
    args = [x, fm, ao, gates, wf, wa, wo, mods]
    out_specs = [row_spec(D_MODEL)]
    out_shape = [jax.ShapeDtypeStruct((rows, D_MODEL), F32)]
    if with_h2:
        in_specs += [_mod_spec(layer, 3, cond_fn), _mod_spec(layer, 4, cond_fn),
                     pl.BlockSpec((1, D_MODEL), lambda i: (0, 0))]
        args += [mods, mods, g2]
        out_specs.append(row_spec(D_MODEL))
        out_shape.append(jax.ShapeDtypeStruct((rows, D_MODEL), BF16))
    return pl.pallas_call(
        functools.partial(_merge_body, with_h2=with_h2),
        grid=(rows // TMW,),
        in_specs=in_specs, out_specs=out_specs, out_shape=out_shape,
        compiler_params=_cparams(("arbitrary",), VMEM_LIMIT),
        name="merge",
    )(*args)


def _ffn_body(x_ref, h_ref, wg_ref, wu_ref, wd_ref, ga_ref, o_ref):
    h = h_ref[...]
    g = _dot(h, wg_ref[...])
    u = _dot(h, wu_ref[...])
    a = g * _sigmoid(g) * u
    o_ref[...] = x_ref[...] + ga_ref[0] * _dot(a, wd_ref[...])


def _ffn_dense(x, h2, wg, wu, wd, mods, layer, cond_fn):
    rows = x.shape[0]
    return pl.pallas_call(
        _ffn_body,
        grid=(rows // TM,),
        in_specs=[pl.BlockSpec((TM, D_MODEL), lambda i: (i, 0)),
                  pl.BlockSpec((TM, D_MODEL), lambda i: (i, 0)),
                  _resident(wg, layer // 2), _resident(wu, layer // 2), _resident(wd, layer // 2),
                  _mod_spec(layer, 5, cond_fn)],
        out_specs=pl.BlockSpec((TM, D_MODEL), lambda i: (i, 0)),
        out_shape=jax.ShapeDtypeStruct((rows, D_MODEL), F32),
        compiler_params=_cparams(("arbitrary",), VMEM_LIMIT),
        name="ffn_dense",
    )(x, h2, wg, wu, wd, mods)


INFO_E0, INFO_E1, INFO_R0, INFO_R1, INFO_W0, INFO_W1 = range(6)


def _router_body(x_ref, sh_ref, sc_ref, g_ref, wr_ref, hb_ref, info_ref, cnt_ref):
    h = _rms_mod(x_ref[...], g_ref[...], sh_ref[0], sc_ref[0])
    hb_ref[...] = h.astype(BF16)
    logits = jnp.dot(h, wr_ref[...], precision=lax.Precision.HIGHEST, preferred_element_type=F32)
    lane = lax.broadcasted_iota(jnp.int32, logits.shape, 1)
    neg_inf = jnp.float32(-jnp.inf)
    lg = jnp.where(lane < N_EXPERTS, logits, neg_inf)
    v0 = jnp.max(lg, axis=1, keepdims=True)
    i0 = jnp.min(jnp.where(lg == v0, lane, LANES), axis=1, keepdims=True)
    oh0 = lane == i0
    lg1 = jnp.where(oh0, neg_inf, lg)
    v1 = jnp.max(lg1, axis=1, keepdims=True)
    i1 = jnp.min(jnp.where(lg1 == v1, lane, LANES), axis=1, keepdims=True)
    oh1 = lane == i1
    e = jnp.exp(v1 - v0)
    w0 = 1.0 / (1.0 + e)
    w1 = e / (1.0 + e)
    oh = jnp.where(oh0 | oh1, 1.0, 0.0)
    row = lax.broadcasted_iota(jnp.int32, (TD, TD), 0)
    col = lax.broadcasted_iota(jnp.int32, (TD, TD), 1)
    tri = jnp.where(row > col, 1.0, 0.0).astype(BF16)
    before = _dot(tri, oh.astype(BF16))
    r0 = jnp.sum(jnp.where(oh0, before, 0.0), axis=1, keepdims=True)
    r1 = jnp.sum(jnp.where(oh1, before, 0.0), axis=1, keepdims=True)
    cnt_ref[0] = jnp.broadcast_to(jnp.sum(oh, axis=0, keepdims=True), (SUBLANES, LANES))
    info = jnp.zeros(logits.shape, F32)
    for idx, val in ((INFO_E0, i0.astype(F32)), (INFO_E1, i1.astype(F32)), (INFO_R0, r0),
                     (INFO_R1, r1), (INFO_W0, w0), (INFO_W1, w1)):
        info = jnp.where(lane == idx, val, info)
    info_ref[...] = info


def _router(x, mods, layer, cond_fn, g2, wr_pad):
    rows = x.shape[0]
    return pl.pallas_call(
        _router_body,
        grid=(rows // TD,),
        in_specs=[pl.BlockSpec((TD, D_MODEL), lambda i: (i, 0)),
                  _mod_spec(layer, 3, cond_fn), _mod_spec(layer, 4, cond_fn),
                  pl.BlockSpec((1, D_MODEL), lambda i: (0, 0)),
                  pl.BlockSpec((D_MODEL, LANES), lambda i: (0, 0))],
        out_specs=[pl.BlockSpec((TD, D_MODEL), lambda i: (i, 0)),
                   pl.BlockSpec((TD, LANES), lambda i: (i, 0)),
                   pl.BlockSpec((1, SUBLANES, LANES), lambda i: (i, 0, 0))],
        out_shape=[jax.ShapeDtypeStruct((rows, D_MODEL), BF16),
                   jax.ShapeDtypeStruct((rows, LANES), F32),
                   jax.ShapeDtypeStruct((rows // TD, SUBLANES, LANES), F32)],
        compiler_params=_cparams(("arbitrary",), VMEM_LIMIT),
        name="router",
    )(x, mods, mods, g2, wr_pad)


def _segment_copies(src_ref, src_off, dst_ref, dst_off, len8, sem, bits, wait):
    for k in reversed(range(bits)):
        size = SUBLANES << k
        done = ((len8 >> (k + 1)) << (k + 1)) * SUBLANES

        @pl.when(((len8 >> k) & 1) == 1)
        def _(size=size, done=done):
            cp = pltpu.make_async_copy(
                src_ref.at[pl.ds(pl.multiple_of(src_off + done, SUBLANES), size)],
                dst_ref.at[pl.ds(pl.multiple_of(dst_off + done, SUBLANES), size)], sem)
            if wait:
                cp.wait()
            else:
                cp.start()


def _slot_positions(e0, e1, r0, r1, seg_ref, base):
    pos0, pos1 = r0, r1
    for e in range(N_EXPERTS):
        start = seg_ref[base + e].astype(F32)
        pos0 = pos0 + jnp.where(e0 == e, start, 0.0)
        pos1 = pos1 + jnp.where(e1 == e, start, 0.0)
    return pos0.astype(jnp.int32), pos1.astype(jnp.int32)


def _dispatch_body(seg_ref, dst_ref, len_ref, tdst_ref, tlen_ref, nt_ref, hb_ref, info_ref, xs_ref,
                   buf_ref, zero_ref, sems, zero_sem, *, min_tiles):
    i = pl.program_id(0)
    last = pl.num_programs(0) - 1
    base = i * N_EXPERTS
    cur = i % 2

    def scatter(tile, slot, wait):
        for e in range(N_EXPERTS):
            k = tile * N_EXPERTS + e
            _segment_copies(buf_ref.at[slot], seg_ref[k], xs_ref, dst_ref[k], len_ref[k],
                            sems.at[slot], SEG_BITS, wait=wait)

    info_t = info_ref[...].T
    row = lambda k: info_t[k:k + 1, :]
    pos0, pos1 = _slot_positions(row(INFO_E0), row(INFO_E1), row(INFO_R0), row(INFO_R1), seg_ref, base)
    slot = lax.broadcasted_iota(jnp.int32, (SB, TD), 0)
    p0 = slot == pos0
    p1 = slot == pos1
    perm = jnp.where(p0 | p1, 1.0, 0.0).astype(BF16)
    buf_ref[cur, :, :D_MODEL] = _dot(perm, hb_ref[...])
    wsel = jnp.where(p0, row(INFO_W0), 0.0) + jnp.where(p1, row(INFO_W1), 0.0)
    buf_ref[cur, :, D_MODEL:] = jnp.broadcast_to(jnp.sum(wsel, axis=1, keepdims=True), (SB, LANES))
    scatter(i, cur, wait=False)

    @pl.when(i > 0)
    def _():
        scatter(i - 1, 1 - cur, wait=True)

    @pl.when(i == last)
    def _():
        zero_ref[...] = jnp.zeros_like(zero_ref)
        spare_tiles = range(min_tiles, xs_ref.shape[0] // TMP)
        spare = [pltpu.make_async_copy(zero_ref, xs_ref.at[pl.ds(j * TMP, TMP)], zero_sem) for j in spare_tiles]
        for wait in (False, True):
            for e in range(N_EXPERTS):
                _segment_copies(zero_ref, 0, xs_ref, tdst_ref[e], tlen_ref[e], zero_sem, TAIL_BITS, wait=wait)
            for j, cp in zip(spare_tiles, spare):
                @pl.when(j >= nt_ref[0])
                def _(cp=cp, wait=wait):
                    if wait:
                        cp.wait()
                    else:
                        cp.start()
        scatter(i, cur, wait=True)


def _dispatch(seg, dst, len8, tail_dst, tail_len8, n_tiles, hb, info, m_rows):
    rows = hb.shape[0]
    return pl.pallas_call(
        functools.partial(_dispatch_body, min_tiles=2 * rows // TMP),
        grid_spec=pltpu.PrefetchScalarGridSpec(
            num_scalar_prefetch=6,
            grid=(rows // TD,),
            in_specs=[pl.BlockSpec((TD, D_MODEL), lambda i, *_: (i, 0)),
                      pl.BlockSpec((TD, LANES), lambda i, *_: (i, 0))],
            out_specs=pl.BlockSpec(memory_space=pl.ANY),
            scratch_shapes=[pltpu.VMEM((2, SB, XS_W), F32), pltpu.VMEM((TMP, XS_W), F32),
                            pltpu.SemaphoreType.DMA((2,)), pltpu.SemaphoreType.DMA(())]),
        out_shape=jax.ShapeDtypeStruct((m_rows, XS_W), F32),
        compiler_params=_cparams(("arbitrary",), VMEM_LIMIT),
        name="dispatch",
    )(seg, dst, len8, tail_dst, tail_len8, n_tiles, hb, info)


def _expert_body(te_ref, nh_ref, nt_ref, xs_ref, wg_ref, wu_ref, wd_ref, ys_ref, acc_ref):
    j = pl.program_id(0)
    f = pl.program_id(1)
    nf = pl.num_programs(1)
    halves = nh_ref[j]

    blocks = [slice(h * TME, (h + 1) * TME) for h in range(HALVES)]

    @pl.when((halves > 0) & (f == 0))
    def _():
        acc_ref[...] = jnp.zeros_like(acc_ref)

    for live in range(1, HALVES + 1):
        @pl.when(halves == live)
        def _(live=live):
            ups = [(_dot(xs_ref[rows, :D_MODEL], wg_ref[0]), _dot(xs_ref[rows, :D_MODEL], wu_ref[0]))
                   for rows in blocks[:live]]
            downs = [_dot(g * _sigmoid(g) * u, wd_ref[0]) for g, u in ups]
            for rows, d in zip(blocks[:live], downs):
                acc_ref[rows, :] += d

    @pl.when(f == nf - 1)
    def _():
        live_rows = lax.broadcasted_iota(jnp.int32, (TMP, 1), 0) < halves * TME
        ys_ref[...] = jnp.where(live_rows, acc_ref[...] * xs_ref[:, D_MODEL:D_MODEL + 1], 0.0)


def _experts(tile_expert, tile_halves, n_tiles, xs, wg, wu, wd):
    m_rows = xs.shape[0]
    nf = wg.shape[2] // FC_E

    def f_idx(j, f, nt):
        return jnp.where(j < nt[0], f, nf - 1)

    def j_idx(j, nt):
        return jnp.minimum(j, nt[0] - 1)

    return pl.pallas_call(
        _expert_body,
        grid_spec=pltpu.PrefetchScalarGridSpec(
            num_scalar_prefetch=3,
            grid=(m_rows // TMP, nf),
            in_specs=[pl.BlockSpec((TMP, XS_W), lambda j, f, te, nh, nt: (j_idx(j, nt), 0)),
                      pl.BlockSpec((1, D_MODEL, FC_E), lambda j, f, te, nh, nt: (te[j], 0, f_idx(j, f, nt))),
                      pl.BlockSpec((1, D_MODEL, FC_E), lambda j, f, te, nh, nt: (te[j], 0, f_idx(j, f, nt))),
                      pl.BlockSpec((1, FC_E, D_MODEL), lambda j, f, te, nh, nt: (te[j], f_idx(j, f, nt), 0))],
            out_specs=pl.BlockSpec((TMP, D_MODEL), lambda j, f, te, nh, nt: (j, 0)),
            scratch_shapes=[pltpu.VMEM((TMP, D_MODEL), F32)]),
        out_shape=jax.ShapeDtypeStruct((m_rows, D_MODEL), F32),
        compiler_params=_cparams(("arbitrary", "arbitrary"), VMEM_LIMIT),
        name="experts",
    )(tile_expert, tile_halves, n_tiles, xs, wg, wu, wd)


def _combine_body(seg_ref, dst_ref, len_ref, ys_ref, x_ref, info_ref, ga_ref, g_ref, o_ref, buf_ref, sems):
    i = pl.program_id(0)
    base = i * N_EXPERTS
    cur = i % 2

    def gather(tile, slot, wait):
        for e in range(N_EXPERTS):
            k = tile * N_EXPERTS + e
            _segment_copies(ys_ref, dst_ref[k], buf_ref.at[slot], seg_ref[k], len_ref[k],
                            sems.at[slot], SEG_BITS, wait=wait)

    @pl.when(i == 0)
    def _():
        buf_ref[...] = jnp.zeros_like(buf_ref)
        gather(0, 0, wait=False)

    @pl.when(i + 1 < pl.num_programs(0))
    def _():
        gather(i + 1, 1 - cur, wait=False)

    info = info_ref[...]
    col = lambda k: info[:, k:k + 1]
    pos0, pos1 = _slot_positions(col(INFO_E0), col(INFO_E1), col(INFO_R0), col(INFO_R1), seg_ref, base)
    slot = lax.broadcasted_iota(jnp.int32, (TD, SB), 1)
    unperm = jnp.where((slot == pos0) | (slot == pos1), 1.0, 0.0).astype(BF16)
    gather(i, cur, wait=True)
    y = _dot(unperm, buf_ref[cur])
    xn = x_ref[...] + ga_ref[0] * y
    r = lax.rsqrt(jnp.mean(xn * xn, axis=-1, keepdims=True) + EPS)
    o_ref[...] = (xn * r) * g_ref[...]


def _combine(seg, dst, len8, ys, x, info, mods, layer, cond_fn, final_g):
    rows = x.shape[0]
    return pl.pallas_call(
        _combine_body,
        grid_spec=pltpu.PrefetchScalarGridSpec(
            num_scalar_prefetch=3,
            grid=(rows // TD,),
            in_specs=[pl.BlockSpec(memory_space=pl.ANY),
                      pl.BlockSpec((TD, D_MODEL), lambda i, *_: (i, 0)),
                      pl.BlockSpec((TD, LANES), lambda i, *_: (i, 0)),
                      _mod_spec(layer, 5, cond_fn),
                      pl.BlockSpec((1, D_MODEL), lambda i, *_: (0, 0))],
            out_specs=pl.BlockSpec((TD, D_MODEL), lambda i, *_: (i, 0)),
            scratch_shapes=[pltpu.VMEM((2, SB, D_MODEL), F32), pltpu.SemaphoreType.DMA((2,))]),
        out_shape=jax.ShapeDtypeStruct((rows, D_MODEL), F32),
        compiler_params=_cparams(("arbitrary",), VMEM_LIMIT),
        name="combine",
    )(seg, dst, len8, ys, x, info, mods, final_g)


def _moe(x, mods, layer, cond_fn, g2, w_router, wg, wu, wd, final_g):
    rows = x.shape[0]
    n_tok_tiles = rows // TD
    wr_pad = jnp.pad(w_router, ((0, 0), (0, LANES - N_EXPERTS)))
    hb, info, cnt = _router(x, mods, layer, cond_fn, g2, wr_pad)
    counts = cnt[:, 0, :N_EXPERTS].astype(jnp.int32)
    seg_len = (counts + SUBLANES - 1) // SUBLANES * SUBLANES
    seg_start = jnp.cumsum(seg_len, axis=1) - seg_len
    group = jnp.sum(seg_len, axis=0)
    group_halves = (group + TME - 1) // TME
    group_tiles = (group + TMP - 1) // TMP
    tile_start = jnp.cumsum(group_tiles) - group_tiles
    group_base = tile_start * TMP
    dst = group_base[None, :] + jnp.cumsum(seg_len, axis=0) - seg_len
    tail_dst = group_base + group
    tail_len8 = (group_tiles * TMP - group) // SUBLANES
    max_tiles = (2 * rows + n_tok_tiles * N_EXPERTS * (SUBLANES - 1) + TMP - 1) // TMP + N_EXPERTS
    n_tiles = jnp.sum(group_tiles).astype(jnp.int32).reshape(1)
    tiles = jnp.arange(max_tiles, dtype=jnp.int32)
    tid = jnp.minimum(tiles, n_tiles[0] - 1)
    tile_expert = (jnp.sum(tid[:, None] >= tile_start[None, :], axis=1) - 1).astype(jnp.int32)
    own = (tile_expert[:, None] == jnp.arange(N_EXPERTS, dtype=jnp.int32)[None, :]).astype(jnp.int32)
    in_group = tid - jnp.sum(own * tile_start[None, :], axis=1)
    tile_halves = jnp.clip(jnp.sum(own * group_halves[None, :], axis=1) - HALVES * in_group, 0, HALVES)
    tile_halves = jnp.where(tiles < n_tiles[0], tile_halves, 0).astype(jnp.int32)
    flat = lambda a: a.reshape(-1).astype(jnp.int32)
    seg, dst, len8 = flat(seg_start), flat(dst), flat(seg_len // SUBLANES)
    xs = _dispatch(seg, dst, len8, flat(tail_dst), flat(tail_len8), n_tiles, hb, info, max_tiles * TMP)
    ys = _experts(tile_expert, tile_halves, n_tiles, xs, wg, wu, wd)
    return _combine(seg, dst, len8, ys, x, info, mods, layer, cond_fn, final_g)


def kernel(x, c, ctx, c_ctx, w_mod, b_mod, norm1_g, norm2_g, w_in, sink, w_fourier, w_attn, w_out,
           w_gate_d, w_up_d, w_down_d, w_router, w_gate_e, w_up_e, w_down_e, final_g):
    batch, seq, d = x.shape
    assert (seq, d) == (SEQ, D_MODEL) and ctx.shape == (batch, CTX_LEN, D_MODEL)
    tb = {k: jnp.asarray(v) for k, v in _tables().items()}
    lat_cond = lambda tile: (lambda i: i // (SEQ // tile))
    ctx_cond = lambda i: 2
    lat_rope = lambda i: i % (SEQ // TMW)
    ctx_rope = lambda i: 0

    cv = jnp.zeros((SUBLANES, D_MODEL), F32).at[:batch].set(c).at[batch].set(c_ctx)
    mods = _modulation(cv, w_mod, b_mod)

    xl = x.reshape(batch * SEQ, D_MODEL)
    xc = ctx.reshape(batch * CTX_LEN, D_MODEL)
    out = None
    for l in range(DEPTH):
        last = l == DEPTH - 1
        g1 = norm1_g[l].reshape(1, D_MODEL)
        g2 = norm2_g[l].reshape(1, D_MODEL)
        wf, wa, wo = w_fourier, w_attn, w_out
        u, q, kv, gates = _in_proj(xl, mods, l, lat_cond(TMW), g1, w_in, tb["cos"], tb["sin"], lat_rope)
        uc, qc, kvc, gates_c = _in_proj(xc, mods, l, ctx_cond, g1, w_in, tb["cos1"], tb["sin0"], ctx_rope)
        fm = _fourier_latent(u, tb, batch)
        ao = _attention_latent(q, kv, kvc, sink[l], batch)
        if l % 2 == 0:
            i = l // 2
            xl, h2 = _merge(xl, fm, ao, gates, wf, wa, wo, mods, l, lat_cond(TMW), g2, True)
            wg, wu, wd = w_gate_d, w_up_d, w_down_d
            xl = _ffn_dense(xl, h2, wg, wu, wd, mods, l, lat_cond(TM))
            if not last:
                fmc = _fourier_ctx(uc, tb, batch)
                aoc = _attention_ctx(qc, kvc, sink[l], batch)
                xc, h2c = _merge(xc, fmc, aoc, gates_c, wf, wa, wo, mods, l, ctx_cond, g2, True)
                xc = _ffn_dense(xc, h2c, wg, wu, wd, mods, l, ctx_cond)
            if last:
                raise NotImplementedError("final norm is fused into the routed-expert combine")
        else:
            i = l // 2
            (xl,) = _merge(xl, fm, ao, gates, wf, wa, wo, mods, l, lat_cond(TMW), g2, False)
            if not last:
                raise NotImplementedError("context update through a routed-expert layer")
            out = _moe(xl, mods, l, lat_cond(TD), g2, w_router[i], w_gate_e[i], w_up_e[i], w_down_e[i],
                       final_g.reshape(1, D_MODEL))
    return out.reshape(batch, SEQ, D_MODEL)
```

```python
import functools
import math

import numpy as np
import jax
import jax.numpy as jnp
from jax import lax
from jax.experimental import pallas as pl
from jax.experimental.pallas import tpu as pltpu

F32 = jnp.float32
BF16 = jnp.bfloat16

D_MODEL = 1024
SEQ = 8192
DEPTH = 2
GRID_W = 64
CTX_LEN = 256
N_GROUPS = 4
GROUP_W = 128
FOURIER_WIDTH = N_GROUPS * GROUP_W
N_Q_HEADS = 8
N_KV_HEADS = 2
HEAD_DIM = 64
ATTN_WIDTH = N_Q_HEADS * HEAD_DIM
KV_WIDTH = N_KV_HEADS * HEAD_DIM
GROUP = N_Q_HEADS // N_KV_HEADS
WINDOW = 128
ROPE_BASE = 10000.0
OFF_Q = FOURIER_WIDTH
OFF_K = OFF_Q + ATTN_WIDTH
OFF_V = OFF_K + KV_WIDTH
OFF_G = OFF_V + KV_WIDTH
IN_WIDTH = OFF_G + 2 * D_MODEL
N_EXPERTS = 8
N_MOD = 6
EPS = 1e-6
NEG = -1e30
LOG2E = 1.0 / math.log(2.0)

LANES = 128
SUBLANES = 8

TM = 256
TMW = 512
SUB = 256
QB = 128
Q_PAD_W = N_Q_HEADS * LANES
KVX_W = (1 + N_KV_HEADS) * LANES
FFT_R = SEQ // GRID_W
FFT_C = GRID_W
TME = 256
HALVES = 4
TMP = HALVES * TME
TD = 512
SB = 2 * TD + N_EXPERTS * SUBLANES
XS_W = D_MODEL + LANES
SEG_BITS = TD.bit_length() - 3
TAIL_BITS = TMP.bit_length() - 4
FC_E = 512
VMEM_LIMIT = 56 * 1024 * 1024


def _cparams(sem, vmem=None):
    return pltpu.CompilerParams(dimension_semantics=sem, vmem_limit_bytes=vmem)


def _dot(a, b):
    return lax.dot_general(a, b, (((1,), (0,)), ((), ())), preferred_element_type=F32)


def _resident(a, layer):
    return pl.BlockSpec((None,) + a.shape[1:], lambda i, *_: (layer, 0, 0), pipeline_mode=pl.Buffered(1))


class _TwoSources:
    def __init__(self, main_ref, extra_ref, use_extra):
        self.main_ref, self.extra_ref, self.use_extra = main_ref, extra_ref, use_extra

    def __getitem__(self, idx):
        return jnp.where(self.use_extra, self.extra_ref[idx], self.main_ref[idx])


def _two_source_specs(main, extra, tile, width):
    n_main = main.shape[0] // tile
    return [pl.BlockSpec((tile, width), lambda i: (jnp.minimum(i, n_main - 1), 0)),
            pl.BlockSpec((tile, width), lambda i: (jnp.maximum(i - n_main, 0), 0))]


def _with_two_sources(body, n_pairs, n_main_tiles):
    def kernel_fn(*refs):
        use_extra = pl.program_id(0) >= n_main_tiles
        merged = [_TwoSources(refs[2 * k], refs[2 * k + 1], use_extra) for k in range(n_pairs)]
        return body(*merged, *refs[2 * n_pairs:])
    return kernel_fn


def _sigmoid(x):
    return 1.0 / (1.0 + jnp.exp(-x))


def _rms_mod(x, g, sh, sc):
    r = lax.rsqrt(jnp.mean(x * x, axis=-1, keepdims=True) + EPS)
    return (x * r) * g * (1.0 + sc) + sh


def _dft_cs(n):
    k = np.arange(n, dtype=np.float64)
    a = 2.0 * np.pi * np.outer(k, k) / n
    return np.cos(a), np.sin(a)


@functools.lru_cache(maxsize=None)
def _tables():
    c128, s128 = _dft_cs(FFT_R)
    c64, s64 = _dft_cs(FFT_C)
    c256, s256 = _dft_cs(CTX_LEN)
    wa = np.concatenate([c128, -s128], axis=0)
    k1 = np.arange(FFT_R, dtype=np.float64)[:, None]
    cc = np.arange(FFT_C, dtype=np.float64)[None, :]
    ang = 2.0 * np.pi * k1 * cc / SEQ
    twc = np.broadcast_to(np.cos(ang)[:, :, None], (FFT_R, FFT_C, LANES))
    tws = np.broadcast_to(np.sin(ang)[:, :, None], (FFT_R, FFT_C, LANES))
    mc = np.block([[c64, s64], [-s64, c64]])
    cg, sg = _dft_cs(GROUP_W)
    mch = np.concatenate([cg, sg], axis=0) / math.sqrt(SEQ * GROUP_W)
    wcx = np.concatenate([c256, s256], axis=0)
    mchc = np.concatenate([cg, -sg], axis=0)
    n_freq = HEAD_DIM // 4
    inv = ROPE_BASE ** (-np.arange(n_freq, dtype=np.float64) / n_freq)
    t = np.arange(SEQ)
    rows = (t // GRID_W).astype(np.float64)[:, None] * inv
    cols = (t % GRID_W).astype(np.float64)[:, None] * inv
    cos_h = np.concatenate([np.cos(rows), np.cos(rows), np.cos(cols), np.cos(cols)], axis=1)
    sin_h = np.concatenate([-np.sin(rows), np.sin(rows), -np.sin(cols), np.sin(cols)], axis=1)
    cos_t = np.concatenate([cos_h, cos_h], axis=1)
    sin_t = np.concatenate([sin_h, sin_h], axis=1)
    f = lambda a: np.ascontiguousarray(a, dtype=np.float32)
    return dict(wa=f(wa), twc=f(twc), tws=f(tws), mc=f(mc), mch=f(mch), wcx=f(wcx), mchc=f(mchc),
                cos=f(np.concatenate([cos_t, np.ones((TMW, LANES))], axis=0)),
                sin=f(np.concatenate([sin_t, np.zeros((TMW, LANES))], axis=0)))


def _mod_body(cv_ref, w_ref, b_ref, o_ref):
    c = cv_ref[...]
    s = c * _sigmoid(c)
    w = w_ref[0]
    outs = [jnp.sum(w * s[:, r:r + 1], axis=0, keepdims=True) for r in range(N_COND)]
    pad = jnp.zeros((SUBLANES - N_COND, MOD_PER_STEP * D_MODEL), F32)
    res = jnp.concatenate(outs + [pad], axis=0)
    for k in range(MOD_PER_STEP):
        o_ref[0, k] = res[:, k * D_MODEL:(k + 1) * D_MODEL] + b_ref[k]


MOD_PER_STEP = 2
N_COND = 3


def _modulation(cv, w_mod, b_mod):
    steps = N_MOD // MOD_PER_STEP
    out = pl.pallas_call(
        _mod_body,
        grid=(DEPTH, steps),
        in_specs=[pl.BlockSpec((D_MODEL, SUBLANES), lambda l, k: (0, 0)),
                  pl.BlockSpec((1, D_MODEL, MOD_PER_STEP * D_MODEL), lambda l, k: (l, 0, k)),
                  pl.BlockSpec((MOD_PER_STEP, 1, D_MODEL), lambda l, k: (l * steps + k, 0, 0))],
        out_specs=pl.BlockSpec((1, MOD_PER_STEP, SUBLANES, D_MODEL), lambda l, k: (l, k, 0, 0)),
        out_shape=jax.ShapeDtypeStruct((DEPTH, N_MOD, SUBLANES, D_MODEL), F32),
        compiler_params=_cparams(("arbitrary", "arbitrary")),
        name="modulation",
    )(cv, w_mod, b_mod.reshape(DEPTH * N_MOD, 1, D_MODEL))
    out = jnp.transpose(out[:, :, :3, :], (0, 2, 1, 3))
    return out.reshape(DEPTH * 3 * N_MOD, 1, D_MODEL)


def _mod_spec(layer, k, cond_fn):
    return pl.BlockSpec((1, 1, D_MODEL),
                        lambda i, *_: ((layer * 3 + cond_fn(i)) * N_MOD + k, 0, 0))


def _in_proj_body(x_ref, sh_ref, sc_ref, g_ref, w_ref, cos_ref, sin_ref,
                  u_ref, q_ref, kv_ref, gt_ref):
    lane = lax.broadcasted_iota(jnp.int32, (SUB, LANES), 1)
    first_half = (lane % (HEAD_DIM // 2)) < (HEAD_DIM // 4)
    quarter = HEAD_DIM // 4
    scale = HEAD_DIM ** -0.5 * LOG2E
    gc = 512
    for sb in range(TMW // SUB):
        rows = slice(sb * SUB, (sb + 1) * SUB)
        hb = _rms_mod(x_ref[rows, :], g_ref[...], sh_ref[0], sc_ref[0]).astype(BF16)
        u_ref[rows, :] = _dot(hb, w_ref[:, 0:OFF_Q]).astype(BF16)
        cos = cos_ref[rows, :]
        sin = sin_ref[rows, :]

        def rope(xs):
            below = pltpu.roll(xs, quarter, 1)
            above = pltpu.roll(xs, LANES - quarter, 1)
            return xs * cos + jnp.where(first_half, above, below) * sin

        qk = _dot(hb, w_ref[:, OFF_Q:OFF_V])
        low = lane < HEAD_DIM
        for j in range(ATTN_WIDTH // LANES):
            r = rope(qk[:, j * LANES:(j + 1) * LANES]) * scale
            swapped = pltpu.roll(r, HEAD_DIM, 1)
            if (2 * j) // GROUP == 0:
                even, odd = jnp.where(low, r, 0.0), jnp.where(low, swapped, 0.0)
            else:
                even, odd = jnp.where(low, 0.0, swapped), jnp.where(low, 0.0, r)
            q_ref[rows, (2 * j) * LANES:(2 * j + 1) * LANES] = even.astype(BF16)
            q_ref[rows, (2 * j + 1) * LANES:(2 * j + 2) * LANES] = odd.astype(BF16)
        kv_ref[rows, 0:LANES] = rope(qk[:, ATTN_WIDTH:ATTN_WIDTH + KV_WIDTH]).astype(BF16)
        v = _dot(hb, w_ref[:, OFF_V:OFF_G])
        kv_ref[rows, LANES:2 * LANES] = jnp.where(low, v, 1.0).astype(BF16)
        kv_ref[rows, 2 * LANES:3 * LANES] = jnp.where(low, pltpu.roll(v, HEAD_DIM, 1), 1.0).astype(BF16)
        for j in range(2 * D_MODEL // gc):
            z = _dot(hb, w_ref[:, OFF_G + j * gc:OFF_G + (j + 1) * gc])
            gt_ref[rows, j * gc:(j + 1) * gc] = _sigmoid(z).astype(BF16)


def _in_proj(x, x_extra, mods, layer, cond_fn, g, w_in, cos, sin, rope_idx):
    if x_extra is None:
        rows = x.shape[0]
        body, x_args = _in_proj_body, [x]
        x_specs = [pl.BlockSpec((TMW, D_MODEL), lambda i: (i, 0))]
    else:
        rows = x.shape[0] + x_extra.shape[0]
        body, x_args = _with_two_sources(_in_proj_body, 1, x.shape[0] // TMW), [x, x_extra]
        x_specs = _two_source_specs(x, x_extra, TMW, D_MODEL)
    return pl.pallas_call(
        body,
        grid=(rows // TMW,),
        in_specs=x_specs + [
                  _mod_spec(layer, 0, cond_fn),
                  _mod_spec(layer, 1, cond_fn),
                  pl.BlockSpec((1, D_MODEL), lambda i: (0, 0)),
                  _resident(w_in, layer),
                  pl.BlockSpec((TMW, LANES), lambda i: (rope_idx(i), 0)),
                  pl.BlockSpec((TMW, LANES), lambda i: (rope_idx(i), 0))],
        out_specs=[pl.BlockSpec((TMW, FOURIER_WIDTH), lambda i: (i, 0)),
                   pl.BlockSpec((TMW, Q_PAD_W), lambda i: (i, 0)),
                   pl.BlockSpec((TMW, KVX_W), lambda i: (i, 0)),
                   pl.BlockSpec((TMW, 2 * D_MODEL), lambda i: (i, 0))],
        out_shape=[jax.ShapeDtypeStruct((rows, FOURIER_WIDTH), BF16),
                   jax.ShapeDtypeStruct((rows, Q_PAD_W), BF16),
                   jax.ShapeDtypeStruct((rows, KVX_W), BF16),
                   jax.ShapeDtypeStruct((rows, 2 * D_MODEL), BF16)],
        compiler_params=_cparams(("arbitrary",), VMEM_LIMIT),
        name="in_proj",
    )(*x_args, mods, mods, g, w_in, cos, sin)


FFT_CB = 16


def _fft_a_body(x_ref, w_ref, o_ref):
    w = w_ref[...]
    xt = jnp.swapaxes(x_ref[...].astype(F32), 0, 1)
    res = jnp.stack([_dot(w, xt[c]) for c in range(FFT_CB)])
    o_ref[0] = jnp.swapaxes(res, 0, 1).astype(BF16)


def _fft_c_body(re_ref, im_ref, tc_ref, ts_ref, mc_ref, mch_ref, o_ref):
    mc = mc_ref[...]
    mch = mch_ref[...]
    xrs, xis = [], []
    for j in range(FFT_CB):
        ar = re_ref[0, j].astype(F32)
        ai = im_ref[0, j].astype(F32)
        tc = jnp.concatenate([tc_ref[j]] * N_GROUPS, axis=1)
        ts = jnp.concatenate([ts_ref[j]] * N_GROUPS, axis=1)
        br = ar * tc + ai * ts
        bi = ai * tc - ar * ts
        x = _dot(mc, jnp.concatenate([br, bi], axis=0))
        xrs.append(x[:FFT_C])
        xis.append(x[FFT_C:])
    xr = jnp.concatenate(xrs, axis=0).astype(BF16)
    xi = jnp.concatenate(xis, axis=0).astype(BF16)
    ys = []
    for g in range(N_GROUPS):
        sl = slice(g * GROUP_W, (g + 1) * GROUP_W)
        ys.append(_dot(jnp.concatenate([xr[:, sl], xi[:, sl]], axis=1), mch))
    y = jnp.concatenate(ys, axis=1).reshape(FFT_CB, FFT_C, FOURIER_WIDTH)
    o_ref[0] = jnp.swapaxes(y, 0, 1).astype(BF16)


def _fourier_latent(u, tb, batch):
    u3 = u.reshape(u.shape[0] // FFT_C, FFT_C, FOURIER_WIDTH)
    a = pl.pallas_call(
        _fft_a_body,
        grid=(batch, FFT_C // FFT_CB),
        in_specs=[pl.BlockSpec((FFT_R, FFT_CB, FOURIER_WIDTH), lambda b, j: (b, j, 0)),
                  pl.BlockSpec((2 * FFT_R, FFT_R), lambda b, j: (0, 0))],
        out_specs=pl.BlockSpec((1, 2 * FFT_R, FFT_CB, FOURIER_WIDTH), lambda b, j: (b, 0, j, 0)),
        out_shape=jax.ShapeDtypeStruct((batch, 2 * FFT_R, FFT_C, FOURIER_WIDTH), BF16),
        compiler_params=_cparams(("arbitrary", "arbitrary"), VMEM_LIMIT),
        name="fft_rows",
    )(u3, tb["wa"].astype(BF16))
    nk = FFT_R // FFT_CB
    y = pl.pallas_call(
        _fft_c_body,
        grid=(batch, nk),
        in_specs=[pl.BlockSpec((1, FFT_CB, FFT_C, FOURIER_WIDTH), lambda b, k: (b, k, 0, 0)),
                  pl.BlockSpec((1, FFT_CB, FFT_C, FOURIER_WIDTH), lambda b, k: (b, nk + k, 0, 0)),
                  pl.BlockSpec((FFT_CB, FFT_C, LANES), lambda b, k: (k, 0, 0)),
                  pl.BlockSpec((FFT_CB, FFT_C, LANES), lambda b, k: (k, 0, 0)),
                  pl.BlockSpec((2 * FFT_C, 2 * FFT_C), lambda b, k: (0, 0)),
                  pl.BlockSpec((2 * GROUP_W, GROUP_W), lambda b, k: (0, 0))],
        out_specs=pl.BlockSpec((1, FFT_C, FFT_CB, FOURIER_WIDTH), lambda b, k: (b, 0, k, 0)),
        out_shape=jax.ShapeDtypeStruct((batch, FFT_C, FFT_R, FOURIER_WIDTH), BF16),
        compiler_params=_cparams(("arbitrary", "arbitrary"), VMEM_LIMIT),
        name="fft_cols",
    )(a, a, tb["twc"], tb["tws"], tb["mc"].astype(BF16), tb["mch"].astype(BF16))
    return y.reshape(batch * SEQ, FOURIER_WIDTH)


def _fourier_ctx_body(u_ref, w_ref, m_ref, o_ref):
    pq = _dot(w_ref[...], u_ref[...])
    p = pq[:CTX_LEN]
    q = pq[CTX_LEN:]
    m = m_ref[...]
    scale = 1.0 / math.sqrt(CTX_LEN * GROUP_W)
    for g in range(N_GROUPS):
        sl = slice(g * GROUP_W, (g + 1) * GROUP_W)
        lhs = jnp.concatenate([p[:, sl], q[:, sl]], axis=1).astype(BF16)
        o_ref[:, sl] = _dot(lhs, m) * scale


def _fourier_ctx(u, tb, batch):
    first = batch * SEQ // CTX_LEN
    return pl.pallas_call(
        _fourier_ctx_body,
        grid=(batch,),
        in_specs=[pl.BlockSpec((CTX_LEN, FOURIER_WIDTH), lambda b: (first + b, 0)),
                  pl.BlockSpec((2 * CTX_LEN, CTX_LEN), lambda b: (0, 0)),
                  pl.BlockSpec((2 * GROUP_W, GROUP_W), lambda b: (0, 0))],
        out_specs=pl.BlockSpec((CTX_LEN, FOURIER_WIDTH), lambda b: (b, 0)),
        out_shape=jax.ShapeDtypeStruct((batch * CTX_LEN, FOURIER_WIDTH), F32),
        compiler_params=_cparams(("arbitrary",)),
        name="fft_ctx",
    )(u, tb["wcx"].astype(BF16), tb["mchc"].astype(BF16))


def _attn_body(sink_ref, q_ref, *refs, local):
    o_ref = refs[-1]
    rows = 2 * QB
    if local:
        kp_ref, kc_ref, kn_ref, kx_ref = refs[:-1]
        groups = [jnp.concatenate([kp_ref[...], kn_ref[...]], axis=0),
                  jnp.concatenate([kc_ref[...], kx_ref[...]], axis=0)]
        n = pl.program_id(1)
        last = pl.num_programs(1) - 1
        qi = lax.broadcasted_iota(jnp.int32, (rows, 2 * QB), 0) % QB
        kj = lax.broadcasted_iota(jnp.int32, (rows, 2 * QB), 1)
        prev_ok = (kj < QB) & (kj >= qi + jnp.where(n == 0, 2 * QB, 0))
        next_ok = (kj >= QB) & (kj - QB <= qi - jnp.where(n == last, 2 * QB, 0))
        masks = [prev_ok | next_ok, None]
    else:
        groups = [refs[0][...]]
        masks = [None]
    lo_q = lax.broadcasted_iota(jnp.int32, (QB, LANES), 1) < HEAD_DIM
    first_head = lax.broadcasted_iota(jnp.int32, (rows, 1), 0) < QB
    slabs = range(ATTN_WIDTH // LANES)
    scores = []
    for slab in slabs:
        qs = jnp.concatenate([q_ref[:, (2 * slab) * LANES:(2 * slab + 1) * LANES],
                              q_ref[:, (2 * slab + 1) * LANES:(2 * slab + 2) * LANES]], axis=0)
        parts = []
        for kv, mask in zip(groups, masks):
            s = lax.dot_general(qs, kv[:, :LANES], (((1,), (1,)), ((), ())), preferred_element_type=F32)
            parts.append(s if mask is None else jnp.where(mask, s, NEG))
        scores.append(parts)
    sinks = [jnp.where(first_head, sink_ref[2 * slab], sink_ref[2 * slab + 1]) * LOG2E for slab in slabs]
    maxes = []
    for parts, sk in zip(scores, sinks):
        m = sk
        for s in parts:
            m = jnp.maximum(m, jnp.max(s, axis=1, keepdims=True))
        maxes.append(m)
    sink_p = [jnp.exp2(sk - m) for sk, m in zip(sinks, maxes)]
    pvs = []
    for slab, parts, m in zip(slabs, scores, maxes):
        hk = (2 * slab) // GROUP
        pv = None
        for s, kv in zip(parts, groups):
            term = _dot(jnp.exp2(s - m).astype(BF16), kv[:, (1 + hk) * LANES:(2 + hk) * LANES])
            pv = term if pv is None else pv + term
        pvs.append(pv)
    for slab, pv, sp in zip(slabs, pvs, sink_p):
        ra, rb = pv[:QB], pv[QB:]
        num = jnp.where(lo_q, ra, pltpu.roll(rb, HEAD_DIM, 1))
        den = jnp.where(lo_q, pltpu.roll(ra, HEAD_DIM, 1), rb) + jnp.where(lo_q, sp[:QB], sp[QB:])
        o_ref[:, slab * LANES:(slab + 1) * LANES] = (num / den).astype(BF16)


def _attention_latent(q, kv, sink, batch):
    nb = SEQ // QB
    kvw = KVX_W
    first_ctx = batch * SEQ // CTX_LEN
    return pl.pallas_call(
        functools.partial(_attn_body, local=True),
        grid=(batch, nb),
        in_specs=[pl.BlockSpec(memory_space=pltpu.SMEM),
                  pl.BlockSpec((QB, Q_PAD_W), lambda b, n: (b * nb + n, 0)),
                  pl.BlockSpec((QB, kvw), lambda b, n: (b * nb + jnp.maximum(n - 1, 0), 0)),
                  pl.BlockSpec((QB, kvw), lambda b, n: (b * nb + n, 0)),
                  pl.BlockSpec((QB, kvw), lambda b, n: (b * nb + jnp.minimum(n + 1, nb - 1), 0)),
                  pl.BlockSpec((CTX_LEN, kvw), lambda b, n: (first_ctx + b, 0))],
        out_specs=pl.BlockSpec((QB, ATTN_WIDTH), lambda b, n: (b * nb + n, 0)),
        out_shape=jax.ShapeDtypeStruct((batch * SEQ, ATTN_WIDTH), BF16),
        compiler_params=_cparams(("arbitrary", "arbitrary")),
        name="attn_latent",
    )(sink, q, kv, kv, kv, kv)


def _attention_ctx(q, kv, sink, batch):
    nb = CTX_LEN // QB
    first_q = batch * SEQ // QB
    first_ctx = batch * SEQ // CTX_LEN
    return pl.pallas_call(
        functools.partial(_attn_body, local=False),
        grid=(batch, nb),
        in_specs=[pl.BlockSpec(memory_space=pltpu.SMEM),
                  pl.BlockSpec((QB, Q_PAD_W), lambda b, n: (first_q + b * nb + n, 0)),
                  pl.BlockSpec((CTX_LEN, KVX_W), lambda b, n: (first_ctx + b, 0))],
        out_specs=pl.BlockSpec((QB, ATTN_WIDTH), lambda b, n: (b * nb + n, 0)),
        out_shape=jax.ShapeDtypeStruct((batch * CTX_LEN, ATTN_WIDTH), BF16),
        compiler_params=_cparams(("arbitrary", "arbitrary")),
        name="attn_ctx",
    )(sink, q, kv)


def _merge_body(x_ref, fm_ref, ao_ref, gt_ref, wf_ref, wa_ref, wo_ref, ga_ref, *refs, with_h2):
    if with_h2:
        sh_ref, sc_ref, g_ref, xo_ref, h_ref = refs
    else:
        (xo_ref,) = refs
    for sb in range(TMW // SUB):
        rows = slice(sb * SUB, (sb + 1) * SUB)
        gt = gt_ref[rows, :].astype(F32)
        y = (gt[:, :D_MODEL] * _dot(fm_ref[rows, :], wf_ref[...])
             + gt[:, D_MODEL:] * _dot(ao_ref[rows, :], wa_ref[...]))
        xn = x_ref[rows, :] + ga_ref[0] * _dot(y, wo_ref[...])
        if with_h2:
            h_ref[rows, :] = _rms_mod(xn, g_ref[...], sh_ref[0], sc_ref[0]).astype(BF16)
        xo_ref[rows, :] = xn


def _merge(rows, x, fm, ao, extras, gates, wf, wa, wo, mods, layer, cond_fn, g2, with_h2):
    row_spec = lambda w: pl.BlockSpec((TMW, w), lambda i: (i, 0))
    widths = (D_MODEL, FOURIER_WIDTH, ATTN_WIDTH)
    body = functools.partial(_merge_body, with_h2=with_h2)
    if extras is None:
        in_specs = [row_spec(w) for w in widths]
        args = [x, fm, ao]
    else:
        main_rows = rows
        rows = rows + extras[0].shape[0]
        body = _with_two_sources(body, len(widths), main_rows // TMW)
        in_specs, args = [], []
        for main, extra, w in zip((x, fm, ao), extras, widths):
            assert main.shape[0] >= main_rows and extra.shape[0] == extras[0].shape[0]
            in_specs += [pl.BlockSpec((TMW, w), lambda i: (jnp.minimum(i, main_rows // TMW - 1), 0)),
                         pl.BlockSpec((TMW, w), lambda i: (jnp.maximum(i - main_rows // TMW, 0), 0))]
            args += [main, extra]
    in_specs += [row_spec(2 * D_MODEL),
                 _resident(wf, layer), _resident(wa, layer), _resident(wo, layer), _mod_spec(layer, 2, cond_fn)]
    args += [gates, wf, wa, wo, mods]
    out_specs = [row_spec(D_MODEL)]
    out_shape = [jax.ShapeDtypeStruct((rows, D_MODEL), F32)]
    if with_h2:
        in_specs += [_mod_spec(layer, 3, cond_fn), _mod_spec(layer, 4, cond_fn),
                     pl.BlockSpec((1, D_MODEL), lambda i: (0, 0))]
        args += [mods, mods, g2]
        out_specs.append(row_spec(D_MODEL))
        out_shape.append(jax.ShapeDtypeStruct((rows, D_MODEL), BF16))
    return pl.pallas_call(
        body,
        grid=(rows // TMW,),
        in_specs=in_specs, out_specs=out_specs, out_shape=out_shape,
        compiler_params=_cparams(("arbitrary",), VMEM_LIMIT),
        name="merge",
    )(*args)


def _ffn_body(x_ref, h_ref, wg_ref, wu_ref, wd_ref, ga_ref, o_ref):
    h = h_ref[...]
    g = _dot(h, wg_ref[...])
    u = _dot(h, wu_ref[...])
    a = g * _sigmoid(g) * u
    o_ref[...] = x_ref[...] + ga_ref[0] * _dot(a, wd_ref[...])


def _ffn_dense(x, h2, wg, wu, wd, mods, layer, cond_fn):
    rows = x.shape[0]
    return pl.pallas_call(
        _ffn_body,
        grid=(rows // TM,),
        in_specs=[pl.BlockSpec((TM, D_MODEL), lambda i: (i, 0)),
                  pl.BlockSpec((TM, D_MODEL), lambda i: (i, 0)),
                  _resident(wg, layer // 2), _resident(wu, layer // 2), _resident(wd, layer // 2),
                  _mod_spec(layer, 5, cond_fn)],
        out_specs=pl.BlockSpec((TM, D_MODEL), lambda i: (i, 0)),
        out_shape=jax.ShapeDtypeStruct((rows, D_MODEL), F32),
        compiler_params=_cparams(("arbitrary",), VMEM_LIMIT),
        name="ffn_dense",
    )(x, h2, wg, wu, wd, mods)


INFO_E0, INFO_E1, INFO_R0, INFO_R1, INFO_W0, INFO_W1 = range(6)


def _router_body(x_ref, sh_ref, sc_ref, g_ref, wr_ref, hb_ref, info_ref, cnt_ref):
    h = _rms_mod(x_ref[...], g_ref[...], sh_ref[0], sc_ref[0])
    hb_ref[...] = h.astype(BF16)
    logits = jnp.dot(h, wr_ref[...], precision=lax.Precision.HIGHEST, preferred_element_type=F32)
    lane = lax.broadcasted_iota(jnp.int32, logits.shape, 1)
    neg_inf = jnp.float32(-jnp.inf)
    lg = jnp.where(lane < N_EXPERTS, logits, neg_inf)
    v0 = jnp.max(lg, axis=1, keepdims=True)
    i0 = jnp.min(jnp.where(lg == v0, lane, LANES), axis=1, keepdims=True)
    oh0 = lane == i0
    lg1 = jnp.where(oh0, neg_inf, lg)
    v1 = jnp.max(lg1, axis=1, keepdims=True)
    i1 = jnp.min(jnp.where(lg1 == v1, lane, LANES), axis=1, keepdims=True)
    oh1 = lane == i1
    e = jnp.exp(v1 - v0)
    w0 = 1.0 / (1.0 + e)
    w1 = e / (1.0 + e)
    oh = jnp.where(oh0 | oh1, 1.0, 0.0)
    row = lax.broadcasted_iota(jnp.int32, (TD, TD), 0)
    col = lax.broadcasted_iota(jnp.int32, (TD, TD), 1)
    tri = jnp.where(row > col, 1.0, 0.0).astype(BF16)
    before = _dot(tri, oh.astype(BF16))
    r0 = jnp.sum(jnp.where(oh0, before, 0.0), axis=1, keepdims=True)
    r1 = jnp.sum(jnp.where(oh1, before, 0.0), axis=1, keepdims=True)
    cnt_ref[0] = jnp.broadcast_to(jnp.sum(oh, axis=0, keepdims=True), (SUBLANES, LANES))
    info = jnp.zeros(logits.shape, F32)
    for idx, val in ((INFO_E0, i0.astype(F32)), (INFO_E1, i1.astype(F32)), (INFO_R0, r0),
                     (INFO_R1, r1), (INFO_W0, w0), (INFO_W1, w1)):
        info = jnp.where(lane == idx, val, info)
    info_ref[...] = info


def _router(x, mods, layer, cond_fn, g2, wr_pad):
    rows = x.shape[0]
    return pl.pallas_call(
        _router_body,
        grid=(rows // TD,),
        in_specs=[pl.BlockSpec((TD, D_MODEL), lambda i: (i, 0)),
                  _mod_spec(layer, 3, cond_fn), _mod_spec(layer, 4, cond_fn),
                  pl.BlockSpec((1, D_MODEL), lambda i: (0, 0)),
                  pl.BlockSpec((D_MODEL, LANES), lambda i: (0, 0))],
        out_specs=[pl.BlockSpec((TD, D_MODEL), lambda i: (i, 0)),
                   pl.BlockSpec((TD, LANES), lambda i: (i, 0)),
                   pl.BlockSpec((1, SUBLANES, LANES), lambda i: (i, 0, 0))],
        out_shape=[jax.ShapeDtypeStruct((rows, D_MODEL), BF16),
                   jax.ShapeDtypeStruct((rows, LANES), F32),
                   jax.ShapeDtypeStruct((rows // TD, SUBLANES, LANES), F32)],
        compiler_params=_cparams(("arbitrary",), VMEM_LIMIT),
        name="router",
    )(x, mods, mods, g2, wr_pad)


def _segment_copies(src_ref, src_off, dst_ref, dst_off, len8, sem, bits, wait):
    for k in reversed(range(bits)):
        size = SUBLANES << k
        done = ((len8 >> (k + 1)) << (k + 1)) * SUBLANES

        @pl.when(((len8 >> k) & 1) == 1)
        def _(size=size, done=done):
            cp = pltpu.make_async_copy(
                src_ref.at[pl.ds(pl.multiple_of(src_off + done, SUBLANES), size)],
                dst_ref.at[pl.ds(pl.multiple_of(dst_off + done, SUBLANES), size)], sem)
            if wait:
                cp.wait()
            else:
                cp.start()


def _slot_positions(e0, e1, r0, r1, seg_ref, base):
    pos0, pos1 = r0, r1
    for e in range(N_EXPERTS):
        start = seg_ref[base + e].astype(F32)
        pos0 = pos0 + jnp.where(e0 == e, start, 0.0)
        pos1 = pos1 + jnp.where(e1 == e, start, 0.0)
    return pos0.astype(jnp.int32), pos1.astype(jnp.int32)


def _dispatch_body(seg_ref, dst_ref, len_ref, tdst_ref, tlen_ref, nt_ref, hb_ref, info_ref, xs_ref,
                   buf_ref, zero_ref, sems, zero_sem, *, min_tiles):
    i = pl.program_id(0)
    last = pl.num_programs(0) - 1
    base = i * N_EXPERTS
    cur = i % 2

    def scatter(tile, slot, wait):
        for e in range(N_EXPERTS):
            k = tile * N_EXPERTS + e
            _segment_copies(buf_ref.at[slot], seg_ref[k], xs_ref, dst_ref[k], len_ref[k],
                            sems.at[slot], SEG_BITS, wait=wait)

    info_t = info_ref[...].T
    row = lambda k: info_t[k:k + 1, :]
    pos0, pos1 = _slot_positions(row(INFO_E0), row(INFO_E1), row(INFO_R0), row(INFO_R1), seg_ref, base)
    slot = lax.broadcasted_iota(jnp.int32, (SB, TD), 0)
    p0 = slot == pos0
    p1 = slot == pos1
    perm = jnp.where(p0 | p1, 1.0, 0.0).astype(BF16)
    buf_ref[cur, :, :D_MODEL] = _dot(perm, hb_ref[...])
    wsel = jnp.where(p0, row(INFO_W0), 0.0) + jnp.where(p1, row(INFO_W1), 0.0)
    buf_ref[cur, :, D_MODEL:] = jnp.broadcast_to(jnp.sum(wsel, axis=1, keepdims=True), (SB, LANES))
    scatter(i, cur, wait=False)

    @pl.when(i > 0)
    def _():
        scatter(i - 1, 1 - cur, wait=True)

    @pl.when(i == last)
    def _():
        zero_ref[...] = jnp.zeros_like(zero_ref)
        spare_tiles = range(min_tiles, xs_ref.shape[0] // TMP)
        spare = [pltpu.make_async_copy(zero_ref, xs_ref.at[pl.ds(j * TMP, TMP)], zero_sem) for j in spare_tiles]
        for wait in (False, True):
            for e in range(N_EXPERTS):
                _segment_copies(zero_ref, 0, xs_ref, tdst_ref[e], tlen_ref[e], zero_sem, TAIL_BITS, wait=wait)
            for j, cp in zip(spare_tiles, spare):
                @pl.when(j >= nt_ref[0])
                def _(cp=cp, wait=wait):
                    if wait:
                        cp.wait()
                    else:
                        cp.start()
        scatter(i, cur, wait=True)


def _dispatch(seg, dst, len8, tail_dst, tail_len8, n_tiles, hb, info, m_rows):
    rows = hb.shape[0]
    return pl.pallas_call(
        functools.partial(_dispatch_body, min_tiles=2 * rows // TMP),
        grid_spec=pltpu.PrefetchScalarGridSpec(
            num_scalar_prefetch=6,
            grid=(rows // TD,),
            in_specs=[pl.BlockSpec((TD, D_MODEL), lambda i, *_: (i, 0)),
                      pl.BlockSpec((TD, LANES), lambda i, *_: (i, 0))],
            out_specs=pl.BlockSpec(memory_space=pl.ANY),
            scratch_shapes=[pltpu.VMEM((2, SB, XS_W), F32), pltpu.VMEM((TMP, XS_W), F32),
                            pltpu.SemaphoreType.DMA((2,)), pltpu.SemaphoreType.DMA(())]),
        out_shape=jax.ShapeDtypeStruct((m_rows, XS_W), F32),
        compiler_params=_cparams(("arbitrary",), VMEM_LIMIT),
        name="dispatch",
    )(seg, dst, len8, tail_dst, tail_len8, n_tiles, hb, info)


def _expert_body(te_ref, nh_ref, nt_ref, xs_ref, wg_ref, wu_ref, wd_ref, ys_ref, acc_ref):
    j = pl.program_id(0)
    f = pl.program_id(1)
    nf = pl.num_programs(1)
    halves = nh_ref[j]

    blocks = [slice(h * TME, (h + 1) * TME) for h in range(HALVES)]

    @pl.when((halves > 0) & (f == 0))
    def _():
        acc_ref[...] = jnp.zeros_like(acc_ref)

    for live in range(1, HALVES + 1):
        @pl.when(halves == live)
        def _(live=live):
            ups = [(_dot(xs_ref[rows, :D_MODEL], wg_ref[0]), _dot(xs_ref[rows, :D_MODEL], wu_ref[0]))
                   for rows in blocks[:live]]
            downs = [_dot(g * _sigmoid(g) * u, wd_ref[0]) for g, u in ups]
            for rows, d in zip(blocks[:live], downs):
                acc_ref[rows, :] += d

    @pl.when(f == nf - 1)
    def _():
        live_rows = lax.broadcasted_iota(jnp.int32, (TMP, 1), 0) < halves * TME
        ys_ref[...] = jnp.where(live_rows, acc_ref[...] * xs_ref[:, D_MODEL:D_MODEL + 1], 0.0)


def _experts(tile_expert, tile_halves, n_tiles, xs, wg, wu, wd):
    m_rows = xs.shape[0]
    nf = wg.shape[2] // FC_E

    def f_idx(j, f, nt):
        last_live = nt[0] - 1
        walk = lambda t, step: jnp.where(t % 2 == 0, step, nf - 1 - step)
        return jnp.where(j < nt[0], walk(j, f), walk(last_live, nf - 1))

    def j_idx(j, nt):
        return jnp.minimum(j, nt[0] - 1)

    return pl.pallas_call(
        _expert_body,
        grid_spec=pltpu.PrefetchScalarGridSpec(
            num_scalar_prefetch=3,
            grid=(m_rows // TMP, nf),
            in_specs=[pl.BlockSpec((TMP, XS_W), lambda j, f, te, nh, nt: (j_idx(j, nt), 0)),
                      pl.BlockSpec((1, D_MODEL, FC_E), lambda j, f, te, nh, nt: (te[j], 0, f_idx(j, f, nt))),
                      pl.BlockSpec((1, D_MODEL, FC_E), lambda j, f, te, nh, nt: (te[j], 0, f_idx(j, f, nt))),
                      pl.BlockSpec((1, FC_E, D_MODEL), lambda j, f, te, nh, nt: (te[j], f_idx(j, f, nt), 0))],
            out_specs=pl.BlockSpec((TMP, D_MODEL), lambda j, f, te, nh, nt: (j, 0)),
            scratch_shapes=[pltpu.VMEM((TMP, D_MODEL), F32)]),
        out_shape=jax.ShapeDtypeStruct((m_rows, D_MODEL), F32),
        compiler_params=_cparams(("arbitrary", "arbitrary"), VMEM_LIMIT),
        name="experts",
    )(tile_expert, tile_halves, n_tiles, xs, wg, wu, wd)


def _combine_body(seg_ref, dst_ref, len_ref, ys_ref, x_ref, info_ref, ga_ref, g_ref, o_ref, buf_ref, sems):
    i = pl.program_id(0)
    base = i * N_EXPERTS
    cur = i % 2

    def gather(tile, slot, wait):
        for e in range(N_EXPERTS):
            k = tile * N_EXPERTS + e
            _segment_copies(ys_ref, dst_ref[k], buf_ref.at[slot], seg_ref[k], len_ref[k],
                            sems.at[slot], SEG_BITS, wait=wait)

    @pl.when(i == 0)
    def _():
        buf_ref[...] = jnp.zeros_like(buf_ref)
        gather(0, 0, wait=False)

    @pl.when(i + 1 < pl.num_programs(0))
    def _():
        gather(i + 1, 1 - cur, wait=False)

    info = info_ref[...]
    col = lambda k: info[:, k:k + 1]
    pos0, pos1 = _slot_positions(col(INFO_E0), col(INFO_E1), col(INFO_R0), col(INFO_R1), seg_ref, base)
    slot = lax.broadcasted_iota(jnp.int32, (TD, SB), 1)
    unperm = jnp.where((slot == pos0) | (slot == pos1), 1.0, 0.0).astype(BF16)
    gather(i, cur, wait=True)
    y = _dot(unperm, buf_ref[cur])
    xn = x_ref[...] + ga_ref[0] * y
    r = lax.rsqrt(jnp.mean(xn * xn, axis=-1, keepdims=True) + EPS)
    o_ref[...] = (xn * r) * g_ref[...]


def _combine(seg, dst, len8, ys, x, info, mods, layer, cond_fn, final_g):
    rows = x.shape[0]
    return pl.pallas_call(
        _combine_body,
        grid_spec=pltpu.PrefetchScalarGridSpec(
            num_scalar_prefetch=3,
            grid=(rows // TD,),
            in_specs=[pl.BlockSpec(memory_space=pl.ANY),
                      pl.BlockSpec((TD, D_MODEL), lambda i, *_: (i, 0)),
                      pl.BlockSpec((TD, LANES), lambda i, *_: (i, 0)),
                      _mod_spec(layer, 5, cond_fn),
                      pl.BlockSpec((1, D_MODEL), lambda i, *_: (0, 0))],
            out_specs=pl.BlockSpec((TD, D_MODEL), lambda i, *_: (i, 0)),
            scratch_shapes=[pltpu.VMEM((2, SB, D_MODEL), F32), pltpu.SemaphoreType.DMA((2,))]),
        out_shape=jax.ShapeDtypeStruct((rows, D_MODEL), F32),
        compiler_params=_cparams(("arbitrary",), VMEM_LIMIT),
        name="combine",
    )(seg, dst, len8, ys, x, info, mods, final_g)


def _moe(x, mods, layer, cond_fn, g2, w_router, wg, wu, wd, final_g):
    rows = x.shape[0]
    n_tok_tiles = rows // TD
    wr_pad = jnp.pad(w_router, ((0, 0), (0, LANES - N_EXPERTS)))
    hb, info, cnt = _router(x, mods, layer, cond_fn, g2, wr_pad)
    counts = cnt[:, 0, :N_EXPERTS].astype(jnp.int32)
    seg_len = (counts + SUBLANES - 1) // SUBLANES * SUBLANES
    seg_start = jnp.cumsum(seg_len, axis=1) - seg_len
    group = jnp.sum(seg_len, axis=0)
    group_halves = (group + TME - 1) // TME
    group_tiles = (group + TMP - 1) // TMP
    tile_start = jnp.cumsum(group_tiles) - group_tiles
    group_base = tile_start * TMP
    dst = group_base[None, :] + jnp.cumsum(seg_len, axis=0) - seg_len
    tail_dst = group_base + group
    tail_len8 = (group_tiles * TMP - group) // SUBLANES
    max_tiles = (2 * rows + n_tok_tiles * N_EXPERTS * (SUBLANES - 1) + TMP - 1) // TMP + N_EXPERTS
    n_tiles = jnp.sum(group_tiles).astype(jnp.int32).reshape(1)
    tiles = jnp.arange(max_tiles, dtype=jnp.int32)
    tid = jnp.minimum(tiles, n_tiles[0] - 1)
    tile_expert = (jnp.sum(tid[:, None] >= tile_start[None, :], axis=1) - 1).astype(jnp.int32)
    own = (tile_expert[:, None] == jnp.arange(N_EXPERTS, dtype=jnp.int32)[None, :]).astype(jnp.int32)
    in_group = tid - jnp.sum(own * tile_start[None, :], axis=1)
    tile_halves = jnp.clip(jnp.sum(own * group_halves[None, :], axis=1) - HALVES * in_group, 0, HALVES)
    tile_halves = jnp.where(tiles < n_tiles[0], tile_halves, 0).astype(jnp.int32)
    flat = lambda a: a.reshape(-1).astype(jnp.int32)
    seg, dst, len8 = flat(seg_start), flat(dst), flat(seg_len // SUBLANES)
    xs = _dispatch(seg, dst, len8, flat(tail_dst), flat(tail_len8), n_tiles, hb, info, max_tiles * TMP)
    ys = _experts(tile_expert, tile_halves, n_tiles, xs, wg, wu, wd)
    return _combine(seg, dst, len8, ys, x, info, mods, layer, cond_fn, final_g)


def kernel(x, c, ctx, c_ctx, w_mod, b_mod, norm1_g, norm2_g, w_in, sink, w_fourier, w_attn, w_out,
           w_gate_d, w_up_d, w_down_d, w_router, w_gate_e, w_up_e, w_down_e, final_g):
    batch, seq, d = x.shape
    assert (seq, d) == (SEQ, D_MODEL) and ctx.shape == (batch, CTX_LEN, D_MODEL)
    tb = {k: jnp.asarray(v) for k, v in _tables().items()}
    n_lat = batch * SEQ
    assert (batch * CTX_LEN) % TMW == 0 and n_lat % TMW == 0

    def cond(tile):
        return lambda i: jnp.where(i < n_lat // tile, i // (SEQ // tile), batch)

    def rope_idx(i):
        return jnp.where(i < n_lat // TMW, i % (SEQ // TMW), SEQ // TMW + i - n_lat // TMW)

    assert batch + 1 == N_COND
    cv = jnp.zeros((D_MODEL, SUBLANES), F32).at[:, :batch].set(c.T).at[:, batch].set(c_ctx)
    mods = _modulation(cv, w_mod, b_mod)

    xs_main, xs_extra = x.reshape(n_lat, D_MODEL), ctx.reshape(batch * CTX_LEN, D_MODEL)
    out = None
    for l in range(DEPTH):
        last = l == DEPTH - 1
        g1 = norm1_g[l].reshape(1, D_MODEL)
        g2 = norm2_g[l].reshape(1, D_MODEL)
        wf, wa, wo = w_fourier, w_attn, w_out
        u, q, kv, gates = _in_proj(xs_main, xs_extra, mods, l, cond(TMW), g1, w_in, tb["cos"], tb["sin"], rope_idx)
        fm = _fourier_latent(u, tb, batch)
        ao = _attention_latent(q, kv, sink[l], batch)
        if l % 2 == 0:
            if last:
                raise NotImplementedError("final norm is fused into the routed-expert combine")
            extras = (xs_extra, _fourier_ctx(u, tb, batch), _attention_ctx(q, kv, sink[l], batch))
            xa, h2 = _merge(n_lat, xs_main, fm, ao, extras, gates, wf, wa, wo, mods, l, cond(TMW), g2, True)
            xs_main = _ffn_dense(xa, h2, w_gate_d, w_up_d, w_down_d, mods, l, cond(TM))
            xs_extra = None
        else:
            if not last:
                raise NotImplementedError("context update through a routed-expert layer")
            i = l // 2
            (xl,) = _merge(n_lat, xs_main, fm, ao, None, gates, wf, wa, wo, mods, l, cond(TMW), g2, False)
            out = _moe(xl, mods, l, cond(TD), g2, w_router[i], w_gate_e[i], w_up_e[i], w_down_e[i],
                       final_g.reshape(1, D_MODEL))
    return out.reshape(batch, SEQ, D_MODEL)
```

```python
import functools
import math

import numpy as np
import jax
import jax.numpy as jnp
from jax import lax
from jax.experimental import pallas as pl
from jax.experimental.pallas import tpu as pltpu

F32 = jnp.float32
BF16 = jnp.bfloat16

D_MODEL = 1024
SEQ = 8192
DEPTH = 2
GRID_W = 64
CTX_LEN = 256
N_GROUPS = 4
GROUP_W = 128
FOURIER_WIDTH = N_GROUPS * GROUP_W
N_Q_HEADS = 8
N_KV_HEADS = 2
HEAD_DIM = 64
ATTN_WIDTH = N_Q_HEADS * HEAD_DIM
KV_WIDTH = N_KV_HEADS * HEAD_DIM
GROUP = N_Q_HEADS // N_KV_HEADS
WINDOW = 128
ROPE_BASE = 10000.0
OFF_Q = FOURIER_WIDTH
OFF_K = OFF_Q + ATTN_WIDTH
OFF_V = OFF_K + KV_WIDTH
OFF_G = OFF_V + KV_WIDTH
IN_WIDTH = OFF_G + 2 * D_MODEL
N_EXPERTS = 8
N_MOD = 6
EPS = 1e-6
NEG = -1e30
LOG2E = 1.0 / math.log(2.0)

LANES = 128
SUBLANES = 8

TM = 256
TMW = 512
SUB = 256
QB = 128
ATT_NQ = 8
Q_PAD_W = N_Q_HEADS * LANES
KVX_W = (1 + N_KV_HEADS) * LANES
FFT_R = SEQ // GRID_W
FFT_C = GRID_W
TME = 256
HALVES = 4
TMP = HALVES * TME
TD = 512
SB = 2 * TD + N_EXPERTS * SUBLANES
XS_W = D_MODEL + LANES
SEG_BITS = TD.bit_length() - 3
TAIL_BITS = TMP.bit_length() - 4
FC_E = 512
VMEM_LIMIT = 56 * 1024 * 1024


def _cparams(sem, vmem=None):
    return pltpu.CompilerParams(dimension_semantics=sem, vmem_limit_bytes=vmem)


def _dot(a, b):
    return lax.dot_general(a, b, (((1,), (0,)), ((), ())), preferred_element_type=F32)


def _resident(a, layer):
    return pl.BlockSpec((None,) + a.shape[1:], lambda i, *_: (layer, 0, 0), pipeline_mode=pl.Buffered(1))


class _TwoSources:
    def __init__(self, main_ref, extra_ref, use_extra):
        self.main_ref, self.extra_ref, self.use_extra = main_ref, extra_ref, use_extra

    def __getitem__(self, idx):
        return jnp.where(self.use_extra, self.extra_ref[idx], self.main_ref[idx])


def _two_source_specs(main, extra, tile, width):
    n_main = main.shape[0] // tile
    return [pl.BlockSpec((tile, width), lambda i: (jnp.minimum(i, n_main - 1), 0)),
            pl.BlockSpec((tile, width), lambda i: (jnp.maximum(i - n_main, 0), 0))]


def _with_two_sources(body, n_pairs, n_main_tiles):
    def kernel_fn(*refs):
        use_extra = pl.program_id(0) >= n_main_tiles
        merged = [_TwoSources(refs[2 * k], refs[2 * k + 1], use_extra) for k in range(n_pairs)]
        return body(*merged, *refs[2 * n_pairs:])
    return kernel_fn


def _sigmoid(x):
    return 1.0 / (1.0 + jnp.exp(-x))


def _rms_mod(x, g, sh, sc):
    r = lax.rsqrt(jnp.mean(x * x, axis=-1, keepdims=True) + EPS)
    return (x * r) * g * (1.0 + sc) + sh


def _dft_cs(n):
    k = np.arange(n, dtype=np.float64)
    a = 2.0 * np.pi * np.outer(k, k) / n
    return np.cos(a), np.sin(a)


@functools.lru_cache(maxsize=None)
def _tables():
    c128, s128 = _dft_cs(FFT_R)
    c64, s64 = _dft_cs(FFT_C)
    c256, s256 = _dft_cs(CTX_LEN)
    wa = np.concatenate([c128, -s128], axis=0)
    k1 = np.arange(FFT_R, dtype=np.float64)[:, None]
    cc = np.arange(FFT_C, dtype=np.float64)[None, :]
    ang = 2.0 * np.pi * k1 * cc / SEQ
    twc = np.broadcast_to(np.cos(ang)[:, :, None], (FFT_R, FFT_C, LANES))
    tws = np.broadcast_to(np.sin(ang)[:, :, None], (FFT_R, FFT_C, LANES))
    mc = np.block([[c64, s64], [-s64, c64]])
    cg, sg = _dft_cs(GROUP_W)
    mch = np.concatenate([cg, sg], axis=0) / math.sqrt(SEQ * GROUP_W)
    wcx = np.concatenate([c256, s256], axis=0)
    mchc = np.concatenate([cg, -sg], axis=0)
    n_freq = HEAD_DIM // 4
    inv = ROPE_BASE ** (-np.arange(n_freq, dtype=np.float64) / n_freq)
    t = np.arange(SEQ)
    rows = (t // GRID_W).astype(np.float64)[:, None] * inv
    cols = (t % GRID_W).astype(np.float64)[:, None] * inv
    cos_h = np.concatenate([np.cos(rows), np.cos(rows), np.cos(cols), np.cos(cols)], axis=1)
    sin_h = np.concatenate([-np.sin(rows), np.sin(rows), -np.sin(cols), np.sin(cols)], axis=1)
    cos_t = np.concatenate([cos_h, cos_h], axis=1)
    sin_t = np.concatenate([sin_h, sin_h], axis=1)
    f = lambda a: np.ascontiguousarray(a, dtype=np.float32)
    return dict(wa=f(wa), twc=f(twc), tws=f(tws), mc=f(mc), mch=f(mch), wcx=f(wcx), mchc=f(mchc),
                cos=f(np.concatenate([cos_t, np.ones((TMW, LANES))], axis=0)),
                sin=f(np.concatenate([sin_t, np.zeros((TMW, LANES))], axis=0)))


def _mod_body(cv_ref, w_ref, b_ref, o_ref):
    c = cv_ref[...]
    s = c * _sigmoid(c)
    w = w_ref[0]
    outs = [jnp.sum(w * s[:, r:r + 1], axis=0, keepdims=True) for r in range(N_COND)]
    pad = jnp.zeros((SUBLANES - N_COND, MOD_PER_STEP * D_MODEL), F32)
    res = jnp.concatenate(outs + [pad], axis=0)
    for k in range(MOD_PER_STEP):
        o_ref[0, k] = res[:, k * D_MODEL:(k + 1) * D_MODEL] + b_ref[k]


MOD_PER_STEP = 2
N_COND = 3


def _modulation(cv, w_mod, b_mod):
    steps = N_MOD // MOD_PER_STEP
    out = pl.pallas_call(
        _mod_body,
        grid=(DEPTH, steps),
        in_specs=[pl.BlockSpec((D_MODEL, SUBLANES), lambda l, k: (0, 0)),
                  pl.BlockSpec((1, D_MODEL, MOD_PER_STEP * D_MODEL), lambda l, k: (l, 0, k)),
                  pl.BlockSpec((MOD_PER_STEP, 1, D_MODEL), lambda l, k: (l * steps + k, 0, 0))],
        out_specs=pl.BlockSpec((1, MOD_PER_STEP, SUBLANES, D_MODEL), lambda l, k: (l, k, 0, 0)),
        out_shape=jax.ShapeDtypeStruct((DEPTH, N_MOD, SUBLANES, D_MODEL), F32),
        compiler_params=_cparams(("arbitrary", "arbitrary")),
        name="modulation",
    )(cv, w_mod, b_mod.reshape(DEPTH * N_MOD, 1, D_MODEL))
    out = jnp.transpose(out[:, :, :3, :], (0, 2, 1, 3))
    return out.reshape(DEPTH * 3 * N_MOD, 1, D_MODEL)


def _mod_spec(layer, k, cond_fn):
    return pl.BlockSpec((1, 1, D_MODEL),
                        lambda i, *_: ((layer * 3 + cond_fn(i)) * N_MOD + k, 0, 0))


def _in_proj_body(x_ref, sh_ref, sc_ref, g_ref, w_ref, cos_ref, sin_ref,
                  u_ref, q_ref, kv_ref, gt_ref):
    lane = lax.broadcasted_iota(jnp.int32, (SUB, LANES), 1)
    first_half = (lane % (HEAD_DIM // 2)) < (HEAD_DIM // 4)
    quarter = HEAD_DIM // 4
    scale = HEAD_DIM ** -0.5 * LOG2E
    gc = 512
    for sb in range(TMW // SUB):
        rows = slice(sb * SUB, (sb + 1) * SUB)
        hb = _rms_mod(x_ref[rows, :], g_ref[...], sh_ref[0], sc_ref[0]).astype(BF16)
        u_ref[rows, :] = _dot(hb, w_ref[:, 0:OFF_Q]).astype(BF16)
        cos = cos_ref[rows, :]
        sin = sin_ref[rows, :]

        def rope(xs):
            below = pltpu.roll(xs, quarter, 1)
            above = pltpu.roll(xs, LANES - quarter, 1)
            return xs * cos + jnp.where(first_half, above, below) * sin

        qk = _dot(hb, w_ref[:, OFF_Q:OFF_V])
        low = lane < HEAD_DIM
        for j in range(ATTN_WIDTH // LANES):
            r = rope(qk[:, j * LANES:(j + 1) * LANES]) * scale
            swapped = pltpu.roll(r, HEAD_DIM, 1)
            if (2 * j) // GROUP == 0:
                even, odd = jnp.where(low, r, 0.0), jnp.where(low, swapped, 0.0)
            else:
                even, odd = jnp.where(low, 0.0, swapped), jnp.where(low, 0.0, r)
            q_ref[rows, (2 * j) * LANES:(2 * j + 1) * LANES] = even.astype(BF16)
            q_ref[rows, (2 * j + 1) * LANES:(2 * j + 2) * LANES] = odd.astype(BF16)
        kv_ref[rows, 0:LANES] = rope(qk[:, ATTN_WIDTH:ATTN_WIDTH + KV_WIDTH]).astype(BF16)
        v = _dot(hb, w_ref[:, OFF_V:OFF_G])
        kv_ref[rows, LANES:2 * LANES] = jnp.where(low, v, 1.0).astype(BF16)
        kv_ref[rows, 2 * LANES:3 * LANES] = jnp.where(low, pltpu.roll(v, HEAD_DIM, 1), 1.0).astype(BF16)
        for j in range(2 * D_MODEL // gc):
            z = _dot(hb, w_ref[:, OFF_G + j * gc:OFF_G + (j + 1) * gc])
            gt_ref[rows, j * gc:(j + 1) * gc] = _sigmoid(z).astype(BF16)


def _in_proj(x, x_extra, mods, layer, cond_fn, g, w_in, cos, sin, rope_idx):
    if x_extra is None:
        rows = x.shape[0]
        body, x_args = _in_proj_body, [x]
        x_specs = [pl.BlockSpec((TMW, D_MODEL), lambda i: (i, 0))]
    else:
        rows = x.shape[0] + x_extra.shape[0]
        body, x_args = _with_two_sources(_in_proj_body, 1, x.shape[0] // TMW), [x, x_extra]
        x_specs = _two_source_specs(x, x_extra, TMW, D_MODEL)
    return pl.pallas_call(
        body,
        grid=(rows // TMW,),
        in_specs=x_specs + [
                  _mod_spec(layer, 0, cond_fn),
                  _mod_spec(layer, 1, cond_fn),
                  pl.BlockSpec((1, D_MODEL), lambda i: (0, 0)),
                  _resident(w_in, layer),
                  pl.BlockSpec((TMW, LANES), lambda i: (rope_idx(i), 0)),
                  pl.BlockSpec((TMW, LANES), lambda i: (rope_idx(i), 0))],
        out_specs=[pl.BlockSpec((TMW, FOURIER_WIDTH), lambda i: (i, 0)),
                   pl.BlockSpec((TMW, Q_PAD_W), lambda i: (i, 0)),
                   pl.BlockSpec((TMW, KVX_W), lambda i: (i, 0)),
                   pl.BlockSpec((TMW, 2 * D_MODEL), lambda i: (i, 0))],
        out_shape=[jax.ShapeDtypeStruct((rows, FOURIER_WIDTH), BF16),
                   jax.ShapeDtypeStruct((rows, Q_PAD_W), BF16),
                   jax.ShapeDtypeStruct((rows, KVX_W), BF16),
                   jax.ShapeDtypeStruct((rows, 2 * D_MODEL), BF16)],
        compiler_params=_cparams(("arbitrary",), VMEM_LIMIT),
        name="in_proj",
    )(*x_args, mods, mods, g, w_in, cos, sin)


FFT_CB = 16


def _fft_a_body(x_ref, w_ref, o_ref):
    w = w_ref[...]
    xt = jnp.swapaxes(x_ref[...].astype(F32), 0, 1)
    res = jnp.stack([_dot(w, xt[c]) for c in range(FFT_CB)])
    o_ref[0] = jnp.swapaxes(res, 0, 1).astype(BF16)


def _fft_c_body(re_ref, im_ref, tc_ref, ts_ref, mc_ref, mch_ref, o_ref):
    mc = mc_ref[...]
    mch = mch_ref[...]
    xrs, xis = [], []
    for j in range(FFT_CB):
        ar = re_ref[0, j].astype(F32)
        ai = im_ref[0, j].astype(F32)
        tc = jnp.concatenate([tc_ref[j]] * N_GROUPS, axis=1)
        ts = jnp.concatenate([ts_ref[j]] * N_GROUPS, axis=1)
        br = ar * tc + ai * ts
        bi = ai * tc - ar * ts
        x = _dot(mc, jnp.concatenate([br, bi], axis=0))
        xrs.append(x[:FFT_C])
        xis.append(x[FFT_C:])
    xr = jnp.concatenate(xrs, axis=0).astype(BF16)
    xi = jnp.concatenate(xis, axis=0).astype(BF16)
    ys = []
    for g in range(N_GROUPS):
        sl = slice(g * GROUP_W, (g + 1) * GROUP_W)
        ys.append(_dot(jnp.concatenate([xr[:, sl], xi[:, sl]], axis=1), mch))
    y = jnp.concatenate(ys, axis=1).reshape(FFT_CB, FFT_C, FOURIER_WIDTH)
    o_ref[0] = jnp.swapaxes(y, 0, 1).astype(BF16)


def _fourier_latent(u, tb, batch):
    u3 = u.reshape(u.shape[0] // FFT_C, FFT_C, FOURIER_WIDTH)
    a = pl.pallas_call(
        _fft_a_body,
        grid=(batch, FFT_C // FFT_CB),
        in_specs=[pl.BlockSpec((FFT_R, FFT_CB, FOURIER_WIDTH), lambda b, j: (b, j, 0)),
                  pl.BlockSpec((2 * FFT_R, FFT_R), lambda b, j: (0, 0))],
        out_specs=pl.BlockSpec((1, 2 * FFT_R, FFT_CB, FOURIER_WIDTH), lambda b, j: (b, 0, j, 0)),
        out_shape=jax.ShapeDtypeStruct((batch, 2 * FFT_R, FFT_C, FOURIER_WIDTH), BF16),
        compiler_params=_cparams(("arbitrary", "arbitrary"), VMEM_LIMIT),
        name="fft_rows",
    )(u3, tb["wa"].astype(BF16))
    nk = FFT_R // FFT_CB
    y = pl.pallas_call(
        _fft_c_body,
        grid=(batch, nk),
        in_specs=[pl.BlockSpec((1, FFT_CB, FFT_C, FOURIER_WIDTH), lambda b, k: (b, k, 0, 0)),
                  pl.BlockSpec((1, FFT_CB, FFT_C, FOURIER_WIDTH), lambda b, k: (b, nk + k, 0, 0)),
                  pl.BlockSpec((FFT_CB, FFT_C, LANES), lambda b, k: (k, 0, 0)),
                  pl.BlockSpec((FFT_CB, FFT_C, LANES), lambda b, k: (k, 0, 0)),
                  pl.BlockSpec((2 * FFT_C, 2 * FFT_C), lambda b, k: (0, 0)),
                  pl.BlockSpec((2 * GROUP_W, GROUP_W), lambda b, k: (0, 0))],
        out_specs=pl.BlockSpec((1, FFT_C, FFT_CB, FOURIER_WIDTH), lambda b, k: (b, 0, k, 0)),
        out_shape=jax.ShapeDtypeStruct((batch, FFT_C, FFT_R, FOURIER_WIDTH), BF16),
        compiler_params=_cparams(("arbitrary", "arbitrary"), VMEM_LIMIT),
        name="fft_cols",
    )(a, a, tb["twc"], tb["tws"], tb["mc"].astype(BF16), tb["mch"].astype(BF16))
    return y.reshape(batch * SEQ, FOURIER_WIDTH)


def _fourier_ctx_body(u_ref, w_ref, m_ref, o_ref):
    pq = _dot(w_ref[...], u_ref[...])
    p = pq[:CTX_LEN]
    q = pq[CTX_LEN:]
    m = m_ref[...]
    scale = 1.0 / math.sqrt(CTX_LEN * GROUP_W)
    for g in range(N_GROUPS):
        sl = slice(g * GROUP_W, (g + 1) * GROUP_W)
        lhs = jnp.concatenate([p[:, sl], q[:, sl]], axis=1).astype(BF16)
        o_ref[:, sl] = _dot(lhs, m) * scale


def _fourier_ctx(u, tb, batch):
    first = batch * SEQ // CTX_LEN
    return pl.pallas_call(
        _fourier_ctx_body,
        grid=(batch,),
        in_specs=[pl.BlockSpec((CTX_LEN, FOURIER_WIDTH), lambda b: (first + b, 0)),
                  pl.BlockSpec((2 * CTX_LEN, CTX_LEN), lambda b: (0, 0)),
                  pl.BlockSpec((2 * GROUP_W, GROUP_W), lambda b: (0, 0))],
        out_specs=pl.BlockSpec((CTX_LEN, FOURIER_WIDTH), lambda b: (b, 0)),
        out_shape=jax.ShapeDtypeStruct((batch * CTX_LEN, FOURIER_WIDTH), F32),
        compiler_params=_cparams(("arbitrary",)),
        name="fft_ctx",
    )(u, tb["wcx"].astype(BF16), tb["mchc"].astype(BF16))


def _attn_body(sink_ref, q_ref, *refs, local, nq):
    o_ref = refs[-1]
    rows = 2 * QB
    if local:
        blocks = [r[...] for r in refs[:nq + 2]]
        ctx_rows = refs[nq + 2][...]
        step = pl.program_id(1)
        last_step = pl.num_programs(1) - 1
        qi = lax.broadcasted_iota(jnp.int32, (rows, 2 * QB), 0) % QB
        kj = lax.broadcasted_iota(jnp.int32, (rows, 2 * QB), 1)
        groups, masks = [], []
        for t in range(nq):
            no_prev = jnp.where(step == 0, 2 * QB, 0) if t == 0 else 0
            no_next = jnp.where(step == last_step, 2 * QB, 0) if t == nq - 1 else 0
            prev_ok = (kj < QB) & (kj >= qi + no_prev)
            next_ok = (kj >= QB) & (kj - QB <= qi - no_next)
            groups.append([jnp.concatenate([blocks[t], blocks[t + 2]], axis=0),
                           jnp.concatenate([blocks[t + 1], ctx_rows], axis=0)])
            masks.append([prev_ok | next_ok, None])
    else:
        groups = [[refs[0][...]]]
        masks = [[None]]
    lo_q = lax.broadcasted_iota(jnp.int32, (QB, LANES), 1) < HEAD_DIM
    first_head = lax.broadcasted_iota(jnp.int32, (rows, 1), 0) < QB
    chains = [(t, slab) for t in range(nq) for slab in range(ATTN_WIDTH // LANES)]
    scores = []
    for t, slab in chains:
        qrows = slice(t * QB, (t + 1) * QB)
        qs = jnp.concatenate([q_ref[qrows, (2 * slab) * LANES:(2 * slab + 1) * LANES],
                              q_ref[qrows, (2 * slab + 1) * LANES:(2 * slab + 2) * LANES]], axis=0)
        parts = []
        for kv, mask in zip(groups[t], masks[t]):
            s = lax.dot_general(qs, kv[:, :LANES], (((1,), (1,)), ((), ())), preferred_element_type=F32)
            parts.append(s if mask is None else jnp.where(mask, s, NEG))
        scores.append(parts)
    sinks = [jnp.where(first_head, sink_ref[2 * slab], sink_ref[2 * slab + 1]) * LOG2E for _, slab in chains]
    maxes = []
    for parts, sk in zip(scores, sinks):
        m = sk
        for s in parts:
            m = jnp.maximum(m, jnp.max(s, axis=1, keepdims=True))
        maxes.append(m)
    sink_p = [jnp.exp2(sk - m) for sk, m in zip(sinks, maxes)]
    pvs = []
    for (t, slab), parts, m in zip(chains, scores, maxes):
        hk = (2 * slab) // GROUP
        pv = None
        for s, kv in zip(parts, groups[t]):
            term = _dot(jnp.exp2(s - m).astype(BF16), kv[:, (1 + hk) * LANES:(2 + hk) * LANES])
            pv = term if pv is None else pv + term
        pvs.append(pv)
    for (t, slab), pv, sp in zip(chains, pvs, sink_p):
        ra, rb = pv[:QB], pv[QB:]
        num = jnp.where(lo_q, ra, pltpu.roll(rb, HEAD_DIM, 1))
        den = jnp.where(lo_q, pltpu.roll(ra, HEAD_DIM, 1), rb) + jnp.where(lo_q, sp[:QB], sp[QB:])
        o_ref[t * QB:(t + 1) * QB, slab * LANES:(slab + 1) * LANES] = (num / den).astype(BF16)


def _attention_latent(q, kv, sink, batch):
    nb = SEQ // QB
    steps = nb // ATT_NQ
    first_ctx = batch * SEQ // CTX_LEN

    def key_block(j):
        return pl.BlockSpec((QB, KVX_W),
                            lambda b, n: (b * nb + jnp.clip(n * ATT_NQ - 1 + j, 0, nb - 1), 0))

    return pl.pallas_call(
        functools.partial(_attn_body, local=True, nq=ATT_NQ),
        grid=(batch, steps),
        in_specs=[pl.BlockSpec(memory_space=pltpu.SMEM),
                  pl.BlockSpec((ATT_NQ * QB, Q_PAD_W), lambda b, n: (b * steps + n, 0))]
                 + [key_block(j) for j in range(ATT_NQ + 2)]
                 + [pl.BlockSpec((CTX_LEN, KVX_W), lambda b, n: (first_ctx + b, 0))],
        out_specs=pl.BlockSpec((ATT_NQ * QB, ATTN_WIDTH), lambda b, n: (b * steps + n, 0)),
        out_shape=jax.ShapeDtypeStruct((batch * SEQ, ATTN_WIDTH), BF16),
        compiler_params=_cparams(("arbitrary", "arbitrary")),
        name="attn_latent",
    )(sink, q, *([kv] * (ATT_NQ + 3)))


def _attention_ctx(q, kv, sink, batch):
    nb = CTX_LEN // QB
    first_q = batch * SEQ // QB
    first_ctx = batch * SEQ // CTX_LEN
    return pl.pallas_call(
        functools.partial(_attn_body, local=False, nq=1),
        grid=(batch, nb),
        in_specs=[pl.BlockSpec(memory_space=pltpu.SMEM),
                  pl.BlockSpec((QB, Q_PAD_W), lambda b, n: (first_q + b * nb + n, 0)),
                  pl.BlockSpec((CTX_LEN, KVX_W), lambda b, n: (first_ctx + b, 0))],
        out_specs=pl.BlockSpec((QB, ATTN_WIDTH), lambda b, n: (b * nb + n, 0)),
        out_shape=jax.ShapeDtypeStruct((batch * CTX_LEN, ATTN_WIDTH), BF16),
        compiler_params=_cparams(("arbitrary", "arbitrary")),
        name="attn_ctx",
    )(sink, q, kv)


def _merge_body(x_ref, fm_ref, ao_ref, gt_ref, wf_ref, wa_ref, wo_ref, ga_ref, *refs, with_h2):
    if with_h2:
        sh_ref, sc_ref, g_ref, xo_ref, h_ref = refs
    else:
        (xo_ref,) = refs
    for sb in range(TMW // SUB):
        rows = slice(sb * SUB, (sb + 1) * SUB)
        gt = gt_ref[rows, :].astype(F32)
        y = (gt[:, :D_MODEL] * _dot(fm_ref[rows, :], wf_ref[...])
             + gt[:, D_MODEL:] * _dot(ao_ref[rows, :], wa_ref[...]))
        xn = x_ref[rows, :] + ga_ref[0] * _dot(y, wo_ref[...])
        if with_h2:
            h_ref[rows, :] = _rms_mod(xn, g_ref[...], sh_ref[0], sc_ref[0]).astype(BF16)
        xo_ref[rows, :] = xn


def _merge(rows, x, fm, ao, extras, gates, wf, wa, wo, mods, layer, cond_fn, g2, with_h2):
    row_spec = lambda w: pl.BlockSpec((TMW, w), lambda i: (i, 0))
    widths = (D_MODEL, FOURIER_WIDTH, ATTN_WIDTH)
    body = functools.partial(_merge_body, with_h2=with_h2)
    if extras is None:
        in_specs = [row_spec(w) for w in widths]
        args = [x, fm, ao]
    else:
        main_rows = rows
        rows = rows + extras[0].shape[0]
        body = _with_two_sources(body, len(widths), main_rows // TMW)
        in_specs, args = [], []
        for main, extra, w in zip((x, fm, ao), extras, widths):
            assert main.shape[0] >= main_rows and extra.shape[0] == extras[0].shape[0]
            in_specs += [pl.BlockSpec((TMW, w), lambda i: (jnp.minimum(i, main_rows // TMW - 1), 0)),
                         pl.BlockSpec((TMW, w), lambda i: (jnp.maximum(i - main_rows // TMW, 0), 0))]
            args += [main, extra]
    in_specs += [row_spec(2 * D_MODEL),
                 _resident(wf, layer), _resident(wa, layer), _resident(wo, layer), _mod_spec(layer, 2, cond_fn)]
    args += [gates, wf, wa, wo, mods]
    out_specs = [row_spec(D_MODEL)]
    out_shape = [jax.ShapeDtypeStruct((rows, D_MODEL), F32)]
    if with_h2:
        in_specs += [_mod_spec(layer, 3, cond_fn), _mod_spec(layer, 4, cond_fn),
                     pl.BlockSpec((1, D_MODEL), lambda i: (0, 0))]
        args += [mods, mods, g2]
        out_specs.append(row_spec(D_MODEL))
        out_shape.append(jax.ShapeDtypeStruct((rows, D_MODEL), BF16))
    return pl.pallas_call(
        body,
        grid=(rows // TMW,),
        in_specs=in_specs, out_specs=out_specs, out_shape=out_shape,
        compiler_params=_cparams(("arbitrary",), VMEM_LIMIT),
        name="merge",
    )(*args)


def _ffn_body(x_ref, h_ref, wg_ref, wu_ref, wd_ref, ga_ref, o_ref):
    h = h_ref[...]
    g = _dot(h, wg_ref[...])
    u = _dot(h, wu_ref[...])
    a = g * _sigmoid(g) * u
    o_ref[...] = x_ref[...] + ga_ref[0] * _dot(a, wd_ref[...])


def _ffn_dense(x, h2, wg, wu, wd, mods, layer, cond_fn):
    rows = x.shape[0]
    return pl.pallas_call(
        _ffn_body,
        grid=(rows // TM,),
        in_specs=[pl.BlockSpec((TM, D_MODEL), lambda i: (i, 0)),
                  pl.BlockSpec((TM, D_MODEL), lambda i: (i, 0)),
                  _resident(wg, layer // 2), _resident(wu, layer // 2), _resident(wd, layer // 2),
                  _mod_spec(layer, 5, cond_fn)],
        out_specs=pl.BlockSpec((TM, D_MODEL), lambda i: (i, 0)),
        out_shape=jax.ShapeDtypeStruct((rows, D_MODEL), F32),
        compiler_params=_cparams(("arbitrary",), VMEM_LIMIT),
        name="ffn_dense",
    )(x, h2, wg, wu, wd, mods)


INFO_E0, INFO_E1, INFO_R0, INFO_R1, INFO_W0, INFO_W1 = range(6)


def _router_body(x_ref, sh_ref, sc_ref, g_ref, wr_ref, hb_ref, info_ref, cnt_ref):
    h = _rms_mod(x_ref[...], g_ref[...], sh_ref[0], sc_ref[0])
    hb_ref[...] = h.astype(BF16)
    logits = jnp.dot(h, wr_ref[...], precision=lax.Precision.HIGHEST, preferred_element_type=F32)
    lane = lax.broadcasted_iota(jnp.int32, logits.shape, 1)
    neg_inf = jnp.float32(-jnp.inf)
    lg = jnp.where(lane < N_EXPERTS, logits, neg_inf)
    v0 = jnp.max(lg, axis=1, keepdims=True)
    i0 = jnp.min(jnp.where(lg == v0, lane, LANES), axis=1, keepdims=True)
    oh0 = lane == i0
    lg1 = jnp.where(oh0, neg_inf, lg)
    v1 = jnp.max(lg1, axis=1, keepdims=True)
    i1 = jnp.min(jnp.where(lg1 == v1, lane, LANES), axis=1, keepdims=True)
    oh1 = lane == i1
    e = jnp.exp(v1 - v0)
    w0 = 1.0 / (1.0 + e)
    w1 = e / (1.0 + e)
    oh = jnp.where(oh0 | oh1, 1.0, 0.0)
    row = lax.broadcasted_iota(jnp.int32, (TD, TD), 0)
    col = lax.broadcasted_iota(jnp.int32, (TD, TD), 1)
    tri = jnp.where(row > col, 1.0, 0.0).astype(BF16)
    before = _dot(tri, oh.astype(BF16))
    r0 = jnp.sum(jnp.where(oh0, before, 0.0), axis=1, keepdims=True)
    r1 = jnp.sum(jnp.where(oh1, before, 0.0), axis=1, keepdims=True)
    cnt_ref[0] = jnp.broadcast_to(jnp.sum(oh, axis=0, keepdims=True), (SUBLANES, LANES))
    info = jnp.zeros(logits.shape, F32)
    for idx, val in ((INFO_E0, i0.astype(F32)), (INFO_E1, i1.astype(F32)), (INFO_R0, r0),
                     (INFO_R1, r1), (INFO_W0, w0), (INFO_W1, w1)):
        info = jnp.where(lane == idx, val, info)
    info_ref[...] = info


def _router(x, mods, layer, cond_fn, g2, wr_pad):
    rows = x.shape[0]
    return pl.pallas_call(
        _router_body,
        grid=(rows // TD,),
        in_specs=[pl.BlockSpec((TD, D_MODEL), lambda i: (i, 0)),
                  _mod_spec(layer, 3, cond_fn), _mod_spec(layer, 4, cond_fn),
                  pl.BlockSpec((1, D_MODEL), lambda i: (0, 0)),
                  pl.BlockSpec((D_MODEL, LANES), lambda i: (0, 0))],
        out_specs=[pl.BlockSpec((TD, D_MODEL), lambda i: (i, 0)),
                   pl.BlockSpec((TD, LANES), lambda i: (i, 0)),
                   pl.BlockSpec((1, SUBLANES, LANES), lambda i: (i, 0, 0))],
        out_shape=[jax.ShapeDtypeStruct((rows, D_MODEL), BF16),
                   jax.ShapeDtypeStruct((rows, LANES), F32),
                   jax.ShapeDtypeStruct((rows // TD, SUBLANES, LANES), F32)],
        compiler_params=_cparams(("arbitrary",), VMEM_LIMIT),
        name="router",
    )(x, mods, mods, g2, wr_pad)


def _segment_copies(src_ref, src_off, dst_ref, dst_off, len8, sem, bits, wait):
    for k in reversed(range(bits)):
        size = SUBLANES << k
        done = ((len8 >> (k + 1)) << (k + 1)) * SUBLANES

        @pl.when(((len8 >> k) & 1) == 1)
        def _(size=size, done=done):
            cp = pltpu.make_async_copy(
                src_ref.at[pl.ds(pl.multiple_of(src_off + done, SUBLANES), size)],
                dst_ref.at[pl.ds(pl.multiple_of(dst_off + done, SUBLANES), size)], sem)
            if wait:
                cp.wait()
            else:
                cp.start()


def _slot_positions(e0, e1, r0, r1, seg_ref, base):
    pos0, pos1 = r0, r1
    for e in range(N_EXPERTS):
        start = seg_ref[base + e].astype(F32)
        pos0 = pos0 + jnp.where(e0 == e, start, 0.0)
        pos1 = pos1 + jnp.where(e1 == e, start, 0.0)
    return pos0.astype(jnp.int32), pos1.astype(jnp.int32)


def _dispatch_body(seg_ref, dst_ref, len_ref, tdst_ref, tlen_ref, nt_ref, hb_ref, info_ref, xs_ref,
                   buf_ref, zero_ref, sems, zero_sem, *, min_tiles):
    i = pl.program_id(0)
    last = pl.num_programs(0) - 1
    base = i * N_EXPERTS
    cur = i % 2

    def scatter(tile, slot, wait):
        for e in range(N_EXPERTS):
            k = tile * N_EXPERTS + e
            _segment_copies(buf_ref.at[slot], seg_ref[k], xs_ref, dst_ref[k], len_ref[k],
                            sems.at[slot], SEG_BITS, wait=wait)

    info_t = info_ref[...].T
    row = lambda k: info_t[k:k + 1, :]
    pos0, pos1 = _slot_positions(row(INFO_E0), row(INFO_E1), row(INFO_R0), row(INFO_R1), seg_ref, base)
    slot = lax.broadcasted_iota(jnp.int32, (SB, TD), 0)
    p0 = slot == pos0
    p1 = slot == pos1
    perm = jnp.where(p0 | p1, 1.0, 0.0).astype(BF16)
    buf_ref[cur, :, :D_MODEL] = _dot(perm, hb_ref[...])
    wsel = jnp.where(p0, row(INFO_W0), 0.0) + jnp.where(p1, row(INFO_W1), 0.0)
    buf_ref[cur, :, D_MODEL:] = jnp.broadcast_to(jnp.sum(wsel, axis=1, keepdims=True), (SB, LANES))
    scatter(i, cur, wait=False)

    @pl.when(i > 0)
    def _():
        scatter(i - 1, 1 - cur, wait=True)

    @pl.when(i == last)
    def _():
        zero_ref[...] = jnp.zeros_like(zero_ref)
        spare_tiles = range(min_tiles, xs_ref.shape[0] // TMP)
        spare = [pltpu.make_async_copy(zero_ref, xs_ref.at[pl.ds(j * TMP, TMP)], zero_sem) for j in spare_tiles]
        for wait in (False, True):
            for e in range(N_EXPERTS):
                _segment_copies(zero_ref, 0, xs_ref, tdst_ref[e], tlen_ref[e], zero_sem, TAIL_BITS, wait=wait)
            for j, cp in zip(spare_tiles, spare):
                @pl.when(j >= nt_ref[0])
                def _(cp=cp, wait=wait):
                    if wait:
                        cp.wait()
                    else:
                        cp.start()
        scatter(i, cur, wait=True)


def _dispatch(seg, dst, len8, tail_dst, tail_len8, n_tiles, hb, info, m_rows):
    rows = hb.shape[0]
    return pl.pallas_call(
        functools.partial(_dispatch_body, min_tiles=2 * rows // TMP),
        grid_spec=pltpu.PrefetchScalarGridSpec(
            num_scalar_prefetch=6,
            grid=(rows // TD,),
            in_specs=[pl.BlockSpec((TD, D_MODEL), lambda i, *_: (i, 0)),
                      pl.BlockSpec((TD, LANES), lambda i, *_: (i, 0))],
            out_specs=pl.BlockSpec(memory_space=pl.ANY),
            scratch_shapes=[pltpu.VMEM((2, SB, XS_W), F32), pltpu.VMEM((TMP, XS_W), F32),
                            pltpu.SemaphoreType.DMA((2,)), pltpu.SemaphoreType.DMA(())]),
        out_shape=jax.ShapeDtypeStruct((m_rows, XS_W), F32),
        compiler_params=_cparams(("arbitrary",), VMEM_LIMIT),
        name="dispatch",
    )(seg, dst, len8, tail_dst, tail_len8, n_tiles, hb, info)


def _expert_body(te_ref, nh_ref, nt_ref, xs_ref, wg_ref, wu_ref, wd_ref, ys_ref, acc_ref):
    j = pl.program_id(0)
    f = pl.program_id(1)
    nf = pl.num_programs(1)
    halves = nh_ref[j]

    blocks = [slice(h * TME, (h + 1) * TME) for h in range(HALVES)]

    @pl.when((halves > 0) & (f == 0))
    def _():
        acc_ref[...] = jnp.zeros_like(acc_ref)

    for live in range(1, HALVES + 1):
        @pl.when(halves == live)
        def _(live=live):
            ups = [(_dot(xs_ref[rows, :D_MODEL], wg_ref[0]), _dot(xs_ref[rows, :D_MODEL], wu_ref[0]))
                   for rows in blocks[:live]]
            downs = [_dot(g * _sigmoid(g) * u, wd_ref[0]) for g, u in ups]
            for rows, d in zip(blocks[:live], downs):
                acc_ref[rows, :] += d

    @pl.when(f == nf - 1)
    def _():
        live_rows = lax.broadcasted_iota(jnp.int32, (TMP, 1), 0) < halves * TME
        ys_ref[...] = jnp.where(live_rows, acc_ref[...] * xs_ref[:, D_MODEL:D_MODEL + 1], 0.0)


def _experts(tile_expert, tile_halves, n_tiles, xs, wg, wu, wd):
    m_rows = xs.shape[0]
    nf = wg.shape[2] // FC_E

    def f_idx(j, f, nt):
        last_live = nt[0] - 1
        walk = lambda t, step: jnp.where(t % 2 == 0, step, nf - 1 - step)
        return jnp.where(j < nt[0], walk(j, f), walk(last_live, nf - 1))

    def j_idx(j, nt):
        return jnp.minimum(j, nt[0] - 1)

    return pl.pallas_call(
        _expert_body,
        grid_spec=pltpu.PrefetchScalarGridSpec(
            num_scalar_prefetch=3,
            grid=(m_rows // TMP, nf),
            in_specs=[pl.BlockSpec((TMP, XS_W), lambda j, f, te, nh, nt: (j_idx(j, nt), 0)),
                      pl.BlockSpec((1, D_MODEL, FC_E), lambda j, f, te, nh, nt: (te[j], 0, f_idx(j, f, nt))),
                      pl.BlockSpec((1, D_MODEL, FC_E), lambda j, f, te, nh, nt: (te[j], 0, f_idx(j, f, nt))),
                      pl.BlockSpec((1, FC_E, D_MODEL), lambda j, f, te, nh, nt: (te[j], f_idx(j, f, nt), 0))],
            out_specs=pl.BlockSpec((TMP, D_MODEL), lambda j, f, te, nh, nt: (j, 0)),
            scratch_shapes=[pltpu.VMEM((TMP, D_MODEL), F32)]),
        out_shape=jax.ShapeDtypeStruct((m_rows, D_MODEL), F32),
        compiler_params=_cparams(("arbitrary", "arbitrary"), VMEM_LIMIT),
        name="experts",
    )(tile_expert, tile_halves, n_tiles, xs, wg, wu, wd)


def _combine_body(seg_ref, dst_ref, len_ref, ys_ref, x_ref, info_ref, ga_ref, g_ref, o_ref, buf_ref, sems):
    i = pl.program_id(0)
    base = i * N_EXPERTS
    cur = i % 2

    def gather(tile, slot, wait):
        for e in range(N_EXPERTS):
            k = tile * N_EXPERTS + e
            _segment_copies(ys_ref, dst_ref[k], buf_ref.at[slot], seg_ref[k], len_ref[k],
                            sems.at[slot], SEG_BITS, wait=wait)

    @pl.when(i == 0)
    def _():
        buf_ref[...] = jnp.zeros_like(buf_ref)
        gather(0, 0, wait=False)

    @pl.when(i + 1 < pl.num_programs(0))
    def _():
        gather(i + 1, 1 - cur, wait=False)

    info = info_ref[...]
    col = lambda k: info[:, k:k + 1]
    pos0, pos1 = _slot_positions(col(INFO_E0), col(INFO_E1), col(INFO_R0), col(INFO_R1), seg_ref, base)
    slot = lax.broadcasted_iota(jnp.int32, (TD, SB), 1)
    unperm = jnp.where((slot == pos0) | (slot == pos1), 1.0, 0.0).astype(BF16)
    gather(i, cur, wait=True)
    y = _dot(unperm, buf_ref[cur])
    xn = x_ref[...] + ga_ref[0] * y
    r = lax.rsqrt(jnp.mean(xn * xn, axis=-1, keepdims=True) + EPS)
    o_ref[...] = (xn * r) * g_ref[...]


def _combine(seg, dst, len8, ys, x, info, mods, layer, cond_fn, final_g):
    rows = x.shape[0]
    return pl.pallas_call(
        _combine_body,
        grid_spec=pltpu.PrefetchScalarGridSpec(
            num_scalar_prefetch=3,
            grid=(rows // TD,),
            in_specs=[pl.BlockSpec(memory_space=pl.ANY),
                      pl.BlockSpec((TD, D_MODEL), lambda i, *_: (i, 0)),
                      pl.BlockSpec((TD, LANES), lambda i, *_: (i, 0)),
                      _mod_spec(layer, 5, cond_fn),
                      pl.BlockSpec((1, D_MODEL), lambda i, *_: (0, 0))],
            out_specs=pl.BlockSpec((TD, D_MODEL), lambda i, *_: (i, 0)),
            scratch_shapes=[pltpu.VMEM((2, SB, D_MODEL), F32), pltpu.SemaphoreType.DMA((2,))]),
        out_shape=jax.ShapeDtypeStruct((rows, D_MODEL), F32),
        compiler_params=_cparams(("arbitrary",), VMEM_LIMIT),
        name="combine",
    )(seg, dst, len8, ys, x, info, mods, final_g)


def _moe(x, mods, layer, cond_fn, g2, w_router, wg, wu, wd, final_g):
    rows = x.shape[0]
    n_tok_tiles = rows // TD
    wr_pad = jnp.pad(w_router, ((0, 0), (0, LANES - N_EXPERTS)))
    hb, info, cnt = _router(x, mods, layer, cond_fn, g2, wr_pad)
    counts = cnt[:, 0, :N_EXPERTS].astype(jnp.int32)
    seg_len = (counts + SUBLANES - 1) // SUBLANES * SUBLANES
    seg_start = jnp.cumsum(seg_len, axis=1) - seg_len
    group = jnp.sum(seg_len, axis=0)
    group_halves = (group + TME - 1) // TME
    group_tiles = (group + TMP - 1) // TMP
    tile_start = jnp.cumsum(group_tiles) - group_tiles
    group_base = tile_start * TMP
    dst = group_base[None, :] + jnp.cumsum(seg_len, axis=0) - seg_len
    tail_dst = group_base + group
    tail_len8 = (group_tiles * TMP - group) // SUBLANES
    max_tiles = (2 * rows + n_tok_tiles * N_EXPERTS * (SUBLANES - 1) + TMP - 1) // TMP + N_EXPERTS
    n_tiles = jnp.sum(group_tiles).astype(jnp.int32).reshape(1)
    tiles = jnp.arange(max_tiles, dtype=jnp.int32)
    tid = jnp.minimum(tiles, n_tiles[0] - 1)
    tile_expert = (jnp.sum(tid[:, None] >= tile_start[None, :], axis=1) - 1).astype(jnp.int32)
    own = (tile_expert[:, None] == jnp.arange(N_EXPERTS, dtype=jnp.int32)[None, :]).astype(jnp.int32)
    in_group = tid - jnp.sum(own * tile_start[None, :], axis=1)
    tile_halves = jnp.clip(jnp.sum(own * group_halves[None, :], axis=1) - HALVES * in_group, 0, HALVES)
    tile_halves = jnp.where(tiles < n_tiles[0], tile_halves, 0).astype(jnp.int32)
    flat = lambda a: a.reshape(-1).astype(jnp.int32)
    seg, dst, len8 = flat(seg_start), flat(dst), flat(seg_len // SUBLANES)
    xs = _dispatch(seg, dst, len8, flat(tail_dst), flat(tail_len8), n_tiles, hb, info, max_tiles * TMP)
    ys = _experts(tile_expert, tile_halves, n_tiles, xs, wg, wu, wd)
    return _combine(seg, dst, len8, ys, x, info, mods, layer, cond_fn, final_g)


def kernel(x, c, ctx, c_ctx, w_mod, b_mod, norm1_g, norm2_g, w_in, sink, w_fourier, w_attn, w_out,
           w_gate_d, w_up_d, w_down_d, w_router, w_gate_e, w_up_e, w_down_e, final_g):
    batch, seq, d = x.shape
    assert (seq, d) == (SEQ, D_MODEL) and ctx.shape == (batch, CTX_LEN, D_MODEL)
    tb = {k: jnp.asarray(v) for k, v in _tables().items()}
    n_lat = batch * SEQ
    assert (batch * CTX_LEN) % TMW == 0 and n_lat % TMW == 0

    def cond(tile):
        return lambda i: jnp.where(i < n_lat // tile, i // (SEQ // tile), batch)

    def rope_idx(i):
        return jnp.where(i < n_lat // TMW, i % (SEQ // TMW), SEQ // TMW + i - n_lat // TMW)

    assert batch + 1 == N_COND
    cv = jnp.zeros((D_MODEL, SUBLANES), F32).at[:, :batch].set(c.T).at[:, batch].set(c_ctx)
    mods = _modulation(cv, w_mod, b_mod)

    xs_main, xs_extra = x.reshape(n_lat, D_MODEL), ctx.reshape(batch * CTX_LEN, D_MODEL)
    out = None
    for l in range(DEPTH):
        last = l == DEPTH - 1
        g1 = norm1_g[l].reshape(1, D_MODEL)
        g2 = norm2_g[l].reshape(1, D_MODEL)
        wf, wa, wo = w_fourier, w_attn, w_out
        u, q, kv, gates = _in_proj(xs_main, xs_extra, mods, l, cond(TMW), g1, w_in, tb["cos"], tb["sin"], rope_idx)
        fm = _fourier_latent(u, tb, batch)
        ao = _attention_latent(q, kv, sink[l], batch)
        if l % 2 == 0:
            if last:
                raise NotImplementedError("final norm is fused into the routed-expert combine")
            extras = (xs_extra, _fourier_ctx(u, tb, batch), _attention_ctx(q, kv, sink[l], batch))
            xa, h2 = _merge(n_lat, xs_main, fm, ao, extras, gates, wf, wa, wo, mods, l, cond(TMW), g2, True)
            xs_main = _ffn_dense(xa, h2, w_gate_d, w_up_d, w_down_d, mods, l, cond(TM))
            xs_extra = None
        else:
            if not last:
                raise NotImplementedError("context update through a routed-expert layer")
            i = l // 2
            (xl,) = _merge(n_lat, xs_main, fm, ao, None, gates, wf, wa, wo, mods, l, cond(TMW), g2, False)
            out = _moe(xl, mods, l, cond(TD), g2, w_router[i], w_gate_e[i], w_up_e[i], w_down_e[i],
                       final_g.reshape(1, D_MODEL))
    return out.reshape(batch, SEQ, D_MODEL)
```

```python
import functools
import math

import numpy as np
import jax
import jax.numpy as jnp
from jax import lax
from jax.experimental import pallas as pl
from jax.experimental.pallas import tpu as pltpu

F32 = jnp.float32
BF16 = jnp.bfloat16

D_MODEL = 1024
SEQ = 8192
DEPTH = 2
GRID_W = 64
CTX_LEN = 256
N_GROUPS = 4
GROUP_W = 128
FOURIER_WIDTH = N_GROUPS * GROUP_W
N_Q_HEADS = 8
N_KV_HEADS = 2
HEAD_DIM = 64
ATTN_WIDTH = N_Q_HEADS * HEAD_DIM
KV_WIDTH = N_KV_HEADS * HEAD_DIM
GROUP = N_Q_HEADS // N_KV_HEADS
WINDOW = 128
ROPE_BASE = 10000.0
OFF_Q = FOURIER_WIDTH
OFF_K = OFF_Q + ATTN_WIDTH
OFF_V = OFF_K + KV_WIDTH
OFF_G = OFF_V + KV_WIDTH
IN_WIDTH = OFF_G + 2 * D_MODEL
N_EXPERTS = 8
N_MOD = 6
EPS = 1e-6
NEG = -1e30
LOG2E = 1.0 / math.log(2.0)

LANES = 128
SUBLANES = 8

TM = 256
TMW = 512
SUB = 256
QB = 128
ATT_NQ = 8
Q_PAD_W = N_Q_HEADS * LANES
KVX_W = (1 + N_KV_HEADS) * LANES
FFT_R = SEQ // GRID_W
FFT_C = GRID_W
TME = 256
HALVES = 4
TMP = HALVES * TME
TD = 512
ROUTER_TILES = 4
SB = 2 * TD + N_EXPERTS * SUBLANES
XS_W = D_MODEL + LANES
SEG_BITS = TD.bit_length() - 3
TAIL_BITS = TMP.bit_length() - 4
FC_E = 512
VMEM_LIMIT = 56 * 1024 * 1024


def _cparams(sem, vmem=None):
    return pltpu.CompilerParams(dimension_semantics=sem, vmem_limit_bytes=vmem)


def _dot(a, b):
    return lax.dot_general(a, b, (((1,), (0,)), ((), ())), preferred_element_type=F32)


def _resident(a, layer):
    return pl.BlockSpec((None,) + a.shape[1:], lambda i, *_: (layer, 0, 0), pipeline_mode=pl.Buffered(1))


class _TwoSources:
    def __init__(self, main_ref, extra_ref, use_extra):
        self.main_ref, self.extra_ref, self.use_extra = main_ref, extra_ref, use_extra

    def __getitem__(self, idx):
        return jnp.where(self.use_extra, self.extra_ref[idx], self.main_ref[idx])


def _two_source_specs(main, extra, tile, width):
    n_main = main.shape[0] // tile
    return [pl.BlockSpec((tile, width), lambda i: (jnp.minimum(i, n_main - 1), 0)),
            pl.BlockSpec((tile, width), lambda i: (jnp.maximum(i - n_main, 0), 0))]


def _with_two_sources(body, n_pairs, n_main_tiles):
    def kernel_fn(*refs):
        use_extra = pl.program_id(0) >= n_main_tiles
        merged = [_TwoSources(refs[2 * k], refs[2 * k + 1], use_extra) for k in range(n_pairs)]
        return body(*merged, *refs[2 * n_pairs:])
    return kernel_fn


def _sigmoid(x):
    return 1.0 / (1.0 + jnp.exp(-x))


def _rms_mod(x, g, sh, sc):
    r = lax.rsqrt(jnp.mean(x * x, axis=-1, keepdims=True) + EPS)
    return (x * r) * g * (1.0 + sc) + sh


def _dft_cs(n):
    k = np.arange(n, dtype=np.float64)
    a = 2.0 * np.pi * np.outer(k, k) / n
    return np.cos(a), np.sin(a)


@functools.lru_cache(maxsize=None)
def _tables():
    c128, s128 = _dft_cs(FFT_R)
    c64, s64 = _dft_cs(FFT_C)
    c256, s256 = _dft_cs(CTX_LEN)
    wa = np.concatenate([c128, -s128], axis=0)
    k1 = np.arange(FFT_R, dtype=np.float64)[:, None]
    cc = np.arange(FFT_C, dtype=np.float64)[None, :]
    ang = 2.0 * np.pi * k1 * cc / SEQ
    twc = np.broadcast_to(np.cos(ang)[:, :, None], (FFT_R, FFT_C, LANES))
    tws = np.broadcast_to(np.sin(ang)[:, :, None], (FFT_R, FFT_C, LANES))
    mc = np.block([[c64, s64], [-s64, c64]])
    cg, sg = _dft_cs(GROUP_W)
    mch = np.concatenate([cg, sg], axis=0) / math.sqrt(SEQ * GROUP_W)
    wcx = np.concatenate([c256, s256], axis=0)
    mchc = np.concatenate([cg, -sg], axis=0)
    n_freq = HEAD_DIM // 4
    inv = ROPE_BASE ** (-np.arange(n_freq, dtype=np.float64) / n_freq)
    t = np.arange(SEQ)
    rows = (t // GRID_W).astype(np.float64)[:, None] * inv
    cols = (t % GRID_W).astype(np.float64)[:, None] * inv
    cos_h = np.concatenate([np.cos(rows), np.cos(rows), np.cos(cols), np.cos(cols)], axis=1)
    sin_h = np.concatenate([-np.sin(rows), np.sin(rows), -np.sin(cols), np.sin(cols)], axis=1)
    cos_t = np.concatenate([cos_h, cos_h], axis=1)
    sin_t = np.concatenate([sin_h, sin_h], axis=1)
    f = lambda a: np.ascontiguousarray(a, dtype=np.float32)
    return dict(wa=f(wa), twc=f(twc), tws=f(tws), mc=f(mc), mch=f(mch), wcx=f(wcx), mchc=f(mchc),
                cos=f(np.concatenate([cos_t, np.ones((TMW, LANES))], axis=0)),
                sin=f(np.concatenate([sin_t, np.zeros((TMW, LANES))], axis=0)))


def _mod_body(cv_ref, w_ref, b_ref, o_ref):
    c = cv_ref[...]
    s = c * _sigmoid(c)
    w = w_ref[0]
    outs = [jnp.sum(w * s[:, r:r + 1], axis=0, keepdims=True) for r in range(N_COND)]
    pad = jnp.zeros((SUBLANES - N_COND, MOD_PER_STEP * D_MODEL), F32)
    res = jnp.concatenate(outs + [pad], axis=0)
    for k in range(MOD_PER_STEP):
        o_ref[0, k] = res[:, k * D_MODEL:(k + 1) * D_MODEL] + b_ref[k]


MOD_PER_STEP = 2
N_COND = 3


def _modulation(cv, w_mod, b_mod):
    steps = N_MOD // MOD_PER_STEP
    out = pl.pallas_call(
        _mod_body,
        grid=(DEPTH, steps),
        in_specs=[pl.BlockSpec((D_MODEL, SUBLANES), lambda l, k: (0, 0)),
                  pl.BlockSpec((1, D_MODEL, MOD_PER_STEP * D_MODEL), lambda l, k: (l, 0, k)),
                  pl.BlockSpec((MOD_PER_STEP, 1, D_MODEL), lambda l, k: (l * steps + k, 0, 0))],
        out_specs=pl.BlockSpec((1, MOD_PER_STEP, SUBLANES, D_MODEL), lambda l, k: (l, k, 0, 0)),
        out_shape=jax.ShapeDtypeStruct((DEPTH, N_MOD, SUBLANES, D_MODEL), F32),
        compiler_params=_cparams(("arbitrary", "arbitrary")),
        name="modulation",
    )(cv, w_mod, b_mod.reshape(DEPTH * N_MOD, 1, D_MODEL))
    out = jnp.transpose(out[:, :, :3, :], (0, 2, 1, 3))
    return out.reshape(DEPTH * 3 * N_MOD, 1, D_MODEL)


def _mod_spec(layer, k, cond_fn):
    return pl.BlockSpec((1, 1, D_MODEL),
                        lambda i, *_: ((layer * 3 + cond_fn(i)) * N_MOD + k, 0, 0))


def _in_proj_body(x_ref, sh_ref, sc_ref, g_ref, w_ref, cos_ref, sin_ref,
                  u_ref, q_ref, kv_ref, gt_ref):
    lane = lax.broadcasted_iota(jnp.int32, (SUB, LANES), 1)
    first_half = (lane % (HEAD_DIM // 2)) < (HEAD_DIM // 4)
    quarter = HEAD_DIM // 4
    scale = HEAD_DIM ** -0.5 * LOG2E
    gc = 512
    for sb in range(TMW // SUB):
        rows = slice(sb * SUB, (sb + 1) * SUB)
        hb = _rms_mod(x_ref[rows, :], g_ref[...], sh_ref[0], sc_ref[0]).astype(BF16)
        u_ref[rows, :] = _dot(hb, w_ref[:, 0:OFF_Q]).astype(BF16)
        cos = cos_ref[rows, :]
        sin = sin_ref[rows, :]

        def rope(xs):
            below = pltpu.roll(xs, quarter, 1)
            above = pltpu.roll(xs, LANES - quarter, 1)
            return xs * cos + jnp.where(first_half, above, below) * sin

        qk = _dot(hb, w_ref[:, OFF_Q:OFF_V])
        low = lane < HEAD_DIM
        for j in range(ATTN_WIDTH // LANES):
            r = rope(qk[:, j * LANES:(j + 1) * LANES]) * scale
            swapped = pltpu.roll(r, HEAD_DIM, 1)
            if (2 * j) // GROUP == 0:
                even, odd = jnp.where(low, r, 0.0), jnp.where(low, swapped, 0.0)
            else:
                even, odd = jnp.where(low, 0.0, swapped), jnp.where(low, 0.0, r)
            q_ref[rows, (2 * j) * LANES:(2 * j + 1) * LANES] = even.astype(BF16)
            q_ref[rows, (2 * j + 1) * LANES:(2 * j + 2) * LANES] = odd.astype(BF16)
        kv_ref[rows, 0:LANES] = rope(qk[:, ATTN_WIDTH:ATTN_WIDTH + KV_WIDTH]).astype(BF16)
        v = _dot(hb, w_ref[:, OFF_V:OFF_G])
        kv_ref[rows, LANES:2 * LANES] = jnp.where(low, v, 1.0).astype(BF16)
        kv_ref[rows, 2 * LANES:3 * LANES] = jnp.where(low, pltpu.roll(v, HEAD_DIM, 1), 1.0).astype(BF16)
        for j in range(2 * D_MODEL // gc):
            z = _dot(hb, w_ref[:, OFF_G + j * gc:OFF_G + (j + 1) * gc])
            gt_ref[rows, j * gc:(j + 1) * gc] = _sigmoid(z).astype(BF16)


def _in_proj(x, x_extra, mods, layer, cond_fn, g, w_in, cos, sin, rope_idx):
    if x_extra is None:
        rows = x.shape[0]
        body, x_args = _in_proj_body, [x]
        x_specs = [pl.BlockSpec((TMW, D_MODEL), lambda i: (i, 0))]
    else:
        rows = x.shape[0] + x_extra.shape[0]
        body, x_args = _with_two_sources(_in_proj_body, 1, x.shape[0] // TMW), [x, x_extra]
        x_specs = _two_source_specs(x, x_extra, TMW, D_MODEL)
    return pl.pallas_call(
        body,
        grid=(rows // TMW,),
        in_specs=x_specs + [
                  _mod_spec(layer, 0, cond_fn),
                  _mod_spec(layer, 1, cond_fn),
                  pl.BlockSpec((1, D_MODEL), lambda i: (0, 0)),
                  _resident(w_in, layer),
                  pl.BlockSpec((TMW, LANES), lambda i: (rope_idx(i), 0)),
                  pl.BlockSpec((TMW, LANES), lambda i: (rope_idx(i), 0))],
        out_specs=[pl.BlockSpec((TMW, FOURIER_WIDTH), lambda i: (i, 0)),
                   pl.BlockSpec((TMW, Q_PAD_W), lambda i: (i, 0)),
                   pl.BlockSpec((TMW, KVX_W), lambda i: (i, 0)),
                   pl.BlockSpec((TMW, 2 * D_MODEL), lambda i: (i, 0))],
        out_shape=[jax.ShapeDtypeStruct((rows, FOURIER_WIDTH), BF16),
                   jax.ShapeDtypeStruct((rows, Q_PAD_W), BF16),
                   jax.ShapeDtypeStruct((rows, KVX_W), BF16),
                   jax.ShapeDtypeStruct((rows, 2 * D_MODEL), BF16)],
        compiler_params=_cparams(("arbitrary",), VMEM_LIMIT),
        name="in_proj",
    )(*x_args, mods, mods, g, w_in, cos, sin)


FFT_CB = 16


def _fft_a_body(x_ref, w_ref, o_ref):
    w = w_ref[...]
    xt = jnp.swapaxes(x_ref[...].astype(F32), 0, 1)
    res = jnp.stack([_dot(w, xt[c]) for c in range(FFT_CB)])
    o_ref[0] = jnp.swapaxes(res, 0, 1).astype(BF16)


def _fft_c_body(re_ref, im_ref, tc_ref, ts_ref, mc_ref, mch_ref, o_ref):
    mc = mc_ref[...]
    mch = mch_ref[...]
    twiddled = []
    for j in range(FFT_CB):
        ar = re_ref[0, j].astype(F32)
        ai = im_ref[0, j].astype(F32)
        tc = jnp.concatenate([tc_ref[j]] * N_GROUPS, axis=1)
        ts = jnp.concatenate([ts_ref[j]] * N_GROUPS, axis=1)
        twiddled.append(jnp.concatenate([ar * tc + ai * ts, ai * tc - ar * ts], axis=0))
    xs = [_dot(mc, b) for b in twiddled]
    xr = jnp.concatenate([x[:FFT_C] for x in xs], axis=0).astype(BF16)
    xi = jnp.concatenate([x[FFT_C:] for x in xs], axis=0).astype(BF16)
    ys = []
    for g in range(N_GROUPS):
        sl = slice(g * GROUP_W, (g + 1) * GROUP_W)
        ys.append(_dot(jnp.concatenate([xr[:, sl], xi[:, sl]], axis=1), mch))
    y = jnp.concatenate(ys, axis=1).reshape(FFT_CB, FFT_C, FOURIER_WIDTH)
    o_ref[0] = jnp.swapaxes(y, 0, 1).astype(BF16)


def _fourier_latent(u, tb, batch):
    u3 = u.reshape(u.shape[0] // FFT_C, FFT_C, FOURIER_WIDTH)
    a = pl.pallas_call(
        _fft_a_body,
        grid=(batch, FFT_C // FFT_CB),
        in_specs=[pl.BlockSpec((FFT_R, FFT_CB, FOURIER_WIDTH), lambda b, j: (b, j, 0)),
                  pl.BlockSpec((2 * FFT_R, FFT_R), lambda b, j: (0, 0))],
        out_specs=pl.BlockSpec((1, 2 * FFT_R, FFT_CB, FOURIER_WIDTH), lambda b, j: (b, 0, j, 0)),
        out_shape=jax.ShapeDtypeStruct((batch, 2 * FFT_R, FFT_C, FOURIER_WIDTH), BF16),
        compiler_params=_cparams(("arbitrary", "arbitrary"), VMEM_LIMIT),
        name="fft_rows",
    )(u3, tb["wa"].astype(BF16))
    nk = FFT_R // FFT_CB
    y = pl.pallas_call(
        _fft_c_body,
        grid=(batch, nk),
        in_specs=[pl.BlockSpec((1, FFT_CB, FFT_C, FOURIER_WIDTH), lambda b, k: (b, k, 0, 0)),
                  pl.BlockSpec((1, FFT_CB, FFT_C, FOURIER_WIDTH), lambda b, k: (b, nk + k, 0, 0)),
                  pl.BlockSpec((FFT_CB, FFT_C, LANES), lambda b, k: (k, 0, 0)),
                  pl.BlockSpec((FFT_CB, FFT_C, LANES), lambda b, k: (k, 0, 0)),
                  pl.BlockSpec((2 * FFT_C, 2 * FFT_C), lambda b, k: (0, 0)),
                  pl.BlockSpec((2 * GROUP_W, GROUP_W), lambda b, k: (0, 0))],
        out_specs=pl.BlockSpec((1, FFT_C, FFT_CB, FOURIER_WIDTH), lambda b, k: (b, 0, k, 0)),
        out_shape=jax.ShapeDtypeStruct((batch, FFT_C, FFT_R, FOURIER_WIDTH), BF16),
        compiler_params=_cparams(("arbitrary", "arbitrary"), VMEM_LIMIT),
        name="fft_cols",
    )(a, a, tb["twc"], tb["tws"], tb["mc"].astype(BF16), tb["mch"].astype(BF16))
    return y.reshape(batch * SEQ, FOURIER_WIDTH)


def _fourier_ctx_body(u_ref, w_ref, m_ref, o_ref):
    pq = _dot(w_ref[...], u_ref[...])
    p = pq[:CTX_LEN]
    q = pq[CTX_LEN:]
    m = m_ref[...]
    scale = 1.0 / math.sqrt(CTX_LEN * GROUP_W)
    for g in range(N_GROUPS):
        sl = slice(g * GROUP_W, (g + 1) * GROUP_W)
        lhs = jnp.concatenate([p[:, sl], q[:, sl]], axis=1).astype(BF16)
        o_ref[:, sl] = _dot(lhs, m) * scale


def _fourier_ctx(u, tb, batch):
    first = batch * SEQ // CTX_LEN
    return pl.pallas_call(
        _fourier_ctx_body,
        grid=(batch,),
        in_specs=[pl.BlockSpec((CTX_LEN, FOURIER_WIDTH), lambda b: (first + b, 0)),
                  pl.BlockSpec((2 * CTX_LEN, CTX_LEN), lambda b: (0, 0)),
                  pl.BlockSpec((2 * GROUP_W, GROUP_W), lambda b: (0, 0))],
        out_specs=pl.BlockSpec((CTX_LEN, FOURIER_WIDTH), lambda b: (b, 0)),
        out_shape=jax.ShapeDtypeStruct((batch * CTX_LEN, FOURIER_WIDTH), F32),
        compiler_params=_cparams(("arbitrary",)),
        name="fft_ctx",
    )(u, tb["wcx"].astype(BF16), tb["mchc"].astype(BF16))


def _attn_body(sink_ref, q_ref, *refs, local, nq):
    o_ref = refs[-1]
    rows = 2 * QB
    if local:
        blocks = [r[...] for r in refs[:nq + 2]]
        ctx_rows = refs[nq + 2][...]
        step = pl.program_id(1)
        last_step = pl.num_programs(1) - 1
        qi = lax.broadcasted_iota(jnp.int32, (rows, 2 * QB), 0) % QB
        kj = lax.broadcasted_iota(jnp.int32, (rows, 2 * QB), 1)
        groups, masks = [], []
        for t in range(nq):
            no_prev = jnp.where(step == 0, 2 * QB, 0) if t == 0 else 0
            no_next = jnp.where(step == last_step, 2 * QB, 0) if t == nq - 1 else 0
            prev_ok = (kj < QB) & (kj >= qi + no_prev)
            next_ok = (kj >= QB) & (kj - QB <= qi - no_next)
            groups.append([jnp.concatenate([blocks[t], blocks[t + 2]], axis=0),
                           jnp.concatenate([blocks[t + 1], ctx_rows], axis=0)])
            masks.append([prev_ok | next_ok, None])
    else:
        groups = [[refs[0][...]]]
        masks = [[None]]
    lo_q = lax.broadcasted_iota(jnp.int32, (QB, LANES), 1) < HEAD_DIM
    first_head = lax.broadcasted_iota(jnp.int32, (rows, 1), 0) < QB
    chains = [(t, slab) for t in range(nq) for slab in range(ATTN_WIDTH // LANES)]
    scores = []
    for t, slab in chains:
        qrows = slice(t * QB, (t + 1) * QB)
        qs = jnp.concatenate([q_ref[qrows, (2 * slab) * LANES:(2 * slab + 1) * LANES],
                              q_ref[qrows, (2 * slab + 1) * LANES:(2 * slab + 2) * LANES]], axis=0)
        parts = []
        for kv, mask in zip(groups[t], masks[t]):
            s = lax.dot_general(qs, kv[:, :LANES], (((1,), (1,)), ((), ())), preferred_element_type=F32)
            parts.append(s if mask is None else jnp.where(mask, s, NEG))
        scores.append(parts)
    sinks = [jnp.where(first_head, sink_ref[2 * slab], sink_ref[2 * slab + 1]) * LOG2E for _, slab in chains]
    maxes = []
    for parts, sk in zip(scores, sinks):
        m = sk
        for s in parts:
            m = jnp.maximum(m, jnp.max(s, axis=1, keepdims=True))
        maxes.append(m)
    sink_p = [jnp.exp2(sk - m) for sk, m in zip(sinks, maxes)]
    pvs = []
    for (t, slab), parts, m in zip(chains, scores, maxes):
        hk = (2 * slab) // GROUP
        pv = None
        for s, kv in zip(parts, groups[t]):
            term = _dot(jnp.exp2(s - m).astype(BF16), kv[:, (1 + hk) * LANES:(2 + hk) * LANES])
            pv = term if pv is None else pv + term
        pvs.append(pv)
    for (t, slab), pv, sp in zip(chains, pvs, sink_p):
        ra, rb = pv[:QB], pv[QB:]
        num = jnp.where(lo_q, ra, pltpu.roll(rb, HEAD_DIM, 1))
        den = jnp.where(lo_q, pltpu.roll(ra, HEAD_DIM, 1), rb) + jnp.where(lo_q, sp[:QB], sp[QB:])
        o_ref[t * QB:(t + 1) * QB, slab * LANES:(slab + 1) * LANES] = (num / den).astype(BF16)


def _attention_latent(q, kv, sink, batch):
    nb = SEQ // QB
    steps = nb // ATT_NQ
    first_ctx = batch * SEQ // CTX_LEN

    def key_block(j):
        return pl.BlockSpec((QB, KVX_W),
                            lambda b, n: (b * nb + jnp.clip(n * ATT_NQ - 1 + j, 0, nb - 1), 0))

    return pl.pallas_call(
        functools.partial(_attn_body, local=True, nq=ATT_NQ),
        grid=(batch, steps),
        in_specs=[pl.BlockSpec(memory_space=pltpu.SMEM),
                  pl.BlockSpec((ATT_NQ * QB, Q_PAD_W), lambda b, n: (b * steps + n, 0))]
                 + [key_block(j) for j in range(ATT_NQ + 2)]
                 + [pl.BlockSpec((CTX_LEN, KVX_W), lambda b, n: (first_ctx + b, 0))],
        out_specs=pl.BlockSpec((ATT_NQ * QB, ATTN_WIDTH), lambda b, n: (b * steps + n, 0)),
        out_shape=jax.ShapeDtypeStruct((batch * SEQ, ATTN_WIDTH), BF16),
        compiler_params=_cparams(("arbitrary", "arbitrary")),
        name="attn_latent",
    )(sink, q, *([kv] * (ATT_NQ + 3)))


def _attention_ctx(q, kv, sink, batch):
    nb = CTX_LEN // QB
    first_q = batch * SEQ // QB
    first_ctx = batch * SEQ // CTX_LEN
    return pl.pallas_call(
        functools.partial(_attn_body, local=False, nq=1),
        grid=(batch, nb),
        in_specs=[pl.BlockSpec(memory_space=pltpu.SMEM),
                  pl.BlockSpec((QB, Q_PAD_W), lambda b, n: (first_q + b * nb + n, 0)),
                  pl.BlockSpec((CTX_LEN, KVX_W), lambda b, n: (first_ctx + b, 0))],
        out_specs=pl.BlockSpec((QB, ATTN_WIDTH), lambda b, n: (b * nb + n, 0)),
        out_shape=jax.ShapeDtypeStruct((batch * CTX_LEN, ATTN_WIDTH), BF16),
        compiler_params=_cparams(("arbitrary", "arbitrary")),
        name="attn_ctx",
    )(sink, q, kv)


def _merge_body(x_ref, fm_ref, ao_ref, gt_ref, wf_ref, wa_ref, wo_ref, ga_ref, *refs, with_h2):
    if with_h2:
        sh_ref, sc_ref, g_ref, xo_ref, h_ref = refs
    else:
        (xo_ref,) = refs
    for sb in range(TMW // SUB):
        rows = slice(sb * SUB, (sb + 1) * SUB)
        gt = gt_ref[rows, :].astype(F32)
        y = (gt[:, :D_MODEL] * _dot(fm_ref[rows, :], wf_ref[...])
             + gt[:, D_MODEL:] * _dot(ao_ref[rows, :], wa_ref[...]))
        xn = x_ref[rows, :] + ga_ref[0] * _dot(y, wo_ref[...])
        if with_h2:
            h_ref[rows, :] = _rms_mod(xn, g_ref[...], sh_ref[0], sc_ref[0]).astype(BF16)
        xo_ref[rows, :] = xn


def _merge(rows, x, fm, ao, extras, gates, wf, wa, wo, mods, layer, cond_fn, g2, with_h2):
    row_spec = lambda w: pl.BlockSpec((TMW, w), lambda i: (i, 0))
    widths = (D_MODEL, FOURIER_WIDTH, ATTN_WIDTH)
    body = functools.partial(_merge_body, with_h2=with_h2)
    if extras is None:
        in_specs = [row_spec(w) for w in widths]
        args = [x, fm, ao]
    else:
        main_rows = rows
        rows = rows + extras[0].shape[0]
        body = _with_two_sources(body, len(widths), main_rows // TMW)
        in_specs, args = [], []
        for main, extra, w in zip((x, fm, ao), extras, widths):
            assert main.shape[0] >= main_rows and extra.shape[0] == extras[0].shape[0]
            in_specs += [pl.BlockSpec((TMW, w), lambda i: (jnp.minimum(i, main_rows // TMW - 1), 0)),
                         pl.BlockSpec((TMW, w), lambda i: (jnp.maximum(i - main_rows // TMW, 0), 0))]
            args += [main, extra]
    in_specs += [row_spec(2 * D_MODEL),
                 _resident(wf, layer), _resident(wa, layer), _resident(wo, layer), _mod_spec(layer, 2, cond_fn)]
    args += [gates, wf, wa, wo, mods]
    out_specs = [row_spec(D_MODEL)]
    out_shape = [jax.ShapeDtypeStruct((rows, D_MODEL), F32)]
    if with_h2:
        in_specs += [_mod_spec(layer, 3, cond_fn), _mod_spec(layer, 4, cond_fn),
                     pl.BlockSpec((1, D_MODEL), lambda i: (0, 0))]
        args += [mods, mods, g2]
        out_specs.append(row_spec(D_MODEL))
        out_shape.append(jax.ShapeDtypeStruct((rows, D_MODEL), BF16))
    return pl.pallas_call(
        body,
        grid=(rows // TMW,),
        in_specs=in_specs, out_specs=out_specs, out_shape=out_shape,
        compiler_params=_cparams(("arbitrary",), VMEM_LIMIT),
        name="merge",
    )(*args)


def _ffn_body(x_ref, h_ref, wg_ref, wu_ref, wd_ref, ga_ref, o_ref):
    h = h_ref[...]
    g = _dot(h, wg_ref[...])
    u = _dot(h, wu_ref[...])
    a = g * _sigmoid(g) * u
    o_ref[...] = x_ref[...] + ga_ref[0] * _dot(a, wd_ref[...])


def _ffn_dense(x, h2, wg, wu, wd, mods, layer, cond_fn):
    rows = x.shape[0]
    return pl.pallas_call(
        _ffn_body,
        grid=(rows // TM,),
        in_specs=[pl.BlockSpec((TM, D_MODEL), lambda i: (i, 0)),
                  pl.BlockSpec((TM, D_MODEL), lambda i: (i, 0)),
                  _resident(wg, layer // 2), _resident(wu, layer // 2), _resident(wd, layer // 2),
                  _mod_spec(layer, 5, cond_fn)],
        out_specs=pl.BlockSpec((TM, D_MODEL), lambda i: (i, 0)),
        out_shape=jax.ShapeDtypeStruct((rows, D_MODEL), F32),
        compiler_params=_cparams(("arbitrary",), VMEM_LIMIT),
        name="ffn_dense",
    )(x, h2, wg, wu, wd, mods)


INFO_E0, INFO_E1, INFO_R0, INFO_R1, INFO_W0, INFO_W1 = range(6)


def _router_body(x_ref, sh_ref, sc_ref, g_ref, wr_ref, hb_ref, info_ref, cnt_ref):
    tiles = [slice(t * TD, (t + 1) * TD) for t in range(ROUTER_TILES)]
    lane = lax.broadcasted_iota(jnp.int32, (TD, LANES), 1)
    neg_inf = jnp.float32(-jnp.inf)
    row = lax.broadcasted_iota(jnp.int32, (TD, TD), 0)
    col = lax.broadcasted_iota(jnp.int32, (TD, TD), 1)
    tri = jnp.where(row > col, 1.0, 0.0).astype(BF16)
    hs = [_rms_mod(x_ref[rows, :], g_ref[...], sh_ref[0], sc_ref[0]) for rows in tiles]
    for rows, h in zip(tiles, hs):
        hb_ref[rows, :] = h.astype(BF16)
    lgs = [jnp.where(lane < N_EXPERTS,
                     jnp.dot(h, wr_ref[...], precision=lax.Precision.HIGHEST, preferred_element_type=F32), neg_inf)
           for h in hs]
    v0s = [jnp.max(lg, axis=1, keepdims=True) for lg in lgs]
    i0s = [jnp.min(jnp.where(lg == v0, lane, LANES), axis=1, keepdims=True) for lg, v0 in zip(lgs, v0s)]
    oh0s = [lane == i0 for i0 in i0s]
    lg1s = [jnp.where(oh0, neg_inf, lg) for oh0, lg in zip(oh0s, lgs)]
    v1s = [jnp.max(lg1, axis=1, keepdims=True) for lg1 in lg1s]
    i1s = [jnp.min(jnp.where(lg1 == v1, lane, LANES), axis=1, keepdims=True) for lg1, v1 in zip(lg1s, v1s)]
    oh1s = [lane == i1 for i1 in i1s]
    ohs = [jnp.where(oh0 | oh1, 1.0, 0.0) for oh0, oh1 in zip(oh0s, oh1s)]
    befores = [_dot(tri, oh.astype(BF16)) for oh in ohs]
    for t, rows in enumerate(tiles):
        e = jnp.exp(v1s[t] - v0s[t])
        w0 = 1.0 / (1.0 + e)
        w1 = e / (1.0 + e)
        r0 = jnp.sum(jnp.where(oh0s[t], befores[t], 0.0), axis=1, keepdims=True)
        r1 = jnp.sum(jnp.where(oh1s[t], befores[t], 0.0), axis=1, keepdims=True)
        cnt_ref[t] = jnp.broadcast_to(jnp.sum(ohs[t], axis=0, keepdims=True), (SUBLANES, LANES))
        info = jnp.zeros((TD, LANES), F32)
        for idx, val in ((INFO_E0, i0s[t].astype(F32)), (INFO_E1, i1s[t].astype(F32)), (INFO_R0, r0),
                         (INFO_R1, r1), (INFO_W0, w0), (INFO_W1, w1)):
            info = jnp.where(lane == idx, val, info)
        info_ref[rows, :] = info


def _router(x, mods, layer, cond_fn, g2, wr_pad):
    rows = x.shape[0]
    step_rows = ROUTER_TILES * TD
    step_cond = lambda i: cond_fn(i * ROUTER_TILES)
    return pl.pallas_call(
        _router_body,
        grid=(rows // step_rows,),
        in_specs=[pl.BlockSpec((step_rows, D_MODEL), lambda i: (i, 0)),
                  _mod_spec(layer, 3, step_cond), _mod_spec(layer, 4, step_cond),
                  pl.BlockSpec((1, D_MODEL), lambda i: (0, 0)),
                  pl.BlockSpec((D_MODEL, LANES), lambda i: (0, 0))],
        out_specs=[pl.BlockSpec((step_rows, D_MODEL), lambda i: (i, 0)),
                   pl.BlockSpec((step_rows, LANES), lambda i: (i, 0)),
                   pl.BlockSpec((ROUTER_TILES, SUBLANES, LANES), lambda i: (i, 0, 0))],
        out_shape=[jax.ShapeDtypeStruct((rows, D_MODEL), BF16),
                   jax.ShapeDtypeStruct((rows, LANES), F32),
                   jax.ShapeDtypeStruct((rows // TD, SUBLANES, LANES), F32)],
        compiler_params=_cparams(("arbitrary",), VMEM_LIMIT),
        name="router",
    )(x, mods, mods, g2, wr_pad)


def _segment_copies(src_ref, src_off, dst_ref, dst_off, len8, sem, bits, wait):
    for k in reversed(range(bits)):
        size = SUBLANES << k
        done = ((len8 >> (k + 1)) << (k + 1)) * SUBLANES

        @pl.when(((len8 >> k) & 1) == 1)
        def _(size=size, done=done):
            cp = pltpu.make_async_copy(
                src_ref.at[pl.ds(pl.multiple_of(src_off + done, SUBLANES), size)],
                dst_ref.at[pl.ds(pl.multiple_of(dst_off + done, SUBLANES), size)], sem)
            if wait:
                cp.wait()
            else:
                cp.start()


def _slot_positions(e0, e1, r0, r1, seg_ref, base):
    pos0, pos1 = r0, r1
    for e in range(N_EXPERTS):
        start = seg_ref[base + e].astype(F32)
        pos0 = pos0 + jnp.where(e0 == e, start, 0.0)
        pos1 = pos1 + jnp.where(e1 == e, start, 0.0)
    return pos0.astype(jnp.int32), pos1.astype(jnp.int32)


def _dispatch_body(seg_ref, dst_ref, len_ref, tdst_ref, tlen_ref, nt_ref, hb_ref, info_ref, xs_ref,
                   buf_ref, zero_ref, sems, zero_sem, *, min_tiles):
    i = pl.program_id(0)
    last = pl.num_programs(0) - 1
    base = i * N_EXPERTS
    cur = i % 2

    def scatter(tile, slot, wait):
        for e in range(N_EXPERTS):
            k = tile * N_EXPERTS + e
            _segment_copies(buf_ref.at[slot], seg_ref[k], xs_ref, dst_ref[k], len_ref[k],
                            sems.at[slot], SEG_BITS, wait=wait)

    info_t = info_ref[...].T
    row = lambda k: info_t[k:k + 1, :]
    pos0, pos1 = _slot_positions(row(INFO_E0), row(INFO_E1), row(INFO_R0), row(INFO_R1), seg_ref, base)
    slot = lax.broadcasted_iota(jnp.int32, (SB, TD), 0)
    p0 = slot == pos0
    p1 = slot == pos1
    perm = jnp.where(p0 | p1, 1.0, 0.0).astype(BF16)
    buf_ref[cur, :, :D_MODEL] = _dot(perm, hb_ref[...])
    wsel = jnp.where(p0, row(INFO_W0), 0.0) + jnp.where(p1, row(INFO_W1), 0.0)
    buf_ref[cur, :, D_MODEL:] = jnp.broadcast_to(jnp.sum(wsel, axis=1, keepdims=True), (SB, LANES))
    scatter(i, cur, wait=False)

    @pl.when(i > 0)
    def _():
        scatter(i - 1, 1 - cur, wait=True)

    @pl.when(i == last)
    def _():
        zero_ref[...] = jnp.zeros_like(zero_ref)
        spare_tiles = range(min_tiles, xs_ref.shape[0] // TMP)
        spare = [pltpu.make_async_copy(zero_ref, xs_ref.at[pl.ds(j * TMP, TMP)], zero_sem) for j in spare_tiles]
        for wait in (False, True):
            for e in range(N_EXPERTS):
                _segment_copies(zero_ref, 0, xs_ref, tdst_ref[e], tlen_ref[e], zero_sem, TAIL_BITS, wait=wait)
            for j, cp in zip(spare_tiles, spare):
                @pl.when(j >= nt_ref[0])
                def _(cp=cp, wait=wait):
                    if wait:
                        cp.wait()
                    else:
                        cp.start()
        scatter(i, cur, wait=True)


def _dispatch(seg, dst, len8, tail_dst, tail_len8, n_tiles, hb, info, m_rows):
    rows = hb.shape[0]
    return pl.pallas_call(
        functools.partial(_dispatch_body, min_tiles=2 * rows // TMP),
        grid_spec=pltpu.PrefetchScalarGridSpec(
            num_scalar_prefetch=6,
            grid=(rows // TD,),
            in_specs=[pl.BlockSpec((TD, D_MODEL), lambda i, *_: (i, 0)),
                      pl.BlockSpec((TD, LANES), lambda i, *_: (i, 0))],
            out_specs=pl.BlockSpec(memory_space=pl.ANY),
            scratch_shapes=[pltpu.VMEM((2, SB, XS_W), F32), pltpu.VMEM((TMP, XS_W), F32),
                            pltpu.SemaphoreType.DMA((2,)), pltpu.SemaphoreType.DMA(())]),
        out_shape=jax.ShapeDtypeStruct((m_rows, XS_W), F32),
        compiler_params=_cparams(("arbitrary",), VMEM_LIMIT),
        name="dispatch",
    )(seg, dst, len8, tail_dst, tail_len8, n_tiles, hb, info)


def _expert_body(te_ref, nh_ref, nt_ref, xs_ref, wg_ref, wu_ref, wd_ref, ys_ref, acc_ref):
    j = pl.program_id(0)
    f = pl.program_id(1)
    nf = pl.num_programs(1)
    halves = nh_ref[j]

    blocks = [slice(h * TME, (h + 1) * TME) for h in range(HALVES)]

    @pl.when((halves > 0) & (f == 0))
    def _():
        acc_ref[...] = jnp.zeros_like(acc_ref)

    for live in range(1, HALVES + 1):
        @pl.when(halves == live)
        def _(live=live):
            ups = [(_dot(xs_ref[rows, :D_MODEL], wg_ref[0]), _dot(xs_ref[rows, :D_MODEL], wu_ref[0]))
                   for rows in blocks[:live]]
            downs = [_dot(g * _sigmoid(g) * u, wd_ref[0]) for g, u in ups]
            for rows, d in zip(blocks[:live], downs):
                acc_ref[rows, :] += d

    @pl.when(f == nf - 1)
    def _():
        live_rows = lax.broadcasted_iota(jnp.int32, (TMP, 1), 0) < halves * TME
        ys_ref[...] = jnp.where(live_rows, acc_ref[...] * xs_ref[:, D_MODEL:D_MODEL + 1], 0.0)


def _experts(tile_expert, tile_halves, n_tiles, xs, wg, wu, wd):
    m_rows = xs.shape[0]
    nf = wg.shape[2] // FC_E

    def f_idx(j, f, nt):
        last_live = nt[0] - 1
        walk = lambda t, step: jnp.where(t % 2 == 0, step, nf - 1 - step)
        return jnp.where(j < nt[0], walk(j, f), walk(last_live, nf - 1))

    def j_idx(j, nt):
        return jnp.minimum(j, nt[0] - 1)

    return pl.pallas_call(
        _expert_body,
        grid_spec=pltpu.PrefetchScalarGridSpec(
            num_scalar_prefetch=3,
            grid=(m_rows // TMP, nf),
            in_specs=[pl.BlockSpec((TMP, XS_W), lambda j, f, te, nh, nt: (j_idx(j, nt), 0)),
                      pl.BlockSpec((1, D_MODEL, FC_E), lambda j, f, te, nh, nt: (te[j], 0, f_idx(j, f, nt))),
                      pl.BlockSpec((1, D_MODEL, FC_E), lambda j, f, te, nh, nt: (te[j], 0, f_idx(j, f, nt))),
                      pl.BlockSpec((1, FC_E, D_MODEL), lambda j, f, te, nh, nt: (te[j], f_idx(j, f, nt), 0))],
            out_specs=pl.BlockSpec((TMP, D_MODEL), lambda j, f, te, nh, nt: (j, 0)),
            scratch_shapes=[pltpu.VMEM((TMP, D_MODEL), F32)]),
        out_shape=jax.ShapeDtypeStruct((m_rows, D_MODEL), F32),
        compiler_params=_cparams(("arbitrary", "arbitrary"), VMEM_LIMIT),
        name="experts",
    )(tile_expert, tile_halves, n_tiles, xs, wg, wu, wd)


def _combine_body(seg_ref, dst_ref, len_ref, ys_ref, x_ref, info_ref, ga_ref, g_ref, o_ref, buf_ref, sems):
    i = pl.program_id(0)
    base = i * N_EXPERTS
    cur = i % 2

    def gather(tile, slot, wait):
        for e in range(N_EXPERTS):
            k = tile * N_EXPERTS + e
            _segment_copies(ys_ref, dst_ref[k], buf_ref.at[slot], seg_ref[k], len_ref[k],
                            sems.at[slot], SEG_BITS, wait=wait)

    @pl.when(i == 0)
    def _():
        buf_ref[...] = jnp.zeros_like(buf_ref)
        gather(0, 0, wait=False)

    @pl.when(i + 1 < pl.num_programs(0))
    def _():
        gather(i + 1, 1 - cur, wait=False)

    info = info_ref[...]
    col = lambda k: info[:, k:k + 1]
    pos0, pos1 = _slot_positions(col(INFO_E0), col(INFO_E1), col(INFO_R0), col(INFO_R1), seg_ref, base)
    slot = lax.broadcasted_iota(jnp.int32, (TD, SB), 1)
    unperm = jnp.where((slot == pos0) | (slot == pos1), 1.0, 0.0).astype(BF16)
    gather(i, cur, wait=True)
    y = _dot(unperm, buf_ref[cur])
    xn = x_ref[...] + ga_ref[0] * y
    r = lax.rsqrt(jnp.mean(xn * xn, axis=-1, keepdims=True) + EPS)
    o_ref[...] = (xn * r) * g_ref[...]


def _combine(seg, dst, len8, ys, x, info, mods, layer, cond_fn, final_g):
    rows = x.shape[0]
    return pl.pallas_call(
        _combine_body,
        grid_spec=pltpu.PrefetchScalarGridSpec(
            num_scalar_prefetch=3,
            grid=(rows // TD,),
            in_specs=[pl.BlockSpec(memory_space=pl.ANY),
                      pl.BlockSpec((TD, D_MODEL), lambda i, *_: (i, 0)),
                      pl.BlockSpec((TD, LANES), lambda i, *_: (i, 0)),
                      _mod_spec(layer, 5, cond_fn),
                      pl.BlockSpec((1, D_MODEL), lambda i, *_: (0, 0))],
            out_specs=pl.BlockSpec((TD, D_MODEL), lambda i, *_: (i, 0)),
            scratch_shapes=[pltpu.VMEM((2, SB, D_MODEL), F32), pltpu.SemaphoreType.DMA((2,))]),
        out_shape=jax.ShapeDtypeStruct((rows, D_MODEL), F32),
        compiler_params=_cparams(("arbitrary",), VMEM_LIMIT),
        name="combine",
    )(seg, dst, len8, ys, x, info, mods, final_g)


def _moe(x, mods, layer, cond_fn, g2, w_router, wg, wu, wd, final_g):
    rows = x.shape[0]
    n_tok_tiles = rows // TD
    wr_pad = jnp.pad(w_router, ((0, 0), (0, LANES - N_EXPERTS)))
    hb, info, cnt = _router(x, mods, layer, cond_fn, g2, wr_pad)
    counts = cnt[:, 0, :N_EXPERTS].astype(jnp.int32)
    seg_len = (counts + SUBLANES - 1) // SUBLANES * SUBLANES
    seg_start = jnp.cumsum(seg_len, axis=1) - seg_len
    group = jnp.sum(seg_len, axis=0)
    group_halves = (group + TME - 1) // TME
    group_tiles = (group + TMP - 1) // TMP
    tile_start = jnp.cumsum(group_tiles) - group_tiles
    group_base = tile_start * TMP
    dst = group_base[None, :] + jnp.cumsum(seg_len, axis=0) - seg_len
    tail_dst = group_base + group
    tail_len8 = (group_tiles * TMP - group) // SUBLANES
    max_tiles = (2 * rows + n_tok_tiles * N_EXPERTS * (SUBLANES - 1) + TMP - 1) // TMP + N_EXPERTS
    n_tiles = jnp.sum(group_tiles).astype(jnp.int32).reshape(1)
    tiles = jnp.arange(max_tiles, dtype=jnp.int32)
    tid = jnp.minimum(tiles, n_tiles[0] - 1)
    tile_expert = (jnp.sum(tid[:, None] >= tile_start[None, :], axis=1) - 1).astype(jnp.int32)
    own = (tile_expert[:, None] == jnp.arange(N_EXPERTS, dtype=jnp.int32)[None, :]).astype(jnp.int32)
    in_group = tid - jnp.sum(own * tile_start[None, :], axis=1)
    tile_halves = jnp.clip(jnp.sum(own * group_halves[None, :], axis=1) - HALVES * in_group, 0, HALVES)
    tile_halves = jnp.where(tiles < n_tiles[0], tile_halves, 0).astype(jnp.int32)
    flat = lambda a: a.reshape(-1).astype(jnp.int32)
    seg, dst, len8 = flat(seg_start), flat(dst), flat(seg_len // SUBLANES)
    xs = _dispatch(seg, dst, len8, flat(tail_dst), flat(tail_len8), n_tiles, hb, info, max_tiles * TMP)
    ys = _experts(tile_expert, tile_halves, n_tiles, xs, wg, wu, wd)
    return _combine(seg, dst, len8, ys, x, info, mods, layer, cond_fn, final_g)


def kernel(x, c, ctx, c_ctx, w_mod, b_mod, norm1_g, norm2_g, w_in, sink, w_fourier, w_attn, w_out,
           w_gate_d, w_up_d, w_down_d, w_router, w_gate_e, w_up_e, w_down_e, final_g):
    batch, seq, d = x.shape
    assert (seq, d) == (SEQ, D_MODEL) and ctx.shape == (batch, CTX_LEN, D_MODEL)
    tb = {k: jnp.asarray(v) for k, v in _tables().items()}
    n_lat = batch * SEQ
    assert (batch * CTX_LEN) % TMW == 0 and n_lat % TMW == 0

    def cond(tile):
        return lambda i: jnp.where(i < n_lat // tile, i // (SEQ // tile), batch)

    def rope_idx(i):
        return jnp.where(i < n_lat // TMW, i % (SEQ // TMW), SEQ // TMW + i - n_lat // TMW)

    assert batch + 1 == N_COND
    cv = jnp.zeros((D_MODEL, SUBLANES), F32).at[:, :batch].set(c.T).at[:, batch].set(c_ctx)
    mods = _modulation(cv, w_mod, b_mod)

    xs_main, xs_extra = x.reshape(n_lat, D_MODEL), ctx.reshape(batch * CTX_LEN, D_MODEL)
    out = None
    for l in range(DEPTH):
        last = l == DEPTH - 1
        g1 = norm1_g[l].reshape(1, D_MODEL)
        g2 = norm2_g[l].reshape(1, D_MODEL)
        wf, wa, wo = w_fourier, w_attn, w_out
        u, q, kv, gates = _in_proj(xs_main, xs_extra, mods, l, cond(TMW), g1, w_in, tb["cos"], tb["sin"], rope_idx)
        fm = _fourier_latent(u, tb, batch)
        ao = _attention_latent(q, kv, sink[l], batch)
        if l % 2 == 0:
            if last:
                raise NotImplementedError("final norm is fused into the routed-expert combine")
            extras = (xs_extra, _fourier_ctx(u, tb, batch), _attention_ctx(q, kv, sink[l], batch))
            xa, h2 = _merge(n_lat, xs_main, fm, ao, extras, gates, wf, wa, wo, mods, l, cond(TMW), g2, True)
            xs_main = _ffn_dense(xa, h2, w_gate_d, w_up_d, w_down_d, mods, l, cond(TM))
            xs_extra = None
        else:
            if not last:
                raise NotImplementedError("context update through a routed-expert layer")
            i = l // 2
            (xl,) = _merge(n_lat, xs_main, fm, ao, None, gates, wf, wa, wo, mods, l, cond(TMW), g2, False)
            out = _moe(xl, mods, l, cond(TD), g2, w_router[i], w_gate_e[i], w_up_e[i], w_down_e[i],
                       final_g.reshape(1, D_MODEL))
    return out.reshape(batch, SEQ, D_MODEL)
```

```python
import functools
import math

import numpy as np
import jax
import jax.numpy as jnp
from jax import lax
from jax.experimental import pallas as pl
from jax.experimental.pallas import tpu as pltpu

F32 = jnp.float32
BF16 = jnp.bfloat16

D_MODEL = 1024
SEQ = 8192
DEPTH = 2
GRID_W = 64
CTX_LEN = 256
N_GROUPS = 4
GROUP_W = 128
FOURIER_WIDTH = N_GROUPS * GROUP_W
N_Q_HEADS = 8
N_KV_HEADS = 2
HEAD_DIM = 64
ATTN_WIDTH = N_Q_HEADS * HEAD_DIM
KV_WIDTH = N_KV_HEADS * HEAD_DIM
GROUP = N_Q_HEADS // N_KV_HEADS
WINDOW = 128
ROPE_BASE = 10000.0
OFF_Q = FOURIER_WIDTH
OFF_K = OFF_Q + ATTN_WIDTH
OFF_V = OFF_K + KV_WIDTH
OFF_G = OFF_V + KV_WIDTH
IN_WIDTH = OFF_G + 2 * D_MODEL
N_EXPERTS = 8
N_MOD = 6
EPS = 1e-6
NEG = -1e30
LOG2E = 1.0 / math.log(2.0)

LANES = 128
SUBLANES = 8
MXU_W = 256

TM = 512
FFN_CHUNKS = 2
TMW = 512
SUB = 256
QB = 128
ATT_NQ = 8
Q_PAD_W = N_Q_HEADS * LANES
KVX_W = (1 + N_KV_HEADS) * LANES
FFT_R = SEQ // GRID_W
FFT_C = GRID_W
TME = 256
HALVES = 4
TMP = HALVES * TME
TD = 512
ROUTER_TILES = 4
SB = 2 * TD + N_EXPERTS * SUBLANES
XS_W = D_MODEL + LANES
SEG_BITS = TD.bit_length() - 3
TAIL_BITS = TMP.bit_length() - 4
FC_E = 512
VMEM_LIMIT = 56 * 1024 * 1024


def _cparams(sem, vmem=None):
    return pltpu.CompilerParams(dimension_semantics=sem, vmem_limit_bytes=vmem)


def _dot(a, b):
    return lax.dot_general(a, b, (((1,), (0,)), ((), ())), preferred_element_type=F32)


def _resident(a, layer):
    return pl.BlockSpec((None,) + a.shape[1:], lambda i, *_: (layer, 0, 0), pipeline_mode=pl.Buffered(1))


class _TwoSources:
    def __init__(self, main_ref, extra_ref, use_extra):
        self.main_ref, self.extra_ref, self.use_extra = main_ref, extra_ref, use_extra

    def __getitem__(self, idx):
        return jnp.where(self.use_extra, self.extra_ref[idx], self.main_ref[idx])


def _two_source_specs(main, extra, tile, width):
    n_main = main.shape[0] // tile
    return [pl.BlockSpec((tile, width), lambda i: (jnp.minimum(i, n_main - 1), 0)),
            pl.BlockSpec((tile, width), lambda i: (jnp.maximum(i - n_main, 0), 0))]


def _with_two_sources(body, n_pairs, n_main_tiles):
    def kernel_fn(*refs):
        use_extra = pl.program_id(0) >= n_main_tiles
        merged = [_TwoSources(refs[2 * k], refs[2 * k + 1], use_extra) for k in range(n_pairs)]
        return body(*merged, *refs[2 * n_pairs:])
    return kernel_fn


def _sigmoid(x):
    return 1.0 / (1.0 + jnp.exp(-x))


def _rms_mod(x, g, sh, sc):
    r = lax.rsqrt(jnp.mean(x * x, axis=-1, keepdims=True) + EPS)
    return (x * r) * g * (1.0 + sc) + sh


def _dft_cs(n):
    k = np.arange(n, dtype=np.float64)
    a = 2.0 * np.pi * np.outer(k, k) / n
    return np.cos(a), np.sin(a)


@functools.lru_cache(maxsize=None)
def _tables():
    c128, s128 = _dft_cs(FFT_R)
    c64, s64 = _dft_cs(FFT_C)
    c256, s256 = _dft_cs(CTX_LEN)
    wa = np.concatenate([c128, -s128], axis=0)
    k1 = np.arange(FFT_R, dtype=np.float64)[:, None]
    cc = np.arange(FFT_C, dtype=np.float64)[None, :]
    ang = 2.0 * np.pi * k1 * cc / SEQ
    twc = np.broadcast_to(np.cos(ang)[:, :, None], (FFT_R, FFT_C, LANES))
    tws = np.broadcast_to(np.sin(ang)[:, :, None], (FFT_R, FFT_C, LANES))
    mc = np.block([[c64, s64], [-s64, c64]])
    cg, sg = _dft_cs(GROUP_W)
    mch = np.concatenate([cg, sg], axis=0) / math.sqrt(SEQ * GROUP_W)
    wcx = np.concatenate([c256, s256], axis=0)
    mchc = np.concatenate([cg, -sg], axis=0)
    n_freq = HEAD_DIM // 4
    inv = ROPE_BASE ** (-np.arange(n_freq, dtype=np.float64) / n_freq)
    t = np.arange(SEQ)
    rows = (t // GRID_W).astype(np.float64)[:, None] * inv
    cols = (t % GRID_W).astype(np.float64)[:, None] * inv
    cos_h = np.concatenate([np.cos(rows), np.cos(rows), np.cos(cols), np.cos(cols)], axis=1)
    sin_h = np.concatenate([-np.sin(rows), np.sin(rows), -np.sin(cols), np.sin(cols)], axis=1)
    cos_t = np.concatenate([cos_h, cos_h], axis=1)
    sin_t = np.concatenate([sin_h, sin_h], axis=1)
    f = lambda a: np.ascontiguousarray(a, dtype=np.float32)
    return dict(wa=f(wa), twc=f(twc), tws=f(tws), mc=f(mc), mch=f(mch), wcx=f(wcx), mchc=f(mchc),
                cos=f(np.concatenate([cos_t, np.ones((TMW, LANES))], axis=0)),
                sin=f(np.concatenate([sin_t, np.zeros((TMW, LANES))], axis=0)))


def _mod_body(cv_ref, w_ref, b_ref, o_ref):
    c = cv_ref[...]
    s = c * _sigmoid(c)
    w = w_ref[0]
    outs = [jnp.sum(w * s[:, r:r + 1], axis=0, keepdims=True) for r in range(N_COND)]
    pad = jnp.zeros((SUBLANES - N_COND, MOD_PER_STEP * D_MODEL), F32)
    res = jnp.concatenate(outs + [pad], axis=0)
    for k in range(MOD_PER_STEP):
        o_ref[0, k] = res[:, k * D_MODEL:(k + 1) * D_MODEL] + b_ref[k]


MOD_PER_STEP = 2
N_COND = 3


def _modulation(cv, w_mod, b_mod):
    steps = N_MOD // MOD_PER_STEP
    out = pl.pallas_call(
        _mod_body,
        grid=(DEPTH, steps),
        in_specs=[pl.BlockSpec((D_MODEL, SUBLANES), lambda l, k: (0, 0)),
                  pl.BlockSpec((1, D_MODEL, MOD_PER_STEP * D_MODEL), lambda l, k: (l, 0, k)),
                  pl.BlockSpec((MOD_PER_STEP, 1, D_MODEL), lambda l, k: (l * steps + k, 0, 0))],
        out_specs=pl.BlockSpec((1, MOD_PER_STEP, SUBLANES, D_MODEL), lambda l, k: (l, k, 0, 0)),
        out_shape=jax.ShapeDtypeStruct((DEPTH, N_MOD, SUBLANES, D_MODEL), F32),
        compiler_params=_cparams(("arbitrary", "arbitrary")),
        name="modulation",
    )(cv, w_mod, b_mod.reshape(DEPTH * N_MOD, 1, D_MODEL))
    out = jnp.transpose(out[:, :, :3, :], (0, 2, 1, 3))
    return out.reshape(DEPTH * 3 * N_MOD, 1, D_MODEL)


def _mod_spec(layer, k, cond_fn):
    return pl.BlockSpec((1, 1, D_MODEL),
                        lambda i, *_: ((layer * 3 + cond_fn(i)) * N_MOD + k, 0, 0))


def _in_proj_body(x_ref, sh_ref, sc_ref, g_ref, w_ref, cos_ref, sin_ref,
                  u_ref, q_ref, kv_ref, gt_ref):
    lane = lax.broadcasted_iota(jnp.int32, (SUB, LANES), 1)
    first_half = (lane % (HEAD_DIM // 2)) < (HEAD_DIM // 4)
    quarter = HEAD_DIM // 4
    scale = HEAD_DIM ** -0.5 * LOG2E
    gc = 512
    for sb in range(TMW // SUB):
        rows = slice(sb * SUB, (sb + 1) * SUB)
        hb = _rms_mod(x_ref[rows, :], g_ref[...], sh_ref[0], sc_ref[0]).astype(BF16)
        u_ref[rows, :] = _dot(hb, w_ref[:, 0:OFF_Q]).astype(BF16)
        cos = cos_ref[rows, :]
        sin = sin_ref[rows, :]

        def rope(xs):
            below = pltpu.roll(xs, quarter, 1)
            above = pltpu.roll(xs, LANES - quarter, 1)
            return xs * cos + jnp.where(first_half, above, below) * sin

        qk = _dot(hb, w_ref[:, OFF_Q:OFF_V])
        low = lane < HEAD_DIM
        for j in range(ATTN_WIDTH // LANES):
            r = rope(qk[:, j * LANES:(j + 1) * LANES]) * scale
            swapped = pltpu.roll(r, HEAD_DIM, 1)
            if (2 * j) // GROUP == 0:
                even, odd = jnp.where(low, r, 0.0), jnp.where(low, swapped, 0.0)
            else:
                even, odd = jnp.where(low, 0.0, swapped), jnp.where(low, 0.0, r)
            q_ref[rows, (2 * j) * LANES:(2 * j + 1) * LANES] = even.astype(BF16)
            q_ref[rows, (2 * j + 1) * LANES:(2 * j + 2) * LANES] = odd.astype(BF16)
        kv_ref[rows, 0:LANES] = rope(qk[:, ATTN_WIDTH:ATTN_WIDTH + KV_WIDTH]).astype(BF16)
        v = _dot(hb, w_ref[:, OFF_V:OFF_G])
        kv_ref[rows, LANES:2 * LANES] = jnp.where(low, v, 1.0).astype(BF16)
        kv_ref[rows, 2 * LANES:3 * LANES] = jnp.where(low, pltpu.roll(v, HEAD_DIM, 1), 1.0).astype(BF16)
        for j in range(2 * D_MODEL // gc):
            z = _dot(hb, w_ref[:, OFF_G + j * gc:OFF_G + (j + 1) * gc])
            gt_ref[rows, j * gc:(j + 1) * gc] = _sigmoid(z).astype(BF16)


def _in_proj(x, x_extra, mods, layer, cond_fn, g, w_in, cos, sin, rope_idx):
    if x_extra is None:
        rows = x.shape[0]
        body, x_args = _in_proj_body, [x]
        x_specs = [pl.BlockSpec((TMW, D_MODEL), lambda i: (i, 0))]
    else:
        rows = x.shape[0] + x_extra.shape[0]
        body, x_args = _with_two_sources(_in_proj_body, 1, x.shape[0] // TMW), [x, x_extra]
        x_specs = _two_source_specs(x, x_extra, TMW, D_MODEL)
    return pl.pallas_call(
        body,
        grid=(rows // TMW,),
        in_specs=x_specs + [
                  _mod_spec(layer, 0, cond_fn),
                  _mod_spec(layer, 1, cond_fn),
                  pl.BlockSpec((1, D_MODEL), lambda i: (0, 0)),
                  _resident(w_in, layer),
                  pl.BlockSpec((TMW, LANES), lambda i: (rope_idx(i), 0)),
                  pl.BlockSpec((TMW, LANES), lambda i: (rope_idx(i), 0))],
        out_specs=[pl.BlockSpec((TMW, FOURIER_WIDTH), lambda i: (i, 0)),
                   pl.BlockSpec((TMW, Q_PAD_W), lambda i: (i, 0)),
                   pl.BlockSpec((TMW, KVX_W), lambda i: (i, 0)),
                   pl.BlockSpec((TMW, 2 * D_MODEL), lambda i: (i, 0))],
        out_shape=[jax.ShapeDtypeStruct((rows, FOURIER_WIDTH), BF16),
                   jax.ShapeDtypeStruct((rows, Q_PAD_W), BF16),
                   jax.ShapeDtypeStruct((rows, KVX_W), BF16),
                   jax.ShapeDtypeStruct((rows, 2 * D_MODEL), BF16)],
        compiler_params=_cparams(("arbitrary",), VMEM_LIMIT),
        name="in_proj",
    )(*x_args, mods, mods, g, w_in, cos, sin)


FFT_CB = 16


def _fft_a_body(x_ref, w_ref, o_ref):
    w = w_ref[...]
    xt = jnp.swapaxes(x_ref[...].astype(F32), 0, 1)
    res = jnp.stack([_dot(w, xt[c]) for c in range(FFT_CB)])
    o_ref[0] = jnp.swapaxes(res, 0, 1).astype(BF16)


def _fft_c_body(re_ref, im_ref, tc_ref, ts_ref, mc_ref, mch_ref, o_ref):
    mc = mc_ref[...]
    mch = mch_ref[...]
    twiddled = []
    for j in range(FFT_CB):
        ar = re_ref[0, j].astype(F32)
        ai = im_ref[0, j].astype(F32)
        tc = jnp.concatenate([tc_ref[j]] * N_GROUPS, axis=1)
        ts = jnp.concatenate([ts_ref[j]] * N_GROUPS, axis=1)
        twiddled.append(jnp.concatenate([ar * tc + ai * ts, ai * tc - ar * ts], axis=0))
    xs = [_dot(mc, b) for b in twiddled]
    xr = jnp.concatenate([x[:FFT_C] for x in xs], axis=0).astype(BF16)
    xi = jnp.concatenate([x[FFT_C:] for x in xs], axis=0).astype(BF16)
    ys = []
    for g in range(N_GROUPS):
        sl = slice(g * GROUP_W, (g + 1) * GROUP_W)
        ys.append(_dot(jnp.concatenate([xr[:, sl], xi[:, sl]], axis=1), mch))
    y = jnp.concatenate(ys, axis=1).reshape(FFT_CB, FFT_C, FOURIER_WIDTH)
    o_ref[0] = jnp.swapaxes(y, 0, 1).astype(BF16)


def _fourier_latent(u, tb, batch):
    u3 = u.reshape(u.shape[0] // FFT_C, FFT_C, FOURIER_WIDTH)
    a = pl.pallas_call(
        _fft_a_body,
        grid=(batch, FFT_C // FFT_CB),
        in_specs=[pl.BlockSpec((FFT_R, FFT_CB, FOURIER_WIDTH), lambda b, j: (b, j, 0)),
                  pl.BlockSpec((2 * FFT_R, FFT_R), lambda b, j: (0, 0))],
        out_specs=pl.BlockSpec((1, 2 * FFT_R, FFT_CB, FOURIER_WIDTH), lambda b, j: (b, 0, j, 0)),
        out_shape=jax.ShapeDtypeStruct((batch, 2 * FFT_R, FFT_C, FOURIER_WIDTH), BF16),
        compiler_params=_cparams(("arbitrary", "arbitrary"), VMEM_LIMIT),
        name="fft_rows",
    )(u3, tb["wa"].astype(BF16))
    nk = FFT_R // FFT_CB
    y = pl.pallas_call(
        _fft_c_body,
        grid=(batch, nk),
        in_specs=[pl.BlockSpec((1, FFT_CB, FFT_C, FOURIER_WIDTH), lambda b, k: (b, k, 0, 0)),
                  pl.BlockSpec((1, FFT_CB, FFT_C, FOURIER_WIDTH), lambda b, k: (b, nk + k, 0, 0)),
                  pl.BlockSpec((FFT_CB, FFT_C, LANES), lambda b, k: (k, 0, 0)),
                  pl.BlockSpec((FFT_CB, FFT_C, LANES), lambda b, k: (k, 0, 0)),
                  pl.BlockSpec((2 * FFT_C, 2 * FFT_C), lambda b, k: (0, 0)),
                  pl.BlockSpec((2 * GROUP_W, GROUP_W), lambda b, k: (0, 0))],
        out_specs=pl.BlockSpec((1, FFT_C, FFT_CB, FOURIER_WIDTH), lambda b, k: (b, 0, k, 0)),
        out_shape=jax.ShapeDtypeStruct((batch, FFT_C, FFT_R, FOURIER_WIDTH), BF16),
        compiler_params=_cparams(("arbitrary", "arbitrary"), VMEM_LIMIT),
        name="fft_cols",
    )(a, a, tb["twc"], tb["tws"], tb["mc"].astype(BF16), tb["mch"].astype(BF16))
    return y.reshape(batch * SEQ, FOURIER_WIDTH)


def _fourier_ctx_body(u_ref, w_ref, m_ref, o_ref):
    pq = _dot(w_ref[...], u_ref[...])
    p = pq[:CTX_LEN]
    q = pq[CTX_LEN:]
    m = m_ref[...]
    scale = 1.0 / math.sqrt(CTX_LEN * GROUP_W)
    for g in range(N_GROUPS):
        sl = slice(g * GROUP_W, (g + 1) * GROUP_W)
        lhs = jnp.concatenate([p[:, sl], q[:, sl]], axis=1).astype(BF16)
        o_ref[:, sl] = _dot(lhs, m) * scale


def _fourier_ctx(u, tb, batch):
    first = batch * SEQ // CTX_LEN
    return pl.pallas_call(
        _fourier_ctx_body,
        grid=(batch,),
        in_specs=[pl.BlockSpec((CTX_LEN, FOURIER_WIDTH), lambda b: (first + b, 0)),
                  pl.BlockSpec((2 * CTX_LEN, CTX_LEN), lambda b: (0, 0)),
                  pl.BlockSpec((2 * GROUP_W, GROUP_W), lambda b: (0, 0))],
        out_specs=pl.BlockSpec((CTX_LEN, FOURIER_WIDTH), lambda b: (b, 0)),
        out_shape=jax.ShapeDtypeStruct((batch * CTX_LEN, FOURIER_WIDTH), F32),
        compiler_params=_cparams(("arbitrary",)),
        name="fft_ctx",
    )(u, tb["wcx"].astype(BF16), tb["mchc"].astype(BF16))


def _attn_body(sink_ref, q_ref, *refs, local, nq):
    o_ref = refs[-1]
    rows = 2 * QB
    if local:
        blocks = [r[...] for r in refs[:nq + 2]]
        ctx_rows = refs[nq + 2][...]
        step = pl.program_id(1)
        last_step = pl.num_programs(1) - 1
        qi = lax.broadcasted_iota(jnp.int32, (rows, 2 * QB), 0) % QB
        kj = lax.broadcasted_iota(jnp.int32, (rows, 2 * QB), 1)
        groups, masks = [], []
        for t in range(nq):
            no_prev = jnp.where(step == 0, 2 * QB, 0) if t == 0 else 0
            no_next = jnp.where(step == last_step, 2 * QB, 0) if t == nq - 1 else 0
            prev_ok = (kj < QB) & (kj >= qi + no_prev)
            next_ok = (kj >= QB) & (kj - QB <= qi - no_next)
            groups.append([jnp.concatenate([blocks[t], blocks[t + 2]], axis=0),
                           jnp.concatenate([blocks[t + 1], ctx_rows], axis=0)])
            masks.append([prev_ok | next_ok, None])
    else:
        groups = [[refs[0][...]]]
        masks = [[None]]
    lo_q = lax.broadcasted_iota(jnp.int32, (QB, LANES), 1) < HEAD_DIM
    first_head = lax.broadcasted_iota(jnp.int32, (rows, 1), 0) < QB
    chains = [(t, slab) for t in range(nq) for slab in range(ATTN_WIDTH // LANES)]
    scores = []
    for t, slab in chains:
        qrows = slice(t * QB, (t + 1) * QB)
        qs = jnp.concatenate([q_ref[qrows, (2 * slab) * LANES:(2 * slab + 1) * LANES],
                              q_ref[qrows, (2 * slab + 1) * LANES:(2 * slab + 2) * LANES]], axis=0)
        parts = []
        for kv, mask in zip(groups[t], masks[t]):
            s = lax.dot_general(qs, kv[:, :LANES], (((1,), (1,)), ((), ())), preferred_element_type=F32)
            parts.append(s if mask is None else jnp.where(mask, s, NEG))
        scores.append(parts)
    sinks = [jnp.where(first_head, sink_ref[2 * slab], sink_ref[2 * slab + 1]) * LOG2E for _, slab in chains]
    maxes = []
    for parts, sk in zip(scores, sinks):
        m = sk
        for s in parts:
            m = jnp.maximum(m, jnp.max(s, axis=1, keepdims=True))
        maxes.append(m)
    sink_p = [jnp.exp2(sk - m) for sk, m in zip(sinks, maxes)]
    pvs = []
    for (t, slab), parts, m in zip(chains, scores, maxes):
        hk = (2 * slab) // GROUP
        pv = None
        for s, kv in zip(parts, groups[t]):
            term = _dot(jnp.exp2(s - m).astype(BF16), kv[:, (1 + hk) * LANES:(2 + hk) * LANES])
            pv = term if pv is None else pv + term
        pvs.append(pv)
    for (t, slab), pv, sp in zip(chains, pvs, sink_p):
        ra, rb = pv[:QB], pv[QB:]
        num = jnp.where(lo_q, ra, pltpu.roll(rb, HEAD_DIM, 1))
        den = jnp.where(lo_q, pltpu.roll(ra, HEAD_DIM, 1), rb) + jnp.where(lo_q, sp[:QB], sp[QB:])
        o_ref[t * QB:(t + 1) * QB, slab * LANES:(slab + 1) * LANES] = (num / den).astype(BF16)


def _attention_latent(q, kv, sink, batch):
    nb = SEQ // QB
    steps = nb // ATT_NQ
    first_ctx = batch * SEQ // CTX_LEN

    def key_block(j):
        return pl.BlockSpec((QB, KVX_W),
                            lambda b, n: (b * nb + jnp.clip(n * ATT_NQ - 1 + j, 0, nb - 1), 0))

    return pl.pallas_call(
        functools.partial(_attn_body, local=True, nq=ATT_NQ),
        grid=(batch, steps),
        in_specs=[pl.BlockSpec(memory_space=pltpu.SMEM),
                  pl.BlockSpec((ATT_NQ * QB, Q_PAD_W), lambda b, n: (b * steps + n, 0))]
                 + [key_block(j) for j in range(ATT_NQ + 2)]
                 + [pl.BlockSpec((CTX_LEN, KVX_W), lambda b, n: (first_ctx + b, 0))],
        out_specs=pl.BlockSpec((ATT_NQ * QB, ATTN_WIDTH), lambda b, n: (b * steps + n, 0)),
        out_shape=jax.ShapeDtypeStruct((batch * SEQ, ATTN_WIDTH), BF16),
        compiler_params=_cparams(("arbitrary", "arbitrary")),
        name="attn_latent",
    )(sink, q, *([kv] * (ATT_NQ + 3)))


def _attention_ctx(q, kv, sink, batch):
    nb = CTX_LEN // QB
    first_q = batch * SEQ // QB
    first_ctx = batch * SEQ // CTX_LEN
    return pl.pallas_call(
        functools.partial(_attn_body, local=False, nq=1),
        grid=(batch, nb),
        in_specs=[pl.BlockSpec(memory_space=pltpu.SMEM),
                  pl.BlockSpec((QB, Q_PAD_W), lambda b, n: (first_q + b * nb + n, 0)),
                  pl.BlockSpec((CTX_LEN, KVX_W), lambda b, n: (first_ctx + b, 0))],
        out_specs=pl.BlockSpec((QB, ATTN_WIDTH), lambda b, n: (b * nb + n, 0)),
        out_shape=jax.ShapeDtypeStruct((batch * CTX_LEN, ATTN_WIDTH), BF16),
        compiler_params=_cparams(("arbitrary", "arbitrary")),
        name="attn_ctx",
    )(sink, q, kv)


def _merge_body(x_ref, fm_ref, ao_ref, gt_ref, wf_ref, wa_ref, wo_ref, ga_ref, *refs, with_h2):
    if with_h2:
        sh_ref, sc_ref, g_ref, xo_ref, h_ref = refs
    else:
        (xo_ref,) = refs
    for sb in range(TMW // SUB):
        rows = slice(sb * SUB, (sb + 1) * SUB)
        gt = gt_ref[rows, :].astype(F32)
        y = (gt[:, :D_MODEL] * _dot(fm_ref[rows, :], wf_ref[...])
             + gt[:, D_MODEL:] * _dot(ao_ref[rows, :], wa_ref[...]))
        xn = x_ref[rows, :] + ga_ref[0] * _dot(y, wo_ref[...])
        if with_h2:
            h_ref[rows, :] = _rms_mod(xn, g_ref[...], sh_ref[0], sc_ref[0]).astype(BF16)
        xo_ref[rows, :] = xn


def _merge(rows, x, fm, ao, extras, gates, wf, wa, wo, mods, layer, cond_fn, g2, with_h2):
    row_spec = lambda w: pl.BlockSpec((TMW, w), lambda i: (i, 0))
    widths = (D_MODEL, FOURIER_WIDTH, ATTN_WIDTH)
    body = functools.partial(_merge_body, with_h2=with_h2)
    if extras is None:
        in_specs = [row_spec(w) for w in widths]
        args = [x, fm, ao]
    else:
        main_rows = rows
        rows = rows + extras[0].shape[0]
        body = _with_two_sources(body, len(widths), main_rows // TMW)
        in_specs, args = [], []
        for main, extra, w in zip((x, fm, ao), extras, widths):
            assert main.shape[0] >= main_rows and extra.shape[0] == extras[0].shape[0]
            in_specs += [pl.BlockSpec((TMW, w), lambda i: (jnp.minimum(i, main_rows // TMW - 1), 0)),
                         pl.BlockSpec((TMW, w), lambda i: (jnp.maximum(i - main_rows // TMW, 0), 0))]
            args += [main, extra]
    in_specs += [row_spec(2 * D_MODEL),
                 _resident(wf, layer), _resident(wa, layer), _resident(wo, layer), _mod_spec(layer, 2, cond_fn)]
    args += [gates, wf, wa, wo, mods]
    out_specs = [row_spec(D_MODEL)]
    out_shape = [jax.ShapeDtypeStruct((rows, D_MODEL), F32)]
    if with_h2:
        in_specs += [_mod_spec(layer, 3, cond_fn), _mod_spec(layer, 4, cond_fn),
                     pl.BlockSpec((1, D_MODEL), lambda i: (0, 0))]
        args += [mods, mods, g2]
        out_specs.append(row_spec(D_MODEL))
        out_shape.append(jax.ShapeDtypeStruct((rows, D_MODEL), BF16))
    return pl.pallas_call(
        body,
        grid=(rows // TMW,),
        in_specs=in_specs, out_specs=out_specs, out_shape=out_shape,
        compiler_params=_cparams(("arbitrary",), VMEM_LIMIT),
        name="merge",
    )(*args)


def _ffn_body(x_ref, h_ref, wg_ref, wu_ref, wd_ref, ga_ref, o_ref):
    d_ff = wg_ref.shape[1]
    mxu_tiles = d_ff // MXU_W
    edges = [MXU_W * (mxu_tiles * c // FFN_CHUNKS) for c in range(FFN_CHUNKS)] + [d_ff]
    blocks = [slice(k * SUB, (k + 1) * SUB) for k in range(TM // SUB)]
    hs = [h_ref[rows, :] for rows in blocks]
    downs = [None] * len(blocks)
    for c in range(FFN_CHUNKS):
        cols = slice(edges[c], edges[c + 1])
        acts = []
        for h in hs:
            g = _dot(h, wg_ref[:, cols])
            u = _dot(h, wu_ref[:, cols])
            acts.append(g * _sigmoid(g) * u)
        for k, a in enumerate(acts):
            d = _dot(a, wd_ref[cols, :])
            downs[k] = d if downs[k] is None else downs[k] + d
    for rows, d in zip(blocks, downs):
        o_ref[rows, :] = x_ref[rows, :] + ga_ref[0] * d


def _ffn_dense(x, h2, wg, wu, wd, mods, layer, cond_fn):
    rows = x.shape[0]
    return pl.pallas_call(
        _ffn_body,
        grid=(rows // TM,),
        in_specs=[pl.BlockSpec((TM, D_MODEL), lambda i: (i, 0)),
                  pl.BlockSpec((TM, D_MODEL), lambda i: (i, 0)),
                  _resident(wg, layer // 2), _resident(wu, layer // 2), _resident(wd, layer // 2),
                  _mod_spec(layer, 5, cond_fn)],
        out_specs=pl.BlockSpec((TM, D_MODEL), lambda i: (i, 0)),
        out_shape=jax.ShapeDtypeStruct((rows, D_MODEL), F32),
        compiler_params=_cparams(("arbitrary",), VMEM_LIMIT),
        name="ffn_dense",
    )(x, h2, wg, wu, wd, mods)


INFO_E0, INFO_E1, INFO_R0, INFO_R1, INFO_W0, INFO_W1 = range(6)


def _router_body(x_ref, sh_ref, sc_ref, g_ref, wr_ref, hb_ref, info_ref, cnt_ref):
    tiles = [slice(t * TD, (t + 1) * TD) for t in range(ROUTER_TILES)]
    lane = lax.broadcasted_iota(jnp.int32, (TD, LANES), 1)
    neg_inf = jnp.float32(-jnp.inf)
    row = lax.broadcasted_iota(jnp.int32, (TD, TD), 0)
    col = lax.broadcasted_iota(jnp.int32, (TD, TD), 1)
    tri = jnp.where(row > col, 1.0, 0.0).astype(BF16)
    hs = [_rms_mod(x_ref[rows, :], g_ref[...], sh_ref[0], sc_ref[0]) for rows in tiles]
    for rows, h in zip(tiles, hs):
        hb_ref[rows, :] = h.astype(BF16)
    lgs = [jnp.where(lane < N_EXPERTS,
                     jnp.dot(h, wr_ref[...], precision=lax.Precision.HIGHEST, preferred_element_type=F32), neg_inf)
           for h in hs]
    v0s = [jnp.max(lg, axis=1, keepdims=True) for lg in lgs]
    i0s = [jnp.min(jnp.where(lg == v0, lane, LANES), axis=1, keepdims=True) for lg, v0 in zip(lgs, v0s)]
    oh0s = [lane == i0 for i0 in i0s]
    lg1s = [jnp.where(oh0, neg_inf, lg) for oh0, lg in zip(oh0s, lgs)]
    v1s = [jnp.max(lg1, axis=1, keepdims=True) for lg1 in lg1s]
    i1s = [jnp.min(jnp.where(lg1 == v1, lane, LANES), axis=1, keepdims=True) for lg1, v1 in zip(lg1s, v1s)]
    oh1s = [lane == i1 for i1 in i1s]
    ohs = [jnp.where(oh0 | oh1, 1.0, 0.0) for oh0, oh1 in zip(oh0s, oh1s)]
    befores = [_dot(tri, oh.astype(BF16)) for oh in ohs]
    for t, rows in enumerate(tiles):
        e = jnp.exp(v1s[t] - v0s[t])
        w0 = 1.0 / (1.0 + e)
        w1 = e / (1.0 + e)
        r0 = jnp.sum(jnp.where(oh0s[t], befores[t], 0.0), axis=1, keepdims=True)
        r1 = jnp.sum(jnp.where(oh1s[t], befores[t], 0.0), axis=1, keepdims=True)
        cnt_ref[t] = jnp.broadcast_to(jnp.sum(ohs[t], axis=0, keepdims=True), (SUBLANES, LANES))
        info = jnp.zeros((TD, LANES), F32)
        for idx, val in ((INFO_E0, i0s[t].astype(F32)), (INFO_E1, i1s[t].astype(F32)), (INFO_R0, r0),
                         (INFO_R1, r1), (INFO_W0, w0), (INFO_W1, w1)):
            info = jnp.where(lane == idx, val, info)
        info_ref[rows, :] = info


def _router(x, mods, layer, cond_fn, g2, wr_pad):
    rows = x.shape[0]
    step_rows = ROUTER_TILES * TD
    step_cond = lambda i: cond_fn(i * ROUTER_TILES)
    return pl.pallas_call(
        _router_body,
        grid=(rows // step_rows,),
        in_specs=[pl.BlockSpec((step_rows, D_MODEL), lambda i: (i, 0)),
                  _mod_spec(layer, 3, step_cond), _mod_spec(layer, 4, step_cond),
                  pl.BlockSpec((1, D_MODEL), lambda i: (0, 0)),
                  pl.BlockSpec((D_MODEL, LANES), lambda i: (0, 0))],
        out_specs=[pl.BlockSpec((step_rows, D_MODEL), lambda i: (i, 0)),
                   pl.BlockSpec((step_rows, LANES), lambda i: (i, 0)),
                   pl.BlockSpec((ROUTER_TILES, SUBLANES, LANES), lambda i: (i, 0, 0))],
        out_shape=[jax.ShapeDtypeStruct((rows, D_MODEL), BF16),
                   jax.ShapeDtypeStruct((rows, LANES), F32),
                   jax.ShapeDtypeStruct((rows // TD, SUBLANES, LANES), F32)],
        compiler_params=_cparams(("arbitrary",), VMEM_LIMIT),
        name="router",
    )(x, mods, mods, g2, wr_pad)


def _segment_copies(src_ref, src_off, dst_ref, dst_off, len8, sem, bits, wait):
    for k in reversed(range(bits)):
        size = SUBLANES << k
        done = ((len8 >> (k + 1)) << (k + 1)) * SUBLANES

        @pl.when(((len8 >> k) & 1) == 1)
        def _(size=size, done=done):
            cp = pltpu.make_async_copy(
                src_ref.at[pl.ds(pl.multiple_of(src_off + done, SUBLANES), size)],
                dst_ref.at[pl.ds(pl.multiple_of(dst_off + done, SUBLANES), size)], sem)
            if wait:
                cp.wait()
            else:
                cp.start()


def _slot_positions(e0, e1, r0, r1, seg_ref, base):
    pos0, pos1 = r0, r1
    for e in range(N_EXPERTS):
        start = seg_ref[base + e].astype(F32)
        pos0 = pos0 + jnp.where(e0 == e, start, 0.0)
        pos1 = pos1 + jnp.where(e1 == e, start, 0.0)
    return pos0.astype(jnp.int32), pos1.astype(jnp.int32)


def _dispatch_body(seg_ref, dst_ref, len_ref, tdst_ref, tlen_ref, nt_ref, hb_ref, info_ref, xs_ref,
                   buf_ref, zero_ref, sems, zero_sem, *, min_tiles):
    i = pl.program_id(0)
    last = pl.num_programs(0) - 1
    base = i * N_EXPERTS
    cur = i % 2

    def scatter(tile, slot, wait):
        for e in range(N_EXPERTS):
            k = tile * N_EXPERTS + e
            _segment_copies(buf_ref.at[slot], seg_ref[k], xs_ref, dst_ref[k], len_ref[k],
                            sems.at[slot], SEG_BITS, wait=wait)

    info_t = info_ref[...].T
    row = lambda k: info_t[k:k + 1, :]
    pos0, pos1 = _slot_positions(row(INFO_E0), row(INFO_E1), row(INFO_R0), row(INFO_R1), seg_ref, base)
    slot = lax.broadcasted_iota(jnp.int32, (SB, TD), 0)
    p0 = slot == pos0
    p1 = slot == pos1
    perm = jnp.where(p0 | p1, 1.0, 0.0).astype(BF16)
    buf_ref[cur, :, :D_MODEL] = _dot(perm, hb_ref[...])
    wsel = jnp.where(p0, row(INFO_W0), 0.0) + jnp.where(p1, row(INFO_W1), 0.0)
    buf_ref[cur, :, D_MODEL:] = jnp.broadcast_to(jnp.sum(wsel, axis=1, keepdims=True), (SB, LANES))
    scatter(i, cur, wait=False)

    @pl.when(i > 0)
    def _():
        scatter(i - 1, 1 - cur, wait=True)

    @pl.when(i == last)
    def _():
        zero_ref[...] = jnp.zeros_like(zero_ref)
        spare_tiles = range(min_tiles, xs_ref.shape[0] // TMP)
        spare = [pltpu.make_async_copy(zero_ref, xs_ref.at[pl.ds(j * TMP, TMP)], zero_sem) for j in spare_tiles]
        for wait in (False, True):
            for e in range(N_EXPERTS):
                _segment_copies(zero_ref, 0, xs_ref, tdst_ref[e], tlen_ref[e], zero_sem, TAIL_BITS, wait=wait)
            for j, cp in zip(spare_tiles, spare):
                @pl.when(j >= nt_ref[0])
                def _(cp=cp, wait=wait):
                    if wait:
                        cp.wait()
                    else:
                        cp.start()
        scatter(i, cur, wait=True)


def _dispatch(seg, dst, len8, tail_dst, tail_len8, n_tiles, hb, info, m_rows):
    rows = hb.shape[0]
    return pl.pallas_call(
        functools.partial(_dispatch_body, min_tiles=2 * rows // TMP),
        grid_spec=pltpu.PrefetchScalarGridSpec(
            num_scalar_prefetch=6,
            grid=(rows // TD,),
            in_specs=[pl.BlockSpec((TD, D_MODEL), lambda i, *_: (i, 0)),
                      pl.BlockSpec((TD, LANES), lambda i, *_: (i, 0))],
            out_specs=pl.BlockSpec(memory_space=pl.ANY),
            scratch_shapes=[pltpu.VMEM((2, SB, XS_W), F32), pltpu.VMEM((TMP, XS_W), F32),
                            pltpu.SemaphoreType.DMA((2,)), pltpu.SemaphoreType.DMA(())]),
        out_shape=jax.ShapeDtypeStruct((m_rows, XS_W), F32),
        compiler_params=_cparams(("arbitrary",), VMEM_LIMIT),
        name="dispatch",
    )(seg, dst, len8, tail_dst, tail_len8, n_tiles, hb, info)


def _expert_body(te_ref, nh_ref, nt_ref, xs_ref, wg_ref, wu_ref, wd_ref, ys_ref, acc_ref):
    j = pl.program_id(0)
    f = pl.program_id(1)
    nf = pl.num_programs(1)
    halves = nh_ref[j]

    blocks = [slice(h * TME, (h + 1) * TME) for h in range(HALVES)]

    @pl.when((halves > 0) & (f == 0))
    def _():
        acc_ref[...] = jnp.zeros_like(acc_ref)

    for live in range(1, HALVES + 1):
        @pl.when(halves == live)
        def _(live=live):
            ups = [(_dot(xs_ref[rows, :D_MODEL], wg_ref[0]), _dot(xs_ref[rows, :D_MODEL], wu_ref[0]))
                   for rows in blocks[:live]]
            downs = [_dot(g * _sigmoid(g) * u, wd_ref[0]) for g, u in ups]
            for rows, d in zip(blocks[:live], downs):
                acc_ref[rows, :] += d

    @pl.when(f == nf - 1)
    def _():
        live_rows = lax.broadcasted_iota(jnp.int32, (TMP, 1), 0) < halves * TME
        ys_ref[...] = jnp.where(live_rows, acc_ref[...] * xs_ref[:, D_MODEL:D_MODEL + 1], 0.0)


def _experts(tile_expert, tile_halves, n_tiles, xs, wg, wu, wd):
    m_rows = xs.shape[0]
    nf = wg.shape[2] // FC_E

    def f_idx(j, f, nt):
        last_live = nt[0] - 1
        walk = lambda t, step: jnp.where(t % 2 == 0, step, nf - 1 - step)
        return jnp.where(j < nt[0], walk(j, f), walk(last_live, nf - 1))

    def j_idx(j, nt):
        return jnp.minimum(j, nt[0] - 1)

    return pl.pallas_call(
        _expert_body,
        grid_spec=pltpu.PrefetchScalarGridSpec(
            num_scalar_prefetch=3,
            grid=(m_rows // TMP, nf),
            in_specs=[pl.BlockSpec((TMP, XS_W), lambda j, f, te, nh, nt: (j_idx(j, nt), 0)),
                      pl.BlockSpec((1, D_MODEL, FC_E), lambda j, f, te, nh, nt: (te[j], 0, f_idx(j, f, nt))),
                      pl.BlockSpec((1, D_MODEL, FC_E), lambda j, f, te, nh, nt: (te[j], 0, f_idx(j, f, nt))),
                      pl.BlockSpec((1, FC_E, D_MODEL), lambda j, f, te, nh, nt: (te[j], f_idx(j, f, nt), 0))],
            out_specs=pl.BlockSpec((TMP, D_MODEL), lambda j, f, te, nh, nt: (j, 0)),
            scratch_shapes=[pltpu.VMEM((TMP, D_MODEL), F32)]),
        out_shape=jax.ShapeDtypeStruct((m_rows, D_MODEL), F32),
        compiler_params=_cparams(("arbitrary", "arbitrary"), VMEM_LIMIT),
        name="experts",
    )(tile_expert, tile_halves, n_tiles, xs, wg, wu, wd)


def _combine_body(seg_ref, dst_ref, len_ref, ys_ref, x_ref, info_ref, ga_ref, g_ref, o_ref, buf_ref, sems):
    i = pl.program_id(0)
    base = i * N_EXPERTS
    cur = i % 2

    def gather(tile, slot, wait):
        for e in range(N_EXPERTS):
            k = tile * N_EXPERTS + e
            _segment_copies(ys_ref, dst_ref[k], buf_ref.at[slot], seg_ref[k], len_ref[k],
                            sems.at[slot], SEG_BITS, wait=wait)

    @pl.when(i == 0)
    def _():
        buf_ref[...] = jnp.zeros_like(buf_ref)
        gather(0, 0, wait=False)

    @pl.when(i + 1 < pl.num_programs(0))
    def _():
        gather(i + 1, 1 - cur, wait=False)

    info = info_ref[...]
    col = lambda k: info[:, k:k + 1]
    pos0, pos1 = _slot_positions(col(INFO_E0), col(INFO_E1), col(INFO_R0), col(INFO_R1), seg_ref, base)
    slot = lax.broadcasted_iota(jnp.int32, (TD, SB), 1)
    unperm = jnp.where((slot == pos0) | (slot == pos1), 1.0, 0.0).astype(BF16)
    gather(i, cur, wait=True)
    y = _dot(unperm, buf_ref[cur])
    xn = x_ref[...] + ga_ref[0] * y
    r = lax.rsqrt(jnp.mean(xn * xn, axis=-1, keepdims=True) + EPS)
    o_ref[...] = (xn * r) * g_ref[...]


def _combine(seg, dst, len8, ys, x, info, mods, layer, cond_fn, final_g):
    rows = x.shape[0]
    return pl.pallas_call(
        _combine_body,
        grid_spec=pltpu.PrefetchScalarGridSpec(
            num_scalar_prefetch=3,
            grid=(rows // TD,),
            in_specs=[pl.BlockSpec(memory_space=pl.ANY),
                      pl.BlockSpec((TD, D_MODEL), lambda i, *_: (i, 0)),
                      pl.BlockSpec((TD, LANES), lambda i, *_: (i, 0)),
                      _mod_spec(layer, 5, cond_fn),
                      pl.BlockSpec((1, D_MODEL), lambda i, *_: (0, 0))],
            out_specs=pl.BlockSpec((TD, D_MODEL), lambda i, *_: (i, 0)),
            scratch_shapes=[pltpu.VMEM((2, SB, D_MODEL), F32), pltpu.SemaphoreType.DMA((2,))]),
        out_shape=jax.ShapeDtypeStruct((rows, D_MODEL), F32),
        compiler_params=_cparams(("arbitrary",), VMEM_LIMIT),
        name="combine",
    )(seg, dst, len8, ys, x, info, mods, final_g)


def _moe(x, mods, layer, cond_fn, g2, w_router, wg, wu, wd, final_g):
    rows = x.shape[0]
    n_tok_tiles = rows // TD
    wr_pad = jnp.pad(w_router, ((0, 0), (0, LANES - N_EXPERTS)))
    hb, info, cnt = _router(x, mods, layer, cond_fn, g2, wr_pad)
    counts = cnt[:, 0, :N_EXPERTS].astype(jnp.int32)
    seg_len = (counts + SUBLANES - 1) // SUBLANES * SUBLANES
    seg_start = jnp.cumsum(seg_len, axis=1) - seg_len
    group = jnp.sum(seg_len, axis=0)
    group_halves = (group + TME - 1) // TME
    group_tiles = (group + TMP - 1) // TMP
    tile_start = jnp.cumsum(group_tiles) - group_tiles
    group_base = tile_start * TMP
    dst = group_base[None, :] + jnp.cumsum(seg_len, axis=0) - seg_len
    tail_dst = group_base + group
    tail_len8 = (group_tiles * TMP - group) // SUBLANES
    max_tiles = (2 * rows + n_tok_tiles * N_EXPERTS * (SUBLANES - 1) + TMP - 1) // TMP + N_EXPERTS
    n_tiles = jnp.sum(group_tiles).astype(jnp.int32).reshape(1)
    tiles = jnp.arange(max_tiles, dtype=jnp.int32)
    tid = jnp.minimum(tiles, n_tiles[0] - 1)
    tile_expert = (jnp.sum(tid[:, None] >= tile_start[None, :], axis=1) - 1).astype(jnp.int32)
    own = (tile_expert[:, None] == jnp.arange(N_EXPERTS, dtype=jnp.int32)[None, :]).astype(jnp.int32)
    in_group = tid - jnp.sum(own * tile_start[None, :], axis=1)
    tile_halves = jnp.clip(jnp.sum(own * group_halves[None, :], axis=1) - HALVES * in_group, 0, HALVES)
    tile_halves = jnp.where(tiles < n_tiles[0], tile_halves, 0).astype(jnp.int32)
    flat = lambda a: a.reshape(-1).astype(jnp.int32)
    seg, dst, len8 = flat(seg_start), flat(dst), flat(seg_len // SUBLANES)
    xs = _dispatch(seg, dst, len8, flat(tail_dst), flat(tail_len8), n_tiles, hb, info, max_tiles * TMP)
    ys = _experts(tile_expert, tile_halves, n_tiles, xs, wg, wu, wd)
    return _combine(seg, dst, len8, ys, x, info, mods, layer, cond_fn, final_g)


def kernel(x, c, ctx, c_ctx, w_mod, b_mod, norm1_g, norm2_g, w_in, sink, w_fourier, w_attn, w_out,
           w_gate_d, w_up_d, w_down_d, w_router, w_gate_e, w_up_e, w_down_e, final_g):
    batch, seq, d = x.shape
    assert (seq, d) == (SEQ, D_MODEL) and ctx.shape == (batch, CTX_LEN, D_MODEL)
    tb = {k: jnp.asarray(v) for k, v in _tables().items()}
    n_lat = batch * SEQ
    assert (batch * CTX_LEN) % TMW == 0 and n_lat % TMW == 0

    def cond(tile):
        return lambda i: jnp.where(i < n_lat // tile, i // (SEQ // tile), batch)

    def rope_idx(i):
        return jnp.where(i < n_lat // TMW, i % (SEQ // TMW), SEQ // TMW + i - n_lat // TMW)

    assert batch + 1 == N_COND
    cv = jnp.zeros((D_MODEL, SUBLANES), F32).at[:, :batch].set(c.T).at[:, batch].set(c_ctx)
    mods = _modulation(cv, w_mod, b_mod)

    xs_main, xs_extra = x.reshape(n_lat, D_MODEL), ctx.reshape(batch * CTX_LEN, D_MODEL)
    out = None
    for l in range(DEPTH):
        last = l == DEPTH - 1
        g1 = norm1_g[l].reshape(1, D_MODEL)
        g2 = norm2_g[l].reshape(1, D_MODEL)
        wf, wa, wo = w_fourier, w_attn, w_out
        u, q, kv, gates = _in_proj(xs_main, xs_extra, mods, l, cond(TMW), g1, w_in, tb["cos"], tb["sin"], rope_idx)
        fm = _fourier_latent(u, tb, batch)
        ao = _attention_latent(q, kv, sink[l], batch)
        if l % 2 == 0:
            if last:
                raise NotImplementedError("final norm is fused into the routed-expert combine")
            extras = (xs_extra, _fourier_ctx(u, tb, batch), _attention_ctx(q, kv, sink[l], batch))
            xa, h2 = _merge(n_lat, xs_main, fm, ao, extras, gates, wf, wa, wo, mods, l, cond(TMW), g2, True)
            xs_main = _ffn_dense(xa, h2, w_gate_d, w_up_d, w_down_d, mods, l, cond(TM))
            xs_extra = None
        else:
            if not last:
                raise NotImplementedError("context update through a routed-expert layer")
            i = l // 2
            (xl,) = _merge(n_lat, xs_main, fm, ao, None, gates, wf, wa, wo, mods, l, cond(TMW), g2, False)
            out = _moe(xl, mods, l, cond(TD), g2, w_router[i], w_gate_e[i], w_up_e[i], w_down_e[i],
                       final_g.reshape(1, D_MODEL))
    return out.reshape(batch, SEQ, D_MODEL)
```

```python
import functools
import math

import numpy as np
import jax
import jax.numpy as jnp
from jax import lax
from jax.experimental import pallas as pl
from jax.experimental.pallas import tpu as pltpu

F32 = jnp.float32
BF16 = jnp.bfloat16

D_MODEL = 1024
SEQ = 8192
DEPTH = 2
GRID_W = 64
CTX_LEN = 256
N_GROUPS = 4
GROUP_W = 128
FOURIER_WIDTH = N_GROUPS * GROUP_W
N_Q_HEADS = 8
N_KV_HEADS = 2
HEAD_DIM = 64
ATTN_WIDTH = N_Q_HEADS * HEAD_DIM
KV_WIDTH = N_KV_HEADS * HEAD_DIM
GROUP = N_Q_HEADS // N_KV_HEADS
WINDOW = 128
ROPE_BASE = 10000.0
OFF_Q = FOURIER_WIDTH
OFF_K = OFF_Q + ATTN_WIDTH
OFF_V = OFF_K + KV_WIDTH
OFF_G = OFF_V + KV_WIDTH
IN_WIDTH = OFF_G + 2 * D_MODEL
N_EXPERTS = 8
N_MOD = 6
EPS = 1e-6
NEG = -1e30
LOG2E = 1.0 / math.log(2.0)

LANES = 128
SUBLANES = 8
MXU_W = 256

TM = 512
FFN_CHUNKS = 2
TMW = 512
SUB = 256
QB = 128
ATT_NQ = 8
Q_PAD_W = N_Q_HEADS * LANES
KVX_W = (1 + N_KV_HEADS) * LANES
FFT_R = SEQ // GRID_W
FFT_C = GRID_W
TME = 256
TILE_BLOCKS = 4
TMP = TILE_BLOCKS * TME
TD = 512
ROUTER_TILES = 4
SB = 2 * TD + N_EXPERTS * SUBLANES
XS_W = D_MODEL + LANES
SEG_BITS = TD.bit_length() - 3
TAIL_BITS = TMP.bit_length() - 4
FC_E = 512
VMEM_LIMIT = 56 * 1024 * 1024


def _cparams(sem, vmem=None):
    return pltpu.CompilerParams(dimension_semantics=sem, vmem_limit_bytes=vmem)


def _dot(a, b):
    return lax.dot_general(a, b, (((1,), (0,)), ((), ())), preferred_element_type=F32)


def _resident(a, layer):
    return pl.BlockSpec((None,) + a.shape[1:], lambda i, *_: (layer, 0, 0), pipeline_mode=pl.Buffered(1))


class _TwoSources:
    def __init__(self, main_ref, extra_ref, use_extra):
        self.main_ref, self.extra_ref, self.use_extra = main_ref, extra_ref, use_extra

    def __getitem__(self, idx):
        return jnp.where(self.use_extra, self.extra_ref[idx], self.main_ref[idx])


def _two_source_specs(main, extra, tile, width):
    n_main = main.shape[0] // tile
    return [pl.BlockSpec((tile, width), lambda i: (jnp.minimum(i, n_main - 1), 0)),
            pl.BlockSpec((tile, width), lambda i: (jnp.maximum(i - n_main, 0), 0))]


def _with_two_sources(body, n_pairs, n_main_tiles):
    def kernel_fn(*refs):
        use_extra = pl.program_id(0) >= n_main_tiles
        merged = [_TwoSources(refs[2 * k], refs[2 * k + 1], use_extra) for k in range(n_pairs)]
        return body(*merged, *refs[2 * n_pairs:])
    return kernel_fn


def _sigmoid(x):
    return 1.0 / (1.0 + jnp.exp(-x))


def _rms_mod(x, g, sh, sc):
    r = lax.rsqrt(jnp.mean(x * x, axis=-1, keepdims=True) + EPS)
    return (x * r) * g * (1.0 + sc) + sh


def _dft_cs(n):
    k = np.arange(n, dtype=np.float64)
    a = 2.0 * np.pi * np.outer(k, k) / n
    return np.cos(a), np.sin(a)


@functools.lru_cache(maxsize=None)
def _tables():
    c128, s128 = _dft_cs(FFT_R)
    c64, s64 = _dft_cs(FFT_C)
    c256, s256 = _dft_cs(CTX_LEN)
    wa = np.concatenate([c128, -s128], axis=0)
    k1 = np.arange(FFT_R, dtype=np.float64)[:, None]
    cc = np.arange(FFT_C, dtype=np.float64)[None, :]
    ang = 2.0 * np.pi * k1 * cc / SEQ
    twc = np.broadcast_to(np.cos(ang)[:, :, None], (FFT_R, FFT_C, LANES))
    tws = np.broadcast_to(np.sin(ang)[:, :, None], (FFT_R, FFT_C, LANES))
    mc = np.block([[c64, s64], [-s64, c64]])
    cg, sg = _dft_cs(GROUP_W)
    mch = np.concatenate([cg, sg], axis=0) / math.sqrt(SEQ * GROUP_W)
    wcx = np.concatenate([c256, s256], axis=0)
    mchc = np.concatenate([cg, -sg], axis=0)
    n_freq = HEAD_DIM // 4
    inv = ROPE_BASE ** (-np.arange(n_freq, dtype=np.float64) / n_freq)
    t = np.arange(SEQ)
    rows = (t // GRID_W).astype(np.float64)[:, None] * inv
    cols = (t % GRID_W).astype(np.float64)[:, None] * inv
    cos_h = np.concatenate([np.cos(rows), np.cos(rows), np.cos(cols), np.cos(cols)], axis=1)
    sin_h = np.concatenate([-np.sin(rows), np.sin(rows), -np.sin(cols), np.sin(cols)], axis=1)
    cos_t = np.concatenate([cos_h, cos_h], axis=1)
    sin_t = np.concatenate([sin_h, sin_h], axis=1)
    f = lambda a: np.ascontiguousarray(a, dtype=np.float32)
    return dict(wa=f(wa), twc=f(twc), tws=f(tws), mc=f(mc), mch=f(mch), wcx=f(wcx), mchc=f(mchc),
                cos=f(np.concatenate([cos_t, np.ones((TMW, LANES))], axis=0)),
                sin=f(np.concatenate([sin_t, np.zeros((TMW, LANES))], axis=0)))


def _mod_body(cv_ref, w_ref, b_ref, o_ref):
    c = cv_ref[...]
    s = c * _sigmoid(c)
    w = w_ref[0]
    outs = [jnp.sum(w * s[:, r:r + 1], axis=0, keepdims=True) for r in range(N_COND)]
    pad = jnp.zeros((SUBLANES - N_COND, MOD_PER_STEP * D_MODEL), F32)
    res = jnp.concatenate(outs + [pad], axis=0)
    for k in range(MOD_PER_STEP):
        o_ref[0, k] = res[:, k * D_MODEL:(k + 1) * D_MODEL] + b_ref[k]


MOD_PER_STEP = 2
N_COND = 3


def _modulation(cv, w_mod, b_mod):
    steps = N_MOD // MOD_PER_STEP
    out = pl.pallas_call(
        _mod_body,
        grid=(DEPTH, steps),
        in_specs=[pl.BlockSpec((D_MODEL, SUBLANES), lambda l, k: (0, 0)),
                  pl.BlockSpec((1, D_MODEL, MOD_PER_STEP * D_MODEL), lambda l, k: (l, 0, k)),
                  pl.BlockSpec((MOD_PER_STEP, 1, D_MODEL), lambda l, k: (l * steps + k, 0, 0))],
        out_specs=pl.BlockSpec((1, MOD_PER_STEP, SUBLANES, D_MODEL), lambda l, k: (l, k, 0, 0)),
        out_shape=jax.ShapeDtypeStruct((DEPTH, N_MOD, SUBLANES, D_MODEL), F32),
        compiler_params=_cparams(("arbitrary", "arbitrary")),
        name="modulation",
    )(cv, w_mod, b_mod.reshape(DEPTH * N_MOD, 1, D_MODEL))
    out = jnp.transpose(out[:, :, :3, :], (0, 2, 1, 3))
    return out.reshape(DEPTH * 3 * N_MOD, 1, D_MODEL)


def _mod_spec(layer, k, cond_fn):
    return pl.BlockSpec((1, 1, D_MODEL),
                        lambda i, *_: ((layer * 3 + cond_fn(i)) * N_MOD + k, 0, 0))


def _in_proj_body(x_ref, sh_ref, sc_ref, g_ref, w_ref, cos_ref, sin_ref,
                  u_ref, q_ref, kv_ref, gt_ref):
    lane = lax.broadcasted_iota(jnp.int32, (SUB, LANES), 1)
    first_half = (lane % (HEAD_DIM // 2)) < (HEAD_DIM // 4)
    quarter = HEAD_DIM // 4
    scale = HEAD_DIM ** -0.5 * LOG2E
    gc = 512
    for sb in range(TMW // SUB):
        rows = slice(sb * SUB, (sb + 1) * SUB)
        hb = _rms_mod(x_ref[rows, :], g_ref[...], sh_ref[0], sc_ref[0]).astype(BF16)
        u_ref[rows, :] = _dot(hb, w_ref[:, 0:OFF_Q]).astype(BF16)
        cos = cos_ref[rows, :]
        sin = sin_ref[rows, :]

        def rope(xs):
            below = pltpu.roll(xs, quarter, 1)
            above = pltpu.roll(xs, LANES - quarter, 1)
            return xs * cos + jnp.where(first_half, above, below) * sin

        qk = _dot(hb, w_ref[:, OFF_Q:OFF_V])
        low = lane < HEAD_DIM
        for j in range(ATTN_WIDTH // LANES):
            r = rope(qk[:, j * LANES:(j + 1) * LANES]) * scale
            swapped = pltpu.roll(r, HEAD_DIM, 1)
            if (2 * j) // GROUP == 0:
                even, odd = jnp.where(low, r, 0.0), jnp.where(low, swapped, 0.0)
            else:
                even, odd = jnp.where(low, 0.0, swapped), jnp.where(low, 0.0, r)
            q_ref[rows, (2 * j) * LANES:(2 * j + 1) * LANES] = even.astype(BF16)
            q_ref[rows, (2 * j + 1) * LANES:(2 * j + 2) * LANES] = odd.astype(BF16)
        kv_ref[rows, 0:LANES] = rope(qk[:, ATTN_WIDTH:ATTN_WIDTH + KV_WIDTH]).astype(BF16)
        v = _dot(hb, w_ref[:, OFF_V:OFF_G])
        kv_ref[rows, LANES:2 * LANES] = jnp.where(low, v, 1.0).astype(BF16)
        kv_ref[rows, 2 * LANES:3 * LANES] = jnp.where(low, pltpu.roll(v, HEAD_DIM, 1), 1.0).astype(BF16)
        for j in range(2 * D_MODEL // gc):
            z = _dot(hb, w_ref[:, OFF_G + j * gc:OFF_G + (j + 1) * gc])
            gt_ref[rows, j * gc:(j + 1) * gc] = _sigmoid(z).astype(BF16)


def _in_proj(x, x_extra, mods, layer, cond_fn, g, w_in, cos, sin, rope_idx):
    if x_extra is None:
        rows = x.shape[0]
        body, x_args = _in_proj_body, [x]
        x_specs = [pl.BlockSpec((TMW, D_MODEL), lambda i: (i, 0))]
    else:
        rows = x.shape[0] + x_extra.shape[0]
        body, x_args = _with_two_sources(_in_proj_body, 1, x.shape[0] // TMW), [x, x_extra]
        x_specs = _two_source_specs(x, x_extra, TMW, D_MODEL)
    return pl.pallas_call(
        body,
        grid=(rows // TMW,),
        in_specs=x_specs + [
                  _mod_spec(layer, 0, cond_fn),
                  _mod_spec(layer, 1, cond_fn),
                  pl.BlockSpec((1, D_MODEL), lambda i: (0, 0)),
                  _resident(w_in, layer),
                  pl.BlockSpec((TMW, LANES), lambda i: (rope_idx(i), 0)),
                  pl.BlockSpec((TMW, LANES), lambda i: (rope_idx(i), 0))],
        out_specs=[pl.BlockSpec((TMW, FOURIER_WIDTH), lambda i: (i, 0)),
                   pl.BlockSpec((TMW, Q_PAD_W), lambda i: (i, 0)),
                   pl.BlockSpec((TMW, KVX_W), lambda i: (i, 0)),
                   pl.BlockSpec((TMW, 2 * D_MODEL), lambda i: (i, 0))],
        out_shape=[jax.ShapeDtypeStruct((rows, FOURIER_WIDTH), BF16),
                   jax.ShapeDtypeStruct((rows, Q_PAD_W), BF16),
                   jax.ShapeDtypeStruct((rows, KVX_W), BF16),
                   jax.ShapeDtypeStruct((rows, 2 * D_MODEL), BF16)],
        compiler_params=_cparams(("arbitrary",), VMEM_LIMIT),
        name="in_proj",
    )(*x_args, mods, mods, g, w_in, cos, sin)


FFT_CB = 16


def _fft_a_body(x_ref, w_ref, o_ref):
    w = w_ref[...]
    xt = jnp.swapaxes(x_ref[...].astype(F32), 0, 1)
    res = jnp.stack([_dot(w, xt[c]) for c in range(FFT_CB)])
    o_ref[0] = jnp.swapaxes(res, 0, 1).astype(BF16)


def _fft_c_body(re_ref, im_ref, tc_ref, ts_ref, mc_ref, mch_ref, o_ref):
    mc = mc_ref[...]
    mch = mch_ref[...]
    twiddled = []
    for j in range(FFT_CB):
        ar = re_ref[0, j].astype(F32)
        ai = im_ref[0, j].astype(F32)
        tc = jnp.concatenate([tc_ref[j]] * N_GROUPS, axis=1)
        ts = jnp.concatenate([ts_ref[j]] * N_GROUPS, axis=1)
        twiddled.append(jnp.concatenate([ar * tc + ai * ts, ai * tc - ar * ts], axis=0))
    xs = [_dot(mc, b) for b in twiddled]
    xr = jnp.concatenate([x[:FFT_C] for x in xs], axis=0).astype(BF16)
    xi = jnp.concatenate([x[FFT_C:] for x in xs], axis=0).astype(BF16)
    ys = []
    for g in range(N_GROUPS):
        sl = slice(g * GROUP_W, (g + 1) * GROUP_W)
        ys.append(_dot(jnp.concatenate([xr[:, sl], xi[:, sl]], axis=1), mch))
    y = jnp.concatenate(ys, axis=1).reshape(FFT_CB, FFT_C, FOURIER_WIDTH)
    o_ref[0] = jnp.swapaxes(y, 0, 1).astype(BF16)


def _fourier_latent(u, tb, batch):
    u3 = u.reshape(u.shape[0] // FFT_C, FFT_C, FOURIER_WIDTH)
    a = pl.pallas_call(
        _fft_a_body,
        grid=(batch, FFT_C // FFT_CB),
        in_specs=[pl.BlockSpec((FFT_R, FFT_CB, FOURIER_WIDTH), lambda b, j: (b, j, 0)),
                  pl.BlockSpec((2 * FFT_R, FFT_R), lambda b, j: (0, 0))],
        out_specs=pl.BlockSpec((1, 2 * FFT_R, FFT_CB, FOURIER_WIDTH), lambda b, j: (b, 0, j, 0)),
        out_shape=jax.ShapeDtypeStruct((batch, 2 * FFT_R, FFT_C, FOURIER_WIDTH), BF16),
        compiler_params=_cparams(("arbitrary", "arbitrary"), VMEM_LIMIT),
        name="fft_rows",
    )(u3, tb["wa"].astype(BF16))
    nk = FFT_R // FFT_CB
    y = pl.pallas_call(
        _fft_c_body,
        grid=(batch, nk),
        in_specs=[pl.BlockSpec((1, FFT_CB, FFT_C, FOURIER_WIDTH), lambda b, k: (b, k, 0, 0)),
                  pl.BlockSpec((1, FFT_CB, FFT_C, FOURIER_WIDTH), lambda b, k: (b, nk + k, 0, 0)),
                  pl.BlockSpec((FFT_CB, FFT_C, LANES), lambda b, k: (k, 0, 0)),
                  pl.BlockSpec((FFT_CB, FFT_C, LANES), lambda b, k: (k, 0, 0)),
                  pl.BlockSpec((2 * FFT_C, 2 * FFT_C), lambda b, k: (0, 0)),
                  pl.BlockSpec((2 * GROUP_W, GROUP_W), lambda b, k: (0, 0))],
        out_specs=pl.BlockSpec((1, FFT_C, FFT_CB, FOURIER_WIDTH), lambda b, k: (b, 0, k, 0)),
        out_shape=jax.ShapeDtypeStruct((batch, FFT_C, FFT_R, FOURIER_WIDTH), BF16),
        compiler_params=_cparams(("arbitrary", "arbitrary"), VMEM_LIMIT),
        name="fft_cols",
    )(a, a, tb["twc"], tb["tws"], tb["mc"].astype(BF16), tb["mch"].astype(BF16))
    return y.reshape(batch * SEQ, FOURIER_WIDTH)


def _fourier_ctx_body(u_ref, w_ref, m_ref, o_ref):
    pq = _dot(w_ref[...], u_ref[...])
    p = pq[:CTX_LEN]
    q = pq[CTX_LEN:]
    m = m_ref[...]
    scale = 1.0 / math.sqrt(CTX_LEN * GROUP_W)
    for g in range(N_GROUPS):
        sl = slice(g * GROUP_W, (g + 1) * GROUP_W)
        lhs = jnp.concatenate([p[:, sl], q[:, sl]], axis=1).astype(BF16)
        o_ref[:, sl] = _dot(lhs, m) * scale


def _fourier_ctx(u, tb, batch):
    first = batch * SEQ // CTX_LEN
    return pl.pallas_call(
        _fourier_ctx_body,
        grid=(batch,),
        in_specs=[pl.BlockSpec((CTX_LEN, FOURIER_WIDTH), lambda b: (first + b, 0)),
                  pl.BlockSpec((2 * CTX_LEN, CTX_LEN), lambda b: (0, 0)),
                  pl.BlockSpec((2 * GROUP_W, GROUP_W), lambda b: (0, 0))],
        out_specs=pl.BlockSpec((CTX_LEN, FOURIER_WIDTH), lambda b: (b, 0)),
        out_shape=jax.ShapeDtypeStruct((batch * CTX_LEN, FOURIER_WIDTH), F32),
        compiler_params=_cparams(("arbitrary",)),
        name="fft_ctx",
    )(u, tb["wcx"].astype(BF16), tb["mchc"].astype(BF16))


def _attn_body(sink_ref, q_ref, *refs, local, nq):
    o_ref = refs[-1]
    rows = 2 * QB
    if local:
        blocks = [r[...] for r in refs[:nq + 2]]
        ctx_rows = refs[nq + 2][...]
        step = pl.program_id(1)
        last_step = pl.num_programs(1) - 1
        qi = lax.broadcasted_iota(jnp.int32, (rows, 2 * QB), 0) % QB
        kj = lax.broadcasted_iota(jnp.int32, (rows, 2 * QB), 1)
        groups, masks = [], []
        for t in range(nq):
            no_prev = jnp.where(step == 0, 2 * QB, 0) if t == 0 else 0
            no_next = jnp.where(step == last_step, 2 * QB, 0) if t == nq - 1 else 0
            prev_ok = (kj < QB) & (kj >= qi + no_prev)
            next_ok = (kj >= QB) & (kj - QB <= qi - no_next)
            groups.append([jnp.concatenate([blocks[t], blocks[t + 2]], axis=0),
                           jnp.concatenate([blocks[t + 1], ctx_rows], axis=0)])
            masks.append([prev_ok | next_ok, None])
    else:
        groups = [[refs[0][...]]]
        masks = [[None]]
    lo_q = lax.broadcasted_iota(jnp.int32, (QB, LANES), 1) < HEAD_DIM
    first_head = lax.broadcasted_iota(jnp.int32, (rows, 1), 0) < QB
    chains = [(t, slab) for t in range(nq) for slab in range(ATTN_WIDTH // LANES)]
    scores = []
    for t, slab in chains:
        qrows = slice(t * QB, (t + 1) * QB)
        qs = jnp.concatenate([q_ref[qrows, (2 * slab) * LANES:(2 * slab + 1) * LANES],
                              q_ref[qrows, (2 * slab + 1) * LANES:(2 * slab + 2) * LANES]], axis=0)
        parts = []
        for kv, mask in zip(groups[t], masks[t]):
            s = lax.dot_general(qs, kv[:, :LANES], (((1,), (1,)), ((), ())), preferred_element_type=F32)
            parts.append(s if mask is None else jnp.where(mask, s, NEG))
        scores.append(parts)
    sinks = [jnp.where(first_head, sink_ref[2 * slab], sink_ref[2 * slab + 1]) * LOG2E for _, slab in chains]
    maxes = []
    for parts, sk in zip(scores, sinks):
        m = sk
        for s in parts:
            m = jnp.maximum(m, jnp.max(s, axis=1, keepdims=True))
        maxes.append(m)
    sink_p = [jnp.exp2(sk - m) for sk, m in zip(sinks, maxes)]
    pvs = []
    for (t, slab), parts, m in zip(chains, scores, maxes):
        hk = (2 * slab) // GROUP
        pv = None
        for s, kv in zip(parts, groups[t]):
            term = _dot(jnp.exp2(s - m).astype(BF16), kv[:, (1 + hk) * LANES:(2 + hk) * LANES])
            pv = term if pv is None else pv + term
        pvs.append(pv)
    for (t, slab), pv, sp in zip(chains, pvs, sink_p):
        ra, rb = pv[:QB], pv[QB:]
        num = jnp.where(lo_q, ra, pltpu.roll(rb, HEAD_DIM, 1))
        den = jnp.where(lo_q, pltpu.roll(ra, HEAD_DIM, 1), rb) + jnp.where(lo_q, sp[:QB], sp[QB:])
        o_ref[t * QB:(t + 1) * QB, slab * LANES:(slab + 1) * LANES] = (num / den).astype(BF16)


def _attention_latent(q, kv, sink, batch):
    nb = SEQ // QB
    steps = nb // ATT_NQ
    first_ctx = batch * SEQ // CTX_LEN

    def key_block(j):
        return pl.BlockSpec((QB, KVX_W),
                            lambda b, n: (b * nb + jnp.clip(n * ATT_NQ - 1 + j, 0, nb - 1), 0))

    return pl.pallas_call(
        functools.partial(_attn_body, local=True, nq=ATT_NQ),
        grid=(batch, steps),
        in_specs=[pl.BlockSpec(memory_space=pltpu.SMEM),
                  pl.BlockSpec((ATT_NQ * QB, Q_PAD_W), lambda b, n: (b * steps + n, 0))]
                 + [key_block(j) for j in range(ATT_NQ + 2)]
                 + [pl.BlockSpec((CTX_LEN, KVX_W), lambda b, n: (first_ctx + b, 0))],
        out_specs=pl.BlockSpec((ATT_NQ * QB, ATTN_WIDTH), lambda b, n: (b * steps + n, 0)),
        out_shape=jax.ShapeDtypeStruct((batch * SEQ, ATTN_WIDTH), BF16),
        compiler_params=_cparams(("arbitrary", "arbitrary")),
        name="attn_latent",
    )(sink, q, *([kv] * (ATT_NQ + 3)))


def _attention_ctx(q, kv, sink, batch):
    nb = CTX_LEN // QB
    first_q = batch * SEQ // QB
    first_ctx = batch * SEQ // CTX_LEN
    return pl.pallas_call(
        functools.partial(_attn_body, local=False, nq=1),
        grid=(batch, nb),
        in_specs=[pl.BlockSpec(memory_space=pltpu.SMEM),
                  pl.BlockSpec((QB, Q_PAD_W), lambda b, n: (first_q + b * nb + n, 0)),
                  pl.BlockSpec((CTX_LEN, KVX_W), lambda b, n: (first_ctx + b, 0))],
        out_specs=pl.BlockSpec((QB, ATTN_WIDTH), lambda b, n: (b * nb + n, 0)),
        out_shape=jax.ShapeDtypeStruct((batch * CTX_LEN, ATTN_WIDTH), BF16),
        compiler_params=_cparams(("arbitrary", "arbitrary")),
        name="attn_ctx",
    )(sink, q, kv)


def _merge_body(x_ref, fm_ref, ao_ref, gt_ref, wf_ref, wa_ref, wo_ref, ga_ref, *refs, with_h2):
    if with_h2:
        sh_ref, sc_ref, g_ref, xo_ref, h_ref = refs
    else:
        (xo_ref,) = refs
    for sb in range(TMW // SUB):
        rows = slice(sb * SUB, (sb + 1) * SUB)
        gt = gt_ref[rows, :].astype(F32)
        y = (gt[:, :D_MODEL] * _dot(fm_ref[rows, :], wf_ref[...])
             + gt[:, D_MODEL:] * _dot(ao_ref[rows, :], wa_ref[...]))
        xn = x_ref[rows, :] + ga_ref[0] * _dot(y, wo_ref[...])
        if with_h2:
            h_ref[rows, :] = _rms_mod(xn, g_ref[...], sh_ref[0], sc_ref[0]).astype(BF16)
        xo_ref[rows, :] = xn


def _merge(rows, x, fm, ao, extras, gates, wf, wa, wo, mods, layer, cond_fn, g2, with_h2):
    row_spec = lambda w: pl.BlockSpec((TMW, w), lambda i: (i, 0))
    widths = (D_MODEL, FOURIER_WIDTH, ATTN_WIDTH)
    body = functools.partial(_merge_body, with_h2=with_h2)
    if extras is None:
        in_specs = [row_spec(w) for w in widths]
        args = [x, fm, ao]
    else:
        main_rows = rows
        rows = rows + extras[0].shape[0]
        body = _with_two_sources(body, len(widths), main_rows // TMW)
        in_specs, args = [], []
        for main, extra, w in zip((x, fm, ao), extras, widths):
            assert main.shape[0] >= main_rows and extra.shape[0] == extras[0].shape[0]
            in_specs += [pl.BlockSpec((TMW, w), lambda i: (jnp.minimum(i, main_rows // TMW - 1), 0)),
                         pl.BlockSpec((TMW, w), lambda i: (jnp.maximum(i - main_rows // TMW, 0), 0))]
            args += [main, extra]
    in_specs += [row_spec(2 * D_MODEL),
                 _resident(wf, layer), _resident(wa, layer), _resident(wo, layer), _mod_spec(layer, 2, cond_fn)]
    args += [gates, wf, wa, wo, mods]
    out_specs = [row_spec(D_MODEL)]
    out_shape = [jax.ShapeDtypeStruct((rows, D_MODEL), F32)]
    if with_h2:
        in_specs += [_mod_spec(layer, 3, cond_fn), _mod_spec(layer, 4, cond_fn),
                     pl.BlockSpec((1, D_MODEL), lambda i: (0, 0))]
        args += [mods, mods, g2]
        out_specs.append(row_spec(D_MODEL))
        out_shape.append(jax.ShapeDtypeStruct((rows, D_MODEL), BF16))
    return pl.pallas_call(
        body,
        grid=(rows // TMW,),
        in_specs=in_specs, out_specs=out_specs, out_shape=out_shape,
        compiler_params=_cparams(("arbitrary",), VMEM_LIMIT),
        name="merge",
    )(*args)


def _ffn_body(x_ref, h_ref, wg_ref, wu_ref, wd_ref, ga_ref, o_ref):
    d_ff = wg_ref.shape[1]
    mxu_tiles = d_ff // MXU_W
    edges = [MXU_W * (mxu_tiles * c // FFN_CHUNKS) for c in range(FFN_CHUNKS)] + [d_ff]
    blocks = [slice(k * SUB, (k + 1) * SUB) for k in range(TM // SUB)]
    hs = [h_ref[rows, :] for rows in blocks]
    downs = [None] * len(blocks)
    for c in range(FFN_CHUNKS):
        cols = slice(edges[c], edges[c + 1])
        acts = []
        for h in hs:
            g = _dot(h, wg_ref[:, cols])
            u = _dot(h, wu_ref[:, cols])
            acts.append(g * _sigmoid(g) * u)
        for k, a in enumerate(acts):
            d = _dot(a, wd_ref[cols, :])
            downs[k] = d if downs[k] is None else downs[k] + d
    for rows, d in zip(blocks, downs):
        o_ref[rows, :] = x_ref[rows, :] + ga_ref[0] * d


def _ffn_dense(x, h2, wg, wu, wd, mods, layer, cond_fn):
    rows = x.shape[0]
    return pl.pallas_call(
        _ffn_body,
        grid=(rows // TM,),
        in_specs=[pl.BlockSpec((TM, D_MODEL), lambda i: (i, 0)),
                  pl.BlockSpec((TM, D_MODEL), lambda i: (i, 0)),
                  _resident(wg, layer // 2), _resident(wu, layer // 2), _resident(wd, layer // 2),
                  _mod_spec(layer, 5, cond_fn)],
        out_specs=pl.BlockSpec((TM, D_MODEL), lambda i: (i, 0)),
        out_shape=jax.ShapeDtypeStruct((rows, D_MODEL), F32),
        compiler_params=_cparams(("arbitrary",), VMEM_LIMIT),
        name="ffn_dense",
    )(x, h2, wg, wu, wd, mods)


INFO_E0, INFO_E1, INFO_R0, INFO_R1, INFO_W0, INFO_W1 = range(6)


def _router_body(x_ref, sh_ref, sc_ref, g_ref, wr_ref, hb_ref, info_ref, cnt_ref):
    tiles = [slice(t * TD, (t + 1) * TD) for t in range(ROUTER_TILES)]
    lane = lax.broadcasted_iota(jnp.int32, (TD, LANES), 1)
    neg_inf = jnp.float32(-jnp.inf)
    row = lax.broadcasted_iota(jnp.int32, (TD, TD), 0)
    col = lax.broadcasted_iota(jnp.int32, (TD, TD), 1)
    tri = jnp.where(row > col, 1.0, 0.0).astype(BF16)
    hs = [_rms_mod(x_ref[rows, :], g_ref[...], sh_ref[0], sc_ref[0]) for rows in tiles]
    for rows, h in zip(tiles, hs):
        hb_ref[rows, :] = h.astype(BF16)
    lgs = [jnp.where(lane < N_EXPERTS,
                     jnp.dot(h, wr_ref[...], precision=lax.Precision.HIGHEST, preferred_element_type=F32), neg_inf)
           for h in hs]
    v0s = [jnp.max(lg, axis=1, keepdims=True) for lg in lgs]
    i0s = [jnp.min(jnp.where(lg == v0, lane, LANES), axis=1, keepdims=True) for lg, v0 in zip(lgs, v0s)]
    oh0s = [lane == i0 for i0 in i0s]
    lg1s = [jnp.where(oh0, neg_inf, lg) for oh0, lg in zip(oh0s, lgs)]
    v1s = [jnp.max(lg1, axis=1, keepdims=True) for lg1 in lg1s]
    i1s = [jnp.min(jnp.where(lg1 == v1, lane, LANES), axis=1, keepdims=True) for lg1, v1 in zip(lg1s, v1s)]
    oh1s = [lane == i1 for i1 in i1s]
    ohs = [jnp.where(oh0 | oh1, 1.0, 0.0) for oh0, oh1 in zip(oh0s, oh1s)]
    befores = [_dot(tri, oh.astype(BF16)) for oh in ohs]
    for t, rows in enumerate(tiles):
        e = jnp.exp(v1s[t] - v0s[t])
        w0 = 1.0 / (1.0 + e)
        w1 = e / (1.0 + e)
        r0 = jnp.sum(jnp.where(oh0s[t], befores[t], 0.0), axis=1, keepdims=True)
        r1 = jnp.sum(jnp.where(oh1s[t], befores[t], 0.0), axis=1, keepdims=True)
        cnt_ref[t] = jnp.broadcast_to(jnp.sum(ohs[t], axis=0, keepdims=True), (SUBLANES, LANES))
        info = jnp.zeros((TD, LANES), F32)
        for idx, val in ((INFO_E0, i0s[t].astype(F32)), (INFO_E1, i1s[t].astype(F32)), (INFO_R0, r0),
                         (INFO_R1, r1), (INFO_W0, w0), (INFO_W1, w1)):
            info = jnp.where(lane == idx, val, info)
        info_ref[rows, :] = info


def _router(x, mods, layer, cond_fn, g2, wr_pad):
    rows = x.shape[0]
    step_rows = ROUTER_TILES * TD
    step_cond = lambda i: cond_fn(i * ROUTER_TILES)
    return pl.pallas_call(
        _router_body,
        grid=(rows // step_rows,),
        in_specs=[pl.BlockSpec((step_rows, D_MODEL), lambda i: (i, 0)),
                  _mod_spec(layer, 3, step_cond), _mod_spec(layer, 4, step_cond),
                  pl.BlockSpec((1, D_MODEL), lambda i: (0, 0)),
                  pl.BlockSpec((D_MODEL, LANES), lambda i: (0, 0))],
        out_specs=[pl.BlockSpec((step_rows, D_MODEL), lambda i: (i, 0)),
                   pl.BlockSpec((step_rows, LANES), lambda i: (i, 0)),
                   pl.BlockSpec((ROUTER_TILES, SUBLANES, LANES), lambda i: (i, 0, 0))],
        out_shape=[jax.ShapeDtypeStruct((rows, D_MODEL), BF16),
                   jax.ShapeDtypeStruct((rows, LANES), F32),
                   jax.ShapeDtypeStruct((rows // TD, SUBLANES, LANES), F32)],
        compiler_params=_cparams(("arbitrary",), VMEM_LIMIT),
        name="router",
    )(x, mods, mods, g2, wr_pad)


def _segment_copies(src_ref, src_off, dst_ref, dst_off, len8, sem, bits, wait):
    for k in reversed(range(bits)):
        size = SUBLANES << k
        done = ((len8 >> (k + 1)) << (k + 1)) * SUBLANES

        @pl.when(((len8 >> k) & 1) == 1)
        def _(size=size, done=done):
            cp = pltpu.make_async_copy(
                src_ref.at[pl.ds(pl.multiple_of(src_off + done, SUBLANES), size)],
                dst_ref.at[pl.ds(pl.multiple_of(dst_off + done, SUBLANES), size)], sem)
            if wait:
                cp.wait()
            else:
                cp.start()


def _slot_positions(e0, e1, r0, r1, seg_ref, base):
    pos0, pos1 = r0, r1
    for e in range(N_EXPERTS):
        start = seg_ref[base + e].astype(F32)
        pos0 = pos0 + jnp.where(e0 == e, start, 0.0)
        pos1 = pos1 + jnp.where(e1 == e, start, 0.0)
    return pos0.astype(jnp.int32), pos1.astype(jnp.int32)


def _dispatch_body(seg_ref, dst_ref, len_ref, tdst_ref, tlen_ref, nt_ref, hb_ref, info_ref, xs_ref,
                   buf_ref, zero_ref, sems, zero_sem, *, min_tiles):
    i = pl.program_id(0)
    last = pl.num_programs(0) - 1
    base = i * N_EXPERTS
    cur = i % 2

    def scatter(tile, slot, wait):
        for e in range(N_EXPERTS):
            k = tile * N_EXPERTS + e
            _segment_copies(buf_ref.at[slot], seg_ref[k], xs_ref, dst_ref[k], len_ref[k],
                            sems.at[slot], SEG_BITS, wait=wait)

    info_t = info_ref[...].T
    row = lambda k: info_t[k:k + 1, :]
    pos0, pos1 = _slot_positions(row(INFO_E0), row(INFO_E1), row(INFO_R0), row(INFO_R1), seg_ref, base)
    slot = lax.broadcasted_iota(jnp.int32, (SB, TD), 0)
    p0 = slot == pos0
    p1 = slot == pos1
    perm = jnp.where(p0 | p1, 1.0, 0.0).astype(BF16)
    buf_ref[cur, :, :D_MODEL] = _dot(perm, hb_ref[...])
    wsel = jnp.where(p0, row(INFO_W0), 0.0) + jnp.where(p1, row(INFO_W1), 0.0)
    buf_ref[cur, :, D_MODEL:] = jnp.broadcast_to(jnp.sum(wsel, axis=1, keepdims=True), (SB, LANES))
    scatter(i, cur, wait=False)

    @pl.when(i > 0)
    def _():
        scatter(i - 1, 1 - cur, wait=True)

    @pl.when(i == last)
    def _():
        zero_ref[...] = jnp.zeros_like(zero_ref)
        spare_tiles = range(min_tiles, xs_ref.shape[0] // TMP)
        spare = [pltpu.make_async_copy(zero_ref, xs_ref.at[pl.ds(j * TMP, TMP)], zero_sem) for j in spare_tiles]
        for wait in (False, True):
            for e in range(N_EXPERTS):
                _segment_copies(zero_ref, 0, xs_ref, tdst_ref[e], tlen_ref[e], zero_sem, TAIL_BITS, wait=wait)
            for j, cp in zip(spare_tiles, spare):
                @pl.when(j >= nt_ref[0])
                def _(cp=cp, wait=wait):
                    if wait:
                        cp.wait()
                    else:
                        cp.start()
        scatter(i, cur, wait=True)


def _dispatch(seg, dst, len8, tail_dst, tail_len8, n_tiles, hb, info, m_rows):
    rows = hb.shape[0]
    return pl.pallas_call(
        functools.partial(_dispatch_body, min_tiles=2 * rows // TMP),
        grid_spec=pltpu.PrefetchScalarGridSpec(
            num_scalar_prefetch=6,
            grid=(rows // TD,),
            in_specs=[pl.BlockSpec((TD, D_MODEL), lambda i, *_: (i, 0)),
                      pl.BlockSpec((TD, LANES), lambda i, *_: (i, 0))],
            out_specs=pl.BlockSpec(memory_space=pl.ANY),
            scratch_shapes=[pltpu.VMEM((2, SB, XS_W), F32), pltpu.VMEM((TMP, XS_W), F32),
                            pltpu.SemaphoreType.DMA((2,)), pltpu.SemaphoreType.DMA(())]),
        out_shape=jax.ShapeDtypeStruct((m_rows, XS_W), F32),
        compiler_params=_cparams(("arbitrary",), VMEM_LIMIT),
        name="dispatch",
    )(seg, dst, len8, tail_dst, tail_len8, n_tiles, hb, info)


def _expert_body(te_ref, nb_ref, nt_ref, xs_ref, wg_ref, wu_ref, wd_ref, ys_ref, acc_ref):
    j = pl.program_id(0)
    f = pl.program_id(1)
    nf = pl.num_programs(1)
    n_live = nb_ref[j]

    blocks = [slice(h * TME, (h + 1) * TME) for h in range(TILE_BLOCKS)]

    @pl.when((n_live > 0) & (f == 0))
    def _():
        acc_ref[...] = jnp.zeros_like(acc_ref)

    for live in range(1, TILE_BLOCKS + 1):
        @pl.when(n_live == live)
        def _(live=live):
            ups = [(_dot(xs_ref[rows, :D_MODEL], wg_ref[0]), _dot(xs_ref[rows, :D_MODEL], wu_ref[0]))
                   for rows in blocks[:live]]
            downs = [_dot(g * _sigmoid(g) * u, wd_ref[0]) for g, u in ups]
            for rows, d in zip(blocks[:live], downs):
                acc_ref[rows, :] += d

    @pl.when(f == nf - 1)
    def _():
        live_rows = lax.broadcasted_iota(jnp.int32, (TMP, 1), 0) < n_live * TME
        ys_ref[...] = jnp.where(live_rows, acc_ref[...] * xs_ref[:, D_MODEL:D_MODEL + 1], 0.0)


def _experts(tile_expert, tile_blocks, n_tiles, xs, wg, wu, wd):
    m_rows = xs.shape[0]
    nf = wg.shape[2] // FC_E

    def f_idx(j, f, nt):
        last_live = nt[0] - 1
        walk = lambda t, step: jnp.where(t % 2 == 0, step, nf - 1 - step)
        return jnp.where(j < nt[0], walk(j, f), walk(last_live, nf - 1))

    def j_idx(j, nt):
        return jnp.minimum(j, nt[0] - 1)

    return pl.pallas_call(
        _expert_body,
        grid_spec=pltpu.PrefetchScalarGridSpec(
            num_scalar_prefetch=3,
            grid=(m_rows // TMP, nf),
            in_specs=[pl.BlockSpec((TMP, XS_W), lambda j, f, te, nb, nt: (j_idx(j, nt), 0)),
                      pl.BlockSpec((1, D_MODEL, FC_E), lambda j, f, te, nb, nt: (te[j], 0, f_idx(j, f, nt))),
                      pl.BlockSpec((1, D_MODEL, FC_E), lambda j, f, te, nb, nt: (te[j], 0, f_idx(j, f, nt))),
                      pl.BlockSpec((1, FC_E, D_MODEL), lambda j, f, te, nb, nt: (te[j], f_idx(j, f, nt), 0))],
            out_specs=pl.BlockSpec((TMP, D_MODEL), lambda j, f, te, nb, nt: (j, 0)),
            scratch_shapes=[pltpu.VMEM((TMP, D_MODEL), F32)]),
        out_shape=jax.ShapeDtypeStruct((m_rows, D_MODEL), F32),
        compiler_params=_cparams(("arbitrary", "arbitrary"), VMEM_LIMIT),
        name="experts",
    )(tile_expert, tile_blocks, n_tiles, xs, wg, wu, wd)


def _combine_body(seg_ref, dst_ref, len_ref, ys_ref, x_ref, info_ref, ga_ref, g_ref, o_ref, buf_ref, sems):
    i = pl.program_id(0)
    base = i * N_EXPERTS
    cur = i % 2

    def gather(tile, slot, wait):
        for e in range(N_EXPERTS):
            k = tile * N_EXPERTS + e
            _segment_copies(ys_ref, dst_ref[k], buf_ref.at[slot], seg_ref[k], len_ref[k],
                            sems.at[slot], SEG_BITS, wait=wait)

    @pl.when(i == 0)
    def _():
        buf_ref[...] = jnp.zeros_like(buf_ref)
        gather(0, 0, wait=False)

    @pl.when(i + 1 < pl.num_programs(0))
    def _():
        gather(i + 1, 1 - cur, wait=False)

    info = info_ref[...]
    col = lambda k: info[:, k:k + 1]
    pos0, pos1 = _slot_positions(col(INFO_E0), col(INFO_E1), col(INFO_R0), col(INFO_R1), seg_ref, base)
    slot = lax.broadcasted_iota(jnp.int32, (TD, SB), 1)
    unperm = jnp.where((slot == pos0) | (slot == pos1), 1.0, 0.0).astype(BF16)
    gather(i, cur, wait=True)
    y = _dot(unperm, buf_ref[cur])
    xn = x_ref[...] + ga_ref[0] * y
    r = lax.rsqrt(jnp.mean(xn * xn, axis=-1, keepdims=True) + EPS)
    o_ref[...] = (xn * r) * g_ref[...]


def _combine(seg, dst, len8, ys, x, info, mods, layer, cond_fn, final_g):
    rows = x.shape[0]
    return pl.pallas_call(
        _combine_body,
        grid_spec=pltpu.PrefetchScalarGridSpec(
            num_scalar_prefetch=3,
            grid=(rows // TD,),
            in_specs=[pl.BlockSpec(memory_space=pl.ANY),
                      pl.BlockSpec((TD, D_MODEL), lambda i, *_: (i, 0)),
                      pl.BlockSpec((TD, LANES), lambda i, *_: (i, 0)),
                      _mod_spec(layer, 5, cond_fn),
                      pl.BlockSpec((1, D_MODEL), lambda i, *_: (0, 0))],
            out_specs=pl.BlockSpec((TD, D_MODEL), lambda i, *_: (i, 0)),
            scratch_shapes=[pltpu.VMEM((2, SB, D_MODEL), F32), pltpu.SemaphoreType.DMA((2,))]),
        out_shape=jax.ShapeDtypeStruct((rows, D_MODEL), F32),
        compiler_params=_cparams(("arbitrary",), VMEM_LIMIT),
        name="combine",
    )(seg, dst, len8, ys, x, info, mods, final_g)


def _moe(x, mods, layer, cond_fn, g2, w_router, wg, wu, wd, final_g):
    rows = x.shape[0]
    n_tok_tiles = rows // TD
    wr_pad = jnp.pad(w_router, ((0, 0), (0, LANES - N_EXPERTS)))
    hb, info, cnt = _router(x, mods, layer, cond_fn, g2, wr_pad)
    counts = cnt[:, 0, :N_EXPERTS].astype(jnp.int32)
    seg_len = (counts + SUBLANES - 1) // SUBLANES * SUBLANES
    seg_start = jnp.cumsum(seg_len, axis=1) - seg_len
    group = jnp.sum(seg_len, axis=0)
    group_blocks = (group + TME - 1) // TME
    group_tiles = (group + TMP - 1) // TMP
    tile_start = jnp.cumsum(group_tiles) - group_tiles
    group_base = tile_start * TMP
    dst = group_base[None, :] + jnp.cumsum(seg_len, axis=0) - seg_len
    tail_dst = group_base + group
    tail_len8 = (group_tiles * TMP - group) // SUBLANES
    max_tiles = (2 * rows + n_tok_tiles * N_EXPERTS * (SUBLANES - 1) + TMP - 1) // TMP + N_EXPERTS
    n_tiles = jnp.sum(group_tiles).astype(jnp.int32).reshape(1)
    tiles = jnp.arange(max_tiles, dtype=jnp.int32)
    tid = jnp.minimum(tiles, n_tiles[0] - 1)
    tile_expert = (jnp.sum(tid[:, None] >= tile_start[None, :], axis=1) - 1).astype(jnp.int32)
    own = (tile_expert[:, None] == jnp.arange(N_EXPERTS, dtype=jnp.int32)[None, :]).astype(jnp.int32)
    in_group = tid - jnp.sum(own * tile_start[None, :], axis=1)
    tile_blocks = jnp.clip(jnp.sum(own * group_blocks[None, :], axis=1) - TILE_BLOCKS * in_group, 0, TILE_BLOCKS)
    tile_blocks = jnp.where(tiles < n_tiles[0], tile_blocks, 0).astype(jnp.int32)
    flat = lambda a: a.reshape(-1).astype(jnp.int32)
    seg, dst, len8 = flat(seg_start), flat(dst), flat(seg_len // SUBLANES)
    xs = _dispatch(seg, dst, len8, flat(tail_dst), flat(tail_len8), n_tiles, hb, info, max_tiles * TMP)
    ys = _experts(tile_expert, tile_blocks, n_tiles, xs, wg, wu, wd)
    return _combine(seg, dst, len8, ys, x, info, mods, layer, cond_fn, final_g)


def kernel(x, c, ctx, c_ctx, w_mod, b_mod, norm1_g, norm2_g, w_in, sink, w_fourier, w_attn, w_out,
           w_gate_d, w_up_d, w_down_d, w_router, w_gate_e, w_up_e, w_down_e, final_g):
    batch, seq, d = x.shape
    assert (seq, d) == (SEQ, D_MODEL) and ctx.shape == (batch, CTX_LEN, D_MODEL)
    tb = {k: jnp.asarray(v) for k, v in _tables().items()}
    n_lat = batch * SEQ
    assert (batch * CTX_LEN) % TMW == 0 and n_lat % TMW == 0

    def cond(tile):
        return lambda i: jnp.where(i < n_lat // tile, i // (SEQ // tile), batch)

    def rope_idx(i):
        return jnp.where(i < n_lat // TMW, i % (SEQ // TMW), SEQ // TMW + i - n_lat // TMW)

    assert batch + 1 == N_COND
    cv = jnp.zeros((D_MODEL, SUBLANES), F32).at[:, :batch].set(c.T).at[:, batch].set(c_ctx)
    mods = _modulation(cv, w_mod, b_mod)

    xs_main, xs_extra = x.reshape(n_lat, D_MODEL), ctx.reshape(batch * CTX_LEN, D_MODEL)
    out = None
    for l in range(DEPTH):
        last = l == DEPTH - 1
        g1 = norm1_g[l].reshape(1, D_MODEL)
        g2 = norm2_g[l].reshape(1, D_MODEL)
        wf, wa, wo = w_fourier, w_attn, w_out
        u, q, kv, gates = _in_proj(xs_main, xs_extra, mods, l, cond(TMW), g1, w_in, tb["cos"], tb["sin"], rope_idx)
        fm = _fourier_latent(u, tb, batch)
        ao = _attention_latent(q, kv, sink[l], batch)
        if l % 2 == 0:
            if last:
                raise NotImplementedError("final norm is fused into the routed-expert combine")
            extras = (xs_extra, _fourier_ctx(u, tb, batch), _attention_ctx(q, kv, sink[l], batch))
            xa, h2 = _merge(n_lat, xs_main, fm, ao, extras, gates, wf, wa, wo, mods, l, cond(TMW), g2, True)
            xs_main = _ffn_dense(xa, h2, w_gate_d, w_up_d, w_down_d, mods, l, cond(TM))
            xs_extra = None
        else:
            if not last:
                raise NotImplementedError("context update through a routed-expert layer")
            i = l // 2
            (xl,) = _merge(n_lat, xs_main, fm, ao, None, gates, wf, wa, wo, mods, l, cond(TMW), g2, False)
            out = _moe(xl, mods, l, cond(TD), g2, w_router[i], w_gate_e[i], w_up_e[i], w_down_e[i],
                       final_g.reshape(1, D_MODEL))
    return out.reshape(batch, SEQ, D_MODEL)
```

```python
import functools
import math

import numpy as np
import jax
import jax.numpy as jnp
from jax import lax
from jax.experimental import pallas as pl
from jax.experimental.pallas import tpu as pltpu

F32 = jnp.float32
BF16 = jnp.bfloat16

D_MODEL = 1024
SEQ = 8192
DEPTH = 2
GRID_W = 64
CTX_LEN = 256
N_GROUPS = 4
GROUP_W = 128
FOURIER_WIDTH = N_GROUPS * GROUP_W
N_Q_HEADS = 8
N_KV_HEADS = 2
HEAD_DIM = 64
ATTN_WIDTH = N_Q_HEADS * HEAD_DIM
KV_WIDTH = N_KV_HEADS * HEAD_DIM
GROUP = N_Q_HEADS // N_KV_HEADS
WINDOW = 128
ROPE_BASE = 10000.0
OFF_Q = FOURIER_WIDTH
OFF_K = OFF_Q + ATTN_WIDTH
OFF_V = OFF_K + KV_WIDTH
OFF_G = OFF_V + KV_WIDTH
IN_WIDTH = OFF_G + 2 * D_MODEL
N_EXPERTS = 8
N_MOD = 6
EPS = 1e-6
NEG = -1e30
LOG2E = 1.0 / math.log(2.0)

LANES = 128
SUBLANES = 8
MXU_W = 256

TM = 512
FFN_CHUNKS = 2
TMW = 512
SUB = 256
QB = 128
ATT_NQ = 8
Q_PAD_W = N_Q_HEADS * LANES
KVX_W = (1 + N_KV_HEADS) * LANES
FFT_R = SEQ // GRID_W
FFT_C = GRID_W
TME = 256
TILE_BLOCKS = 4
TMP = TILE_BLOCKS * TME
TD = 512
ROUTER_TILES = 4
SB = 2 * TD + N_EXPERTS * SUBLANES
XS_W = D_MODEL + LANES
SEG_BITS = TD.bit_length() - 3
TAIL_BITS = TMP.bit_length() - 4
FC_E = 512
VMEM_LIMIT = 56 * 1024 * 1024


def _cparams(sem, vmem=None):
    return pltpu.CompilerParams(dimension_semantics=sem, vmem_limit_bytes=vmem)


def _dot(a, b):
    return lax.dot_general(a, b, (((1,), (0,)), ((), ())), preferred_element_type=F32)


def _resident(a, layer):
    return pl.BlockSpec((None,) + a.shape[1:], lambda i, *_: (layer, 0, 0), pipeline_mode=pl.Buffered(1))


def _two_source_specs(main, extra, tile, width):
    n_main = main.shape[0] // tile
    return [pl.BlockSpec((tile, width), lambda i: (jnp.minimum(i, n_main - 1), 0)),
            pl.BlockSpec((tile, width), lambda i: (jnp.maximum(i - n_main, 0), 0))]


def _with_two_sources(body, n_pairs, n_main_tiles):
    def kernel_fn(*refs):
        use_extra = pl.program_id(0) >= n_main_tiles
        rest = refs[2 * n_pairs:]

        @pl.when(jnp.logical_not(use_extra))
        def _():
            body(*[refs[2 * k] for k in range(n_pairs)], *rest)

        @pl.when(use_extra)
        def _():
            body(*[refs[2 * k + 1] for k in range(n_pairs)], *rest)
    return kernel_fn


def _sigmoid(x):
    return 1.0 / (1.0 + jnp.exp(-x))


def _rms_mod(x, g, sh, sc):
    r = lax.rsqrt(jnp.mean(x * x, axis=-1, keepdims=True) + EPS)
    return (x * r) * g * (1.0 + sc) + sh


def _dft_cs(n):
    k = np.arange(n, dtype=np.float64)
    a = 2.0 * np.pi * np.outer(k, k) / n
    return np.cos(a), np.sin(a)


@functools.lru_cache(maxsize=None)
def _tables():
    c128, s128 = _dft_cs(FFT_R)
    c64, s64 = _dft_cs(FFT_C)
    c256, s256 = _dft_cs(CTX_LEN)
    wa = np.concatenate([c128, -s128], axis=0)
    k1 = np.arange(FFT_R, dtype=np.float64)[:, None]
    cc = np.arange(FFT_C, dtype=np.float64)[None, :]
    ang = 2.0 * np.pi * k1 * cc / SEQ
    twc = np.broadcast_to(np.cos(ang)[:, :, None], (FFT_R, FFT_C, LANES))
    tws = np.broadcast_to(np.sin(ang)[:, :, None], (FFT_R, FFT_C, LANES))
    mc = np.block([[c64, s64], [-s64, c64]])
    cg, sg = _dft_cs(GROUP_W)
    mch = np.concatenate([cg, sg], axis=0) / math.sqrt(SEQ * GROUP_W)
    wcx = np.concatenate([c256, s256], axis=0)
    mchc = np.concatenate([cg, -sg], axis=0)
    n_freq = HEAD_DIM // 4
    inv = ROPE_BASE ** (-np.arange(n_freq, dtype=np.float64) / n_freq)
    t = np.arange(SEQ)
    rows = (t // GRID_W).astype(np.float64)[:, None] * inv
    cols = (t % GRID_W).astype(np.float64)[:, None] * inv
    cos_h = np.concatenate([np.cos(rows), np.cos(rows), np.cos(cols), np.cos(cols)], axis=1)
    sin_h = np.concatenate([-np.sin(rows), np.sin(rows), -np.sin(cols), np.sin(cols)], axis=1)
    cos_t = np.concatenate([cos_h, cos_h], axis=1)
    sin_t = np.concatenate([sin_h, sin_h], axis=1)
    f = lambda a: np.ascontiguousarray(a, dtype=np.float32)
    return dict(wa=f(wa), twc=f(twc), tws=f(tws), mc=f(mc), mch=f(mch), wcx=f(wcx), mchc=f(mchc),
                cos=f(np.concatenate([cos_t, np.ones((TMW, LANES))], axis=0)),
                sin=f(np.concatenate([sin_t, np.zeros((TMW, LANES))], axis=0)))


def _mod_body(cv_ref, w_ref, b_ref, o_ref):
    c = cv_ref[...]
    s = c * _sigmoid(c)
    w = w_ref[0]
    outs = [jnp.sum(w * s[:, r:r + 1], axis=0, keepdims=True) for r in range(N_COND)]
    pad = jnp.zeros((SUBLANES - N_COND, MOD_PER_STEP * D_MODEL), F32)
    res = jnp.concatenate(outs + [pad], axis=0)
    for k in range(MOD_PER_STEP):
        o_ref[0, k] = res[:, k * D_MODEL:(k + 1) * D_MODEL] + b_ref[k]


MOD_PER_STEP = 2
N_COND = 3


def _modulation(cv, w_mod, b_mod):
    steps = N_MOD // MOD_PER_STEP
    out = pl.pallas_call(
        _mod_body,
        grid=(DEPTH, steps),
        in_specs=[pl.BlockSpec((D_MODEL, SUBLANES), lambda l, k: (0, 0)),
                  pl.BlockSpec((1, D_MODEL, MOD_PER_STEP * D_MODEL), lambda l, k: (l, 0, k)),
                  pl.BlockSpec((MOD_PER_STEP, 1, D_MODEL), lambda l, k: (l * steps + k, 0, 0))],
        out_specs=pl.BlockSpec((1, MOD_PER_STEP, SUBLANES, D_MODEL), lambda l, k: (l, k, 0, 0)),
        out_shape=jax.ShapeDtypeStruct((DEPTH, N_MOD, SUBLANES, D_MODEL), F32),
        compiler_params=_cparams(("arbitrary", "arbitrary")),
        name="modulation",
    )(cv, w_mod, b_mod.reshape(DEPTH * N_MOD, 1, D_MODEL))
    out = jnp.transpose(out[:, :, :3, :], (0, 2, 1, 3))
    return out.reshape(DEPTH * 3 * N_MOD, 1, D_MODEL)


def _mod_spec(layer, k, cond_fn):
    return pl.BlockSpec((1, 1, D_MODEL),
                        lambda i, *_: ((layer * 3 + cond_fn(i)) * N_MOD + k, 0, 0))


def _in_proj_body(x_ref, sh_ref, sc_ref, g_ref, w_ref, cos_ref, sin_ref,
                  u_ref, q_ref, kv_ref, gt_ref):
    lane = lax.broadcasted_iota(jnp.int32, (SUB, LANES), 1)
    first_half = (lane % (HEAD_DIM // 2)) < (HEAD_DIM // 4)
    quarter = HEAD_DIM // 4
    scale = HEAD_DIM ** -0.5 * LOG2E
    gc = 512
    for sb in range(TMW // SUB):
        rows = slice(sb * SUB, (sb + 1) * SUB)
        hb = _rms_mod(x_ref[rows, :], g_ref[...], sh_ref[0], sc_ref[0]).astype(BF16)
        u_ref[rows, :] = _dot(hb, w_ref[:, 0:OFF_Q]).astype(BF16)
        cos = cos_ref[rows, :]
        sin = sin_ref[rows, :]

        def rope(xs):
            below = pltpu.roll(xs, quarter, 1)
            above = pltpu.roll(xs, LANES - quarter, 1)
            return xs * cos + jnp.where(first_half, above, below) * sin

        qk = _dot(hb, w_ref[:, OFF_Q:OFF_V])
        low = lane < HEAD_DIM
        for j in range(ATTN_WIDTH // LANES):
            r = rope(qk[:, j * LANES:(j + 1) * LANES]) * scale
            swapped = pltpu.roll(r, HEAD_DIM, 1)
            if (2 * j) // GROUP == 0:
                even, odd = jnp.where(low, r, 0.0), jnp.where(low, swapped, 0.0)
            else:
                even, odd = jnp.where(low, 0.0, swapped), jnp.where(low, 0.0, r)
            q_ref[rows, (2 * j) * LANES:(2 * j + 1) * LANES] = even.astype(BF16)
            q_ref[rows, (2 * j + 1) * LANES:(2 * j + 2) * LANES] = odd.astype(BF16)
        kv_ref[rows, 0:LANES] = rope(qk[:, ATTN_WIDTH:ATTN_WIDTH + KV_WIDTH]).astype(BF16)
        v = _dot(hb, w_ref[:, OFF_V:OFF_G])
        kv_ref[rows, LANES:2 * LANES] = jnp.where(low, v, 1.0).astype(BF16)
        kv_ref[rows, 2 * LANES:3 * LANES] = jnp.where(low, pltpu.roll(v, HEAD_DIM, 1), 1.0).astype(BF16)
        for j in range(2 * D_MODEL // gc):
            z = _dot(hb, w_ref[:, OFF_G + j * gc:OFF_G + (j + 1) * gc])
            gt_ref[rows, j * gc:(j + 1) * gc] = _sigmoid(z).astype(BF16)


def _in_proj(x, x_extra, mods, layer, cond_fn, g, w_in, cos, sin, rope_idx):
    if x_extra is None:
        rows = x.shape[0]
        body, x_args = _in_proj_body, [x]
        x_specs = [pl.BlockSpec((TMW, D_MODEL), lambda i: (i, 0))]
    else:
        rows = x.shape[0] + x_extra.shape[0]
        body, x_args = _with_two_sources(_in_proj_body, 1, x.shape[0] // TMW), [x, x_extra]
        x_specs = _two_source_specs(x, x_extra, TMW, D_MODEL)
    return pl.pallas_call(
        body,
        grid=(rows // TMW,),
        in_specs=x_specs + [
                  _mod_spec(layer, 0, cond_fn),
                  _mod_spec(layer, 1, cond_fn),
                  pl.BlockSpec((1, D_MODEL), lambda i: (0, 0)),
                  _resident(w_in, layer),
                  pl.BlockSpec((TMW, LANES), lambda i: (rope_idx(i), 0)),
                  pl.BlockSpec((TMW, LANES), lambda i: (rope_idx(i), 0))],
        out_specs=[pl.BlockSpec((TMW, FOURIER_WIDTH), lambda i: (i, 0)),
                   pl.BlockSpec((TMW, Q_PAD_W), lambda i: (i, 0)),
                   pl.BlockSpec((TMW, KVX_W), lambda i: (i, 0)),
                   pl.BlockSpec((TMW, 2 * D_MODEL), lambda i: (i, 0))],
        out_shape=[jax.ShapeDtypeStruct((rows, FOURIER_WIDTH), BF16),
                   jax.ShapeDtypeStruct((rows, Q_PAD_W), BF16),
                   jax.ShapeDtypeStruct((rows, KVX_W), BF16),
                   jax.ShapeDtypeStruct((rows, 2 * D_MODEL), BF16)],
        compiler_params=_cparams(("arbitrary",), VMEM_LIMIT),
        name="in_proj",
    )(*x_args, mods, mods, g, w_in, cos, sin)


FFT_CB = 16


def _fft_a_body(x_ref, w_ref, o_ref):
    w = w_ref[...]
    xt = jnp.swapaxes(x_ref[...].astype(F32), 0, 1)
    res = jnp.stack([_dot(w, xt[c]) for c in range(FFT_CB)])
    o_ref[0] = jnp.swapaxes(res, 0, 1).astype(BF16)


def _fft_c_body(re_ref, im_ref, tc_ref, ts_ref, mc_ref, mch_ref, o_ref):
    mc = mc_ref[...]
    mch = mch_ref[...]
    twiddled = []
    for j in range(FFT_CB):
        ar = re_ref[0, j].astype(F32)
        ai = im_ref[0, j].astype(F32)
        tc = jnp.concatenate([tc_ref[j]] * N_GROUPS, axis=1)
        ts = jnp.concatenate([ts_ref[j]] * N_GROUPS, axis=1)
        twiddled.append(jnp.concatenate([ar * tc + ai * ts, ai * tc - ar * ts], axis=0))
    xs = [_dot(mc, b) for b in twiddled]
    xr = jnp.concatenate([x[:FFT_C] for x in xs], axis=0).astype(BF16)
    xi = jnp.concatenate([x[FFT_C:] for x in xs], axis=0).astype(BF16)
    ys = []
    for g in range(N_GROUPS):
        sl = slice(g * GROUP_W, (g + 1) * GROUP_W)
        ys.append(_dot(jnp.concatenate([xr[:, sl], xi[:, sl]], axis=1), mch))
    y = jnp.concatenate(ys, axis=1).reshape(FFT_CB, FFT_C, FOURIER_WIDTH)
    o_ref[0] = jnp.swapaxes(y, 0, 1).astype(BF16)


def _fourier_latent(u, tb, batch):
    u3 = u.reshape(u.shape[0] // FFT_C, FFT_C, FOURIER_WIDTH)
    a = pl.pallas_call(
        _fft_a_body,
        grid=(batch, FFT_C // FFT_CB),
        in_specs=[pl.BlockSpec((FFT_R, FFT_CB, FOURIER_WIDTH), lambda b, j: (b, j, 0)),
                  pl.BlockSpec((2 * FFT_R, FFT_R), lambda b, j: (0, 0))],
        out_specs=pl.BlockSpec((1, 2 * FFT_R, FFT_CB, FOURIER_WIDTH), lambda b, j: (b, 0, j, 0)),
        out_shape=jax.ShapeDtypeStruct((batch, 2 * FFT_R, FFT_C, FOURIER_WIDTH), BF16),
        compiler_params=_cparams(("arbitrary", "arbitrary"), VMEM_LIMIT),
        name="fft_rows",
    )(u3, tb["wa"].astype(BF16))
    nk = FFT_R // FFT_CB
    y = pl.pallas_call(
        _fft_c_body,
        grid=(batch, nk),
        in_specs=[pl.BlockSpec((1, FFT_CB, FFT_C, FOURIER_WIDTH), lambda b, k: (b, k, 0, 0)),
                  pl.BlockSpec((1, FFT_CB, FFT_C, FOURIER_WIDTH), lambda b, k: (b, nk + k, 0, 0)),
                  pl.BlockSpec((FFT_CB, FFT_C, LANES), lambda b, k: (k, 0, 0)),
                  pl.BlockSpec((FFT_CB, FFT_C, LANES), lambda b, k: (k, 0, 0)),
                  pl.BlockSpec((2 * FFT_C, 2 * FFT_C), lambda b, k: (0, 0)),
                  pl.BlockSpec((2 * GROUP_W, GROUP_W), lambda b, k: (0, 0))],
        out_specs=pl.BlockSpec((1, FFT_C, FFT_CB, FOURIER_WIDTH), lambda b, k: (b, 0, k, 0)),
        out_shape=jax.ShapeDtypeStruct((batch, FFT_C, FFT_R, FOURIER_WIDTH), BF16),
        compiler_params=_cparams(("arbitrary", "arbitrary"), VMEM_LIMIT),
        name="fft_cols",
    )(a, a, tb["twc"], tb["tws"], tb["mc"].astype(BF16), tb["mch"].astype(BF16))
    return y.reshape(batch * SEQ, FOURIER_WIDTH)


def _fourier_ctx_body(u_ref, w_ref, m_ref, o_ref):
    pq = _dot(w_ref[...], u_ref[...])
    p = pq[:CTX_LEN]
    q = pq[CTX_LEN:]
    m = m_ref[...]
    scale = 1.0 / math.sqrt(CTX_LEN * GROUP_W)
    for g in range(N_GROUPS):
        sl = slice(g * GROUP_W, (g + 1) * GROUP_W)
        lhs = jnp.concatenate([p[:, sl], q[:, sl]], axis=1).astype(BF16)
        o_ref[:, sl] = _dot(lhs, m) * scale


def _fourier_ctx(u, tb, batch):
    first = batch * SEQ // CTX_LEN
    return pl.pallas_call(
        _fourier_ctx_body,
        grid=(batch,),
        in_specs=[pl.BlockSpec((CTX_LEN, FOURIER_WIDTH), lambda b: (first + b, 0)),
                  pl.BlockSpec((2 * CTX_LEN, CTX_LEN), lambda b: (0, 0)),
                  pl.BlockSpec((2 * GROUP_W, GROUP_W), lambda b: (0, 0))],
        out_specs=pl.BlockSpec((CTX_LEN, FOURIER_WIDTH), lambda b: (b, 0)),
        out_shape=jax.ShapeDtypeStruct((batch * CTX_LEN, FOURIER_WIDTH), F32),
        compiler_params=_cparams(("arbitrary",)),
        name="fft_ctx",
    )(u, tb["wcx"].astype(BF16), tb["mchc"].astype(BF16))


def _attn_body(sink_ref, q_ref, *refs, local, nq):
    o_ref = refs[-1]
    rows = 2 * QB
    if local:
        blocks = [r[...] for r in refs[:nq + 2]]
        ctx_rows = refs[nq + 2][...]
        step = pl.program_id(1)
        last_step = pl.num_programs(1) - 1
        qi = lax.broadcasted_iota(jnp.int32, (rows, 2 * QB), 0) % QB
        kj = lax.broadcasted_iota(jnp.int32, (rows, 2 * QB), 1)
        groups, masks = [], []
        for t in range(nq):
            no_prev = jnp.where(step == 0, 2 * QB, 0) if t == 0 else 0
            no_next = jnp.where(step == last_step, 2 * QB, 0) if t == nq - 1 else 0
            prev_ok = (kj < QB) & (kj >= qi + no_prev)
            next_ok = (kj >= QB) & (kj - QB <= qi - no_next)
            groups.append([jnp.concatenate([blocks[t], blocks[t + 2]], axis=0),
                           jnp.concatenate([blocks[t + 1], ctx_rows], axis=0)])
            masks.append([prev_ok | next_ok, None])
    else:
        groups = [[refs[0][...]]]
        masks = [[None]]
    lo_q = lax.broadcasted_iota(jnp.int32, (QB, LANES), 1) < HEAD_DIM
    first_head = lax.broadcasted_iota(jnp.int32, (rows, 1), 0) < QB
    chains = [(t, slab) for t in range(nq) for slab in range(ATTN_WIDTH // LANES)]
    scores = []
    for t, slab in chains:
        qrows = slice(t * QB, (t + 1) * QB)
        qs = jnp.concatenate([q_ref[qrows, (2 * slab) * LANES:(2 * slab + 1) * LANES],
                              q_ref[qrows, (2 * slab + 1) * LANES:(2 * slab + 2) * LANES]], axis=0)
        parts = []
        for kv, mask in zip(groups[t], masks[t]):
            s = lax.dot_general(qs, kv[:, :LANES], (((1,), (1,)), ((), ())), preferred_element_type=F32)
            parts.append(s if mask is None else jnp.where(mask, s, NEG))
        scores.append(parts)
    sinks = [jnp.where(first_head, sink_ref[2 * slab], sink_ref[2 * slab + 1]) * LOG2E for _, slab in chains]
    maxes = []
    for parts, sk in zip(scores, sinks):
        m = sk
        for s in parts:
            m = jnp.maximum(m, jnp.max(s, axis=1, keepdims=True))
        maxes.append(m)
    sink_p = [jnp.exp2(sk - m) for sk, m in zip(sinks, maxes)]
    pvs = []
    for (t, slab), parts, m in zip(chains, scores, maxes):
        hk = (2 * slab) // GROUP
        pv = None
        for s, kv in zip(parts, groups[t]):
            term = _dot(jnp.exp2(s - m).astype(BF16), kv[:, (1 + hk) * LANES:(2 + hk) * LANES])
            pv = term if pv is None else pv + term
        pvs.append(pv)
    for (t, slab), pv, sp in zip(chains, pvs, sink_p):
        ra, rb = pv[:QB], pv[QB:]
        num = jnp.where(lo_q, ra, pltpu.roll(rb, HEAD_DIM, 1))
        den = jnp.where(lo_q, pltpu.roll(ra, HEAD_DIM, 1), rb) + jnp.where(lo_q, sp[:QB], sp[QB:])
        o_ref[t * QB:(t + 1) * QB, slab * LANES:(slab + 1) * LANES] = (num / den).astype(BF16)


def _attention_latent(q, kv, sink, batch):
    nb = SEQ // QB
    steps = nb // ATT_NQ
    first_ctx = batch * SEQ // CTX_LEN

    def key_block(j):
        return pl.BlockSpec((QB, KVX_W),
                            lambda b, n: (b * nb + jnp.clip(n * ATT_NQ - 1 + j, 0, nb - 1), 0))

    return pl.pallas_call(
        functools.partial(_attn_body, local=True, nq=ATT_NQ),
        grid=(batch, steps),
        in_specs=[pl.BlockSpec(memory_space=pltpu.SMEM),
                  pl.BlockSpec((ATT_NQ * QB, Q_PAD_W), lambda b, n: (b * steps + n, 0))]
                 + [key_block(j) for j in range(ATT_NQ + 2)]
                 + [pl.BlockSpec((CTX_LEN, KVX_W), lambda b, n: (first_ctx + b, 0))],
        out_specs=pl.BlockSpec((ATT_NQ * QB, ATTN_WIDTH), lambda b, n: (b * steps + n, 0)),
        out_shape=jax.ShapeDtypeStruct((batch * SEQ, ATTN_WIDTH), BF16),
        compiler_params=_cparams(("arbitrary", "arbitrary")),
        name="attn_latent",
    )(sink, q, *([kv] * (ATT_NQ + 3)))


def _attention_ctx(q, kv, sink, batch):
    nb = CTX_LEN // QB
    first_q = batch * SEQ // QB
    first_ctx = batch * SEQ // CTX_LEN
    return pl.pallas_call(
        functools.partial(_attn_body, local=False, nq=1),
        grid=(batch, nb),
        in_specs=[pl.BlockSpec(memory_space=pltpu.SMEM),
                  pl.BlockSpec((QB, Q_PAD_W), lambda b, n: (first_q + b * nb + n, 0)),
                  pl.BlockSpec((CTX_LEN, KVX_W), lambda b, n: (first_ctx + b, 0))],
        out_specs=pl.BlockSpec((QB, ATTN_WIDTH), lambda b, n: (b * nb + n, 0)),
        out_shape=jax.ShapeDtypeStruct((batch * CTX_LEN, ATTN_WIDTH), BF16),
        compiler_params=_cparams(("arbitrary", "arbitrary")),
        name="attn_ctx",
    )(sink, q, kv)


def _merge_body(x_ref, fm_ref, ao_ref, gt_ref, wf_ref, wa_ref, wo_ref, ga_ref, *refs, with_h2):
    if with_h2:
        sh_ref, sc_ref, g_ref, xo_ref, h_ref = refs
    else:
        (xo_ref,) = refs
    for sb in range(TMW // SUB):
        rows = slice(sb * SUB, (sb + 1) * SUB)
        gt = gt_ref[rows, :].astype(F32)
        y = (gt[:, :D_MODEL] * _dot(fm_ref[rows, :], wf_ref[...])
             + gt[:, D_MODEL:] * _dot(ao_ref[rows, :], wa_ref[...]))
        xn = x_ref[rows, :] + ga_ref[0] * _dot(y, wo_ref[...])
        if with_h2:
            h_ref[rows, :] = _rms_mod(xn, g_ref[...], sh_ref[0], sc_ref[0]).astype(BF16)
        xo_ref[rows, :] = xn


def _merge(rows, x, fm, ao, extras, gates, wf, wa, wo, mods, layer, cond_fn, g2, with_h2):
    row_spec = lambda w: pl.BlockSpec((TMW, w), lambda i: (i, 0))
    widths = (D_MODEL, FOURIER_WIDTH, ATTN_WIDTH)
    body = functools.partial(_merge_body, with_h2=with_h2)
    if extras is None:
        in_specs = [row_spec(w) for w in widths]
        args = [x, fm, ao]
    else:
        main_rows = rows
        rows = rows + extras[0].shape[0]
        body = _with_two_sources(body, len(widths), main_rows // TMW)
        in_specs, args = [], []
        for main, extra, w in zip((x, fm, ao), extras, widths):
            assert main.shape[0] >= main_rows and extra.shape[0] == extras[0].shape[0]
            in_specs += [pl.BlockSpec((TMW, w), lambda i: (jnp.minimum(i, main_rows // TMW - 1), 0)),
                         pl.BlockSpec((TMW, w), lambda i: (jnp.maximum(i - main_rows // TMW, 0), 0))]
            args += [main, extra]
    in_specs += [row_spec(2 * D_MODEL),
                 _resident(wf, layer), _resident(wa, layer), _resident(wo, layer), _mod_spec(layer, 2, cond_fn)]
    args += [gates, wf, wa, wo, mods]
    out_specs = [row_spec(D_MODEL)]
    out_shape = [jax.ShapeDtypeStruct((rows, D_MODEL), F32)]
    if with_h2:
        in_specs += [_mod_spec(layer, 3, cond_fn), _mod_spec(layer, 4, cond_fn),
                     pl.BlockSpec((1, D_MODEL), lambda i: (0, 0))]
        args += [mods, mods, g2]
        out_specs.append(row_spec(D_MODEL))
        out_shape.append(jax.ShapeDtypeStruct((rows, D_MODEL), BF16))
    return pl.pallas_call(
        body,
        grid=(rows // TMW,),
        in_specs=in_specs, out_specs=out_specs, out_shape=out_shape,
        compiler_params=_cparams(("arbitrary",), VMEM_LIMIT),
        name="merge",
    )(*args)


def _ffn_body(x_ref, h_ref, wg_ref, wu_ref, wd_ref, ga_ref, o_ref):
    d_ff = wg_ref.shape[1]
    mxu_tiles = d_ff // MXU_W
    edges = [MXU_W * (mxu_tiles * c // FFN_CHUNKS) for c in range(FFN_CHUNKS)] + [d_ff]
    blocks = [slice(k * SUB, (k + 1) * SUB) for k in range(TM // SUB)]
    hs = [h_ref[rows, :] for rows in blocks]
    downs = [None] * len(blocks)
    for c in range(FFN_CHUNKS):
        cols = slice(edges[c], edges[c + 1])
        acts = []
        for h in hs:
            g = _dot(h, wg_ref[:, cols])
            u = _dot(h, wu_ref[:, cols])
            acts.append(g * _sigmoid(g) * u)
        for k, a in enumerate(acts):
            d = _dot(a, wd_ref[cols, :])
            downs[k] = d if downs[k] is None else downs[k] + d
    for rows, d in zip(blocks, downs):
        o_ref[rows, :] = x_ref[rows, :] + ga_ref[0] * d


def _ffn_dense(x, h2, wg, wu, wd, mods, layer, cond_fn):
    rows = x.shape[0]
    return pl.pallas_call(
        _ffn_body,
        grid=(rows // TM,),
        in_specs=[pl.BlockSpec((TM, D_MODEL), lambda i: (i, 0)),
                  pl.BlockSpec((TM, D_MODEL), lambda i: (i, 0)),
                  _resident(wg, layer // 2), _resident(wu, layer // 2), _resident(wd, layer // 2),
                  _mod_spec(layer, 5, cond_fn)],
        out_specs=pl.BlockSpec((TM, D_MODEL), lambda i: (i, 0)),
        out_shape=jax.ShapeDtypeStruct((rows, D_MODEL), F32),
        compiler_params=_cparams(("arbitrary",), VMEM_LIMIT),
        name="ffn_dense",
    )(x, h2, wg, wu, wd, mods)


INFO_E0, INFO_E1, INFO_R0, INFO_R1, INFO_W0, INFO_W1 = range(6)


def _router_body(x_ref, sh_ref, sc_ref, g_ref, wr_ref, hb_ref, info_ref, cnt_ref):
    tiles = [slice(t * TD, (t + 1) * TD) for t in range(ROUTER_TILES)]
    lane = lax.broadcasted_iota(jnp.int32, (TD, LANES), 1)
    neg_inf = jnp.float32(-jnp.inf)
    row = lax.broadcasted_iota(jnp.int32, (TD, TD), 0)
    col = lax.broadcasted_iota(jnp.int32, (TD, TD), 1)
    tri = jnp.where(row > col, 1.0, 0.0).astype(BF16)
    hs = [_rms_mod(x_ref[rows, :], g_ref[...], sh_ref[0], sc_ref[0]) for rows in tiles]
    for rows, h in zip(tiles, hs):
        hb_ref[rows, :] = h.astype(BF16)
    lgs = [jnp.where(lane < N_EXPERTS,
                     jnp.dot(h, wr_ref[...], precision=lax.Precision.HIGHEST, preferred_element_type=F32), neg_inf)
           for h in hs]
    v0s = [jnp.max(lg, axis=1, keepdims=True) for lg in lgs]
    i0s = [jnp.min(jnp.where(lg == v0, lane, LANES), axis=1, keepdims=True) for lg, v0 in zip(lgs, v0s)]
    oh0s = [lane == i0 for i0 in i0s]
    lg1s = [jnp.where(oh0, neg_inf, lg) for oh0, lg in zip(oh0s, lgs)]
    v1s = [jnp.max(lg1, axis=1, keepdims=True) for lg1 in lg1s]
    i1s = [jnp.min(jnp.where(lg1 == v1, lane, LANES), axis=1, keepdims=True) for lg1, v1 in zip(lg1s, v1s)]
    oh1s = [lane == i1 for i1 in i1s]
    ohs = [jnp.where(oh0 | oh1, 1.0, 0.0) for oh0, oh1 in zip(oh0s, oh1s)]
    befores = [_dot(tri, oh.astype(BF16)) for oh in ohs]
    for t, rows in enumerate(tiles):
        e = jnp.exp(v1s[t] - v0s[t])
        w0 = 1.0 / (1.0 + e)
        w1 = e / (1.0 + e)
        r0 = jnp.sum(jnp.where(oh0s[t], befores[t], 0.0), axis=1, keepdims=True)
        r1 = jnp.sum(jnp.where(oh1s[t], befores[t], 0.0), axis=1, keepdims=True)
        cnt_ref[t] = jnp.broadcast_to(jnp.sum(ohs[t], axis=0, keepdims=True), (SUBLANES, LANES))
        info = jnp.zeros((TD, LANES), F32)
        for idx, val in ((INFO_E0, i0s[t].astype(F32)), (INFO_E1, i1s[t].astype(F32)), (INFO_R0, r0),
                         (INFO_R1, r1), (INFO_W0, w0), (INFO_W1, w1)):
            info = jnp.where(lane == idx, val, info)
        info_ref[rows, :] = info


def _router(x, mods, layer, cond_fn, g2, wr_pad):
    rows = x.shape[0]
    step_rows = ROUTER_TILES * TD
    step_cond = lambda i: cond_fn(i * ROUTER_TILES)
    return pl.pallas_call(
        _router_body,
        grid=(rows // step_rows,),
        in_specs=[pl.BlockSpec((step_rows, D_MODEL), lambda i: (i, 0)),
                  _mod_spec(layer, 3, step_cond), _mod_spec(layer, 4, step_cond),
                  pl.BlockSpec((1, D_MODEL), lambda i: (0, 0)),
                  pl.BlockSpec((D_MODEL, LANES), lambda i: (0, 0))],
        out_specs=[pl.BlockSpec((step_rows, D_MODEL), lambda i: (i, 0)),
                   pl.BlockSpec((step_rows, LANES), lambda i: (i, 0)),
                   pl.BlockSpec((ROUTER_TILES, SUBLANES, LANES), lambda i: (i, 0, 0))],
        out_shape=[jax.ShapeDtypeStruct((rows, D_MODEL), BF16),
                   jax.ShapeDtypeStruct((rows, LANES), F32),
                   jax.ShapeDtypeStruct((rows // TD, SUBLANES, LANES), F32)],
        compiler_params=_cparams(("arbitrary",), VMEM_LIMIT),
        name="router",
    )(x, mods, mods, g2, wr_pad)


def _segment_copies(src_ref, src_off, dst_ref, dst_off, len8, sem, bits, wait):
    for k in reversed(range(bits)):
        size = SUBLANES << k
        done = ((len8 >> (k + 1)) << (k + 1)) * SUBLANES

        @pl.when(((len8 >> k) & 1) == 1)
        def _(size=size, done=done):
            cp = pltpu.make_async_copy(
                src_ref.at[pl.ds(pl.multiple_of(src_off + done, SUBLANES), size)],
                dst_ref.at[pl.ds(pl.multiple_of(dst_off + done, SUBLANES), size)], sem)
            if wait:
                cp.wait()
            else:
                cp.start()


def _slot_positions(e0, e1, r0, r1, seg_ref, base):
    pos0, pos1 = r0, r1
    for e in range(N_EXPERTS):
        start = seg_ref[base + e].astype(F32)
        pos0 = pos0 + jnp.where(e0 == e, start, 0.0)
        pos1 = pos1 + jnp.where(e1 == e, start, 0.0)
    return pos0.astype(jnp.int32), pos1.astype(jnp.int32)


def _dispatch_body(seg_ref, dst_ref, len_ref, tdst_ref, tlen_ref, nt_ref, hb_ref, info_ref, xs_ref,
                   buf_ref, zero_ref, sems, zero_sem, *, min_tiles):
    i = pl.program_id(0)
    last = pl.num_programs(0) - 1
    base = i * N_EXPERTS
    cur = i % 2

    def scatter(tile, slot, wait):
        for e in range(N_EXPERTS):
            k = tile * N_EXPERTS + e
            _segment_copies(buf_ref.at[slot], seg_ref[k], xs_ref, dst_ref[k], len_ref[k],
                            sems.at[slot], SEG_BITS, wait=wait)

    info_t = info_ref[...].T
    row = lambda k: info_t[k:k + 1, :]
    pos0, pos1 = _slot_positions(row(INFO_E0), row(INFO_E1), row(INFO_R0), row(INFO_R1), seg_ref, base)
    slot = lax.broadcasted_iota(jnp.int32, (SB, TD), 0)
    p0 = slot == pos0
    p1 = slot == pos1
    perm = jnp.where(p0 | p1, 1.0, 0.0).astype(BF16)
    buf_ref[cur, :, :D_MODEL] = _dot(perm, hb_ref[...])
    wsel = jnp.where(p0, row(INFO_W0), 0.0) + jnp.where(p1, row(INFO_W1), 0.0)
    buf_ref[cur, :, D_MODEL:] = jnp.broadcast_to(jnp.sum(wsel, axis=1, keepdims=True), (SB, LANES))
    scatter(i, cur, wait=False)

    @pl.when(i > 0)
    def _():
        scatter(i - 1, 1 - cur, wait=True)

    @pl.when(i == last)
    def _():
        zero_ref[...] = jnp.zeros_like(zero_ref)
        spare_tiles = range(min_tiles, xs_ref.shape[0] // TMP)
        spare = [pltpu.make_async_copy(zero_ref, xs_ref.at[pl.ds(j * TMP, TMP)], zero_sem) for j in spare_tiles]
        for wait in (False, True):
            for e in range(N_EXPERTS):
                _segment_copies(zero_ref, 0, xs_ref, tdst_ref[e], tlen_ref[e], zero_sem, TAIL_BITS, wait=wait)
            for j, cp in zip(spare_tiles, spare):
                @pl.when(j >= nt_ref[0])
                def _(cp=cp, wait=wait):
                    if wait:
                        cp.wait()
                    else:
                        cp.start()
        scatter(i, cur, wait=True)


def _dispatch(seg, dst, len8, tail_dst, tail_len8, n_tiles, hb, info, m_rows):
    rows = hb.shape[0]
    return pl.pallas_call(
        functools.partial(_dispatch_body, min_tiles=2 * rows // TMP),
        grid_spec=pltpu.PrefetchScalarGridSpec(
            num_scalar_prefetch=6,
            grid=(rows // TD,),
            in_specs=[pl.BlockSpec((TD, D_MODEL), lambda i, *_: (i, 0)),
                      pl.BlockSpec((TD, LANES), lambda i, *_: (i, 0))],
            out_specs=pl.BlockSpec(memory_space=pl.ANY),
            scratch_shapes=[pltpu.VMEM((2, SB, XS_W), F32), pltpu.VMEM((TMP, XS_W), F32),
                            pltpu.SemaphoreType.DMA((2,)), pltpu.SemaphoreType.DMA(())]),
        out_shape=jax.ShapeDtypeStruct((m_rows, XS_W), F32),
        compiler_params=_cparams(("arbitrary",), VMEM_LIMIT),
        name="dispatch",
    )(seg, dst, len8, tail_dst, tail_len8, n_tiles, hb, info)


def _expert_body(te_ref, nb_ref, nt_ref, xs_ref, wg_ref, wu_ref, wd_ref, ys_ref, acc_ref):
    j = pl.program_id(0)
    f = pl.program_id(1)
    nf = pl.num_programs(1)
    n_live = nb_ref[j]

    blocks = [slice(h * TME, (h + 1) * TME) for h in range(TILE_BLOCKS)]

    @pl.when((n_live > 0) & (f == 0))
    def _():
        acc_ref[...] = jnp.zeros_like(acc_ref)

    for live in range(1, TILE_BLOCKS + 1):
        @pl.when(n_live == live)
        def _(live=live):
            ups = [(_dot(xs_ref[rows, :D_MODEL], wg_ref[0]), _dot(xs_ref[rows, :D_MODEL], wu_ref[0]))
                   for rows in blocks[:live]]
            downs = [_dot(g * _sigmoid(g) * u, wd_ref[0]) for g, u in ups]
            for rows, d in zip(blocks[:live], downs):
                acc_ref[rows, :] += d

    @pl.when(f == nf - 1)
    def _():
        live_rows = lax.broadcasted_iota(jnp.int32, (TMP, 1), 0) < n_live * TME
        ys_ref[...] = jnp.where(live_rows, acc_ref[...] * xs_ref[:, D_MODEL:D_MODEL + 1], 0.0)


def _experts(tile_expert, tile_blocks, n_tiles, xs, wg, wu, wd):
    m_rows = xs.shape[0]
    nf = wg.shape[2] // FC_E

    def f_idx(j, f, nt):
        last_live = nt[0] - 1
        walk = lambda t, step: jnp.where(t % 2 == 0, step, nf - 1 - step)
        return jnp.where(j < nt[0], walk(j, f), walk(last_live, nf - 1))

    def j_idx(j, nt):
        return jnp.minimum(j, nt[0] - 1)

    return pl.pallas_call(
        _expert_body,
        grid_spec=pltpu.PrefetchScalarGridSpec(
            num_scalar_prefetch=3,
            grid=(m_rows // TMP, nf),
            in_specs=[pl.BlockSpec((TMP, XS_W), lambda j, f, te, nb, nt: (j_idx(j, nt), 0)),
                      pl.BlockSpec((1, D_MODEL, FC_E), lambda j, f, te, nb, nt: (te[j], 0, f_idx(j, f, nt))),
                      pl.BlockSpec((1, D_MODEL, FC_E), lambda j, f, te, nb, nt: (te[j], 0, f_idx(j, f, nt))),
                      pl.BlockSpec((1, FC_E, D_MODEL), lambda j, f, te, nb, nt: (te[j], f_idx(j, f, nt), 0))],
            out_specs=pl.BlockSpec((TMP, D_MODEL), lambda j, f, te, nb, nt: (j, 0)),
            scratch_shapes=[pltpu.VMEM((TMP, D_MODEL), F32)]),
        out_shape=jax.ShapeDtypeStruct((m_rows, D_MODEL), F32),
        compiler_params=_cparams(("arbitrary", "arbitrary"), VMEM_LIMIT),
        name="experts",
    )(tile_expert, tile_blocks, n_tiles, xs, wg, wu, wd)


def _combine_body(seg_ref, dst_ref, len_ref, ys_ref, x_ref, info_ref, ga_ref, g_ref, o_ref, buf_ref, sems):
    i = pl.program_id(0)
    base = i * N_EXPERTS
    cur = i % 2

    def gather(tile, slot, wait):
        for e in range(N_EXPERTS):
            k = tile * N_EXPERTS + e
            _segment_copies(ys_ref, dst_ref[k], buf_ref.at[slot], seg_ref[k], len_ref[k],
                            sems.at[slot], SEG_BITS, wait=wait)

    @pl.when(i == 0)
    def _():
        buf_ref[...] = jnp.zeros_like(buf_ref)
        gather(0, 0, wait=False)

    @pl.when(i + 1 < pl.num_programs(0))
    def _():
        gather(i + 1, 1 - cur, wait=False)

    info = info_ref[...]
    col = lambda k: info[:, k:k + 1]
    pos0, pos1 = _slot_positions(col(INFO_E0), col(INFO_E1), col(INFO_R0), col(INFO_R1), seg_ref, base)
    slot = lax.broadcasted_iota(jnp.int32, (TD, SB), 1)
    unperm = jnp.where((slot == pos0) | (slot == pos1), 1.0, 0.0).astype(BF16)
    gather(i, cur, wait=True)
    y = _dot(unperm, buf_ref[cur])
    xn = x_ref[...] + ga_ref[0] * y
    r = lax.rsqrt(jnp.mean(xn * xn, axis=-1, keepdims=True) + EPS)
    o_ref[...] = (xn * r) * g_ref[...]


def _combine(seg, dst, len8, ys, x, info, mods, layer, cond_fn, final_g):
    rows = x.shape[0]
    return pl.pallas_call(
        _combine_body,
        grid_spec=pltpu.PrefetchScalarGridSpec(
            num_scalar_prefetch=3,
            grid=(rows // TD,),
            in_specs=[pl.BlockSpec(memory_space=pl.ANY),
                      pl.BlockSpec((TD, D_MODEL), lambda i, *_: (i, 0)),
                      pl.BlockSpec((TD, LANES), lambda i, *_: (i, 0)),
                      _mod_spec(layer, 5, cond_fn),
                      pl.BlockSpec((1, D_MODEL), lambda i, *_: (0, 0))],
            out_specs=pl.BlockSpec((TD, D_MODEL), lambda i, *_: (i, 0)),
            scratch_shapes=[pltpu.VMEM((2, SB, D_MODEL), F32), pltpu.SemaphoreType.DMA((2,))]),
        out_shape=jax.ShapeDtypeStruct((rows, D_MODEL), F32),
        compiler_params=_cparams(("arbitrary",), VMEM_LIMIT),
        name="combine",
    )(seg, dst, len8, ys, x, info, mods, final_g)


def _moe(x, mods, layer, cond_fn, g2, w_router, wg, wu, wd, final_g):
    rows = x.shape[0]
    n_tok_tiles = rows // TD
    wr_pad = jnp.pad(w_router, ((0, 0), (0, LANES - N_EXPERTS)))
    hb, info, cnt = _router(x, mods, layer, cond_fn, g2, wr_pad)
    counts = cnt[:, 0, :N_EXPERTS].astype(jnp.int32)
    seg_len = (counts + SUBLANES - 1) // SUBLANES * SUBLANES
    seg_start = jnp.cumsum(seg_len, axis=1) - seg_len
    group = jnp.sum(seg_len, axis=0)
    group_blocks = (group + TME - 1) // TME
    group_tiles = (group + TMP - 1) // TMP
    tile_start = jnp.cumsum(group_tiles) - group_tiles
    group_base = tile_start * TMP
    dst = group_base[None, :] + jnp.cumsum(seg_len, axis=0) - seg_len
    tail_dst = group_base + group
    tail_len8 = (group_tiles * TMP - group) // SUBLANES
    max_tiles = (2 * rows + n_tok_tiles * N_EXPERTS * (SUBLANES - 1) + TMP - 1) // TMP + N_EXPERTS
    n_tiles = jnp.sum(group_tiles).astype(jnp.int32).reshape(1)
    tiles = jnp.arange(max_tiles, dtype=jnp.int32)
    tid = jnp.minimum(tiles, n_tiles[0] - 1)
    tile_expert = (jnp.sum(tid[:, None] >= tile_start[None, :], axis=1) - 1).astype(jnp.int32)
    own = (tile_expert[:, None] == jnp.arange(N_EXPERTS, dtype=jnp.int32)[None, :]).astype(jnp.int32)
    in_group = tid - jnp.sum(own * tile_start[None, :], axis=1)
    tile_blocks = jnp.clip(jnp.sum(own * group_blocks[None, :], axis=1) - TILE_BLOCKS * in_group, 0, TILE_BLOCKS)
    tile_blocks = jnp.where(tiles < n_tiles[0], tile_blocks, 0).astype(jnp.int32)
    flat = lambda a: a.reshape(-1).astype(jnp.int32)
    seg, dst, len8 = flat(seg_start), flat(dst), flat(seg_len // SUBLANES)
    xs = _dispatch(seg, dst, len8, flat(tail_dst), flat(tail_len8), n_tiles, hb, info, max_tiles * TMP)
    ys = _experts(tile_expert, tile_blocks, n_tiles, xs, wg, wu, wd)
    return _combine(seg, dst, len8, ys, x, info, mods, layer, cond_fn, final_g)


def kernel(x, c, ctx, c_ctx, w_mod, b_mod, norm1_g, norm2_g, w_in, sink, w_fourier, w_attn, w_out,
           w_gate_d, w_up_d, w_down_d, w_router, w_gate_e, w_up_e, w_down_e, final_g):
    batch, seq, d = x.shape
    assert (seq, d) == (SEQ, D_MODEL) and ctx.shape == (batch, CTX_LEN, D_MODEL)
    tb = {k: jnp.asarray(v) for k, v in _tables().items()}
    n_lat = batch * SEQ
    assert (batch * CTX_LEN) % TMW == 0 and n_lat % TMW == 0

    def cond(tile):
        return lambda i: jnp.where(i < n_lat // tile, i // (SEQ // tile), batch)

    def rope_idx(i):
        return jnp.where(i < n_lat // TMW, i % (SEQ // TMW), SEQ // TMW + i - n_lat // TMW)

    assert batch + 1 == N_COND
    cv = jnp.zeros((D_MODEL, SUBLANES), F32).at[:, :batch].set(c.T).at[:, batch].set(c_ctx)
    mods = _modulation(cv, w_mod, b_mod)

    xs_main, xs_extra = x.reshape(n_lat, D_MODEL), ctx.reshape(batch * CTX_LEN, D_MODEL)
    out = None
    for l in range(DEPTH):
        last = l == DEPTH - 1
        g1 = norm1_g[l].reshape(1, D_MODEL)
        g2 = norm2_g[l].reshape(1, D_MODEL)
        wf, wa, wo = w_fourier, w_attn, w_out
        u, q, kv, gates = _in_proj(xs_main, xs_extra, mods, l, cond(TMW), g1, w_in, tb["cos"], tb["sin"], rope_idx)
        fm = _fourier_latent(u, tb, batch)
        ao = _attention_latent(q, kv, sink[l], batch)
        if l % 2 == 0:
            if last:
                raise NotImplementedError("final norm is fused into the routed-expert combine")
            extras = (xs_extra, _fourier_ctx(u, tb, batch), _attention_ctx(q, kv, sink[l], batch))
            xa, h2 = _merge(n_lat, xs_main, fm, ao, extras, gates, wf, wa, wo, mods, l, cond(TMW), g2, True)
            xs_main = _ffn_dense(xa, h2, w_gate_d, w_up_d, w_down_d, mods, l, cond(TM))
            xs_extra = None
        else:
            if not last:
                raise NotImplementedError("context update through a routed-expert layer")
            i = l // 2
            (xl,) = _merge(n_lat, xs_main, fm, ao, None, gates, wf, wa, wo, mods, l, cond(TMW), g2, False)
            out = _moe(xl, mods, l, cond(TD), g2, w_router[i], w_gate_e[i], w_up_e[i], w_down_e[i],
                       final_g.reshape(1, D_MODEL))
    return out.reshape(batch, SEQ, D_MODEL)
```
